```python
import math
import jax, jax.numpy as jnp
from jax import lax
import numpy as np

D_MODEL = 1024
BATCH = 8
SEQ = 4096
DEPTH = 4

N_A = DEPTH // 2
N_B = DEPTH - N_A
POOL_WINDOWS = (2, 4, 8, 16)
N_POOL_GROUPS = len(POOL_WINDOWS)
GROUP_CH = D_MODEL // N_POOL_GROUPS
HEAD_DIM = 64
N_Q_HEADS = D_MODEL // HEAD_DIM
N_KV_HEADS = 4
Q_PER_KV = N_Q_HEADS // N_KV_HEADS
WINDOW = 128
BLOCK = 128
ROPE_THETA = 10000.0
ATTN_SCALE = 1.0 / math.sqrt(HEAD_DIM)
NEG_INF = -1e30
D_FF = 2816
CONV_WIDTH = 3
RMS_EPS = 1e-6

kernel_name = "yoco_pool_swa_sink_hybrid"


def rms_norm(x, g):
    xf = x.astype(jnp.float32)
    y = xf * lax.rsqrt(jnp.mean(xf * xf, axis=-1, keepdims=True) + RMS_EPS)
    return (y * g.astype(jnp.float32)).astype(x.dtype)


def pool_mixer(h, w_pool, scale):
    B, S, D = h.shape
    hf = h.astype(jnp.float32)
    csum = jnp.concatenate([jnp.zeros((B, 1, D), jnp.float32), jnp.cumsum(hf, axis=1)], axis=1)
    t = jnp.arange(1, S + 1)
    diffs = []
    for gi, w in enumerate(POOL_WINDOWS):
        sl = slice(gi * GROUP_CH, (gi + 1) * GROUP_CH)
        lo = jnp.maximum(t - w, 0)
        cnt = jnp.minimum(t, w).astype(jnp.float32)
        mean = (csum[:, 1:, sl] - csum[:, lo, sl]) / cnt[None, :, None]
        diffs.append(mean - hf[..., sl])
    d = jnp.stack(diffs, axis=2).astype(h.dtype)
    y = jnp.einsum('bsgc,gcd->bsgd', d, w_pool).reshape(B, S, D)
    return y * scale


def conv_glu_ffn(h, w_in, conv_w, conv_b, w_out):
    S = h.shape[1]
    u = h @ w_in
    up = jnp.pad(u, ((0, 0), (CONV_WIDTH - 1, 0), (0, 0)))
    u = sum(conv_w[k] * up[:, k:k + S] for k in range(CONV_WIDTH)) + conv_b
    gate, val = jnp.split(u, 2, axis=-1)
    return (jax.nn.gelu(gate, approximate=True) * val) @ w_out


def rope(x, cos, sin):
    xf = x.astype(jnp.float32)
    x1, x2 = jnp.split(xf, 2, axis=-1)
    return jnp.concatenate([x1 * cos - x2 * sin, x2 * cos + x1 * sin], axis=-1).astype(x.dtype)


def rope_tables(positions):
    inv_freq = 1.0 / (ROPE_THETA ** (jnp.arange(0, HEAD_DIM, 2, dtype=jnp.float32) / HEAD_DIM))
    ang = positions.astype(jnp.float32)[..., None] * inv_freq
    return jnp.cos(ang)[:, :, None, :], jnp.sin(ang)[:, :, None, :]


def band_blocks(t):
    B, S = t.shape[:2]
    nb = S // BLOCK
    tb = t.reshape(B, nb, BLOCK, N_KV_HEADS, HEAD_DIM)
    prev = jnp.pad(tb, ((0, 0), (1, 0), (0, 0), (0, 0), (0, 0)))[:, :-1]
    return jnp.concatenate([prev, tb], axis=2).astype(jnp.float32)


def swa_sink_attention(q, kk, vv, sinks):
    B, S = q.shape[:2]
    nb = S // BLOCK
    qb = q.reshape(B, nb, BLOCK, N_KV_HEADS, Q_PER_KV, HEAD_DIM).astype(jnp.float32) * ATTN_SCALE
    s = jnp.einsum('bnqhgd,bnkhd->bnhgqk', qb, kk)
    qi = jnp.arange(BLOCK)[:, None]
    kj = jnp.arange(2 * BLOCK)[None, :]
    rel = BLOCK + qi - kj
    blk = jnp.arange(nb)[:, None, None]
    valid = (rel >= 0) & (rel < WINDOW) & (blk * BLOCK + kj - BLOCK >= 0)
    s = jnp.where(valid[None, :, None, None], s, NEG_INF)
    sink = sinks.astype(jnp.float32).reshape(N_KV_HEADS, Q_PER_KV)[None, None, :, :, None, None]
    m = jnp.maximum(jnp.max(s, axis=-1, keepdims=True), sink)
    p = jnp.exp(s - m)
    denom = jnp.sum(p, axis=-1) + jnp.exp(sink - m)[..., 0]
    o = jnp.einsum('bnhgqk,bnkhd->bnhgqd', p, vv) / denom[..., None]
    o = o.transpose(0, 1, 4, 2, 3, 5).reshape(B, S, N_Q_HEADS * HEAD_DIM)
    return o.astype(q.dtype)


def _fwd_setup_inputs(seed: int = 0) -> dict:
    key = jax.random.key(seed)
    ks = jax.random.split(key, 20)
    f32 = jnp.float32
    D, F = D_MODEL, D_FF
    HQD, HKVD = N_Q_HEADS * HEAD_DIM, N_KV_HEADS * HEAD_DIM

    def gain(k, shape):
        return 1.0 + 0.05 * jax.random.normal(k, shape, f32)

    x = jax.random.normal(ks[0], (BATCH, SEQ, D), f32)
    positions = jnp.broadcast_to(jnp.arange(SEQ, dtype=jnp.int32)[None, :], (BATCH, SEQ))
    return {
        "x": x,
        "positions": positions,
        "mix_pre_g": gain(ks[1], (DEPTH, D)),
        "mix_post_g": gain(ks[2], (DEPTH, D)),
        "pool_w": jax.random.normal(ks[3], (N_A, N_POOL_GROUPS, GROUP_CH, GROUP_CH), f32) * GROUP_CH ** -0.5,
        "pool_scale": 1.0 + 0.1 * jax.random.normal(ks[4], (N_A, D), f32),
        "kv_norm_g": gain(ks[5], (D,)),
        "w_kv": jax.random.normal(ks[6], (D, 2 * HKVD), f32) * D ** -0.5,
        "w_q": jax.random.normal(ks[7], (N_B, D, HQD), f32) * D ** -0.5,
        "w_o": jax.random.normal(ks[8], (N_B, HQD, D), f32) * HQD ** -0.5,
        "sinks": jax.random.normal(ks[9], (N_B, N_Q_HEADS), f32),
        "ffn_pre_g": gain(ks[10], (DEPTH, D)),
        "ffn_post_g": gain(ks[11], (DEPTH, D)),
        "ffn_w_in": jax.random.normal(ks[12], (DEPTH, D, 2 * F), f32) * D ** -0.5,
        "ffn_conv_w": jax.random.normal(ks[13], (DEPTH, CONV_WIDTH, 2 * F), f32) * CONV_WIDTH ** -0.5,
        "ffn_conv_b": 0.01 * jax.random.normal(ks[14], (DEPTH, 2 * F), f32),
        "ffn_w_out": jax.random.normal(ks[15], (DEPTH, F, D), f32) * F ** -0.5,
    }


def _fwd_reference(x, positions, mix_pre_g, mix_post_g, pool_w, pool_scale, kv_norm_g, w_kv,
              w_q, w_o, sinks, ffn_pre_g, ffn_post_g, ffn_w_in, ffn_conv_w, ffn_conv_b, ffn_w_out):
    B, S, D = x.shape
    cos, sin = rope_tables(positions)
    kk = vv = None
    for layer in range(DEPTH):
        h = rms_norm(x, mix_pre_g[layer])
        if layer < N_A:
            m = pool_mixer(h, pool_w[layer], pool_scale[layer])
        else:
            if layer == N_A:
                hkv = rms_norm(x, kv_norm_g)
                kv = (hkv @ w_kv).reshape(B, S, 2, N_KV_HEADS, HEAD_DIM)
                k_shared = rope(kv[:, :, 0], cos, sin)
                kk, vv = band_blocks(k_shared), band_blocks(kv[:, :, 1])
            j = layer - N_A
            q = rope((h @ w_q[j]).reshape(B, S, N_Q_HEADS, HEAD_DIM), cos, sin)
            m = swa_sink_attention(q, kk, vv, sinks[j]) @ w_o[j]
        x = x + rms_norm(m, mix_post_g[layer])
        f = conv_glu_ffn(rms_norm(x, ffn_pre_g[layer]), ffn_w_in[layer], ffn_conv_w[layer],
                         ffn_conv_b[layer], ffn_w_out[layer])
        x = x + rms_norm(f, ffn_post_g[layer])
    return x


import jax as _jax
import jax.numpy as _jnp

TWIN_FORMAT = 'train_step'
FWD_PARAMS = ['x', 'positions', 'mix_pre_g', 'mix_post_g', 'pool_w', 'pool_scale', 'kv_norm_g', 'w_kv', 'w_q', 'w_o', 'sinks', 'ffn_pre_g', 'ffn_post_g', 'ffn_w_in', 'ffn_conv_w', 'ffn_conv_b', 'ffn_w_out']
TWIN_WEIGHTS = ['mix_pre_g', 'mix_post_g', 'pool_w', 'pool_scale', 'kv_norm_g', 'w_kv', 'w_q', 'w_o', 'sinks', 'ffn_pre_g', 'ffn_post_g', 'ffn_w_in', 'ffn_conv_w', 'ffn_conv_b', 'ffn_w_out']
TWIN_DIFF_INPUT = 'x'
TWIN_INPUTS = ['x', 'positions', 'mix_pre_g', 'mix_post_g', 'pool_w', 'pool_scale', 'kv_norm_g', 'w_kv', 'w_q', 'w_o', 'sinks', 'ffn_pre_g', 'ffn_post_g', 'ffn_w_in', 'ffn_conv_w', 'ffn_conv_b', 'ffn_w_out', 'loss_target', 'm_mix_pre_g', 'm_mix_post_g', 'm_pool_w', 'm_pool_scale', 'm_kv_norm_g', 'm_w_kv', 'm_w_q', 'm_w_o', 'm_sinks', 'm_ffn_pre_g', 'm_ffn_post_g', 'm_ffn_w_in', 'm_ffn_conv_w', 'm_ffn_conv_b', 'm_ffn_w_out', 'v_mix_pre_g', 'v_mix_post_g', 'v_pool_w', 'v_pool_scale', 'v_kv_norm_g', 'v_w_kv', 'v_w_q', 'v_w_o', 'v_sinks', 'v_ffn_pre_g', 'v_ffn_post_g', 'v_ffn_w_in', 'v_ffn_conv_w', 'v_ffn_conv_b', 'v_ffn_w_out']
TWIN_OUTPUTS = ['loss', 'grad_x', 'grad_mix_pre_g', 'grad_mix_post_g', 'grad_pool_w', 'grad_pool_scale', 'grad_kv_norm_g', 'grad_w_kv', 'grad_w_q', 'grad_w_o', 'grad_sinks', 'grad_ffn_pre_g', 'grad_ffn_post_g', 'grad_ffn_w_in', 'grad_ffn_conv_w', 'grad_ffn_conv_b', 'grad_ffn_w_out', 'delta_mix_pre_g', 'delta_mix_post_g', 'delta_pool_w', 'delta_pool_scale', 'delta_kv_norm_g', 'delta_w_kv', 'delta_w_q', 'delta_w_o', 'delta_sinks', 'delta_ffn_pre_g', 'delta_ffn_post_g', 'delta_ffn_w_in', 'delta_ffn_conv_w', 'delta_ffn_conv_b', 'delta_ffn_w_out', 'new_m_mix_pre_g', 'new_m_mix_post_g', 'new_m_pool_w', 'new_m_pool_scale', 'new_m_kv_norm_g', 'new_m_w_kv', 'new_m_w_q', 'new_m_w_o', 'new_m_sinks', 'new_m_ffn_pre_g', 'new_m_ffn_post_g', 'new_m_ffn_w_in', 'new_m_ffn_conv_w', 'new_m_ffn_conv_b', 'new_m_ffn_w_out', 'new_v_mix_pre_g', 'new_v_mix_post_g', 'new_v_pool_w', 'new_v_pool_scale', 'new_v_kv_norm_g', 'new_v_w_kv', 'new_v_w_q', 'new_v_w_o', 'new_v_sinks', 'new_v_ffn_pre_g', 'new_v_ffn_post_g', 'new_v_ffn_w_in', 'new_v_ffn_conv_w', 'new_v_ffn_conv_b', 'new_v_ffn_w_out']
TWIN_LEAF_KINDS = {'loss': 'loss', 'grad_x': 'grad_x', 'grad_mix_pre_g': 'grad_w', 'grad_mix_post_g': 'grad_w', 'grad_pool_w': 'grad_w', 'grad_pool_scale': 'grad_w', 'grad_kv_norm_g': 'grad_w', 'grad_w_kv': 'grad_w', 'grad_w_q': 'grad_w', 'grad_w_o': 'grad_w', 'grad_sinks': 'grad_w', 'grad_ffn_pre_g': 'grad_w', 'grad_ffn_post_g': 'grad_w', 'grad_ffn_w_in': 'grad_w', 'grad_ffn_conv_w': 'grad_w', 'grad_ffn_conv_b': 'grad_w', 'grad_ffn_w_out': 'grad_w', 'delta_mix_pre_g': 'delta_w', 'delta_mix_post_g': 'delta_w', 'delta_pool_w': 'delta_w', 'delta_pool_scale': 'delta_w', 'delta_kv_norm_g': 'delta_w', 'delta_w_kv': 'delta_w', 'delta_w_q': 'delta_w', 'delta_w_o': 'delta_w', 'delta_sinks': 'delta_w', 'delta_ffn_pre_g': 'delta_w', 'delta_ffn_post_g': 'delta_w', 'delta_ffn_w_in': 'delta_w', 'delta_ffn_conv_w': 'delta_w', 'delta_ffn_conv_b': 'delta_w', 'delta_ffn_w_out': 'delta_w', 'new_m_mix_pre_g': 'new_m', 'new_m_mix_post_g': 'new_m', 'new_m_pool_w': 'new_m', 'new_m_pool_scale': 'new_m', 'new_m_kv_norm_g': 'new_m', 'new_m_w_kv': 'new_m', 'new_m_w_q': 'new_m', 'new_m_w_o': 'new_m', 'new_m_sinks': 'new_m', 'new_m_ffn_pre_g': 'new_m', 'new_m_ffn_post_g': 'new_m', 'new_m_ffn_w_in': 'new_m', 'new_m_ffn_conv_w': 'new_m', 'new_m_ffn_conv_b': 'new_m', 'new_m_ffn_w_out': 'new_m', 'new_v_mix_pre_g': 'new_v', 'new_v_mix_post_g': 'new_v', 'new_v_pool_w': 'new_v', 'new_v_pool_scale': 'new_v', 'new_v_kv_norm_g': 'new_v', 'new_v_w_kv': 'new_v', 'new_v_w_q': 'new_v', 'new_v_w_o': 'new_v', 'new_v_sinks': 'new_v', 'new_v_ffn_pre_g': 'new_v', 'new_v_ffn_post_g': 'new_v', 'new_v_ffn_w_in': 'new_v', 'new_v_ffn_conv_w': 'new_v', 'new_v_ffn_conv_b': 'new_v', 'new_v_ffn_w_out': 'new_v'}


def _forward(args):
    return _fwd_reference(*[args[k] for k in FWD_PARAMS])


def _output_shape():
    out = _jax.eval_shape(lambda: _forward(_fwd_setup_inputs(0)))
    return out.shape, out.dtype

N_MICROBATCH = 1
ADAM_LR = 0.001
ADAM_B1 = 0.9
ADAM_B2 = 0.999
ADAM_EPS = 1e-08
ADAM_WD = 0.01
ADAM_STEP = 10
PER_EXAMPLE_BATCH_AXIS = {'x': 0, 'positions': 0, 'loss_target': 0}
SHARED_INPUTS = []
_WEIGHT_DTYPES = {'mix_pre_g': _jnp.float32, 'mix_post_g': _jnp.float32, 'pool_w': _jnp.float32, 'pool_scale': _jnp.float32, 'kv_norm_g': _jnp.float32, 'w_kv': _jnp.float32, 'w_q': _jnp.float32, 'w_o': _jnp.float32, 'sinks': _jnp.float32, 'ffn_pre_g': _jnp.float32, 'ffn_post_g': _jnp.float32, 'ffn_w_in': _jnp.float32, 'ffn_conv_w': _jnp.float32, 'ffn_conv_b': _jnp.float32, 'ffn_w_out': _jnp.float32}
MOMENT_SCALE = {'mix_pre_g': 2.190792e+00, 'mix_post_g': 3.228388e+01, 'pool_w': 2.742379e+00, 'pool_scale': 3.854729e+00, 'kv_norm_g': 1.812671e+00, 'w_kv': 2.467736e+00, 'w_q': 8.264120e-01, 'w_o': 9.744228e-01, 'sinks': 2.432731e-01, 'ffn_pre_g': 1.588200e+00, 'ffn_post_g': 3.185568e+01, 'ffn_w_in': 6.749360e-01, 'ffn_conv_w': 6.798541e-01, 'ffn_conv_b': 2.407919e+00, 'ffn_w_out': 1.181231e+00}


def _to_microbatches(a, axis):
    t = _jnp.moveaxis(a, axis, 0)
    t = t.reshape((N_MICROBATCH, t.shape[0] // N_MICROBATCH) + t.shape[1:])
    return _jnp.moveaxis(t, 1, axis + 1)


def setup_inputs(seed: int = 0) -> dict:
    inp = _fwd_setup_inputs(seed)
    key = _jax.random.fold_in(_jax.random.key(seed), 7919)
    shape, _ = _output_shape()
    out = dict(inp)
    out["loss_target"] = _jax.random.normal(_jax.random.fold_in(key, 0), shape, _jnp.float32)
    for i, name in enumerate(TWIN_WEIGHTS):
        w = inp[name].astype(_jnp.float32)
        if MOMENT_SCALE is None:
            s = _jnp.sqrt(_jnp.mean(_jnp.square(w)) + 1e-30)
        else:
            s = MOMENT_SCALE[name]
        km, kv = _jax.random.split(_jax.random.fold_in(key, i + 1))
        out[name] = w
        out["m_" + name] = s * _jax.random.normal(km, w.shape, _jnp.float32)
        out["v_" + name] = (s * s) * _jax.random.uniform(kv, w.shape, _jnp.float32, 0.5, 1.5)
    if N_MICROBATCH > 1:
        for name, axis in PER_EXAMPLE_BATCH_AXIS.items():
            out[name] = _to_microbatches(out[name], axis)
    return {'x': out['x'], 'positions': out['positions'], 'mix_pre_g': out['mix_pre_g'], 'mix_post_g': out['mix_post_g'], 'pool_w': out['pool_w'], 'pool_scale': out['pool_scale'], 'kv_norm_g': out['kv_norm_g'], 'w_kv': out['w_kv'], 'w_q': out['w_q'], 'w_o': out['w_o'], 'sinks': out['sinks'], 'ffn_pre_g': out['ffn_pre_g'], 'ffn_post_g': out['ffn_post_g'], 'ffn_w_in': out['ffn_w_in'], 'ffn_conv_w': out['ffn_conv_w'], 'ffn_conv_b': out['ffn_conv_b'], 'ffn_w_out': out['ffn_w_out'], 'loss_target': out['loss_target'], 'm_mix_pre_g': out['m_mix_pre_g'], 'm_mix_post_g': out['m_mix_post_g'], 'm_pool_w': out['m_pool_w'], 'm_pool_scale': out['m_pool_scale'], 'm_kv_norm_g': out['m_kv_norm_g'], 'm_w_kv': out['m_w_kv'], 'm_w_q': out['m_w_q'], 'm_w_o': out['m_w_o'], 'm_sinks': out['m_sinks'], 'm_ffn_pre_g': out['m_ffn_pre_g'], 'm_ffn_post_g': out['m_ffn_post_g'], 'm_ffn_w_in': out['m_ffn_w_in'], 'm_ffn_conv_w': out['m_ffn_conv_w'], 'm_ffn_conv_b': out['m_ffn_conv_b'], 'm_ffn_w_out': out['m_ffn_w_out'], 'v_mix_pre_g': out['v_mix_pre_g'], 'v_mix_post_g': out['v_mix_post_g'], 'v_pool_w': out['v_pool_w'], 'v_pool_scale': out['v_pool_scale'], 'v_kv_norm_g': out['v_kv_norm_g'], 'v_w_kv': out['v_w_kv'], 'v_w_q': out['v_w_q'], 'v_w_o': out['v_w_o'], 'v_sinks': out['v_sinks'], 'v_ffn_pre_g': out['v_ffn_pre_g'], 'v_ffn_post_g': out['v_ffn_post_g'], 'v_ffn_w_in': out['v_ffn_w_in'], 'v_ffn_conv_w': out['v_ffn_conv_w'], 'v_ffn_conv_b': out['v_ffn_conv_b'], 'v_ffn_w_out': out['v_ffn_w_out']}


def _loss(weights, diff, rest, loss_target):
    with _jax.named_scope("forward"):
        args = {**rest, TWIN_DIFF_INPUT: diff, **{k: w.astype(_WEIGHT_DTYPES[k]) for k, w in weights.items()}}
        y = _forward(args)
    with _jax.named_scope("loss_head"):
        err = _jnp.square(y.astype(_jnp.float32) - loss_target)
        return 0.5 * _jnp.sum(_jnp.mean(err, axis=-1)) if err.ndim else 0.5 * err


def _adamw(w, g, m, v):
    m = ADAM_B1 * m + (1.0 - ADAM_B1) * g
    v = ADAM_B2 * v + (1.0 - ADAM_B2) * _jnp.square(g)
    m_hat = m / (1.0 - ADAM_B1 ** ADAM_STEP)
    v_hat = v / (1.0 - ADAM_B2 ** ADAM_STEP)
    delta = -ADAM_LR * (m_hat / (_jnp.sqrt(v_hat) + ADAM_EPS) + ADAM_WD * w)
    return delta, m, v


def reference(x, positions, mix_pre_g, mix_post_g, pool_w, pool_scale, kv_norm_g, w_kv, w_q, w_o, sinks, ffn_pre_g, ffn_post_g, ffn_w_in, ffn_conv_w, ffn_conv_b, ffn_w_out, loss_target, m_mix_pre_g, m_mix_post_g, m_pool_w, m_pool_scale, m_kv_norm_g, m_w_kv, m_w_q, m_w_o, m_sinks, m_ffn_pre_g, m_ffn_post_g, m_ffn_w_in, m_ffn_conv_w, m_ffn_conv_b, m_ffn_w_out, v_mix_pre_g, v_mix_post_g, v_pool_w, v_pool_scale, v_kv_norm_g, v_w_kv, v_w_q, v_w_o, v_sinks, v_ffn_pre_g, v_ffn_post_g, v_ffn_w_in, v_ffn_conv_w, v_ffn_conv_b, v_ffn_w_out):
    given = dict(x=x, positions=positions, mix_pre_g=mix_pre_g, mix_post_g=mix_post_g, pool_w=pool_w, pool_scale=pool_scale, kv_norm_g=kv_norm_g, w_kv=w_kv, w_q=w_q, w_o=w_o, sinks=sinks, ffn_pre_g=ffn_pre_g, ffn_post_g=ffn_post_g, ffn_w_in=ffn_w_in, ffn_conv_w=ffn_conv_w, ffn_conv_b=ffn_conv_b, ffn_w_out=ffn_w_out, loss_target=loss_target, m_mix_pre_g=m_mix_pre_g, m_mix_post_g=m_mix_post_g, m_pool_w=m_pool_w, m_pool_scale=m_pool_scale, m_kv_norm_g=m_kv_norm_g, m_w_kv=m_w_kv, m_w_q=m_w_q, m_w_o=m_w_o, m_sinks=m_sinks, m_ffn_pre_g=m_ffn_pre_g, m_ffn_post_g=m_ffn_post_g, m_ffn_w_in=m_ffn_w_in, m_ffn_conv_w=m_ffn_conv_w, m_ffn_conv_b=m_ffn_conv_b, m_ffn_w_out=m_ffn_w_out, v_mix_pre_g=v_mix_pre_g, v_mix_post_g=v_mix_post_g, v_pool_w=v_pool_w, v_pool_scale=v_pool_scale, v_kv_norm_g=v_kv_norm_g, v_w_kv=v_w_kv, v_w_q=v_w_q, v_w_o=v_w_o, v_sinks=v_sinks, v_ffn_pre_g=v_ffn_pre_g, v_ffn_post_g=v_ffn_post_g, v_ffn_w_in=v_ffn_w_in, v_ffn_conv_w=v_ffn_conv_w, v_ffn_conv_b=v_ffn_conv_b, v_ffn_w_out=v_ffn_w_out)
    weights = {n: given[n] for n in TWIN_WEIGHTS}
    shared = {n: given[n] for n in SHARED_INPUTS}
    per_example = {n: given[n] for n in ['x', 'positions']}
    grad_fn = _jax.value_and_grad(_loss, argnums=(0, 1))

    def one_microbatch(ex, loss_target):
        ex = dict(ex)
        diff = ex.pop(TWIN_DIFF_INPUT)
        return grad_fn(weights, diff, {**shared, **ex}, loss_target)

    if N_MICROBATCH == 1:
        loss, (grad_w, grad_x) = one_microbatch(per_example, given["loss_target"])
    else:
        def body(carry, xs):
            loss_sum, grad_sum = carry
            l_k, (gw_k, gx_k) = one_microbatch(xs[0], xs[1])
            with _jax.named_scope("update"):
                return (loss_sum + l_k, _jax.tree.map(_jnp.add, grad_sum, gw_k)), gx_k

        init = (_jnp.zeros((), _jnp.float32), _jax.tree.map(_jnp.zeros_like, weights))
        (loss, grad_w), grad_x = _jax.lax.scan(body, init, (per_example, given["loss_target"]))
    with _jax.named_scope("update"):
        delta_w, new_m, new_v = {}, {}, {}
        for n in TWIN_WEIGHTS:
            delta_w[n], new_m[n], new_v[n] = _adamw(weights[n], grad_w[n], given["m_" + n], given["v_" + n])
    return (loss, grad_x, *[grad_w[n] for n in TWIN_WEIGHTS], *[delta_w[n] for n in TWIN_WEIGHTS],
            *[new_m[n] for n in TWIN_WEIGHTS], *[new_v[n] for n in TWIN_WEIGHTS])
```

```python
import functools
import math

import jax
import jax.numpy as jnp
from jax import lax
from jax.experimental import pallas as pl
from jax.experimental.pallas import tpu as pltpu

F32 = jnp.float32
BF16 = jnp.bfloat16
MESH = pl.DeviceIdType.MESH
ANY = pl.BlockSpec(memory_space=pl.ANY)

HEAD_DIM = 64
N_KV_HEADS = 4
KV_DIM = 2 * N_KV_HEADS * HEAD_DIM
WINDOW = 128
BLOCK = 128
POOL_WINDOWS = (2, 4, 8, 16)
POOL_HALO = 16
CONV_HALO = 8
ROPE_THETA = 10000.0
ATTN_SCALE = 1.0 / math.sqrt(HEAD_DIM)
NEG_INF = -1e30
RMS_EPS = 1e-6
ADAM_LR, ADAM_B1, ADAM_B2, ADAM_EPS, ADAM_WD, ADAM_STEP = 0.001, 0.9, 0.999, 1e-08, 0.01, 10
N_SHARDS = 4
LANES = 128
VMEM_LIMIT_BYTES = 48 << 20

WEIGHTS = ['mix_pre_g', 'mix_post_g', 'pool_w', 'pool_scale', 'kv_norm_g', 'w_kv', 'w_q', 'w_o', 'sinks',
           'ffn_pre_g', 'ffn_post_g', 'ffn_w_in', 'ffn_conv_w', 'ffn_conv_b', 'ffn_w_out']


def _call(body, *, name, out_shape, grid=None, in_specs=None, out_specs=None, scratch_shapes=(), dims=None,
          grid_spec=None):
    params = pltpu.CompilerParams(dimension_semantics=dims, vmem_limit_bytes=VMEM_LIMIT_BYTES)
    if grid_spec is not None:
        return pl.pallas_call(body, name=name, out_shape=out_shape, grid_spec=grid_spec, compiler_params=params)
    kw = {} if grid is None else dict(grid=grid)
    return pl.pallas_call(body, name=name, out_shape=out_shape, in_specs=in_specs, out_specs=out_specs,
                          scratch_shapes=list(scratch_shapes), compiler_params=params, **kw)


def _tile(n, pref, mult=8):
    if n <= pref:
        return n
    for t in range(pref, 0, -1):
        if n % t == 0 and t % mult == 0:
            return t
    raise ValueError((n, pref, mult))


def _sds(shape, dtype):
    return jax.ShapeDtypeStruct(tuple(shape), dtype)


def _perm4(j):
    return (j % 2) * 2 + j // 2


def _matmul(a, b, mode, out_dtype, name, tm, tn, tk, la=None, lb=None, b_perm=False, out_perm=False):
    a2, b2 = a.shape[-2:], b.shape[-2:]
    if mode == 'nn':
        (M, K), (K2, N) = a2, b2
    elif mode == 'nt':
        (M, K), (N, K2) = a2, b2
    else:
        (K, M), (K2, N) = a2, b2
    assert K == K2, (name, a.shape, b.shape)
    tm, tn, tk = _tile(M, tm), _tile(N, tn, LANES), _tile(K, tk, LANES if mode != 'tn' else 16)
    assert M % tm == 0 and N % tn == 0 and K % tk == 0
    nk = K // tk
    grid = (N // tn, M // tm, nk)

    def spec(l, blk, imap):
        if l is None:
            return pl.BlockSpec(blk, imap)
        return pl.BlockSpec((None,) + blk, lambda j, i, k: (l,) + imap(j, i, k))

    pj = _perm4 if b_perm else (lambda j: j)
    if mode == 'nn':
        a_spec = spec(la, (tm, tk), lambda j, i, k: (i, k))
        b_spec = spec(lb, (tk, tn), lambda j, i, k: (k, pj(j)))
        dn = (((1,), (0,)), ((), ()))
    elif mode == 'nt':
        a_spec = spec(la, (tm, tk), lambda j, i, k: (i, k))
        b_spec = spec(lb, (tn, tk), lambda j, i, k: (j, pj(k)))
        dn = (((1,), (1,)), ((), ()))
    else:
        a_spec = spec(la, (tk, tm), lambda j, i, k: (k, i))
        b_spec = spec(lb, (tk, tn), lambda j, i, k: (k, j))
        dn = (((0,), (0,)), ((), ()))
    po = _perm4 if out_perm else (lambda j: j)
    o_spec = pl.BlockSpec((tm, tn), lambda j, i, k: (i, po(j)))

    def body(a_ref, b_ref, o_ref, *acc):
        prod = lax.dot_general(a_ref[...].astype(BF16), b_ref[...].astype(BF16), dn, preferred_element_type=F32)
        if nk == 1:
            o_ref[...] = prod.astype(o_ref.dtype)
        else:
            k = pl.program_id(2)

            @pl.when(k == 0)
            def _():
                acc[0][...] = prod

            @pl.when(k > 0)
            def _():
                acc[0][...] += prod

            @pl.when(k == nk - 1)
            def _():
                o_ref[...] = acc[0][...].astype(o_ref.dtype)

    scratch = [] if nk == 1 else [pltpu.VMEM((tm, tn), F32)]
    return _call(body, name=name, out_shape=_sds((M, N), out_dtype), grid=grid, in_specs=[a_spec, b_spec],
                 out_specs=o_spec, scratch_shapes=scratch, dims=("parallel", "parallel", "arbitrary"))(a, b)


def _rstd(x):
    return lax.rsqrt(jnp.mean(x * x, axis=-1, keepdims=True) + RMS_EPS)


def _rms_fwd(x, g, out_dtype, name):
    S, D = x.shape
    tr = _tile(S, 256)

    def body(x_ref, g_ref, o_ref):
        xv = x_ref[...]
        o_ref[...] = (xv * _rstd(xv) * g_ref[...]).astype(o_ref.dtype)

    row = pl.BlockSpec((tr, D), lambda i: (i, 0))
    vec = pl.BlockSpec((1, D), lambda i: (0, 0))
    return _call(body, name=name, out_shape=_sds((S, D), out_dtype), grid=(S // tr,), in_specs=[row, vec],
                 out_specs=row, dims=("parallel",))(x, g)


def _res_rms_fwd(x, f, g, name):
    S, D = x.shape
    tr = _tile(S, 256)

    def body(x_ref, f_ref, g_ref, o_ref):
        fv = f_ref[...]
        o_ref[...] = x_ref[...] + fv * _rstd(fv) * g_ref[...]

    row = pl.BlockSpec((tr, D), lambda i: (i, 0))
    vec = pl.BlockSpec((1, D), lambda i: (0, 0))
    return _call(body, name=name, out_shape=_sds((S, D), F32), grid=(S // tr,), in_specs=[row, row, vec],
                 out_specs=row, dims=("parallel",))(x, f, g)


def _rms_bwd_math(xin, g, dy):
    r = _rstd(xin)
    xh = xin * r
    gy = dy * g
    dx = r * (gy - xh * jnp.mean(gy * xh, axis=-1, keepdims=True))
    return dx, dy * xh


def _rms_bwd(xin, g, dy, res, name):
    S, D = xin.shape
    tr = _tile(S, 256)
    has_res = res is not None

    def body(*refs):
        if has_res:
            x_ref, g_ref, dy_ref, res_ref, dx_ref, dg_ref = refs
        else:
            x_ref, g_ref, dy_ref, dx_ref, dg_ref = refs
        dx, dgr = _rms_bwd_math(x_ref[...], g_ref[...], dy_ref[...])
        dx_ref[...] = dx + res_ref[...] if has_res else dx
        i = pl.program_id(0)

        @pl.when(i == 0)
        def _():
            dg_ref[...] = jnp.zeros_like(dg_ref)

        dg_ref[...] += jnp.sum(dgr, axis=0, keepdims=True)

    row = pl.BlockSpec((tr, D), lambda i: (i, 0))
    vec = pl.BlockSpec((1, D), lambda i: (0, 0))
    ins = [xin, g, dy] + ([res] if has_res else [])
    in_specs = [row, vec, row] + ([row] if has_res else [])
    return _call(body, name=name, out_shape=(_sds((S, D), F32), _sds((1, D), F32)), grid=(S // tr,),
                 in_specs=in_specs, out_specs=(row, vec), dims=("arbitrary",))(*ins)


def _loss_grad(y, target, name):
    S, D = y.shape
    tr = _tile(S, 256)

    def body(y_ref, t_ref, dy_ref, acc_ref):
        e = y_ref[...] - t_ref[...]
        dy_ref[...] = e * (1.0 / D)
        i = pl.program_id(0)

        @pl.when(i == 0)
        def _():
            acc_ref[...] = jnp.zeros_like(acc_ref)

        acc_ref[...] += jnp.sum(e * e, axis=0, keepdims=True)

    row = pl.BlockSpec((tr, D), lambda i: (i, 0))
    vec = pl.BlockSpec((1, D), lambda i: (0, 0))
    return _call(body, name=name, out_shape=(_sds((S, D), F32), _sds((1, D), F32)), grid=(S // tr,),
                 in_specs=[row, row], out_specs=(row, vec), dims=("arbitrary",))(y, target)


def _pool_counts(t0, rows):
    return t0 + lax.broadcasted_iota(jnp.int32, (rows, 1), 0)


def _pool_fwd(x, g, name):
    S, D = x.shape
    gc = D // len(POOL_WINDOWS)
    tp = _tile(S, 256)

    def body(x_ref, g_ref, d_ref, ext_ref):
        i = pl.program_id(0)

        @pl.when(i == 0)
        def _():
            ext_ref[pl.ds(0, POOL_HALO), :] = jnp.zeros((POOL_HALO, D), F32)

        xv = x_ref[...]
        ext_ref[pl.ds(POOL_HALO, tp), :] = xv * _rstd(xv) * g_ref[...]
        t = _pool_counts(i * tp, tp)
        for gi, w in enumerate(POOL_WINDOWS):
            cols = slice(gi * gc, (gi + 1) * gc)
            s = ext_ref[:, cols]
            h = s[POOL_HALO:]
            sh = 1
            while sh < w:
                s = s + pltpu.roll(s, sh, 0)
                sh *= 2
            cnt = jnp.minimum(t + 1, w).astype(F32)
            d_ref[:, cols] = (s[POOL_HALO:] / cnt - h).astype(d_ref.dtype)
        ext_ref[pl.ds(0, POOL_HALO), :] = ext_ref[pl.ds(tp, POOL_HALO), :]

    row = pl.BlockSpec((tp, D), lambda i: (i, 0))
    vec = pl.BlockSpec((1, D), lambda i: (0, 0))
    return _call(body, name=name, out_shape=_sds((S, D), BF16), grid=(S // tp,), in_specs=[row, vec],
                 out_specs=row, scratch_shapes=[pltpu.VMEM((tp + POOL_HALO, D), F32)], dims=("arbitrary",))(x, g)


def _pool_mm_fwd(d, wp, l, scale, x, gpost, name):
    S, D = x.shape
    ng = len(POOL_WINDOWS)
    gc = D // ng
    tp = _tile(S, 256)

    def body(d_ref, w_ref, sc_ref, x_ref, g_ref, y_ref, o_ref):
        for gi in range(ng):
            cols = slice(gi * gc, (gi + 1) * gc)
            y_ref[:, cols] = jnp.dot(d_ref[:, cols], w_ref[gi], preferred_element_type=F32)
        m = y_ref[...] * sc_ref[...]
        o_ref[...] = x_ref[...] + m * _rstd(m) * g_ref[...]

    row = pl.BlockSpec((tp, D), lambda i: (i, 0))
    vec = pl.BlockSpec((1, D), lambda i: (0, 0))
    wsp = pl.BlockSpec((None, ng, gc, gc), lambda i: (l, 0, 0, 0))
    return _call(body, name=name, out_shape=(_sds((S, D), F32), _sds((S, D), F32)), grid=(S // tp,),
                 in_specs=[row, wsp, vec, row, vec], out_specs=(row, row), dims=("parallel",))(d, wp, scale, x, gpost)


def _pool_mm_bwd(dx, y, d, wp, l, scale, gpost, name):
    S, D = dx.shape
    ng = len(POOL_WINDOWS)
    gc = D // ng
    tp = _tile(S, 256)

    def body(dx_ref, y_ref, d_ref, w_ref, sc_ref, g_ref, dd_ref, dw_ref, dsc_ref, dg_ref):
        i = pl.program_id(0)

        @pl.when(i == 0)
        def _():
            dw_ref[...] = jnp.zeros_like(dw_ref)
            dsc_ref[...] = jnp.zeros_like(dsc_ref)
            dg_ref[...] = jnp.zeros_like(dg_ref)

        yv = y_ref[...]
        sc = sc_ref[...]
        dm, dgr = _rms_bwd_math(yv * sc, g_ref[...], dx_ref[...])
        dg_ref[...] += jnp.sum(dgr, axis=0, keepdims=True)
        dsc_ref[...] += jnp.sum(dm * yv, axis=0, keepdims=True)
        dyv = (dm * sc).astype(BF16)
        for gi in range(ng):
            cols = slice(gi * gc, (gi + 1) * gc)
            dyg = dyv[:, cols]
            dd_ref[:, cols] = lax.dot_general(dyg, w_ref[gi], (((1,), (1,)), ((), ())), preferred_element_type=F32)
            dw_ref[gi] += lax.dot_general(d_ref[:, cols], dyg, (((0,), (0,)), ((), ())), preferred_element_type=F32)

    row = pl.BlockSpec((tp, D), lambda i: (i, 0))
    vec = pl.BlockSpec((1, D), lambda i: (0, 0))
    wsp = pl.BlockSpec((None, ng, gc, gc), lambda i: (l, 0, 0, 0))
    dwsp = pl.BlockSpec((ng, gc, gc), lambda i: (0, 0, 0))
    return _call(body, name=name,
                 out_shape=(_sds((S, D), F32), _sds((ng, gc, gc), F32), _sds((1, D), F32), _sds((1, D), F32)),
                 grid=(S // tp,), in_specs=[row, row, row, wsp, vec, vec], out_specs=(row, dwsp, vec, vec),
                 dims=("arbitrary",))(dx, y, d, wp, scale, gpost)


def _pool_bwd(dd, x, g, res, name):
    S, D = x.shape
    gc = D // len(POOL_WINDOWS)
    tp = _tile(S, 256)
    nt = S // tp

    def body(dd_ref, x_ref, g_ref, res_ref, dx_ref, dg_ref, ext_ref, dh_ref):
        i = pl.program_id(0)

        @pl.when(i == 0)
        def _():
            ext_ref[pl.ds(tp, POOL_HALO), :] = jnp.zeros((POOL_HALO, D), F32)
            dg_ref[...] = jnp.zeros_like(dg_ref)

        t = _pool_counts((nt - 1 - i) * tp, tp)
        for gi, w in enumerate(POOL_WINDOWS):
            cols = slice(gi * gc, (gi + 1) * gc)
            ddv = dd_ref[:, cols]
            ext_ref[pl.ds(0, tp), cols] = ddv / jnp.minimum(t + 1, w).astype(F32)
            s = ext_ref[:, cols]
            sh = 1
            while sh < w:
                s = s + pltpu.roll(s, tp + POOL_HALO - sh, 0)
                sh *= 2
            dh_ref[:, cols] = s[:tp] - ddv
        ext_ref[pl.ds(tp, POOL_HALO), :] = ext_ref[pl.ds(0, POOL_HALO), :]
        dx, dgr = _rms_bwd_math(x_ref[...], g_ref[...], dh_ref[...])
        dx_ref[...] = dx + res_ref[...]
        dg_ref[...] += jnp.sum(dgr, axis=0, keepdims=True)

    row = pl.BlockSpec((tp, D), lambda i: (nt - 1 - i, 0))
    vec = pl.BlockSpec((1, D), lambda i: (0, 0))
    return _call(body, name=name, out_shape=(_sds((S, D), F32), _sds((1, D), F32)), grid=(nt,),
                 in_specs=[row, row, vec, row], out_specs=(row, vec),
                 scratch_shapes=[pltpu.VMEM((tp + POOL_HALO, D), F32), pltpu.VMEM((tp, D), F32)],
                 dims=("arbitrary",))(dd, x, g, res)


def _gelu(x):
    return 0.5 * x * (1.0 + jnp.tanh(0.7978845608028654 * (x + 0.044715 * x * x * x)))


def _gelu_grad(x):
    th = jnp.tanh(0.7978845608028654 * (x + 0.044715 * x * x * x))
    return 0.5 * (1.0 + th) + 0.5 * x * (1.0 - th * th) * 0.7978845608028654 * (1.0 + 3.0 * 0.044715 * x * x)


def _conv_taps(ext_ref, cols, tt):
    e = ext_ref[:, cols]
    return e[CONV_HALO:], pltpu.roll(e, 1, 0)[CONV_HALO:], pltpu.roll(e, 2, 0)[CONV_HALO:]


def _conv_glu_fwd(u, cw, cb, name):
    S, F2 = u.shape
    wd = F2 // 4
    tt = _tile(S, 256)

    def body(u_ref, cw_ref, cb_ref, a_ref, ext_ref):
        it = pl.program_id(1)

        @pl.when(it == 0)
        def _():
            ext_ref[pl.ds(0, CONV_HALO), :] = jnp.zeros((CONV_HALO, 2 * wd), F32)

        ext_ref[pl.ds(CONV_HALO, tt), :] = u_ref[...]
        for cc in range(wd // LANES):
            act = []
            for half in range(2):
                cols = slice(half * wd + cc * LANES, half * wd + (cc + 1) * LANES)
                u0, u1, u2 = _conv_taps(ext_ref, cols, tt)
                act.append(cw_ref[2:3, cols] * u0 + cw_ref[1:2, cols] * u1 + cw_ref[0:1, cols] * u2 + cb_ref[:, cols])
            a_ref[:, cc * LANES:(cc + 1) * LANES] = (_gelu(act[0]) * act[1]).astype(a_ref.dtype)
        ext_ref[pl.ds(0, CONV_HALO), :] = ext_ref[pl.ds(tt, CONV_HALO), :]

    return _call(body, name=name, out_shape=_sds((S, F2 // 2), BF16), grid=(2, S // tt),
                 in_specs=[pl.BlockSpec((tt, 2 * wd), lambda h, t: (t, h)), pl.BlockSpec((8, 2 * wd), lambda h, t: (0, h)),
                           pl.BlockSpec((1, 2 * wd), lambda h, t: (0, h))],
                 out_specs=pl.BlockSpec((tt, wd), lambda h, t: (t, h)),
                 scratch_shapes=[pltpu.VMEM((tt + CONV_HALO, 2 * wd), F32)], dims=("parallel", "arbitrary"))(u, cw, cb)


def _conv_glu_bwd1(u, da, cw, cb, name):
    S, F2 = u.shape
    wd = F2 // 4
    tt = _tile(S, 256)

    def body(u_ref, da_ref, cw_ref, cb_ref, duc_ref, acc_ref, ext_ref):
        it = pl.program_id(1)

        @pl.when(it == 0)
        def _():
            ext_ref[pl.ds(0, CONV_HALO), :] = jnp.zeros((CONV_HALO, 2 * wd), F32)
            acc_ref[...] = jnp.zeros_like(acc_ref)

        ext_ref[pl.ds(CONV_HALO, tt), :] = u_ref[...]
        for cc in range(wd // LANES):
            taps, act = [], []
            for half in range(2):
                cols = slice(half * wd + cc * LANES, half * wd + (cc + 1) * LANES)
                u0, u1, u2 = _conv_taps(ext_ref, cols, tt)
                taps.append((u0, u1, u2))
                act.append(cw_ref[2:3, cols] * u0 + cw_ref[1:2, cols] * u1 + cw_ref[0:1, cols] * u2 + cb_ref[:, cols])
            dav = da_ref[:, cc * LANES:(cc + 1) * LANES]
            dact = (dav * act[1] * _gelu_grad(act[0]), dav * _gelu(act[0]))
            for half in range(2):
                cols = slice(half * wd + cc * LANES, half * wd + (cc + 1) * LANES)
                duc_ref[:, cols] = dact[half]
                u0, u1, u2 = taps[half]
                acc_ref[2:3, cols] += jnp.sum(dact[half] * u0, axis=0, keepdims=True)
                acc_ref[1:2, cols] += jnp.sum(dact[half] * u1, axis=0, keepdims=True)
                acc_ref[0:1, cols] += jnp.sum(dact[half] * u2, axis=0, keepdims=True)
                acc_ref[3:4, cols] += jnp.sum(dact[half], axis=0, keepdims=True)
        ext_ref[pl.ds(0, CONV_HALO), :] = ext_ref[pl.ds(tt, CONV_HALO), :]

    wide = pl.BlockSpec((tt, 2 * wd), lambda h, t: (t, h))
    acc = pl.BlockSpec((8, 2 * wd), lambda h, t: (0, h))
    return _call(body, name=name, out_shape=(_sds((S, F2), F32), _sds((8, F2), F32)), grid=(2, S // tt),
                 in_specs=[wide, pl.BlockSpec((tt, wd), lambda h, t: (t, h)), acc, pl.BlockSpec((1, 2 * wd), lambda h, t: (0, h))],
                 out_specs=(wide, acc), scratch_shapes=[pltpu.VMEM((tt + CONV_HALO, 2 * wd), F32)],
                 dims=("parallel", "arbitrary"))(u, da, cw, cb)


def _conv_bwd2(duc, cw, name):
    S, F2 = duc.shape
    wd = F2 // 4
    tt = _tile(S, 256)
    nt = S // tt

    def body(duc_ref, cw_ref, du_ref, ext_ref):
        it = pl.program_id(1)

        @pl.when(it == 0)
        def _():
            ext_ref[pl.ds(tt, CONV_HALO), :] = jnp.zeros((CONV_HALO, 2 * wd), F32)

        ext_ref[pl.ds(0, tt), :] = duc_ref[...]
        for cc in range(2 * wd // LANES):
            cols = slice(cc * LANES, (cc + 1) * LANES)
            e = ext_ref[:, cols]
            n = tt + CONV_HALO
            du = (cw_ref[2:3, cols] * e[:tt] + cw_ref[1:2, cols] * pltpu.roll(e, n - 1, 0)[:tt]
                  + cw_ref[0:1, cols] * pltpu.roll(e, n - 2, 0)[:tt])
            du_ref[:, cols] = du.astype(du_ref.dtype)
        ext_ref[pl.ds(tt, CONV_HALO), :] = ext_ref[pl.ds(0, CONV_HALO), :]

    wide = pl.BlockSpec((tt, 2 * wd), lambda h, t: (nt - 1 - t, h))
    return _call(body, name=name, out_shape=_sds((S, F2), BF16), grid=(2, nt),
                 in_specs=[wide, pl.BlockSpec((8, 2 * wd), lambda h, t: (0, h))], out_specs=wide,
                 scratch_shapes=[pltpu.VMEM((tt + CONV_HALO, 2 * wd), F32)], dims=("parallel", "arbitrary"))(duc, cw)


def _rope_chunk(x, cosv, sinv):
    lane = lax.broadcasted_iota(jnp.int32, x.shape, 1)
    partner = jnp.where(lane % HEAD_DIM < HEAD_DIM // 2, pltpu.roll(x, LANES - HEAD_DIM // 2, 1),
                        pltpu.roll(x, HEAD_DIM // 2, 1))
    return x * cosv + partner * sinv


def _rope(x, width, cos_t, sin_t, name):
    S = x.shape[0]
    tr = _tile(S, 256)

    def body(x_ref, c_ref, s_ref, o_ref):
        for cc in range(width // LANES):
            cols = slice(cc * LANES, (cc + 1) * LANES)
            o_ref[:, cols] = _rope_chunk(x_ref[:, cols], c_ref[...], s_ref[...])

    row = pl.BlockSpec((tr, width), lambda i: (i, 0))
    tab = pl.BlockSpec((tr, LANES), lambda i: (i, 0))
    return _call(body, name=name, out_shape=_sds((S, width), F32), grid=(S // tr,), in_specs=[row, tab, tab],
                 out_specs=row, dims=("parallel",))(x, cos_t, sin_t)


def _attn_mask(n):
    row = lax.broadcasted_iota(jnp.int32, (BLOCK, 2 * BLOCK), 0)
    col = lax.broadcasted_iota(jnp.int32, (BLOCK, 2 * BLOCK), 1)
    rel = BLOCK + row - col
    return (rel >= 0) & (rel < WINDOW) & (n * BLOCK + col - BLOCK >= 0)


def _head_place(h, qpk):
    hk = h // qpk
    return h // 2, h % 2, hk // 2, hk % 2


def _attn_specs(S, D):
    nb = S // BLOCK
    kvw = KV_DIM // 2
    qsp = pl.BlockSpec((BLOCK, D), lambda n: (n, 0))
    prev = lambda n: jnp.maximum(n - 1, 0)
    kp = pl.BlockSpec((BLOCK, kvw), lambda n: (prev(n), 0))
    ko = pl.BlockSpec((BLOCK, kvw), lambda n: (n, 0))
    vp = pl.BlockSpec((BLOCK, kvw), lambda n: (prev(n), 1))
    vo = pl.BlockSpec((BLOCK, kvw), lambda n: (n, 1))
    stat = pl.BlockSpec((BLOCK, LANES), lambda n: (n, 0))
    smem = pl.BlockSpec(memory_space=pltpu.SMEM)
    return nb, kvw, qsp, kp, ko, vp, vo, stat, smem


def _attn_fwd(q, k, kv, sinks, name):
    S, D = q.shape
    nh = D // HEAD_DIM
    qpk = nh // N_KV_HEADS
    nb, kvw, qsp, kp, ko, vp, vo, stat, smem = _attn_specs(S, D)

    def body(q_ref, kp_ref, ko_ref, vp_ref, vo_ref, s_ref, o_ref, l_ref):
        n = pl.program_id(0)
        valid = _attn_mask(n)
        lane = lax.broadcasted_iota(jnp.int32, (BLOCK, LANES), 1)
        lacc = jnp.zeros((BLOCK, LANES), F32)
        pair = None
        for h in range(nh):
            qc, qpar, kc, kpar = _head_place(h, qpk)
            kcols = slice(kc * LANES, (kc + 1) * LANES)
            k2 = jnp.concatenate([kp_ref[:, kcols], ko_ref[:, kcols]], axis=0).astype(BF16)
            v2 = jnp.concatenate([vp_ref[:, kcols], vo_ref[:, kcols]], axis=0).astype(BF16)
            x = q_ref[:, qc * LANES:(qc + 1) * LANES] * ATTN_SCALE
            if qpar != kpar:
                x = pltpu.roll(x, HEAD_DIM, 1)
            keep = (lane >= kpar * HEAD_DIM) & (lane < (kpar + 1) * HEAD_DIM)
            qm = jnp.where(keep, x, 0.0).astype(BF16)
            s = lax.dot_general(qm, k2, (((1,), (1,)), ((), ())), preferred_element_type=F32)
            s = jnp.where(valid, s, NEG_INF)
            sink = s_ref[h]
            m = jnp.maximum(jnp.max(s, axis=1, keepdims=True), sink)
            p = jnp.exp(s - m)
            den = jnp.sum(p, axis=1, keepdims=True) + jnp.exp(sink - m)
            of = jnp.dot(p.astype(BF16), v2, preferred_element_type=F32) / den
            if qpar != kpar:
                of = pltpu.roll(of, HEAD_DIM, 1)
            lacc = jnp.where(lane == h, m + jnp.log(den), lacc)
            if qpar == 0:
                pair = of
            else:
                o_ref[:, qc * LANES:(qc + 1) * LANES] = jnp.where(lane < HEAD_DIM, pair, of).astype(o_ref.dtype)
        l_ref[...] = lacc

    return _call(body, name=name, out_shape=(_sds((S, D), BF16), _sds((S, LANES), F32)), grid=(nb,),
                 in_specs=[qsp, kp, ko, vp, vo, smem], out_specs=(qsp, stat), dims=("parallel",))(q, k, k, kv, kv, sinks)


def _attn_bwd(q, k, kv, do, lse, sinks, name):
    S, D = q.shape
    nh = D // HEAD_DIM
    qpk = nh // N_KV_HEADS
    nb, kvw, qsp, kp, ko, vp, vo, stat, smem = _attn_specs(S, D)

    def body(q_ref, kp_ref, ko_ref, vp_ref, vo_ref, do_ref, l_ref, s_ref,
             dq_ref, dkp_ref, dko_ref, dvp_ref, dvo_ref, ds_ref):
        n = pl.program_id(0)

        @pl.when(n == 0)
        def _():
            ds_ref[...] = jnp.zeros_like(ds_ref)

        valid = _attn_mask(n)
        lane = lax.broadcasted_iota(jnp.int32, (BLOCK, LANES), 1)
        lane8 = lax.broadcasted_iota(jnp.int32, (8, LANES), 1)
        lv = l_ref[...]
        dsink = jnp.zeros((8, LANES), F32)
        pair = None
        for kc in range(N_KV_HEADS // 2):
            kcols = slice(kc * LANES, (kc + 1) * LANES)
            k2 = jnp.concatenate([kp_ref[:, kcols], ko_ref[:, kcols]], axis=0).astype(BF16)
            v2 = jnp.concatenate([vp_ref[:, kcols], vo_ref[:, kcols]], axis=0).astype(BF16)
            dk2 = jnp.zeros((2 * BLOCK, LANES), F32)
            dv2 = jnp.zeros((2 * BLOCK, LANES), F32)
            for h in range(2 * kc * qpk, 2 * (kc + 1) * qpk):
                qc, qpar, _, kpar = _head_place(h, qpk)
                keep = (lane >= kpar * HEAD_DIM) & (lane < (kpar + 1) * HEAD_DIM)
                x = q_ref[:, qc * LANES:(qc + 1) * LANES] * ATTN_SCALE
                g = do_ref[:, qc * LANES:(qc + 1) * LANES]
                if qpar != kpar:
                    x = pltpu.roll(x, HEAD_DIM, 1)
                    g = pltpu.roll(g, HEAD_DIM, 1)
                qm = jnp.where(keep, x, 0.0).astype(BF16)
                gm = jnp.where(keep, g, 0.0).astype(BF16)
                s = lax.dot_general(qm, k2, (((1,), (1,)), ((), ())), preferred_element_type=F32)
                lh = jnp.sum(jnp.where(lane == h, lv, 0.0), axis=1, keepdims=True)
                p = jnp.where(valid, jnp.exp(s - lh), 0.0)
                dp = lax.dot_general(gm, v2, (((1,), (1,)), ((), ())), preferred_element_type=F32)
                delta = jnp.sum(p * dp, axis=1, keepdims=True)
                dsb = (p * (dp - delta)).astype(BF16)
                dsink = dsink - jnp.where(lane8 == h, jnp.sum(jnp.exp(s_ref[h] - lh) * delta), 0.0)
                dqf = jnp.dot(dsb, k2, preferred_element_type=F32) * ATTN_SCALE
                if qpar != kpar:
                    dqf = pltpu.roll(dqf, HEAD_DIM, 1)
                dk2 = dk2 + lax.dot_general(dsb, qm, (((0,), (0,)), ((), ())), preferred_element_type=F32)
                dv2 = dv2 + lax.dot_general(p.astype(BF16), gm, (((0,), (0,)), ((), ())), preferred_element_type=F32)
                if qpar == 0:
                    pair = dqf
                else:
                    dq_ref[:, qc * LANES:(qc + 1) * LANES] = jnp.where(lane < HEAD_DIM, pair, dqf)
            dkp_ref[:, kcols] = dk2[:BLOCK]
            dko_ref[:, kcols] = dk2[BLOCK:]
            dvp_ref[:, kcols] = dv2[:BLOCK]
            dvo_ref[:, kcols] = dv2[BLOCK:]
        ds_ref[...] += dsink

    kvo = pl.BlockSpec((BLOCK, kvw), lambda n: (n, 0))
    acc = pl.BlockSpec((8, LANES), lambda n: (0, 0))
    part = _sds((S, kvw), F32)
    return _call(body, name=name, out_shape=(_sds((S, D), F32), part, part, part, part, _sds((8, LANES), F32)),
                 grid=(nb,), in_specs=[qsp, kp, ko, vp, vo, qsp, stat, smem],
                 out_specs=(qsp, kvo, kvo, kvo, kvo, acc), dims=("arbitrary",))(q, k, k, kv, kv, do, lse, sinks)


def _kv_grad(parts, cos_t, sin_neg_t, name):
    S, kvw = parts[0][0].shape
    nb = S // BLOCK
    flat = [a for p in parts for a in p]
    nl = len(parts)

    def body(*refs):
        c_ref, s_ref, o_ref = refs[4 * nl], refs[4 * nl + 1], refs[4 * nl + 2]
        n = pl.program_id(0)
        last = n == nb - 1
        dk = jnp.zeros((BLOCK, kvw), F32)
        dv = jnp.zeros((BLOCK, kvw), F32)
        for li in range(nl):
            kn, kown, vn, vown = refs[4 * li:4 * li + 4]
            dk = dk + kown[...] + jnp.where(last, 0.0, kn[...])
            dv = dv + vown[...] + jnp.where(last, 0.0, vn[...])
        for cc in range(kvw // LANES):
            cols = slice(cc * LANES, (cc + 1) * LANES)
            o_ref[:, cols] = _rope_chunk(dk[:, cols], c_ref[...], s_ref[...])
        o_ref[:, kvw:] = dv

    own = pl.BlockSpec((BLOCK, kvw), lambda n: (n, 0))
    nxt = pl.BlockSpec((BLOCK, kvw), lambda n: (jnp.minimum(n + 1, nb - 1), 0))
    tab = pl.BlockSpec((BLOCK, LANES), lambda n: (n, 0))
    return _call(body, name=name, out_shape=_sds((S, 2 * kvw), F32), grid=(nb,),
                 in_specs=[nxt, own, nxt, own] * nl + [tab, tab],
                 out_specs=pl.BlockSpec((BLOCK, 2 * kvw), lambda n: (n, 0)), dims=("parallel",))(*flat, cos_t, sin_neg_t)


def _sum_blocks(name, qc, grid, out_shape, out_block, out_imap, ins):
    nin = len(ins)

    def body(qc_ref, *refs):
        acc = refs[0][...]
        for r in refs[1:nin]:
            acc = acc + r[...]
        refs[nin][...] = acc

    gs = pltpu.PrefetchScalarGridSpec(
        num_scalar_prefetch=1, grid=grid, in_specs=[pl.BlockSpec(b, m) for _, b, m in ins],
        out_specs=pl.BlockSpec(out_block, out_imap))
    return _call(body, name=name, out_shape=_sds(out_shape, F32), grid_spec=gs,
                 dims=("parallel",) * len(grid))(qc, *[a for a, _, _ in ins])


def _adamw(w, g, m, v, name):
    shape = w.shape
    C = shape[-1]
    R = w.size // C
    tr = _tile(R, max(8, (1 << 18) // C))

    def body(w_ref, g_ref, m_ref, v_ref, d_ref, nm_ref, nv_ref):
        gv = g_ref[...]
        nm = ADAM_B1 * m_ref[...] + (1.0 - ADAM_B1) * gv
        nv = ADAM_B2 * v_ref[...] + (1.0 - ADAM_B2) * (gv * gv)
        m_hat = nm / (1.0 - ADAM_B1 ** ADAM_STEP)
        v_hat = nv / (1.0 - ADAM_B2 ** ADAM_STEP)
        d_ref[...] = -ADAM_LR * (m_hat / (jnp.sqrt(v_hat) + ADAM_EPS) + ADAM_WD * w_ref[...])
        nm_ref[...] = nm
        nv_ref[...] = nv

    blk = pl.BlockSpec((tr, C), lambda i: (i, 0))
    flat = _sds((R, C), F32)
    outs = _call(body, name=name, out_shape=(flat, flat, flat), grid=(R // tr,), in_specs=[blk] * 4,
                 out_specs=(blk, blk, blk), dims=("parallel",))(*[a.reshape(R, C) for a in (w, g, m, v)])
    return tuple(o.reshape(shape) for o in outs)


def _place():
    x, y, c = lax.axis_index("x"), lax.axis_index("y"), lax.axis_index("c")
    chips = [(1 - x, y), (x, 1 - y), (1 - x, 1 - y)]
    return x, y, c, chips


def _at(ref, nd, dims):
    idx = [slice(None)] * nd
    for d, v in dims.items():
        idx[d] = pl.ds(v[0], v[1]) if isinstance(v, tuple) else v
    return ref.at[tuple(idx)]


def _remote(src, dst, send_sem, recv_sem, dev):
    return pltpu.make_async_remote_copy(src_ref=src, dst_ref=dst, send_sem=send_sem, recv_sem=recv_sem,
                                        device_id=dev, device_id_type=MESH)


def _gather_shards(shards, specs, name):
    n = len(shards)
    fulls = []
    for a, (sd, hd) in zip(shards, specs):
        shp = list(a.shape)
        shp[sd] *= N_SHARDS
        fulls.append(_sds(shp, a.dtype))

    def body(*refs):
        ins, outs = refs[:n], refs[n:2 * n]
        send_sems, recv_sems, loc_sems = refs[2 * n:]
        x, y, c, chips = _place()
        q = 2 * x + y
        me, sibling = (x, y, c), (x, y, 1 - c)
        locals_, sends = [], []
        for ai, (sd, hd) in enumerate(specs):
            shp = shards[ai].shape
            nd, ss, hs = len(shp), shp[sd], shp[hd] // 2
            loc = pltpu.make_async_copy(ins[ai], _at(outs[ai], nd, {sd: (q * ss, ss)}), loc_sems.at[ai])
            loc.start()
            locals_.append(loc)
            for j, ch in enumerate(chips):
                cp = _remote(_at(ins[ai], nd, {hd: (c * hs, hs)}),
                             _at(outs[ai], nd, {sd: (q * ss, ss), hd: (c * hs, hs)}),
                             send_sems.at[6 * ai + j], recv_sems.at[6 * ai + j], (ch[0], ch[1], c))
                cp.start()
                sends.append(cp)
        for ai, (sd, hd) in enumerate(specs):
            shp = shards[ai].shape
            nd, ss, hs = len(shp), shp[sd], shp[hd] // 2
            for j, ch in enumerate(chips):
                land = _at(outs[ai], nd, {sd: ((2 * ch[0] + ch[1]) * ss, ss), hd: (c * hs, hs)})
                _remote(land, land, send_sems.at[6 * ai + j], recv_sems.at[6 * ai + j], me).wait_recv()
                fw = _remote(land, land, send_sems.at[6 * ai + 3 + j], recv_sems.at[6 * ai + 3 + j], sibling)
                fw.start()
                sends.append(fw)
        for ai, (sd, hd) in enumerate(specs):
            shp = shards[ai].shape
            nd, ss, hs = len(shp), shp[sd], shp[hd] // 2
            for j, ch in enumerate(chips):
                land = _at(outs[ai], nd, {sd: ((2 * ch[0] + ch[1]) * ss, ss), hd: ((1 - c) * hs, hs)})
                _remote(land, land, send_sems.at[6 * ai + 3 + j], recv_sems.at[6 * ai + 3 + j], me).wait_recv()
        for cp in sends:
            cp.wait_send()
        for loc in locals_:
            loc.wait()

    return _call(body, name=name, out_shape=tuple(fulls), in_specs=[ANY] * n, out_specs=tuple([ANY] * n),
                 scratch_shapes=[pltpu.SemaphoreType.DMA((6 * n,)), pltpu.SemaphoreType.DMA((6 * n,)),
                                 pltpu.SemaphoreType.DMA((n,))])(*shards)


def _half_dims(shape, hd, c):
    hs = shape[hd] // 2
    return {hd: (c * hs, hs)}


def _swap_halves(grads, specs, name):
    n = len(grads)
    outs_shape = []
    for a, (sd, hd) in zip(grads, specs):
        shp = list(a.shape)
        shp[hd] //= 2
        outs_shape.append(_sds(shp, F32))

    def body(*refs):
        ins, outs = refs[:n], refs[n:2 * n]
        send_sems, recv_sems = refs[2 * n:]
        x, y, c, _ = _place()
        cps = []
        for ai, (sd, hd) in enumerate(specs):
            shp = grads[ai].shape
            cp = _remote(_at(ins[ai], len(shp), _half_dims(shp, hd, 1 - c)), outs[ai],
                         send_sems.at[ai], recv_sems.at[ai], (x, y, 1 - c))
            cp.start()
            cps.append(cp)
        for cp in cps:
            cp.wait()

    return _call(body, name=name, out_shape=tuple(outs_shape), in_specs=[ANY] * n, out_specs=tuple([ANY] * n),
                 scratch_shapes=[pltpu.SemaphoreType.DMA((n,)), pltpu.SemaphoreType.DMA((n,))])(*grads)


def _scatter_shards(sums, specs, name):
    n = len(sums)
    outs_shape = []
    for a, (sd, hd) in zip(sums, specs):
        shp = list(a.shape)
        shp[sd] //= N_SHARDS
        outs_shape.append(_sds([3] + shp, F32))

    def body(*refs):
        ins, outs = refs[:n], refs[n:2 * n]
        send_sems, recv_sems = refs[2 * n:]
        x, y, c, chips = _place()
        cps = []
        for ai, (sd, hd) in enumerate(specs):
            shp = sums[ai].shape
            ss = shp[sd] // N_SHARDS
            for j, ch in enumerate(chips):
                cp = _remote(_at(ins[ai], len(shp), {sd: ((2 * ch[0] + ch[1]) * ss, ss)}), outs[ai].at[j],
                             send_sems.at[3 * ai + j], recv_sems.at[3 * ai + j], (ch[0], ch[1], c))
                cp.start()
                cps.append(cp)
        for cp in cps:
            cp.wait()

    return _call(body, name=name, out_shape=tuple(outs_shape), in_specs=[ANY] * n, out_specs=tuple([ANY] * n),
                 scratch_shapes=[pltpu.SemaphoreType.DMA((3 * n,)), pltpu.SemaphoreType.DMA((3 * n,))])(*sums)


def _share_halves(totals, dests, name):
    n = len(totals)
    out_shapes = []
    for oi, oshape, _, _ in dests:
        if oi == len(out_shapes):
            out_shapes.append(_sds(oshape, F32))
    no = len(out_shapes)

    def body(*refs):
        ins, outs = refs[:n], refs[n:n + no]
        send_sems, recv_sems, loc_sems = refs[n + no:]
        x, y, c, _ = _place()
        def half(ai, which):
            oi, oshape, fixed, hd = dests[ai]
            dims = dict(fixed)
            hs = oshape[hd] // 2
            dims[hd] = (which * hs, hs)
            return _at(outs[oi], len(oshape), dims)

        locals_, sends = [], []
        for ai in range(n):
            loc = pltpu.make_async_copy(ins[ai], half(ai, c), loc_sems.at[ai])
            loc.start()
            locals_.append(loc)
            cp = _remote(ins[ai], half(ai, c), send_sems.at[ai], recv_sems.at[ai], (x, y, 1 - c))
            cp.start()
            sends.append(cp)
        for ai in range(n):
            land = half(ai, 1 - c)
            _remote(land, land, send_sems.at[ai], recv_sems.at[ai], (x, y, c)).wait_recv()
        for cp in sends:
            cp.wait_send()
        for loc in locals_:
            loc.wait()

    return _call(body, name=name, out_shape=tuple(out_shapes), in_specs=[ANY] * n, out_specs=tuple([ANY] * no),
                 scratch_shapes=[pltpu.SemaphoreType.DMA((n,)), pltpu.SemaphoreType.DMA((n,)),
                                 pltpu.SemaphoreType.DMA((n,))])(*totals)


def _gather_small(v, name):
    R, C = v.shape

    def body(x_ref, out_ref, send_sems, recv_sems, local_sem):
        x, y, c, chips = _place()
        me, sibling = (x, y, c), (x, y, 1 - c)

        def rows(px, py, pc):
            return out_ref.at[pl.ds((4 * px + 2 * py + pc) * R, R), :]

        def copy(k, block, to, src=None):
            return _remote(rows(*block) if src is None else src, rows(*block), send_sems.at[k], recv_sems.at[k], to)

        mine = pltpu.make_async_copy(x_ref, rows(*me), local_sem)
        mine.start()
        first = [copy(0, me, sibling, src=x_ref)]
        first += [copy(1 + j, me, (ch[0], ch[1], c), src=x_ref) for j, ch in enumerate(chips)]
        for cp in first:
            cp.start()
        passed = [copy(4 + j, (ch[0], ch[1], c), sibling) for j, ch in enumerate(chips)]
        for j, ch in enumerate(chips):
            copy(1 + j, (ch[0], ch[1], c), me).wait_recv()
            passed[j].start()
        copy(0, sibling, me).wait_recv()
        for j, ch in enumerate(chips):
            copy(4 + j, (ch[0], ch[1], 1 - c), me).wait_recv()
        for cp in first + passed:
            cp.wait_send()
        mine.wait()

    vm = pl.BlockSpec(memory_space=pltpu.VMEM)
    return _call(body, name=name, out_shape=_sds((8 * R, C), v.dtype), in_specs=[vm], out_specs=vm,
                 scratch_shapes=[pltpu.SemaphoreType.DMA((7,)), pltpu.SemaphoreType.DMA((7,)),
                                 pltpu.SemaphoreType.DMA])(v)


def _sum8(g, name):
    _, R, C = g.shape

    def body(g_ref, o_ref):
        acc = g_ref[0]
        for d in range(1, 8):
            acc = acc + g_ref[d]
        o_ref[...] = acc

    return _call(body, name=name, out_shape=_sds((R, C), F32), in_specs=[pl.BlockSpec(memory_space=pltpu.VMEM)],
                 out_specs=pl.BlockSpec(memory_space=pltpu.VMEM))(g)


def _rope_tables(positions):
    inv_freq = 1.0 / (ROPE_THETA ** (jnp.arange(0, HEAD_DIM, 2, dtype=F32) / HEAD_DIM))
    ang = positions.astype(F32)[:, None] * inv_freq
    cosv, sinv = jnp.cos(ang), jnp.sin(ang)
    return jnp.tile(cosv, (1, 4)), jnp.tile(jnp.concatenate([-sinv, sinv], axis=1), (1, 2))


def _blocked(a):
    parts = jnp.split(a, 4, axis=-1)
    return jnp.concatenate([parts[0], parts[2], parts[1], parts[3]], axis=-1)


def _local_step(x, target, positions, W):
    S, D = x.shape
    depth = W['mix_pre_g'].shape[0]
    n_a = depth // 2
    cos_t, sin_t = _rope_tables(positions)
    row = lambda a, l: a[l][None]
    cw = [jnp.pad(_blocked(W['ffn_conv_w'][l]), ((0, 5), (0, 0))) for l in range(depth)]
    cb = [_blocked(W['ffn_conv_b'][l])[None] for l in range(depth)]
    sv = {}
    kv = k_rot = None
    for l in range(depth):
        t = f"l{l}"
        sv[l, 'x_in'] = x
        if l < n_a:
            d = _pool_fwd(x, row(W['mix_pre_g'], l), "pool_fwd_" + t)
            y, x = _pool_mm_fwd(d, W['pool_w'], l, row(W['pool_scale'], l), x, row(W['mix_post_g'], l), "pool_mm_fwd_" + t)
            sv[l, 'd'], sv[l, 'y'] = d, y
        else:
            j = l - n_a
            if l == n_a:
                hkv = _rms_fwd(x, W['kv_norm_g'][None], BF16, "kv_norm")
                kv = _matmul(hkv, W['w_kv'], 'nn', F32, "kv_proj", 512, 512, 1024)
                k_rot = _rope(kv, KV_DIM // 2, cos_t, sin_t, "k_rope")
                sv['hkv'] = hkv
            h = _rms_fwd(x, row(W['mix_pre_g'], l), BF16, "q_norm_" + t)
            qraw = _matmul(h, W['w_q'], 'nn', F32, "q_proj_" + t, 512, 1024, 1024, lb=j)
            q = _rope(qraw, D, cos_t, sin_t, "q_rope_" + t)
            o, lse = _attn_fwd(q, k_rot, kv, W['sinks'][j], "attn_fwd_" + t)
            m = _matmul(o, W['w_o'], 'nn', F32, "o_proj_" + t, 512, 1024, 1024, lb=j)
            x = _res_rms_fwd(x, m, row(W['mix_post_g'], l), "mix_post_" + t)
            sv[l, 'h'], sv[l, 'q'], sv[l, 'o'], sv[l, 'lse'], sv[l, 'm'] = h, q, o, lse, m
        sv[l, 'x1'] = x
        h2 = _rms_fwd(x, row(W['ffn_pre_g'], l), BF16, "ffn_norm_" + t)
        u = _matmul(h2, W['ffn_w_in'], 'nn', F32, "ffn_up_" + t, 512, W['ffn_w_in'].shape[2] // 4, 1024, lb=l, b_perm=True)
        a = _conv_glu_fwd(u, cw[l], cb[l], "ffn_glu_" + t)
        f = _matmul(a, W['ffn_w_out'], 'nn', F32, "ffn_down_" + t, 512, 1024, 2816, lb=l)
        x = _res_rms_fwd(x, f, row(W['ffn_post_g'], l), "ffn_post_" + t)
        sv[l, 'h2'], sv[l, 'u'], sv[l, 'a'], sv[l, 'f'] = h2, u, a, f

    dx, sq = _loss_grad(x, target, "loss")
    G = {}
    kv_parts = []
    for l in reversed(range(depth)):
        t = f"l{l}"
        wd = W['ffn_w_in'].shape[2] // 4
        df, G['ffn_post_g', l] = _rms_bwd(sv[l, 'f'], row(W['ffn_post_g'], l), dx, None, "ffn_post_bwd_" + t)
        da = _matmul(df, W['ffn_w_out'], 'nt', F32, "ffn_down_dx_" + t, 512, wd, 1024, lb=l)
        G['ffn_w_out', l] = _matmul(sv[l, 'a'], df, 'tn', F32, "ffn_down_dw_" + t, wd, 1024, 512)
        duc, acc = _conv_glu_bwd1(sv[l, 'u'], da, cw[l], cb[l], "ffn_glu_bwd_" + t)
        G['ffn_conv_w', l] = _blocked(acc[0:3])
        G['ffn_conv_b', l] = _blocked(acc[3:4])
        du = _conv_bwd2(duc, cw[l], "ffn_conv_bwd_" + t)
        dh2 = _matmul(du, W['ffn_w_in'], 'nt', F32, "ffn_up_dx_" + t, 512, 1024, wd, lb=l, b_perm=True)
        G['ffn_w_in', l] = _matmul(sv[l, 'h2'], du, 'tn', F32, "ffn_up_dw_" + t, 1024, wd, 512, out_perm=True)
        dx, G['ffn_pre_g', l] = _rms_bwd(sv[l, 'x1'], row(W['ffn_pre_g'], l), dh2, dx, "ffn_norm_bwd_" + t)
        if l < n_a:
            dd, G['pool_w', l], G['pool_scale', l], G['mix_post_g', l] = _pool_mm_bwd(
                dx, sv[l, 'y'], sv[l, 'd'], W['pool_w'], l, row(W['pool_scale'], l), row(W['mix_post_g'], l), "pool_mm_bwd_" + t)
            dx, G['mix_pre_g', l] = _pool_bwd(dd, sv[l, 'x_in'], row(W['mix_pre_g'], l), dx, "pool_bwd_" + t)
        else:
            j = l - n_a
            dm, G['mix_post_g', l] = _rms_bwd(sv[l, 'm'], row(W['mix_post_g'], l), dx, None, "mix_post_bwd_" + t)
            do = _matmul(dm, W['w_o'], 'nt', F32, "o_proj_dx_" + t, 512, 1024, 1024, lb=j)
            G['w_o', j] = _matmul(sv[l, 'o'], dm, 'tn', F32, "o_proj_dw_" + t, 1024, 1024, 512)
            dq, dkn, dko, dvn, dvo, dsk = _attn_bwd(sv[l, 'q'], k_rot, kv, do, sv[l, 'lse'], W['sinks'][j], "attn_bwd_" + t)
            G['sinks', j] = dsk[0:1]
            kv_parts.append((dkn, dko, dvn, dvo))
            dqraw = _rope(dq, D, cos_t, -sin_t, "q_rope_bwd_" + t)
            dh = _matmul(dqraw, W['w_q'], 'nt', F32, "q_proj_dx_" + t, 512, 1024, 1024, lb=j)
            G['w_q', j] = _matmul(sv[l, 'h'], dqraw, 'tn', F32, "q_proj_dw_" + t, 1024, 1024, 512)
            dx, G['mix_pre_g', l] = _rms_bwd(sv[l, 'x_in'], row(W['mix_pre_g'], l), dh, dx, "q_norm_bwd_" + t)
            if l == n_a:
                dkv = _kv_grad(kv_parts, cos_t, -sin_t, "kv_grad")
                dhkv = _matmul(dkv, W['w_kv'], 'nt', F32, "kv_proj_dx", 512, 1024, 512)
                G['w_kv'] = _matmul(sv['hkv'], dkv, 'tn', F32, "kv_proj_dw", 1024, 512, 512)
                dx, G['kv_norm_g'] = _rms_bwd(sv[l, 'x_in'], W['kv_norm_g'][None], dhkv, dx, "kv_norm_bwd")
    return sq, dx, G


GATHERED = [('pool_w', 2, 0, True), ('w_kv', 0, 1, True), ('w_q', 1, 0, True), ('w_o', 1, 0, True),
            ('ffn_w_in', 2, 0, True), ('ffn_w_out', 1, 0, True), ('ffn_conv_w', 2, 0, False), ('pool_scale', 1, 0, False)]
SMALL = ['mix_pre_g', 'mix_post_g', 'kv_norm_g', 'sinks', 'ffn_pre_g', 'ffn_post_g', 'ffn_conv_b', 'ffn_conv_w', 'pool_scale']


def _reduce_big(G, depth, qc):
    n_a = depth // 2
    pieces = []
    for l in range(depth):
        pieces.append((G['ffn_w_in', l], 1, 0, 'ffn_w_in', {0: l}, 1))
        pieces.append((G['ffn_w_out', l], 0, 1, 'ffn_w_out', {0: l}, 2))
    for j in range(depth - n_a):
        pieces.append((G['w_q', j], 0, 1, 'w_q', {0: j}, 2))
        pieces.append((G['w_o', j], 0, 1, 'w_o', {0: j}, 2))
    pieces.append((G['w_kv'], 0, 1, 'w_kv', {}, 1))
    for l in range(n_a):
        pieces.append((G['pool_w', l], 1, 0, 'pool_w', {0: l}, 1))
    arrs = [p[0] for p in pieces]
    specs = [(p[1], p[2]) for p in pieces]

    theirs = _swap_halves(arrs, specs, "grad_swap_halves")
    sums = []
    for pi, (a, (sd, hd), r) in enumerate(zip(arrs, specs, theirs)):
        shp = r.shape
        nd = len(shp)
        if nd == 3:
            blk, grid = tuple(shp), (1,)
            mine = lambda i, s: (s[1], 0, 0)
            zero = lambda i, s: (0, 0, 0)
        elif hd == 0:
            tr = _tile(shp[0], max(8, (1 << 18) // shp[1]))
            blk, grid = (tr, shp[1]), (shp[0] // tr,)
            nblk = shp[0] // tr
            mine = lambda i, s, nblk=nblk: (s[1] * nblk + i, 0)
            zero = lambda i, s: (i, 0)
        else:
            tr = _tile(shp[0], max(8, (1 << 18) // shp[1]))
            blk, grid = (tr, shp[1]), (shp[0] // tr,)
            mine = lambda i, s: (i, s[1])
            zero = lambda i, s: (i, 0)
        sums.append(_sum_blocks(f"grad_chip_sum_{pi}", qc, grid, shp, blk, zero, [(a, blk, mine), (r, blk, zero)]))

    recvd = _scatter_shards(sums, specs, "grad_scatter")
    totals = []
    for pi, (s_arr, (sd, hd), r) in enumerate(zip(sums, specs, recvd)):
        shp = r.shape[1:]
        nd = len(shp)
        if nd == 3:
            blk, grid = tuple(shp), (1,)
            mine = lambda i, s: (0, s[0], 0)
            zero = lambda i, s: (0, 0, 0)
            rk = [lambda i, s, k=k: (k, 0, 0, 0) for k in range(3)]
        elif sd == 1:
            tr = _tile(shp[0], max(8, (1 << 18) // shp[1]))
            blk, grid = (tr, shp[1]), (shp[0] // tr,)
            mine = lambda i, s: (i, s[0])
            zero = lambda i, s: (i, 0)
            rk = [lambda i, s, k=k: (k, i, 0) for k in range(3)]
        else:
            tr = _tile(shp[0], max(8, (1 << 18) // shp[1]))
            blk, grid = (tr, shp[1]), (shp[0] // tr,)
            nblk = shp[0] // tr
            mine = lambda i, s, nblk=nblk: (s[0] * nblk + i, 0)
            zero = lambda i, s: (i, 0)
            rk = [lambda i, s, k=k: (k, i, 0) for k in range(3)]
        ins = [(s_arr, blk, mine)] + [(r, (None,) + blk, rk[k]) for k in range(3)]
        totals.append(_sum_blocks(f"grad_total_{pi}", qc, grid, shp, blk, zero, ins))

    names, dests = [], []
    for (a, sd, hd, oname, fixed, ohd), tot in zip(pieces, totals):
        if oname not in names:
            names.append(oname)
        piece_shard = list(a.shape)
        piece_shard[sd] //= N_SHARDS
        if fixed:
            n_stack = sum(1 for p in pieces if p[3] == oname)
            oshape = [n_stack] + piece_shard
        else:
            oshape = piece_shard
        dests.append((names.index(oname), tuple(oshape), fixed, ohd))
    outs = _share_halves(totals, dests, "grad_share_halves")
    return dict(zip(names, outs))


def _pack_small(parts):
    rows, offs, r = [], [], 0
    for a in parts:
        flat = a.reshape(-1)
        nr = -(-flat.size // LANES)
        rows.append(jnp.pad(flat, (0, nr * LANES - flat.size)).reshape(nr, LANES))
        offs.append((r, nr, a.shape))
        r += nr
    pad = (-r) % 8
    if pad:
        rows.append(jnp.zeros((pad, LANES), F32))
    return jnp.concatenate(rows, axis=0), offs


def _unpack_small(packed, offs):
    return [packed[r:r + nr].reshape(-1)[:math.prod(shape)].reshape(shape) for r, nr, shape in offs]


def kernel(x, positions, mix_pre_g, mix_post_g, pool_w, pool_scale, kv_norm_g, w_kv, w_q, w_o, sinks, ffn_pre_g, ffn_post_g, ffn_w_in, ffn_conv_w, ffn_conv_b, ffn_w_out, loss_target, m_mix_pre_g, m_mix_post_g, m_pool_w, m_pool_scale, m_kv_norm_g, m_w_kv, m_w_q, m_w_o, m_sinks, m_ffn_pre_g, m_ffn_post_g, m_ffn_w_in, m_ffn_conv_w, m_ffn_conv_b, m_ffn_w_out, v_mix_pre_g, v_mix_post_g, v_pool_w, v_pool_scale, v_kv_norm_g, v_w_kv, v_w_q, v_w_o, v_sinks, v_ffn_pre_g, v_ffn_post_g, v_ffn_w_in, v_ffn_conv_w, v_ffn_conv_b, v_ffn_w_out):
    w = dict(mix_pre_g=mix_pre_g, mix_post_g=mix_post_g, pool_w=pool_w, pool_scale=pool_scale, kv_norm_g=kv_norm_g,
             w_kv=w_kv, w_q=w_q, w_o=w_o, sinks=sinks, ffn_pre_g=ffn_pre_g, ffn_post_g=ffn_post_g, ffn_w_in=ffn_w_in,
             ffn_conv_w=ffn_conv_w, ffn_conv_b=ffn_conv_b, ffn_w_out=ffn_w_out)
    mom = dict(mix_pre_g=m_mix_pre_g, mix_post_g=m_mix_post_g, pool_w=m_pool_w, pool_scale=m_pool_scale,
               kv_norm_g=m_kv_norm_g, w_kv=m_w_kv, w_q=m_w_q, w_o=m_w_o, sinks=m_sinks, ffn_pre_g=m_ffn_pre_g,
               ffn_post_g=m_ffn_post_g, ffn_w_in=m_ffn_w_in, ffn_conv_w=m_ffn_conv_w, ffn_conv_b=m_ffn_conv_b,
               ffn_w_out=m_ffn_w_out)
    var = dict(mix_pre_g=v_mix_pre_g, mix_post_g=v_mix_post_g, pool_w=v_pool_w, pool_scale=v_pool_scale,
               kv_norm_g=v_kv_norm_g, w_kv=v_w_kv, w_q=v_w_q, w_o=v_w_o, sinks=v_sinks, ffn_pre_g=v_ffn_pre_g,
               ffn_post_g=v_ffn_post_g, ffn_w_in=v_ffn_w_in, ffn_conv_w=v_ffn_conv_w, ffn_conv_b=v_ffn_conv_b,
               ffn_w_out=v_ffn_w_out)
    depth = mix_pre_g.shape[0]
    q_chip = 2 * lax.axis_index("x") + lax.axis_index("y")
    qc = jnp.stack([q_chip, lax.axis_index("c")]).astype(jnp.int32)

    shards = [w[n].astype(BF16) if as_bf16 else w[n] for n, _, _, as_bf16 in GATHERED]
    fulls = _gather_shards(shards, [(sd, hd) for _, sd, hd, _ in GATHERED], "weight_gather")
    W = dict(w)
    for (n, _, _, _), full in zip(GATHERED, fulls):
        W[n] = full

    sq, dx, G = _local_step(x[0], loss_target[0], positions[0], W)
    loss = 0.5 / x.shape[-1] * lax.psum(jnp.sum(sq), ("x", "y", "c"))

    big = _reduce_big(G, depth, qc)

    n_a = depth // 2
    small_local = {
        'mix_pre_g': jnp.concatenate([G['mix_pre_g', l] for l in range(depth)], axis=0),
        'mix_post_g': jnp.concatenate([G['mix_post_g', l] for l in range(depth)], axis=0),
        'kv_norm_g': G['kv_norm_g'][0],
        'sinks': jnp.concatenate([G['sinks', j][:, :sinks.shape[1]] for j in range(depth - n_a)], axis=0),
        'ffn_pre_g': jnp.concatenate([G['ffn_pre_g', l] for l in range(depth)], axis=0),
        'ffn_post_g': jnp.concatenate([G['ffn_post_g', l] for l in range(depth)], axis=0),
        'ffn_conv_b': jnp.concatenate([G['ffn_conv_b', l] for l in range(depth)], axis=0),
        'ffn_conv_w': jnp.stack([G['ffn_conv_w', l] for l in range(depth)], axis=0),
        'pool_scale': jnp.concatenate([G['pool_scale', l] for l in range(n_a)], axis=0),
    }
    packed, offs = _pack_small([small_local[n] for n in SMALL])
    gathered = _gather_small(packed, "small_grad_gather")
    summed = _sum8(gathered.reshape(8, packed.shape[0], LANES), "small_grad_sum")
    grads = dict(zip(SMALL, _unpack_small(summed, offs)))
    wd = ffn_conv_w.shape[2]
    grads['ffn_conv_w'] = lax.dynamic_slice_in_dim(grads['ffn_conv_w'], q_chip * wd, wd, axis=2)
    ps = pool_scale.shape[1]
    grads['pool_scale'] = lax.dynamic_slice_in_dim(grads['pool_scale'], q_chip * ps, ps, axis=1)
    grads.update(big)

    w_small, o_w = _pack_small([w[n] for n in SMALL])
    g_small, _ = _pack_small([grads[n] for n in SMALL])
    m_small, _ = _pack_small([mom[n] for n in SMALL])
    v_small, _ = _pack_small([var[n] for n in SMALL])
    upd = _adamw(w_small, g_small, m_small, v_small, "adamw_small")
    delta, new_m, new_v = ({n: a for n, a in zip(SMALL, _unpack_small(u, o_w))} for u in upd)
    for n in WEIGHTS:
        if n not in SMALL:
            delta[n], new_m[n], new_v[n] = _adamw(w[n], grads[n], mom[n], var[n], "adamw_" + n)

    return (loss, dx[None], *[grads[n] for n in WEIGHTS], *[delta[n] for n in WEIGHTS],
            *[new_m[n] for n in WEIGHTS], *[new_v[n] for n in WEIGHTS])
```

```python
import functools
import math

import jax
import jax.numpy as jnp
from jax import lax
from jax.experimental import pallas as pl
from jax.experimental.pallas import tpu as pltpu

F32 = jnp.float32
BF16 = jnp.bfloat16
MESH = pl.DeviceIdType.MESH
ANY = pl.BlockSpec(memory_space=pl.ANY)

HEAD_DIM = 64
N_KV_HEADS = 4
KV_DIM = 2 * N_KV_HEADS * HEAD_DIM
WINDOW = 128
BLOCK = 128
POOL_WINDOWS = (2, 4, 8, 16)
POOL_HALO = 16
CONV_HALO = 8
ROPE_THETA = 10000.0
ATTN_SCALE = 1.0 / math.sqrt(HEAD_DIM)
NEG_INF = -1e30
RMS_EPS = 1e-6
ADAM_LR, ADAM_B1, ADAM_B2, ADAM_EPS, ADAM_WD, ADAM_STEP = 0.001, 0.9, 0.999, 1e-08, 0.01, 10
N_SHARDS = 4
LANES = 128
VMEM_LIMIT_BYTES = 48 << 20

WEIGHTS = ['mix_pre_g', 'mix_post_g', 'pool_w', 'pool_scale', 'kv_norm_g', 'w_kv', 'w_q', 'w_o', 'sinks',
           'ffn_pre_g', 'ffn_post_g', 'ffn_w_in', 'ffn_conv_w', 'ffn_conv_b', 'ffn_w_out']


def _call(body, *, name, out_shape, grid=None, in_specs=None, out_specs=None, scratch_shapes=(), dims=None,
          grid_spec=None, aliases=None):
    params = pltpu.CompilerParams(dimension_semantics=dims, vmem_limit_bytes=VMEM_LIMIT_BYTES)
    kw = {} if aliases is None else dict(input_output_aliases=aliases)
    if grid_spec is not None:
        return pl.pallas_call(body, name=name, out_shape=out_shape, grid_spec=grid_spec, compiler_params=params, **kw)
    if grid is not None:
        kw['grid'] = grid
    return pl.pallas_call(body, name=name, out_shape=out_shape, in_specs=in_specs, out_specs=out_specs,
                          scratch_shapes=list(scratch_shapes), compiler_params=params, **kw)


def _tile(n, pref, mult=8):
    if n <= pref:
        return n
    for t in range(pref, 0, -1):
        if n % t == 0 and t % mult == 0:
            return t
    raise ValueError((n, pref, mult))


def _sds(shape, dtype):
    return jax.ShapeDtypeStruct(tuple(shape), dtype)


def _perm4(j):
    return (j % 2) * 2 + j // 2


def _matmul(a, b, mode, out_dtype, name, tm, tn, tk, la=None, lb=None, b_perm=False, out_perm=False):
    a2, b2 = a.shape[-2:], b.shape[-2:]
    if mode == 'nn':
        (M, K), (K2, N) = a2, b2
    elif mode == 'nt':
        (M, K), (N, K2) = a2, b2
    else:
        (K, M), (K2, N) = a2, b2
    assert K == K2, (name, a.shape, b.shape)
    tm, tn, tk = _tile(M, tm), _tile(N, tn, LANES), _tile(K, tk, LANES if mode != 'tn' else 16)
    assert M % tm == 0 and N % tn == 0 and K % tk == 0
    nk = K // tk
    grid = (N // tn, M // tm, nk)

    def spec(l, blk, imap):
        if l is None:
            return pl.BlockSpec(blk, imap)
        return pl.BlockSpec((None,) + blk, lambda j, i, k: (l,) + imap(j, i, k))

    pj = _perm4 if b_perm else (lambda j: j)
    if mode == 'nn':
        a_spec = spec(la, (tm, tk), lambda j, i, k: (i, k))
        b_spec = spec(lb, (tk, tn), lambda j, i, k: (k, pj(j)))
        dn = (((1,), (0,)), ((), ()))
    elif mode == 'nt':
        a_spec = spec(la, (tm, tk), lambda j, i, k: (i, k))
        b_spec = spec(lb, (tn, tk), lambda j, i, k: (j, pj(k)))
        dn = (((1,), (1,)), ((), ()))
    else:
        a_spec = spec(la, (tk, tm), lambda j, i, k: (k, i))
        b_spec = spec(lb, (tk, tn), lambda j, i, k: (k, j))
        dn = (((0,), (0,)), ((), ()))
    po = _perm4 if out_perm else (lambda j: j)
    o_spec = pl.BlockSpec((tm, tn), lambda j, i, k: (i, po(j)))

    def body(a_ref, b_ref, o_ref, *acc):
        prod = lax.dot_general(a_ref[...].astype(BF16), b_ref[...].astype(BF16), dn, preferred_element_type=F32)
        if nk == 1:
            o_ref[...] = prod.astype(o_ref.dtype)
        else:
            k = pl.program_id(2)

            @pl.when(k == 0)
            def _():
                acc[0][...] = prod

            @pl.when(k > 0)
            def _():
                acc[0][...] += prod

            @pl.when(k == nk - 1)
            def _():
                o_ref[...] = acc[0][...].astype(o_ref.dtype)

    scratch = [] if nk == 1 else [pltpu.VMEM((tm, tn), F32)]
    return _call(body, name=name, out_shape=_sds((M, N), out_dtype), grid=grid, in_specs=[a_spec, b_spec],
                 out_specs=o_spec, scratch_shapes=scratch, dims=("parallel", "parallel", "arbitrary"))(a, b)


def _rstd(x):
    return lax.rsqrt(jnp.mean(x * x, axis=-1, keepdims=True) + RMS_EPS)


def _rms_fwd(x, g, out_dtype, name):
    S, D = x.shape
    tr = _tile(S, 256)

    def body(x_ref, g_ref, o_ref):
        xv = x_ref[...]
        o_ref[...] = (xv * _rstd(xv) * g_ref[...]).astype(o_ref.dtype)

    row = pl.BlockSpec((tr, D), lambda i: (i, 0))
    vec = pl.BlockSpec((1, D), lambda i: (0, 0))
    return _call(body, name=name, out_shape=_sds((S, D), out_dtype), grid=(S // tr,), in_specs=[row, vec],
                 out_specs=row, dims=("parallel",))(x, g)


def _res_rms_fwd(x, f, g, name):
    S, D = x.shape
    tr = _tile(S, 256)

    def body(x_ref, f_ref, g_ref, o_ref):
        fv = f_ref[...]
        o_ref[...] = x_ref[...] + fv * _rstd(fv) * g_ref[...]

    row = pl.BlockSpec((tr, D), lambda i: (i, 0))
    vec = pl.BlockSpec((1, D), lambda i: (0, 0))
    return _call(body, name=name, out_shape=_sds((S, D), F32), grid=(S // tr,), in_specs=[row, row, vec],
                 out_specs=row, dims=("parallel",))(x, f, g)


def _rms_bwd_math(xin, g, dy):
    r = _rstd(xin)
    xh = xin * r
    gy = dy * g
    dx = r * (gy - xh * jnp.mean(gy * xh, axis=-1, keepdims=True))
    return dx, dy * xh


def _rms_bwd(xin, g, dy, res, name):
    S, D = xin.shape
    tr = _tile(S, 256)
    has_res = res is not None

    def body(*refs):
        if has_res:
            x_ref, g_ref, dy_ref, res_ref, dx_ref, dg_ref = refs
        else:
            x_ref, g_ref, dy_ref, dx_ref, dg_ref = refs
        dx, dgr = _rms_bwd_math(x_ref[...], g_ref[...], dy_ref[...])
        dx_ref[...] = dx + res_ref[...] if has_res else dx
        i = pl.program_id(0)

        @pl.when(i == 0)
        def _():
            dg_ref[...] = jnp.zeros_like(dg_ref)

        dg_ref[...] += jnp.sum(dgr, axis=0, keepdims=True)

    row = pl.BlockSpec((tr, D), lambda i: (i, 0))
    vec = pl.BlockSpec((1, D), lambda i: (0, 0))
    ins = [xin, g, dy] + ([res] if has_res else [])
    in_specs = [row, vec, row] + ([row] if has_res else [])
    return _call(body, name=name, out_shape=(_sds((S, D), F32), _sds((1, D), F32)), grid=(S // tr,),
                 in_specs=in_specs, out_specs=(row, vec), dims=("arbitrary",))(*ins)


def _loss_grad(y, target, name):
    S, D = y.shape
    tr = _tile(S, 256)

    def body(y_ref, t_ref, dy_ref, acc_ref):
        e = y_ref[...] - t_ref[...]
        dy_ref[...] = e * (1.0 / D)
        i = pl.program_id(0)

        @pl.when(i == 0)
        def _():
            acc_ref[...] = jnp.zeros_like(acc_ref)

        acc_ref[...] += jnp.sum(e * e, axis=0, keepdims=True)

    row = pl.BlockSpec((tr, D), lambda i: (i, 0))
    vec = pl.BlockSpec((1, D), lambda i: (0, 0))
    return _call(body, name=name, out_shape=(_sds((S, D), F32), _sds((1, D), F32)), grid=(S // tr,),
                 in_specs=[row, row], out_specs=(row, vec), dims=("arbitrary",))(y, target)


def _pool_counts(t0, rows):
    return t0 + lax.broadcasted_iota(jnp.int32, (rows, 1), 0)


def _pool_fwd(x, g, name):
    S, D = x.shape
    gc = D // len(POOL_WINDOWS)
    tp = _tile(S, 256)

    def body(x_ref, g_ref, d_ref, ext_ref):
        i = pl.program_id(0)

        @pl.when(i == 0)
        def _():
            ext_ref[pl.ds(0, POOL_HALO), :] = jnp.zeros((POOL_HALO, D), F32)

        xv = x_ref[...]
        ext_ref[pl.ds(POOL_HALO, tp), :] = xv * _rstd(xv) * g_ref[...]
        t = _pool_counts(i * tp, tp)
        for gi, w in enumerate(POOL_WINDOWS):
            cols = slice(gi * gc, (gi + 1) * gc)
            s = ext_ref[:, cols]
            h = s[POOL_HALO:]
            sh = 1
            while sh < w:
                s = s + pltpu.roll(s, sh, 0)
                sh *= 2
            cnt = jnp.minimum(t + 1, w).astype(F32)
            d_ref[:, cols] = (s[POOL_HALO:] / cnt - h).astype(d_ref.dtype)
        ext_ref[pl.ds(0, POOL_HALO), :] = ext_ref[pl.ds(tp, POOL_HALO), :]

    row = pl.BlockSpec((tp, D), lambda i: (i, 0))
    vec = pl.BlockSpec((1, D), lambda i: (0, 0))
    return _call(body, name=name, out_shape=_sds((S, D), BF16), grid=(S // tp,), in_specs=[row, vec],
                 out_specs=row, scratch_shapes=[pltpu.VMEM((tp + POOL_HALO, D), F32)], dims=("arbitrary",))(x, g)


def _pool_mm_fwd(d, wp, l, scale, x, gpost, name):
    S, D = x.shape
    ng = len(POOL_WINDOWS)
    gc = D // ng
    tp = _tile(S, 256)

    def body(d_ref, w_ref, sc_ref, x_ref, g_ref, y_ref, o_ref):
        for gi in range(ng):
            cols = slice(gi * gc, (gi + 1) * gc)
            y_ref[:, cols] = jnp.dot(d_ref[:, cols], w_ref[gi], preferred_element_type=F32)
        m = y_ref[...] * sc_ref[...]
        o_ref[...] = x_ref[...] + m * _rstd(m) * g_ref[...]

    row = pl.BlockSpec((tp, D), lambda i: (i, 0))
    vec = pl.BlockSpec((1, D), lambda i: (0, 0))
    wsp = pl.BlockSpec((None, ng, gc, gc), lambda i: (l, 0, 0, 0))
    return _call(body, name=name, out_shape=(_sds((S, D), F32), _sds((S, D), F32)), grid=(S // tp,),
                 in_specs=[row, wsp, vec, row, vec], out_specs=(row, row), dims=("parallel",))(d, wp, scale, x, gpost)


def _pool_mm_bwd(dx, y, d, wp, l, scale, gpost, name):
    S, D = dx.shape
    ng = len(POOL_WINDOWS)
    gc = D // ng
    tp = _tile(S, 256)

    def body(dx_ref, y_ref, d_ref, w_ref, sc_ref, g_ref, dd_ref, dw_ref, dsc_ref, dg_ref):
        i = pl.program_id(0)

        @pl.when(i == 0)
        def _():
            dw_ref[...] = jnp.zeros_like(dw_ref)
            dsc_ref[...] = jnp.zeros_like(dsc_ref)
            dg_ref[...] = jnp.zeros_like(dg_ref)

        yv = y_ref[...]
        sc = sc_ref[...]
        dm, dgr = _rms_bwd_math(yv * sc, g_ref[...], dx_ref[...])
        dg_ref[...] += jnp.sum(dgr, axis=0, keepdims=True)
        dsc_ref[...] += jnp.sum(dm * yv, axis=0, keepdims=True)
        dyv = (dm * sc).astype(BF16)
        for gi in range(ng):
            cols = slice(gi * gc, (gi + 1) * gc)
            dyg = dyv[:, cols]
            dd_ref[:, cols] = lax.dot_general(dyg, w_ref[gi], (((1,), (1,)), ((), ())), preferred_element_type=F32)
            dw_ref[gi] += lax.dot_general(d_ref[:, cols], dyg, (((0,), (0,)), ((), ())), preferred_element_type=F32)

    row = pl.BlockSpec((tp, D), lambda i: (i, 0))
    vec = pl.BlockSpec((1, D), lambda i: (0, 0))
    wsp = pl.BlockSpec((None, ng, gc, gc), lambda i: (l, 0, 0, 0))
    dwsp = pl.BlockSpec((ng, gc, gc), lambda i: (0, 0, 0))
    return _call(body, name=name,
                 out_shape=(_sds((S, D), F32), _sds((ng, gc, gc), F32), _sds((1, D), F32), _sds((1, D), F32)),
                 grid=(S // tp,), in_specs=[row, row, row, wsp, vec, vec], out_specs=(row, dwsp, vec, vec),
                 dims=("arbitrary",))(dx, y, d, wp, scale, gpost)


def _pool_bwd(dd, x, g, res, name):
    S, D = x.shape
    gc = D // len(POOL_WINDOWS)
    tp = _tile(S, 256)
    nt = S // tp

    def body(dd_ref, x_ref, g_ref, res_ref, dx_ref, dg_ref, ext_ref, dh_ref):
        i = pl.program_id(0)

        @pl.when(i == 0)
        def _():
            ext_ref[pl.ds(tp, POOL_HALO), :] = jnp.zeros((POOL_HALO, D), F32)
            dg_ref[...] = jnp.zeros_like(dg_ref)

        t = _pool_counts((nt - 1 - i) * tp, tp)
        for gi, w in enumerate(POOL_WINDOWS):
            cols = slice(gi * gc, (gi + 1) * gc)
            ddv = dd_ref[:, cols]
            ext_ref[pl.ds(0, tp), cols] = ddv / jnp.minimum(t + 1, w).astype(F32)
            s = ext_ref[:, cols]
            sh = 1
            while sh < w:
                s = s + pltpu.roll(s, tp + POOL_HALO - sh, 0)
                sh *= 2
            dh_ref[:, cols] = s[:tp] - ddv
        ext_ref[pl.ds(tp, POOL_HALO), :] = ext_ref[pl.ds(0, POOL_HALO), :]
        dx, dgr = _rms_bwd_math(x_ref[...], g_ref[...], dh_ref[...])
        dx_ref[...] = dx + res_ref[...]
        dg_ref[...] += jnp.sum(dgr, axis=0, keepdims=True)

    row = pl.BlockSpec((tp, D), lambda i: (nt - 1 - i, 0))
    vec = pl.BlockSpec((1, D), lambda i: (0, 0))
    return _call(body, name=name, out_shape=(_sds((S, D), F32), _sds((1, D), F32)), grid=(nt,),
                 in_specs=[row, row, vec, row], out_specs=(row, vec),
                 scratch_shapes=[pltpu.VMEM((tp + POOL_HALO, D), F32), pltpu.VMEM((tp, D), F32)],
                 dims=("arbitrary",))(dd, x, g, res)


def _gelu(x):
    return 0.5 * x * (1.0 + jnp.tanh(0.7978845608028654 * (x + 0.044715 * x * x * x)))


def _gelu_grad(x):
    th = jnp.tanh(0.7978845608028654 * (x + 0.044715 * x * x * x))
    return 0.5 * (1.0 + th) + 0.5 * x * (1.0 - th * th) * 0.7978845608028654 * (1.0 + 3.0 * 0.044715 * x * x)


def _conv_taps(ext_ref, cols, tt):
    e = ext_ref[:, cols]
    return e[CONV_HALO:], pltpu.roll(e, 1, 0)[CONV_HALO:], pltpu.roll(e, 2, 0)[CONV_HALO:]


def _conv_glu_fwd(u, cw, cb, name):
    S, F2 = u.shape
    wd = F2 // 4
    tt = _tile(S, 256)

    def body(u_ref, cw_ref, cb_ref, a_ref, ext_ref):
        it = pl.program_id(1)

        @pl.when(it == 0)
        def _():
            ext_ref[pl.ds(0, CONV_HALO), :] = jnp.zeros((CONV_HALO, 2 * wd), F32)

        ext_ref[pl.ds(CONV_HALO, tt), :] = u_ref[...]
        for cc in range(wd // LANES):
            act = []
            for half in range(2):
                cols = slice(half * wd + cc * LANES, half * wd + (cc + 1) * LANES)
                u0, u1, u2 = _conv_taps(ext_ref, cols, tt)
                act.append(cw_ref[2:3, cols] * u0 + cw_ref[1:2, cols] * u1 + cw_ref[0:1, cols] * u2 + cb_ref[:, cols])
            a_ref[:, cc * LANES:(cc + 1) * LANES] = (_gelu(act[0]) * act[1]).astype(a_ref.dtype)
        ext_ref[pl.ds(0, CONV_HALO), :] = ext_ref[pl.ds(tt, CONV_HALO), :]

    return _call(body, name=name, out_shape=_sds((S, F2 // 2), BF16), grid=(2, S // tt),
                 in_specs=[pl.BlockSpec((tt, 2 * wd), lambda h, t: (t, h)), pl.BlockSpec((8, 2 * wd), lambda h, t: (0, h)),
                           pl.BlockSpec((1, 2 * wd), lambda h, t: (0, h))],
                 out_specs=pl.BlockSpec((tt, wd), lambda h, t: (t, h)),
                 scratch_shapes=[pltpu.VMEM((tt + CONV_HALO, 2 * wd), F32)], dims=("parallel", "arbitrary"))(u, cw, cb)


def _conv_glu_bwd1(u, da, cw, cb, name):
    S, F2 = u.shape
    wd = F2 // 4
    tt = _tile(S, 256)

    def body(u_ref, da_ref, cw_ref, cb_ref, duc_ref, acc_ref, ext_ref):
        it = pl.program_id(1)

        @pl.when(it == 0)
        def _():
            ext_ref[pl.ds(0, CONV_HALO), :] = jnp.zeros((CONV_HALO, 2 * wd), F32)
            acc_ref[...] = jnp.zeros_like(acc_ref)

        ext_ref[pl.ds(CONV_HALO, tt), :] = u_ref[...]
        for cc in range(wd // LANES):
            taps, act = [], []
            for half in range(2):
                cols = slice(half * wd + cc * LANES, half * wd + (cc + 1) * LANES)
                u0, u1, u2 = _conv_taps(ext_ref, cols, tt)
                taps.append((u0, u1, u2))
                act.append(cw_ref[2:3, cols] * u0 + cw_ref[1:2, cols] * u1 + cw_ref[0:1, cols] * u2 + cb_ref[:, cols])
            dav = da_ref[:, cc * LANES:(cc + 1) * LANES]
            dact = (dav * act[1] * _gelu_grad(act[0]), dav * _gelu(act[0]))
            for half in range(2):
                cols = slice(half * wd + cc * LANES, half * wd + (cc + 1) * LANES)
                duc_ref[:, cols] = dact[half]
                u0, u1, u2 = taps[half]
                acc_ref[2:3, cols] += jnp.sum(dact[half] * u0, axis=0, keepdims=True)
                acc_ref[1:2, cols] += jnp.sum(dact[half] * u1, axis=0, keepdims=True)
                acc_ref[0:1, cols] += jnp.sum(dact[half] * u2, axis=0, keepdims=True)
                acc_ref[3:4, cols] += jnp.sum(dact[half], axis=0, keepdims=True)
        ext_ref[pl.ds(0, CONV_HALO), :] = ext_ref[pl.ds(tt, CONV_HALO), :]

    wide = pl.BlockSpec((tt, 2 * wd), lambda h, t: (t, h))
    acc = pl.BlockSpec((8, 2 * wd), lambda h, t: (0, h))
    return _call(body, name=name, out_shape=(_sds((S, F2), F32), _sds((8, F2), F32)), grid=(2, S // tt),
                 in_specs=[wide, pl.BlockSpec((tt, wd), lambda h, t: (t, h)), acc, pl.BlockSpec((1, 2 * wd), lambda h, t: (0, h))],
                 out_specs=(wide, acc), scratch_shapes=[pltpu.VMEM((tt + CONV_HALO, 2 * wd), F32)],
                 dims=("parallel", "arbitrary"))(u, da, cw, cb)


def _conv_bwd2(duc, cw, name):
    S, F2 = duc.shape
    wd = F2 // 4
    tt = _tile(S, 256)
    nt = S // tt

    def body(duc_ref, cw_ref, du_ref, ext_ref):
        it = pl.program_id(1)

        @pl.when(it == 0)
        def _():
            ext_ref[pl.ds(tt, CONV_HALO), :] = jnp.zeros((CONV_HALO, 2 * wd), F32)

        ext_ref[pl.ds(0, tt), :] = duc_ref[...]
        for cc in range(2 * wd // LANES):
            cols = slice(cc * LANES, (cc + 1) * LANES)
            e = ext_ref[:, cols]
            n = tt + CONV_HALO
            du = (cw_ref[2:3, cols] * e[:tt] + cw_ref[1:2, cols] * pltpu.roll(e, n - 1, 0)[:tt]
                  + cw_ref[0:1, cols] * pltpu.roll(e, n - 2, 0)[:tt])
            du_ref[:, cols] = du.astype(du_ref.dtype)
        ext_ref[pl.ds(tt, CONV_HALO), :] = ext_ref[pl.ds(0, CONV_HALO), :]

    wide = pl.BlockSpec((tt, 2 * wd), lambda h, t: (nt - 1 - t, h))
    return _call(body, name=name, out_shape=_sds((S, F2), BF16), grid=(2, nt),
                 in_specs=[wide, pl.BlockSpec((8, 2 * wd), lambda h, t: (0, h))], out_specs=wide,
                 scratch_shapes=[pltpu.VMEM((tt + CONV_HALO, 2 * wd), F32)], dims=("parallel", "arbitrary"))(duc, cw)


def _rope_chunk(x, cosv, sinv):
    lane = lax.broadcasted_iota(jnp.int32, x.shape, 1)
    partner = jnp.where(lane % HEAD_DIM < HEAD_DIM // 2, pltpu.roll(x, LANES - HEAD_DIM // 2, 1),
                        pltpu.roll(x, HEAD_DIM // 2, 1))
    return x * cosv + partner * sinv


def _rope(x, width, cos_t, sin_t, name):
    S = x.shape[0]
    tr = _tile(S, 256)

    def body(x_ref, c_ref, s_ref, o_ref):
        for cc in range(width // LANES):
            cols = slice(cc * LANES, (cc + 1) * LANES)
            o_ref[:, cols] = _rope_chunk(x_ref[:, cols], c_ref[...], s_ref[...])

    row = pl.BlockSpec((tr, width), lambda i: (i, 0))
    tab = pl.BlockSpec((tr, LANES), lambda i: (i, 0))
    return _call(body, name=name, out_shape=_sds((S, width), F32), grid=(S // tr,), in_specs=[row, tab, tab],
                 out_specs=row, dims=("parallel",))(x, cos_t, sin_t)


def _attn_mask(n, reps):
    row = lax.broadcasted_iota(jnp.int32, (reps * BLOCK, 2 * BLOCK), 0) & (BLOCK - 1)
    col = lax.broadcasted_iota(jnp.int32, (reps * BLOCK, 2 * BLOCK), 1)
    rel = BLOCK + row - col
    return (rel >= 0) & (rel < WINDOW) & (n * BLOCK + col - BLOCK >= 0)


def _per_head_column(values, reps):
    grp = lax.broadcasted_iota(jnp.int32, (reps * BLOCK, 1), 0) // BLOCK
    col = jnp.zeros((reps * BLOCK, 1), F32)
    for g, v in enumerate(values):
        col = jnp.where(grp == g, v, col)
    return col


def _stack_heads(ref, hk, qpk, keep, scale):
    parts = []
    for g in range(qpk):
        qc, qpar, _, kpar = _head_place(hk * qpk + g, qpk)
        x = ref[:, qc * LANES:(qc + 1) * LANES]
        if scale != 1.0:
            x = x * scale
        if qpar != kpar:
            x = pltpu.roll(x, HEAD_DIM, 1)
        parts.append(jnp.where(keep, x, 0.0).astype(BF16))
    return jnp.concatenate(parts, axis=0)


def _unstack_heads(vals, ref, hk, qpk, lane, dtype):
    pair = None
    for g in range(qpk):
        qc, qpar, _, kpar = _head_place(hk * qpk + g, qpk)
        v = vals[g * BLOCK:(g + 1) * BLOCK]
        if qpar != kpar:
            v = pltpu.roll(v, HEAD_DIM, 1)
        if qpar == 0:
            pair = v
        else:
            ref[:, qc * LANES:(qc + 1) * LANES] = jnp.where(lane < HEAD_DIM, pair, v).astype(dtype)


def _head_place(h, qpk):
    hk = h // qpk
    return h // 2, h % 2, hk // 2, hk % 2


def _attn_specs(S, D):
    nb = S // BLOCK
    kvw = KV_DIM // 2
    qsp = pl.BlockSpec((BLOCK, D), lambda n: (n, 0))
    prev = lambda n: jnp.maximum(n - 1, 0)
    kp = pl.BlockSpec((BLOCK, kvw), lambda n: (prev(n), 0))
    ko = pl.BlockSpec((BLOCK, kvw), lambda n: (n, 0))
    vp = pl.BlockSpec((BLOCK, kvw), lambda n: (prev(n), 1))
    vo = pl.BlockSpec((BLOCK, kvw), lambda n: (n, 1))
    stat = pl.BlockSpec((BLOCK, LANES), lambda n: (n, 0))
    smem = pl.BlockSpec(memory_space=pltpu.SMEM)
    return nb, kvw, qsp, kp, ko, vp, vo, stat, smem


def _attn_fwd(q, k, kv, sinks, name):
    S, D = q.shape
    nh = D // HEAD_DIM
    qpk = nh // N_KV_HEADS
    nb, kvw, qsp, kp, ko, vp, vo, stat, smem = _attn_specs(S, D)

    def body(q_ref, kp_ref, ko_ref, vp_ref, vo_ref, s_ref, o_ref, l_ref):
        n = pl.program_id(0)
        valid = _attn_mask(n, qpk)
        lane = lax.broadcasted_iota(jnp.int32, (BLOCK, LANES), 1)
        lacc = jnp.zeros((BLOCK, LANES), F32)
        for hk in range(N_KV_HEADS):
            kc, kpar = hk // 2, hk % 2
            kcols = slice(kc * LANES, (kc + 1) * LANES)
            k2 = jnp.concatenate([kp_ref[:, kcols], ko_ref[:, kcols]], axis=0).astype(BF16)
            v2 = jnp.concatenate([vp_ref[:, kcols], vo_ref[:, kcols]], axis=0).astype(BF16)
            keep = (lane >= kpar * HEAD_DIM) & (lane < (kpar + 1) * HEAD_DIM)
            qm = _stack_heads(q_ref, hk, qpk, keep, ATTN_SCALE)
            s = lax.dot_general(qm, k2, (((1,), (1,)), ((), ())), preferred_element_type=F32)
            s = jnp.where(valid, s, NEG_INF)
            sink = _per_head_column([s_ref[hk * qpk + g] for g in range(qpk)], qpk)
            m = jnp.maximum(jnp.max(s, axis=1, keepdims=True), sink)
            p = jnp.exp(s - m)
            den = jnp.sum(p, axis=1, keepdims=True) + jnp.exp(sink - m)
            of = jnp.dot(p.astype(BF16), v2, preferred_element_type=F32) / den
            lse = m + jnp.log(den)
            for g in range(qpk):
                lacc = jnp.where(lane == hk * qpk + g, lse[g * BLOCK:(g + 1) * BLOCK], lacc)
            _unstack_heads(of, o_ref, hk, qpk, lane, o_ref.dtype)
        l_ref[...] = lacc

    return _call(body, name=name, out_shape=(_sds((S, D), BF16), _sds((S, LANES), F32)), grid=(nb,),
                 in_specs=[qsp, kp, ko, vp, vo, smem], out_specs=(qsp, stat), dims=("parallel",))(q, k, k, kv, kv, sinks)


def _attn_bwd(q, k, kv, do, lse, sinks, name):
    S, D = q.shape
    nh = D // HEAD_DIM
    qpk = nh // N_KV_HEADS
    nb, kvw, qsp, kp, ko, vp, vo, stat, smem = _attn_specs(S, D)

    def body(q_ref, kp_ref, ko_ref, vp_ref, vo_ref, do_ref, l_ref, s_ref,
             dq_ref, dkp_ref, dko_ref, dvp_ref, dvo_ref, ds_ref):
        n = pl.program_id(0)

        @pl.when(n == 0)
        def _():
            ds_ref[...] = jnp.zeros_like(ds_ref)

        valid = _attn_mask(n, qpk)
        lane = lax.broadcasted_iota(jnp.int32, (BLOCK, LANES), 1)
        lane8 = lax.broadcasted_iota(jnp.int32, (8, LANES), 1)
        lv = l_ref[...]
        dsink = jnp.zeros((8, LANES), F32)
        for kc in range(N_KV_HEADS // 2):
            kcols = slice(kc * LANES, (kc + 1) * LANES)
            k2 = jnp.concatenate([kp_ref[:, kcols], ko_ref[:, kcols]], axis=0).astype(BF16)
            v2 = jnp.concatenate([vp_ref[:, kcols], vo_ref[:, kcols]], axis=0).astype(BF16)
            dk2 = jnp.zeros((2 * BLOCK, LANES), F32)
            dv2 = jnp.zeros((2 * BLOCK, LANES), F32)
            for kpar in range(2):
                hk = 2 * kc + kpar
                heads = [hk * qpk + g for g in range(qpk)]
                keep = (lane >= kpar * HEAD_DIM) & (lane < (kpar + 1) * HEAD_DIM)
                qm = _stack_heads(q_ref, hk, qpk, keep, ATTN_SCALE)
                gm = _stack_heads(do_ref, hk, qpk, keep, 1.0)
                s = lax.dot_general(qm, k2, (((1,), (1,)), ((), ())), preferred_element_type=F32)
                lh = jnp.concatenate([jnp.sum(jnp.where(lane == h, lv, 0.0), axis=1, keepdims=True) for h in heads], axis=0)
                p = jnp.where(valid, jnp.exp(s - lh), 0.0)
                dp = lax.dot_general(gm, v2, (((1,), (1,)), ((), ())), preferred_element_type=F32)
                delta = jnp.sum(p * dp, axis=1, keepdims=True)
                dsb = (p * (dp - delta)).astype(BF16)
                lost = jnp.exp(_per_head_column([s_ref[h] for h in heads], qpk) - lh) * delta
                for g, h in enumerate(heads):
                    dsink = dsink - jnp.where(lane8 == h, jnp.sum(lost[g * BLOCK:(g + 1) * BLOCK]), 0.0)
                dqf = jnp.dot(dsb, k2, preferred_element_type=F32) * ATTN_SCALE
                _unstack_heads(dqf, dq_ref, hk, qpk, lane, F32)
                dk2 = dk2 + lax.dot_general(dsb, qm, (((0,), (0,)), ((), ())), preferred_element_type=F32)
                dv2 = dv2 + lax.dot_general(p.astype(BF16), gm, (((0,), (0,)), ((), ())), preferred_element_type=F32)
            dkp_ref[:, kcols] = dk2[:BLOCK]
            dko_ref[:, kcols] = dk2[BLOCK:]
            dvp_ref[:, kcols] = dv2[:BLOCK]
            dvo_ref[:, kcols] = dv2[BLOCK:]
        ds_ref[...] += dsink

    kvo = pl.BlockSpec((BLOCK, kvw), lambda n: (n, 0))
    acc = pl.BlockSpec((8, LANES), lambda n: (0, 0))
    part = _sds((S, kvw), F32)
    return _call(body, name=name, out_shape=(_sds((S, D), F32), part, part, part, part, _sds((8, LANES), F32)),
                 grid=(nb,), in_specs=[qsp, kp, ko, vp, vo, qsp, stat, smem],
                 out_specs=(qsp, kvo, kvo, kvo, kvo, acc), dims=("arbitrary",))(q, k, k, kv, kv, do, lse, sinks)


def _kv_grad(parts, cos_t, sin_neg_t, name):
    S, kvw = parts[0][0].shape
    nb = S // BLOCK
    flat = [a for p in parts for a in p]
    nl = len(parts)

    def body(*refs):
        c_ref, s_ref, o_ref = refs[4 * nl], refs[4 * nl + 1], refs[4 * nl + 2]
        n = pl.program_id(0)
        last = n == nb - 1
        dk = jnp.zeros((BLOCK, kvw), F32)
        dv = jnp.zeros((BLOCK, kvw), F32)
        for li in range(nl):
            kn, kown, vn, vown = refs[4 * li:4 * li + 4]
            dk = dk + kown[...] + jnp.where(last, 0.0, kn[...])
            dv = dv + vown[...] + jnp.where(last, 0.0, vn[...])
        for cc in range(kvw // LANES):
            cols = slice(cc * LANES, (cc + 1) * LANES)
            o_ref[:, cols] = _rope_chunk(dk[:, cols], c_ref[...], s_ref[...])
        o_ref[:, kvw:] = dv

    own = pl.BlockSpec((BLOCK, kvw), lambda n: (n, 0))
    nxt = pl.BlockSpec((BLOCK, kvw), lambda n: (jnp.minimum(n + 1, nb - 1), 0))
    tab = pl.BlockSpec((BLOCK, LANES), lambda n: (n, 0))
    return _call(body, name=name, out_shape=_sds((S, 2 * kvw), F32), grid=(nb,),
                 in_specs=[nxt, own, nxt, own] * nl + [tab, tab],
                 out_specs=pl.BlockSpec((BLOCK, 2 * kvw), lambda n: (n, 0)), dims=("parallel",))(*flat, cos_t, sin_neg_t)


def _sum_blocks(name, qc, grid, out_shape, out_block, out_imap, ins, out_dtype=F32, into=None):
    nin = len(ins)

    def body(qc_ref, *refs):
        acc = refs[0][...].astype(F32)
        for r in refs[1:nin]:
            acc = acc + r[...].astype(F32)
        refs[-1][...] = acc.astype(refs[-1].dtype)

    in_specs = [pl.BlockSpec(b, m) for _, b, m in ins]
    operands = [a for a, _, _ in ins]
    aliases = None
    if into is not None:
        in_specs.append(ANY)
        operands.append(into)
        aliases = {1 + nin: 0}
    gs = pltpu.PrefetchScalarGridSpec(num_scalar_prefetch=1, grid=grid, in_specs=in_specs,
                                      out_specs=pl.BlockSpec(out_block, out_imap))
    return _call(body, name=name, out_shape=_sds(out_shape, out_dtype), grid_spec=gs,
                 dims=("parallel",) * len(grid), aliases=aliases)(qc, *operands)


def _adamw(w, g, m, v, name):
    shape = w.shape
    C = shape[-1]
    R = w.size // C
    tr = _tile(R, max(8, (1 << 18) // C))

    def body(w_ref, g_ref, m_ref, v_ref, d_ref, nm_ref, nv_ref):
        gv = g_ref[...]
        nm = ADAM_B1 * m_ref[...] + (1.0 - ADAM_B1) * gv
        nv = ADAM_B2 * v_ref[...] + (1.0 - ADAM_B2) * (gv * gv)
        m_hat = nm / (1.0 - ADAM_B1 ** ADAM_STEP)
        v_hat = nv / (1.0 - ADAM_B2 ** ADAM_STEP)
        d_ref[...] = -ADAM_LR * (m_hat / (jnp.sqrt(v_hat) + ADAM_EPS) + ADAM_WD * w_ref[...])
        nm_ref[...] = nm
        nv_ref[...] = nv

    blk = pl.BlockSpec((tr, C), lambda i: (i, 0))
    flat = _sds((R, C), F32)
    outs = _call(body, name=name, out_shape=(flat, flat, flat), grid=(R // tr,), in_specs=[blk] * 4,
                 out_specs=(blk, blk, blk), dims=("parallel",))(*[a.reshape(R, C) for a in (w, g, m, v)])
    return tuple(o.reshape(shape) for o in outs)


def _place():
    x, y, c = lax.axis_index("x"), lax.axis_index("y"), lax.axis_index("c")
    chips = [(1 - x, y), (x, 1 - y), (1 - x, 1 - y)]
    return x, y, c, chips


def _at(ref, nd, dims):
    idx = [slice(None)] * nd
    for d, v in dims.items():
        idx[d] = pl.ds(v[0], v[1]) if isinstance(v, tuple) else v
    return ref.at[tuple(idx)]


def _remote(src, dst, send_sem, recv_sem, dev):
    return pltpu.make_async_remote_copy(src_ref=src, dst_ref=dst, send_sem=send_sem, recv_sem=recv_sem,
                                        device_id=dev, device_id_type=MESH)


def _gather_shards(shards, specs, name):
    n = len(shards)
    fulls = []
    for a, (sd, hd) in zip(shards, specs):
        shp = list(a.shape)
        shp[sd] *= N_SHARDS
        fulls.append(_sds(shp, a.dtype))

    def body(*refs):
        ins, outs = refs[:n], refs[n:2 * n]
        send_sems, recv_sems, loc_sems = refs[2 * n:]
        x, y, c, chips = _place()
        q = 2 * x + y
        me, sibling = (x, y, c), (x, y, 1 - c)
        locals_, sends = [], []
        for ai, (sd, hd) in enumerate(specs):
            shp = shards[ai].shape
            nd, ss, hs = len(shp), shp[sd], shp[hd] // 2
            loc = pltpu.make_async_copy(ins[ai], _at(outs[ai], nd, {sd: (q * ss, ss)}), loc_sems.at[ai])
            loc.start()
            locals_.append(loc)
            for j, ch in enumerate(chips):
                cp = _remote(_at(ins[ai], nd, {hd: (c * hs, hs)}),
                             _at(outs[ai], nd, {sd: (q * ss, ss), hd: (c * hs, hs)}),
                             send_sems.at[6 * ai + j], recv_sems.at[6 * ai + j], (ch[0], ch[1], c))
                cp.start()
                sends.append(cp)
        for ai, (sd, hd) in enumerate(specs):
            shp = shards[ai].shape
            nd, ss, hs = len(shp), shp[sd], shp[hd] // 2
            for j, ch in enumerate(chips):
                land = _at(outs[ai], nd, {sd: ((2 * ch[0] + ch[1]) * ss, ss), hd: (c * hs, hs)})
                _remote(land, land, send_sems.at[6 * ai + j], recv_sems.at[6 * ai + j], me).wait_recv()
                fw = _remote(land, land, send_sems.at[6 * ai + 3 + j], recv_sems.at[6 * ai + 3 + j], sibling)
                fw.start()
                sends.append(fw)
        for ai, (sd, hd) in enumerate(specs):
            shp = shards[ai].shape
            nd, ss, hs = len(shp), shp[sd], shp[hd] // 2
            for j, ch in enumerate(chips):
                land = _at(outs[ai], nd, {sd: ((2 * ch[0] + ch[1]) * ss, ss), hd: ((1 - c) * hs, hs)})
                _remote(land, land, send_sems.at[6 * ai + 3 + j], recv_sems.at[6 * ai + 3 + j], me).wait_recv()
        for cp in sends:
            cp.wait_send()
        for loc in locals_:
            loc.wait()

    return _call(body, name=name, out_shape=tuple(fulls), in_specs=[ANY] * n, out_specs=tuple([ANY] * n),
                 scratch_shapes=[pltpu.SemaphoreType.DMA((6 * n,)), pltpu.SemaphoreType.DMA((6 * n,)),
                                 pltpu.SemaphoreType.DMA((n,))])(*shards)


def _half_dims(shape, hd, c):
    hs = shape[hd] // 2
    return {hd: (c * hs, hs)}


def _swap_halves(grads, specs, name):
    n = len(grads)
    outs_shape = []
    for a, (sd, hd) in zip(grads, specs):
        shp = list(a.shape)
        shp[hd] //= 2
        outs_shape.append(_sds(shp, F32))

    def body(*refs):
        ins, outs = refs[:n], refs[n:2 * n]
        send_sems, recv_sems = refs[2 * n:]
        x, y, c, _ = _place()
        cps = []
        for ai, (sd, hd) in enumerate(specs):
            shp = grads[ai].shape
            cp = _remote(_at(ins[ai], len(shp), _half_dims(shp, hd, 1 - c)), outs[ai],
                         send_sems.at[ai], recv_sems.at[ai], (x, y, 1 - c))
            cp.start()
            cps.append(cp)
        for cp in cps:
            cp.wait()

    return _call(body, name=name, out_shape=tuple(outs_shape), in_specs=[ANY] * n, out_specs=tuple([ANY] * n),
                 scratch_shapes=[pltpu.SemaphoreType.DMA((n,)), pltpu.SemaphoreType.DMA((n,))])(*grads)


def _scatter_shards(sums, specs, name):
    n = len(sums)
    outs_shape = []
    for a, (sd, hd) in zip(sums, specs):
        shp = list(a.shape)
        shp[sd] //= N_SHARDS
        outs_shape.append(_sds([3] + shp, a.dtype))

    def body(*refs):
        ins, outs = refs[:n], refs[n:2 * n]
        send_sems, recv_sems = refs[2 * n:]
        x, y, c, chips = _place()
        cps = []
        for ai, (sd, hd) in enumerate(specs):
            shp = sums[ai].shape
            ss = shp[sd] // N_SHARDS
            for j, ch in enumerate(chips):
                cp = _remote(_at(ins[ai], len(shp), {sd: ((2 * ch[0] + ch[1]) * ss, ss)}), outs[ai].at[j],
                             send_sems.at[3 * ai + j], recv_sems.at[3 * ai + j], (ch[0], ch[1], c))
                cp.start()
                cps.append(cp)
        for cp in cps:
            cp.wait()

    return _call(body, name=name, out_shape=tuple(outs_shape), in_specs=[ANY] * n, out_specs=tuple([ANY] * n),
                 scratch_shapes=[pltpu.SemaphoreType.DMA((3 * n,)), pltpu.SemaphoreType.DMA((3 * n,))])(*sums)


def _share_halves(arrs, half_axes, name):
    n = len(arrs)

    def body(*refs):
        ins, outs = refs[:n], refs[n:2 * n]
        send_sems, recv_sems = refs[2 * n:]
        x, y, c, _ = _place()

        def half(ref, ai, which):
            shp = arrs[ai].shape
            hs = shp[half_axes[ai]] // 2
            return _at(ref, len(shp), {half_axes[ai]: (which * hs, hs)})

        sends = []
        for ai in range(n):
            cp = _remote(half(ins[ai], ai, c), half(outs[ai], ai, c), send_sems.at[ai], recv_sems.at[ai], (x, y, 1 - c))
            cp.start()
            sends.append(cp)
        for ai in range(n):
            land = half(outs[ai], ai, 1 - c)
            _remote(land, land, send_sems.at[ai], recv_sems.at[ai], (x, y, c)).wait_recv()
        for cp in sends:
            cp.wait_send()

    return _call(body, name=name, out_shape=tuple(_sds(a.shape, a.dtype) for a in arrs), in_specs=[ANY] * n,
                 out_specs=tuple([ANY] * n), aliases={i: i for i in range(n)},
                 scratch_shapes=[pltpu.SemaphoreType.DMA((n,)), pltpu.SemaphoreType.DMA((n,))])(*arrs)


def _gather_small(v, name):
    R, C = v.shape

    def body(x_ref, out_ref, send_sems, recv_sems, local_sem):
        x, y, c, chips = _place()
        me, sibling = (x, y, c), (x, y, 1 - c)

        def rows(px, py, pc):
            return out_ref.at[pl.ds((4 * px + 2 * py + pc) * R, R), :]

        def copy(k, block, to, src=None):
            return _remote(rows(*block) if src is None else src, rows(*block), send_sems.at[k], recv_sems.at[k], to)

        mine = pltpu.make_async_copy(x_ref, rows(*me), local_sem)
        mine.start()
        first = [copy(0, me, sibling, src=x_ref)]
        first += [copy(1 + j, me, (ch[0], ch[1], c), src=x_ref) for j, ch in enumerate(chips)]
        for cp in first:
            cp.start()
        passed = [copy(4 + j, (ch[0], ch[1], c), sibling) for j, ch in enumerate(chips)]
        for j, ch in enumerate(chips):
            copy(1 + j, (ch[0], ch[1], c), me).wait_recv()
            passed[j].start()
        copy(0, sibling, me).wait_recv()
        for j, ch in enumerate(chips):
            copy(4 + j, (ch[0], ch[1], 1 - c), me).wait_recv()
        for cp in first + passed:
            cp.wait_send()
        mine.wait()

    vm = pl.BlockSpec(memory_space=pltpu.VMEM)
    return _call(body, name=name, out_shape=_sds((8 * R, C), v.dtype), in_specs=[vm], out_specs=vm,
                 scratch_shapes=[pltpu.SemaphoreType.DMA((7,)), pltpu.SemaphoreType.DMA((7,)),
                                 pltpu.SemaphoreType.DMA])(v)


def _sum8(g, name):
    _, R, C = g.shape

    def body(g_ref, o_ref):
        acc = g_ref[0]
        for d in range(1, 8):
            acc = acc + g_ref[d]
        o_ref[...] = acc

    return _call(body, name=name, out_shape=_sds((R, C), F32), in_specs=[pl.BlockSpec(memory_space=pltpu.VMEM)],
                 out_specs=pl.BlockSpec(memory_space=pltpu.VMEM))(g)


def _rope_tables(positions):
    inv_freq = 1.0 / (ROPE_THETA ** (jnp.arange(0, HEAD_DIM, 2, dtype=F32) / HEAD_DIM))
    ang = positions.astype(F32)[:, None] * inv_freq
    cosv, sinv = jnp.cos(ang), jnp.sin(ang)
    return jnp.tile(cosv, (1, 4)), jnp.tile(jnp.concatenate([-sinv, sinv], axis=1), (1, 2))


def _blocked(a):
    parts = jnp.split(a, 4, axis=-1)
    return jnp.concatenate([parts[0], parts[2], parts[1], parts[3]], axis=-1)


def _local_step(x, target, positions, W):
    S, D = x.shape
    depth = W['mix_pre_g'].shape[0]
    n_a = depth // 2
    cos_t, sin_t = _rope_tables(positions)
    row = lambda a, l: a[l][None]
    cw = [jnp.pad(_blocked(W['ffn_conv_w'][l]), ((0, 5), (0, 0))) for l in range(depth)]
    cb = [_blocked(W['ffn_conv_b'][l])[None] for l in range(depth)]
    sv = {}
    kv = k_rot = None
    for l in range(depth):
        t = f"l{l}"
        sv[l, 'x_in'] = x
        if l < n_a:
            d = _pool_fwd(x, row(W['mix_pre_g'], l), "pool_fwd_" + t)
            y, x = _pool_mm_fwd(d, W['pool_w'], l, row(W['pool_scale'], l), x, row(W['mix_post_g'], l), "pool_mm_fwd_" + t)
            sv[l, 'd'], sv[l, 'y'] = d, y
        else:
            j = l - n_a
            if l == n_a:
                hkv = _rms_fwd(x, W['kv_norm_g'][None], BF16, "kv_norm")
                kv = _matmul(hkv, W['w_kv'], 'nn', F32, "kv_proj", 512, 512, 1024)
                k_rot = _rope(kv, KV_DIM // 2, cos_t, sin_t, "k_rope")
                sv['hkv'] = hkv
            h = _rms_fwd(x, row(W['mix_pre_g'], l), BF16, "q_norm_" + t)
            qraw = _matmul(h, W['w_q'], 'nn', F32, "q_proj_" + t, 512, 1024, 1024, lb=j)
            q = _rope(qraw, D, cos_t, sin_t, "q_rope_" + t)
            o, lse = _attn_fwd(q, k_rot, kv, W['sinks'][j], "attn_fwd_" + t)
            m = _matmul(o, W['w_o'], 'nn', F32, "o_proj_" + t, 512, 1024, 1024, lb=j)
            x = _res_rms_fwd(x, m, row(W['mix_post_g'], l), "mix_post_" + t)
            sv[l, 'h'], sv[l, 'q'], sv[l, 'o'], sv[l, 'lse'], sv[l, 'm'] = h, q, o, lse, m
        sv[l, 'x1'] = x
        h2 = _rms_fwd(x, row(W['ffn_pre_g'], l), BF16, "ffn_norm_" + t)
        u = _matmul(h2, W['ffn_w_in'], 'nn', F32, "ffn_up_" + t, 512, W['ffn_w_in'].shape[2] // 4, 1024, lb=l, b_perm=True)
        a = _conv_glu_fwd(u, cw[l], cb[l], "ffn_glu_" + t)
        f = _matmul(a, W['ffn_w_out'], 'nn', F32, "ffn_down_" + t, 512, 1024, 2816, lb=l)
        x = _res_rms_fwd(x, f, row(W['ffn_post_g'], l), "ffn_post_" + t)
        sv[l, 'h2'], sv[l, 'u'], sv[l, 'a'], sv[l, 'f'] = h2, u, a, f

    dx, sq = _loss_grad(x, target, "loss")
    G = {}
    kv_parts = []
    for l in reversed(range(depth)):
        t = f"l{l}"
        wd = W['ffn_w_in'].shape[2] // 4
        df, G['ffn_post_g', l] = _rms_bwd(sv[l, 'f'], row(W['ffn_post_g'], l), dx, None, "ffn_post_bwd_" + t)
        da = _matmul(df, W['ffn_w_out'], 'nt', F32, "ffn_down_dx_" + t, 512, wd, 1024, lb=l)
        G['ffn_w_out', l] = _matmul(sv[l, 'a'], df, 'tn', F32, "ffn_down_dw_" + t, wd, 1024, 1024)
        duc, acc = _conv_glu_bwd1(sv[l, 'u'], da, cw[l], cb[l], "ffn_glu_bwd_" + t)
        G['ffn_conv_w', l] = _blocked(acc[0:3])
        G['ffn_conv_b', l] = _blocked(acc[3:4])
        du = _conv_bwd2(duc, cw[l], "ffn_conv_bwd_" + t)
        dh2 = _matmul(du, W['ffn_w_in'], 'nt', F32, "ffn_up_dx_" + t, 512, 1024, wd, lb=l, b_perm=True)
        G['ffn_w_in', l] = _matmul(sv[l, 'h2'], du, 'tn', F32, "ffn_up_dw_" + t, 1024, wd, 2048, out_perm=True)
        dx, G['ffn_pre_g', l] = _rms_bwd(sv[l, 'x1'], row(W['ffn_pre_g'], l), dh2, dx, "ffn_norm_bwd_" + t)
        if l < n_a:
            dd, G['pool_w', l], G['pool_scale', l], G['mix_post_g', l] = _pool_mm_bwd(
                dx, sv[l, 'y'], sv[l, 'd'], W['pool_w'], l, row(W['pool_scale'], l), row(W['mix_post_g'], l), "pool_mm_bwd_" + t)
            dx, G['mix_pre_g', l] = _pool_bwd(dd, sv[l, 'x_in'], row(W['mix_pre_g'], l), dx, "pool_bwd_" + t)
        else:
            j = l - n_a
            dm, G['mix_post_g', l] = _rms_bwd(sv[l, 'm'], row(W['mix_post_g'], l), dx, None, "mix_post_bwd_" + t)
            do = _matmul(dm, W['w_o'], 'nt', F32, "o_proj_dx_" + t, 512, 1024, 1024, lb=j)
            G['w_o', j] = _matmul(sv[l, 'o'], dm, 'tn', F32, "o_proj_dw_" + t, 1024, 1024, 1024)
            dq, dkn, dko, dvn, dvo, dsk = _attn_bwd(sv[l, 'q'], k_rot, kv, do, sv[l, 'lse'], W['sinks'][j], "attn_bwd_" + t)
            G['sinks', j] = dsk[0:1]
            kv_parts.append((dkn, dko, dvn, dvo))
            dqraw = _rope(dq, D, cos_t, -sin_t, "q_rope_bwd_" + t)
            dh = _matmul(dqraw, W['w_q'], 'nt', F32, "q_proj_dx_" + t, 512, 1024, 1024, lb=j)
            G['w_q', j] = _matmul(sv[l, 'h'], dqraw, 'tn', F32, "q_proj_dw_" + t, 1024, 1024, 1024)
            dx, G['mix_pre_g', l] = _rms_bwd(sv[l, 'x_in'], row(W['mix_pre_g'], l), dh, dx, "q_norm_bwd_" + t)
            if l == n_a:
                dkv = _kv_grad(kv_parts, cos_t, -sin_t, "kv_grad")
                dhkv = _matmul(dkv, W['w_kv'], 'nt', F32, "kv_proj_dx", 512, 1024, 512)
                G['w_kv'] = _matmul(sv['hkv'], dkv, 'tn', F32, "kv_proj_dw", 1024, 512, 1024)
                dx, G['kv_norm_g'] = _rms_bwd(sv[l, 'x_in'], W['kv_norm_g'][None], dhkv, dx, "kv_norm_bwd")
    return sq, dx, G


GATHERED = [('pool_w', 2, 0, True), ('w_kv', 0, 1, True), ('w_q', 1, 0, True), ('w_o', 1, 0, True),
            ('ffn_w_in', 2, 0, True), ('ffn_w_out', 1, 0, True), ('ffn_conv_w', 2, 0, False), ('pool_scale', 1, 0, False)]
SMALL = ['mix_pre_g', 'mix_post_g', 'kv_norm_g', 'sinks', 'ffn_pre_g', 'ffn_post_g', 'ffn_conv_b', 'ffn_conv_w', 'pool_scale']


def _reduce_big(G, depth, qc):
    n_a = depth // 2
    pieces = []
    for l in range(depth):
        pieces.append((G['ffn_w_in', l], 1, 0, 'ffn_w_in', {0: l}, 1))
        pieces.append((G['ffn_w_out', l], 0, 1, 'ffn_w_out', {0: l}, 2))
    for j in range(depth - n_a):
        pieces.append((G['w_q', j], 0, 1, 'w_q', {0: j}, 2))
        pieces.append((G['w_o', j], 0, 1, 'w_o', {0: j}, 2))
    pieces.append((G['w_kv'], 0, 1, 'w_kv', {}, 1))
    for l in range(n_a):
        pieces.append((G['pool_w', l], 1, 0, 'pool_w', {0: l}, 1))
    arrs = [p[0] for p in pieces]
    specs = [(p[1], p[2]) for p in pieces]

    theirs = _swap_halves(arrs, specs, "grad_swap_halves")
    sums = []
    for pi, (a, (sd, hd), r) in enumerate(zip(arrs, specs, theirs)):
        shp = r.shape
        nd = len(shp)
        if nd == 3:
            blk, grid = tuple(shp), (1,)
            mine = lambda i, s: (s[1], 0, 0)
            zero = lambda i, s: (0, 0, 0)
        elif hd == 0:
            tr = _tile(shp[0], max(16, (1 << 18) // shp[1]), 16)
            blk, grid = (tr, shp[1]), (shp[0] // tr,)
            nblk = shp[0] // tr
            mine = lambda i, s, nblk=nblk: (s[1] * nblk + i, 0)
            zero = lambda i, s: (i, 0)
        else:
            tr = _tile(shp[0], max(16, (1 << 18) // shp[1]), 16)
            blk, grid = (tr, shp[1]), (shp[0] // tr,)
            mine = lambda i, s: (i, s[1])
            zero = lambda i, s: (i, 0)
        sums.append(_sum_blocks(f"grad_chip_sum_{pi}", qc, grid, shp, blk, zero, [(a, blk, mine), (r, blk, zero)],
                                out_dtype=BF16))

    recvd = _scatter_shards(sums, specs, "grad_scatter")
    outs, half_axes = {}, {}
    for pi, ((a, sd, hd, oname, fixed, ohd), s_arr, r) in enumerate(zip(pieces, sums, recvd)):
        shp = r.shape[1:]
        nd = len(shp)
        lead = (fixed[0],) if fixed else ()
        none = (None,) if fixed else ()
        n_stack = sum(1 for p in pieces if p[3] == oname)
        if nd == 3:
            blk, grid = tuple(shp), (1,)
            mine = lambda i, s: (0, s[0], 0)
            rk = [lambda i, s, k=k: (k, 0, 0, 0) for k in range(3)]
            oshape = (n_stack, 2 * shp[0]) + tuple(shp[1:])
            oblk = none + blk
            omap = lambda i, s, lead=lead: lead + (s[1], 0, 0)
        elif sd == 1:
            tr = _tile(shp[0], max(16, (1 << 18) // shp[1]), 16)
            blk, grid = (tr, shp[1]), (shp[0] // tr,)
            nblk = shp[0] // tr
            mine = lambda i, s: (i, s[0])
            rk = [lambda i, s, k=k: (k, i, 0) for k in range(3)]
            oshape = (n_stack, 2 * shp[0], shp[1])
            oblk = none + blk
            omap = lambda i, s, lead=lead, nblk=nblk: lead + (s[1] * nblk + i, 0)
        else:
            tr = _tile(shp[0], max(16, (1 << 18) // shp[1]), 16)
            blk, grid = (tr, shp[1]), (shp[0] // tr,)
            nblk = shp[0] // tr
            mine = lambda i, s, nblk=nblk: (s[0] * nblk + i, 0)
            rk = [lambda i, s, k=k: (k, i, 0) for k in range(3)]
            oshape = ((n_stack,) if fixed else ()) + (shp[0], 2 * shp[1])
            oblk = none + blk
            omap = lambda i, s, lead=lead: lead + (i, s[1])
        ins = [(s_arr, blk, mine)] + [(r, (None,) + blk, rk[k]) for k in range(3)]
        outs[oname] = _sum_blocks(f"grad_total_{pi}", qc, grid, oshape, oblk, omap, ins, into=outs.get(oname))
        half_axes[oname] = ohd
    names = list(outs)
    whole = _share_halves([outs[n] for n in names], [half_axes[n] for n in names], "grad_share_halves")
    return dict(zip(names, whole))


def _pack_small(parts):
    rows, offs, r = [], [], 0
    for a in parts:
        flat = a.reshape(-1)
        nr = -(-flat.size // (8 * LANES)) * 8
        rows.append(jnp.pad(flat, (0, nr * LANES - flat.size)).reshape(nr, LANES))
        offs.append((r, nr, a.shape))
        r += nr
    return jnp.concatenate(rows, axis=0), offs


def _unpack_small(packed, offs):
    return [packed[r:r + nr].reshape(-1)[:math.prod(shape)].reshape(shape) for r, nr, shape in offs]


def kernel(x, positions, mix_pre_g, mix_post_g, pool_w, pool_scale, kv_norm_g, w_kv, w_q, w_o, sinks, ffn_pre_g, ffn_post_g, ffn_w_in, ffn_conv_w, ffn_conv_b, ffn_w_out, loss_target, m_mix_pre_g, m_mix_post_g, m_pool_w, m_pool_scale, m_kv_norm_g, m_w_kv, m_w_q, m_w_o, m_sinks, m_ffn_pre_g, m_ffn_post_g, m_ffn_w_in, m_ffn_conv_w, m_ffn_conv_b, m_ffn_w_out, v_mix_pre_g, v_mix_post_g, v_pool_w, v_pool_scale, v_kv_norm_g, v_w_kv, v_w_q, v_w_o, v_sinks, v_ffn_pre_g, v_ffn_post_g, v_ffn_w_in, v_ffn_conv_w, v_ffn_conv_b, v_ffn_w_out):
    w = dict(mix_pre_g=mix_pre_g, mix_post_g=mix_post_g, pool_w=pool_w, pool_scale=pool_scale, kv_norm_g=kv_norm_g,
             w_kv=w_kv, w_q=w_q, w_o=w_o, sinks=sinks, ffn_pre_g=ffn_pre_g, ffn_post_g=ffn_post_g, ffn_w_in=ffn_w_in,
             ffn_conv_w=ffn_conv_w, ffn_conv_b=ffn_conv_b, ffn_w_out=ffn_w_out)
    mom = dict(mix_pre_g=m_mix_pre_g, mix_post_g=m_mix_post_g, pool_w=m_pool_w, pool_scale=m_pool_scale,
               kv_norm_g=m_kv_norm_g, w_kv=m_w_kv, w_q=m_w_q, w_o=m_w_o, sinks=m_sinks, ffn_pre_g=m_ffn_pre_g,
               ffn_post_g=m_ffn_post_g, ffn_w_in=m_ffn_w_in, ffn_conv_w=m_ffn_conv_w, ffn_conv_b=m_ffn_conv_b,
               ffn_w_out=m_ffn_w_out)
    var = dict(mix_pre_g=v_mix_pre_g, mix_post_g=v_mix_post_g, pool_w=v_pool_w, pool_scale=v_pool_scale,
               kv_norm_g=v_kv_norm_g, w_kv=v_w_kv, w_q=v_w_q, w_o=v_w_o, sinks=v_sinks, ffn_pre_g=v_ffn_pre_g,
               ffn_post_g=v_ffn_post_g, ffn_w_in=v_ffn_w_in, ffn_conv_w=v_ffn_conv_w, ffn_conv_b=v_ffn_conv_b,
               ffn_w_out=v_ffn_w_out)
    depth = mix_pre_g.shape[0]
    q_chip = 2 * lax.axis_index("x") + lax.axis_index("y")
    qc = jnp.stack([q_chip, lax.axis_index("c")]).astype(jnp.int32)

    shards = [w[n].astype(BF16) if as_bf16 else w[n] for n, _, _, as_bf16 in GATHERED]
    fulls = _gather_shards(shards, [(sd, hd) for _, sd, hd, _ in GATHERED], "weight_gather")
    W = dict(w)
    for (n, _, _, _), full in zip(GATHERED, fulls):
        W[n] = full

    sq, dx, G = _local_step(x[0], loss_target[0], positions[0], W)
    loss = 0.5 / x.shape[-1] * lax.psum(jnp.sum(sq), ("x", "y", "c"))

    big = _reduce_big(G, depth, qc)

    n_a = depth // 2
    small_local = {
        'mix_pre_g': jnp.concatenate([G['mix_pre_g', l] for l in range(depth)], axis=0),
        'mix_post_g': jnp.concatenate([G['mix_post_g', l] for l in range(depth)], axis=0),
        'kv_norm_g': G['kv_norm_g'][0],
        'sinks': jnp.concatenate([G['sinks', j][:, :sinks.shape[1]] for j in range(depth - n_a)], axis=0),
        'ffn_pre_g': jnp.concatenate([G['ffn_pre_g', l] for l in range(depth)], axis=0),
        'ffn_post_g': jnp.concatenate([G['ffn_post_g', l] for l in range(depth)], axis=0),
        'ffn_conv_b': jnp.concatenate([G['ffn_conv_b', l] for l in range(depth)], axis=0),
        'ffn_conv_w': jnp.stack([G['ffn_conv_w', l] for l in range(depth)], axis=0),
        'pool_scale': jnp.concatenate([G['pool_scale', l] for l in range(n_a)], axis=0),
    }
    packed, offs = _pack_small([small_local[n] for n in SMALL])
    gathered = _gather_small(packed, "small_grad_gather")
    summed = _sum8(gathered.reshape(8, packed.shape[0], LANES), "small_grad_sum")
    grads = dict(zip(SMALL, _unpack_small(summed, offs)))
    wd = ffn_conv_w.shape[2]
    grads['ffn_conv_w'] = lax.dynamic_slice_in_dim(grads['ffn_conv_w'], q_chip * wd, wd, axis=2)
    ps = pool_scale.shape[1]
    grads['pool_scale'] = lax.dynamic_slice_in_dim(grads['pool_scale'], q_chip * ps, ps, axis=1)
    grads.update(big)

    w_small, o_w = _pack_small([w[n] for n in SMALL])
    g_small, _ = _pack_small([grads[n] for n in SMALL])
    m_small, _ = _pack_small([mom[n] for n in SMALL])
    v_small, _ = _pack_small([var[n] for n in SMALL])
    upd = _adamw(w_small, g_small, m_small, v_small, "adamw_small")
    delta, new_m, new_v = ({n: a for n, a in zip(SMALL, _unpack_small(u, o_w))} for u in upd)
    for n in WEIGHTS:
        if n not in SMALL:
            delta[n], new_m[n], new_v[n] = _adamw(w[n], grads[n], mom[n], var[n], "adamw_" + n)

    return (loss, dx[None], *[grads[n] for n in WEIGHTS], *[delta[n] for n in WEIGHTS],
            *[new_m[n] for n in WEIGHTS], *[new_v[n] for n in WEIGHTS])
```

```python
import functools
import math

import jax
import jax.numpy as jnp
from jax import lax
from jax.experimental import pallas as pl
from jax.experimental.pallas import tpu as pltpu

F32 = jnp.float32
BF16 = jnp.bfloat16
MESH = pl.DeviceIdType.MESH
ANY = pl.BlockSpec(memory_space=pl.ANY)
HBM = pl.BlockSpec(memory_space=pltpu.HBM)
VMEM = pl.BlockSpec(memory_space=pltpu.VMEM)
SEM = pl.BlockSpec(memory_space=pltpu.SEMAPHORE)
EFFECT = pltpu.SideEffectType.DATAFLOW_SIDE_EFFECTING

HEAD_DIM = 64
N_KV_HEADS = 4
KV_DIM = 2 * N_KV_HEADS * HEAD_DIM
WINDOW = 128
BLOCK = 128
POOL_WINDOWS = (2, 4, 8, 16)
POOL_HALO = 16
CONV_HALO = 8
ROPE_THETA = 10000.0
ATTN_SCALE = 1.0 / math.sqrt(HEAD_DIM)
NEG_INF = -1e30
RMS_EPS = 1e-6
ADAM_LR, ADAM_B1, ADAM_B2, ADAM_EPS, ADAM_WD, ADAM_STEP = 0.001, 0.9, 0.999, 1e-08, 0.01, 10
N_SHARDS = 4
LANES = 128
VMEM_LIMIT_BYTES = 48 << 20

WEIGHTS = ['mix_pre_g', 'mix_post_g', 'pool_w', 'pool_scale', 'kv_norm_g', 'w_kv', 'w_q', 'w_o', 'sinks',
           'ffn_pre_g', 'ffn_post_g', 'ffn_w_in', 'ffn_conv_w', 'ffn_conv_b', 'ffn_w_out']


def _call(body, *, name, out_shape, grid=None, in_specs=None, out_specs=None, scratch_shapes=(), dims=None,
          grid_spec=None, aliases=None):
    params = pltpu.CompilerParams(dimension_semantics=dims, vmem_limit_bytes=VMEM_LIMIT_BYTES)
    kw = {} if aliases is None else dict(input_output_aliases=aliases)
    if grid_spec is not None:
        return pl.pallas_call(body, name=name, out_shape=out_shape, grid_spec=grid_spec, compiler_params=params, **kw)
    if grid is not None:
        kw['grid'] = grid
    return pl.pallas_call(body, name=name, out_shape=out_shape, in_specs=in_specs, out_specs=out_specs,
                          scratch_shapes=list(scratch_shapes), compiler_params=params, **kw)


def _tile(n, pref, mult=8):
    if n <= pref:
        return n
    for t in range(pref, 0, -1):
        if n % t == 0 and t % mult == 0:
            return t
    raise ValueError((n, pref, mult))


def _sds(shape, dtype):
    return jax.ShapeDtypeStruct(tuple(shape), dtype)


def _perm4(j):
    return (j % 2) * 2 + j // 2


def _matmul(a, b, mode, out_dtype, name, tm, tn, tk, b_blocks=False, out_perm=False):
    a2 = a.shape
    b2 = (b.shape[1], 4 * b.shape[2]) if b_blocks else b.shape
    if mode == 'nn':
        (M, K), (K2, N) = a2, b2
    elif mode == 'nt':
        (M, K), (N, K2) = a2, b2
    else:
        (K, M), (K2, N) = a2, b2
    assert K == K2, (name, a.shape, b.shape)
    tm, tn, tk = _tile(M, tm), _tile(N, tn, LANES), _tile(K, tk, LANES if mode != 'tn' else 16)
    assert M % tm == 0 and N % tn == 0 and K % tk == 0
    nk = K // tk
    grid = (N // tn, M // tm, nk)

    if mode == 'nn':
        a_spec = pl.BlockSpec((tm, tk), lambda j, i, k: (i, k))
        if b_blocks:
            assert tn == b.shape[2]
            b_spec = pl.BlockSpec((None, tk, tn), lambda j, i, k: (_perm4(j), k, 0))
        else:
            b_spec = pl.BlockSpec((tk, tn), lambda j, i, k: (k, j))
        dn = (((1,), (0,)), ((), ()))
    elif mode == 'nt':
        a_spec = pl.BlockSpec((tm, tk), lambda j, i, k: (i, k))
        if b_blocks:
            assert tk == b.shape[2]
            b_spec = pl.BlockSpec((None, tn, tk), lambda j, i, k: (_perm4(k), j, 0))
        else:
            b_spec = pl.BlockSpec((tn, tk), lambda j, i, k: (j, k))
        dn = (((1,), (1,)), ((), ()))
    else:
        a_spec = pl.BlockSpec((tk, tm), lambda j, i, k: (k, i))
        b_spec = pl.BlockSpec((tk, tn), lambda j, i, k: (k, j))
        dn = (((0,), (0,)), ((), ()))
    po = _perm4 if out_perm else (lambda j: j)
    o_spec = pl.BlockSpec((tm, tn), lambda j, i, k: (i, po(j)))

    def body(a_ref, b_ref, o_ref, *acc):
        prod = lax.dot_general(a_ref[...].astype(BF16), b_ref[...].astype(BF16), dn, preferred_element_type=F32)
        if nk == 1:
            o_ref[...] = prod.astype(o_ref.dtype)
        else:
            k = pl.program_id(2)

            @pl.when(k == 0)
            def _():
                acc[0][...] = prod

            @pl.when(k > 0)
            def _():
                acc[0][...] += prod

            @pl.when(k == nk - 1)
            def _():
                o_ref[...] = acc[0][...].astype(o_ref.dtype)

    scratch = [] if nk == 1 else [pltpu.VMEM((tm, tn), F32)]
    return _call(body, name=name, out_shape=_sds((M, N), out_dtype), grid=grid, in_specs=[a_spec, b_spec],
                 out_specs=o_spec, scratch_shapes=scratch, dims=("parallel", "parallel", "arbitrary"))(a, b)


def _rstd(x):
    return lax.rsqrt(jnp.mean(x * x, axis=-1, keepdims=True) + RMS_EPS)


def _rms_fwd(x, g, out_dtype, name):
    S, D = x.shape
    tr = _tile(S, 256)

    def body(x_ref, g_ref, o_ref):
        xv = x_ref[...]
        o_ref[...] = (xv * _rstd(xv) * g_ref[...]).astype(o_ref.dtype)

    row = pl.BlockSpec((tr, D), lambda i: (i, 0))
    vec = pl.BlockSpec((1, D), lambda i: (0, 0))
    return _call(body, name=name, out_shape=_sds((S, D), out_dtype), grid=(S // tr,), in_specs=[row, vec],
                 out_specs=row, dims=("parallel",))(x, g)


def _res_rms_fwd(x, f, g, name):
    S, D = x.shape
    tr = _tile(S, 256)

    def body(x_ref, f_ref, g_ref, o_ref):
        fv = f_ref[...]
        o_ref[...] = x_ref[...] + fv * _rstd(fv) * g_ref[...]

    row = pl.BlockSpec((tr, D), lambda i: (i, 0))
    vec = pl.BlockSpec((1, D), lambda i: (0, 0))
    return _call(body, name=name, out_shape=_sds((S, D), F32), grid=(S // tr,), in_specs=[row, row, vec],
                 out_specs=row, dims=("parallel",))(x, f, g)


def _rms_bwd_math(xin, g, dy):
    r = _rstd(xin)
    xh = xin * r
    gy = dy * g
    dx = r * (gy - xh * jnp.mean(gy * xh, axis=-1, keepdims=True))
    return dx, dy * xh


def _rms_bwd(xin, g, dy, res, name):
    S, D = xin.shape
    tr = _tile(S, 256)
    has_res = res is not None

    def body(*refs):
        if has_res:
            x_ref, g_ref, dy_ref, res_ref, dx_ref, dg_ref = refs
        else:
            x_ref, g_ref, dy_ref, dx_ref, dg_ref = refs
        dx, dgr = _rms_bwd_math(x_ref[...], g_ref[...], dy_ref[...])
        dx_ref[...] = dx + res_ref[...] if has_res else dx
        i = pl.program_id(0)

        @pl.when(i == 0)
        def _():
            dg_ref[...] = jnp.zeros_like(dg_ref)

        dg_ref[...] += jnp.sum(dgr, axis=0, keepdims=True)

    row = pl.BlockSpec((tr, D), lambda i: (i, 0))
    vec = pl.BlockSpec((1, D), lambda i: (0, 0))
    ins = [xin, g, dy] + ([res] if has_res else [])
    in_specs = [row, vec, row] + ([row] if has_res else [])
    return _call(body, name=name, out_shape=(_sds((S, D), F32), _sds((1, D), F32)), grid=(S // tr,),
                 in_specs=in_specs, out_specs=(row, vec), dims=("arbitrary",))(*ins)


def _loss_grad(y, target, name):
    S, D = y.shape
    tr = _tile(S, 256)

    def body(y_ref, t_ref, dy_ref, acc_ref):
        e = y_ref[...] - t_ref[...]
        dy_ref[...] = e * (1.0 / D)
        i = pl.program_id(0)

        @pl.when(i == 0)
        def _():
            acc_ref[...] = jnp.zeros_like(acc_ref)

        acc_ref[...] += jnp.sum(e * e, axis=0, keepdims=True)

    row = pl.BlockSpec((tr, D), lambda i: (i, 0))
    vec = pl.BlockSpec((1, D), lambda i: (0, 0))
    return _call(body, name=name, out_shape=(_sds((S, D), F32), _sds((1, D), F32)), grid=(S // tr,),
                 in_specs=[row, row], out_specs=(row, vec), dims=("arbitrary",))(y, target)


def _pool_counts(t0, rows):
    return t0 + lax.broadcasted_iota(jnp.int32, (rows, 1), 0)


def _pool_fwd(x, g, name):
    S, D = x.shape
    gc = D // len(POOL_WINDOWS)
    tp = _tile(S, 256)

    def body(x_ref, g_ref, d_ref, ext_ref):
        i = pl.program_id(0)

        @pl.when(i == 0)
        def _():
            ext_ref[pl.ds(0, POOL_HALO), :] = jnp.zeros((POOL_HALO, D), F32)

        xv = x_ref[...]
        ext_ref[pl.ds(POOL_HALO, tp), :] = xv * _rstd(xv) * g_ref[...]
        t = _pool_counts(i * tp, tp)
        for gi, w in enumerate(POOL_WINDOWS):
            cols = slice(gi * gc, (gi + 1) * gc)
            s = ext_ref[:, cols]
            h = s[POOL_HALO:]
            sh = 1
            while sh < w:
                s = s + pltpu.roll(s, sh, 0)
                sh *= 2
            cnt = jnp.minimum(t + 1, w).astype(F32)
            d_ref[:, cols] = (s[POOL_HALO:] / cnt - h).astype(d_ref.dtype)
        ext_ref[pl.ds(0, POOL_HALO), :] = ext_ref[pl.ds(tp, POOL_HALO), :]

    row = pl.BlockSpec((tp, D), lambda i: (i, 0))
    vec = pl.BlockSpec((1, D), lambda i: (0, 0))
    return _call(body, name=name, out_shape=_sds((S, D), BF16), grid=(S // tp,), in_specs=[row, vec],
                 out_specs=row, scratch_shapes=[pltpu.VMEM((tp + POOL_HALO, D), F32)], dims=("arbitrary",))(x, g)


def _pool_mm_fwd(d, wp, scale, x, gpost, name):
    S, D = x.shape
    ng = len(POOL_WINDOWS)
    gc = D // ng
    tp = _tile(S, 256)

    def body(d_ref, w_ref, sc_ref, x_ref, g_ref, y_ref, o_ref):
        for gi in range(ng):
            cols = slice(gi * gc, (gi + 1) * gc)
            y_ref[:, cols] = jnp.dot(d_ref[:, cols], w_ref[gi], preferred_element_type=F32)
        m = y_ref[...] * sc_ref[...]
        o_ref[...] = x_ref[...] + m * _rstd(m) * g_ref[...]

    row = pl.BlockSpec((tp, D), lambda i: (i, 0))
    vec = pl.BlockSpec((1, D), lambda i: (0, 0))
    wsp = pl.BlockSpec((ng, gc, gc), lambda i: (0, 0, 0))
    return _call(body, name=name, out_shape=(_sds((S, D), F32), _sds((S, D), F32)), grid=(S // tp,),
                 in_specs=[row, wsp, vec, row, vec], out_specs=(row, row), dims=("parallel",))(d, wp, scale, x, gpost)


def _pool_mm_bwd(dx, y, d, wp, scale, gpost, name):
    S, D = dx.shape
    ng = len(POOL_WINDOWS)
    gc = D // ng
    tp = _tile(S, 256)

    def body(dx_ref, y_ref, d_ref, w_ref, sc_ref, g_ref, dd_ref, dw_ref, dsc_ref, dg_ref):
        i = pl.program_id(0)

        @pl.when(i == 0)
        def _():
            dw_ref[...] = jnp.zeros_like(dw_ref)
            dsc_ref[...] = jnp.zeros_like(dsc_ref)
            dg_ref[...] = jnp.zeros_like(dg_ref)

        yv = y_ref[...]
        sc = sc_ref[...]
        dm, dgr = _rms_bwd_math(yv * sc, g_ref[...], dx_ref[...])
        dg_ref[...] += jnp.sum(dgr, axis=0, keepdims=True)
        dsc_ref[...] += jnp.sum(dm * yv, axis=0, keepdims=True)
        dyv = (dm * sc).astype(BF16)
        for gi in range(ng):
            cols = slice(gi * gc, (gi + 1) * gc)
            dyg = dyv[:, cols]
            dd_ref[:, cols] = lax.dot_general(dyg, w_ref[gi], (((1,), (1,)), ((), ())), preferred_element_type=F32)
            dw_ref[gi] += lax.dot_general(d_ref[:, cols], dyg, (((0,), (0,)), ((), ())), preferred_element_type=F32)

    row = pl.BlockSpec((tp, D), lambda i: (i, 0))
    vec = pl.BlockSpec((1, D), lambda i: (0, 0))
    wsp = pl.BlockSpec((ng, gc, gc), lambda i: (0, 0, 0))
    dwsp = pl.BlockSpec((ng, gc, gc), lambda i: (0, 0, 0))
    return _call(body, name=name,
                 out_shape=(_sds((S, D), F32), _sds((ng, gc, gc), F32), _sds((1, D), F32), _sds((1, D), F32)),
                 grid=(S // tp,), in_specs=[row, row, row, wsp, vec, vec], out_specs=(row, dwsp, vec, vec),
                 dims=("arbitrary",))(dx, y, d, wp, scale, gpost)


def _pool_bwd(dd, x, g, res, name):
    S, D = x.shape
    gc = D // len(POOL_WINDOWS)
    tp = _tile(S, 256)
    nt = S // tp

    def body(dd_ref, x_ref, g_ref, res_ref, dx_ref, dg_ref, ext_ref, dh_ref):
        i = pl.program_id(0)

        @pl.when(i == 0)
        def _():
            ext_ref[pl.ds(tp, POOL_HALO), :] = jnp.zeros((POOL_HALO, D), F32)
            dg_ref[...] = jnp.zeros_like(dg_ref)

        t = _pool_counts((nt - 1 - i) * tp, tp)
        for gi, w in enumerate(POOL_WINDOWS):
            cols = slice(gi * gc, (gi + 1) * gc)
            ddv = dd_ref[:, cols]
            ext_ref[pl.ds(0, tp), cols] = ddv / jnp.minimum(t + 1, w).astype(F32)
            s = ext_ref[:, cols]
            sh = 1
            while sh < w:
                s = s + pltpu.roll(s, tp + POOL_HALO - sh, 0)
                sh *= 2
            dh_ref[:, cols] = s[:tp] - ddv
        ext_ref[pl.ds(tp, POOL_HALO), :] = ext_ref[pl.ds(0, POOL_HALO), :]
        dx, dgr = _rms_bwd_math(x_ref[...], g_ref[...], dh_ref[...])
        dx_ref[...] = dx + res_ref[...]
        dg_ref[...] += jnp.sum(dgr, axis=0, keepdims=True)

    row = pl.BlockSpec((tp, D), lambda i: (nt - 1 - i, 0))
    vec = pl.BlockSpec((1, D), lambda i: (0, 0))
    return _call(body, name=name, out_shape=(_sds((S, D), F32), _sds((1, D), F32)), grid=(nt,),
                 in_specs=[row, row, vec, row], out_specs=(row, vec),
                 scratch_shapes=[pltpu.VMEM((tp + POOL_HALO, D), F32), pltpu.VMEM((tp, D), F32)],
                 dims=("arbitrary",))(dd, x, g, res)


def _gelu(x):
    return 0.5 * x * (1.0 + jnp.tanh(0.7978845608028654 * (x + 0.044715 * x * x * x)))


def _gelu_grad(x):
    th = jnp.tanh(0.7978845608028654 * (x + 0.044715 * x * x * x))
    return 0.5 * (1.0 + th) + 0.5 * x * (1.0 - th * th) * 0.7978845608028654 * (1.0 + 3.0 * 0.044715 * x * x)


def _conv_taps(ext_ref, cols, tt):
    e = ext_ref[:, cols]
    return e[CONV_HALO:], pltpu.roll(e, 1, 0)[CONV_HALO:], pltpu.roll(e, 2, 0)[CONV_HALO:]


def _conv_glu_fwd(u, cw, cb, name):
    S, F2 = u.shape
    wd = F2 // 4
    tt = _tile(S, 256)

    def body(u_ref, cw_ref, cb_ref, a_ref, ext_ref):
        it = pl.program_id(1)

        @pl.when(it == 0)
        def _():
            ext_ref[pl.ds(0, CONV_HALO), :] = jnp.zeros((CONV_HALO, 2 * wd), F32)

        ext_ref[pl.ds(CONV_HALO, tt), :] = u_ref[...]
        for cc in range(wd // LANES):
            act = []
            for half in range(2):
                cols = slice(half * wd + cc * LANES, half * wd + (cc + 1) * LANES)
                u0, u1, u2 = _conv_taps(ext_ref, cols, tt)
                act.append(cw_ref[2:3, cols] * u0 + cw_ref[1:2, cols] * u1 + cw_ref[0:1, cols] * u2 + cb_ref[:, cols])
            a_ref[:, cc * LANES:(cc + 1) * LANES] = (_gelu(act[0]) * act[1]).astype(a_ref.dtype)
        ext_ref[pl.ds(0, CONV_HALO), :] = ext_ref[pl.ds(tt, CONV_HALO), :]

    return _call(body, name=name, out_shape=_sds((S, F2 // 2), BF16), grid=(2, S // tt),
                 in_specs=[pl.BlockSpec((tt, 2 * wd), lambda h, t: (t, h)), pl.BlockSpec((8, 2 * wd), lambda h, t: (0, h)),
                           pl.BlockSpec((1, 2 * wd), lambda h, t: (0, h))],
                 out_specs=pl.BlockSpec((tt, wd), lambda h, t: (t, h)),
                 scratch_shapes=[pltpu.VMEM((tt + CONV_HALO, 2 * wd), F32)], dims=("parallel", "arbitrary"))(u, cw, cb)


def _conv_glu_bwd1(u, da, cw, cb, name):
    S, F2 = u.shape
    wd = F2 // 4
    tt = _tile(S, 256)

    def body(u_ref, da_ref, cw_ref, cb_ref, duc_ref, acc_ref, ext_ref):
        it = pl.program_id(1)

        @pl.when(it == 0)
        def _():
            ext_ref[pl.ds(0, CONV_HALO), :] = jnp.zeros((CONV_HALO, 2 * wd), F32)
            acc_ref[...] = jnp.zeros_like(acc_ref)

        ext_ref[pl.ds(CONV_HALO, tt), :] = u_ref[...]
        for cc in range(wd // LANES):
            taps, act = [], []
            for half in range(2):
                cols = slice(half * wd + cc * LANES, half * wd + (cc + 1) * LANES)
                u0, u1, u2 = _conv_taps(ext_ref, cols, tt)
                taps.append((u0, u1, u2))
                act.append(cw_ref[2:3, cols] * u0 + cw_ref[1:2, cols] * u1 + cw_ref[0:1, cols] * u2 + cb_ref[:, cols])
            dav = da_ref[:, cc * LANES:(cc + 1) * LANES]
            dact = (dav * act[1] * _gelu_grad(act[0]), dav * _gelu(act[0]))
            for half in range(2):
                cols = slice(half * wd + cc * LANES, half * wd + (cc + 1) * LANES)
                duc_ref[:, cols] = dact[half]
                u0, u1, u2 = taps[half]
                acc_ref[2:3, cols] += jnp.sum(dact[half] * u0, axis=0, keepdims=True)
                acc_ref[1:2, cols] += jnp.sum(dact[half] * u1, axis=0, keepdims=True)
                acc_ref[0:1, cols] += jnp.sum(dact[half] * u2, axis=0, keepdims=True)
                acc_ref[3:4, cols] += jnp.sum(dact[half], axis=0, keepdims=True)
        ext_ref[pl.ds(0, CONV_HALO), :] = ext_ref[pl.ds(tt, CONV_HALO), :]

    wide = pl.BlockSpec((tt, 2 * wd), lambda h, t: (t, h))
    acc = pl.BlockSpec((8, 2 * wd), lambda h, t: (0, h))
    return _call(body, name=name, out_shape=(_sds((S, F2), F32), _sds((8, F2), F32)), grid=(2, S // tt),
                 in_specs=[wide, pl.BlockSpec((tt, wd), lambda h, t: (t, h)), acc, pl.BlockSpec((1, 2 * wd), lambda h, t: (0, h))],
                 out_specs=(wide, acc), scratch_shapes=[pltpu.VMEM((tt + CONV_HALO, 2 * wd), F32)],
                 dims=("parallel", "arbitrary"))(u, da, cw, cb)


def _conv_bwd2(duc, cw, name):
    S, F2 = duc.shape
    wd = F2 // 4
    tt = _tile(S, 256)
    nt = S // tt

    def body(duc_ref, cw_ref, du_ref, ext_ref):
        it = pl.program_id(1)

        @pl.when(it == 0)
        def _():
            ext_ref[pl.ds(tt, CONV_HALO), :] = jnp.zeros((CONV_HALO, 2 * wd), F32)

        ext_ref[pl.ds(0, tt), :] = duc_ref[...]
        for cc in range(2 * wd // LANES):
            cols = slice(cc * LANES, (cc + 1) * LANES)
            e = ext_ref[:, cols]
            n = tt + CONV_HALO
            du = (cw_ref[2:3, cols] * e[:tt] + cw_ref[1:2, cols] * pltpu.roll(e, n - 1, 0)[:tt]
                  + cw_ref[0:1, cols] * pltpu.roll(e, n - 2, 0)[:tt])
            du_ref[:, cols] = du.astype(du_ref.dtype)
        ext_ref[pl.ds(tt, CONV_HALO), :] = ext_ref[pl.ds(0, CONV_HALO), :]

    wide = pl.BlockSpec((tt, 2 * wd), lambda h, t: (nt - 1 - t, h))
    return _call(body, name=name, out_shape=_sds((S, F2), BF16), grid=(2, nt),
                 in_specs=[wide, pl.BlockSpec((8, 2 * wd), lambda h, t: (0, h))], out_specs=wide,
                 scratch_shapes=[pltpu.VMEM((tt + CONV_HALO, 2 * wd), F32)], dims=("parallel", "arbitrary"))(duc, cw)


def _rope_chunk(x, cosv, sinv):
    lane = lax.broadcasted_iota(jnp.int32, x.shape, 1)
    partner = jnp.where(lane % HEAD_DIM < HEAD_DIM // 2, pltpu.roll(x, LANES - HEAD_DIM // 2, 1),
                        pltpu.roll(x, HEAD_DIM // 2, 1))
    return x * cosv + partner * sinv


def _rope(x, width, cos_t, sin_t, name):
    S = x.shape[0]
    tr = _tile(S, 256)

    def body(x_ref, c_ref, s_ref, o_ref):
        for cc in range(width // LANES):
            cols = slice(cc * LANES, (cc + 1) * LANES)
            o_ref[:, cols] = _rope_chunk(x_ref[:, cols], c_ref[...], s_ref[...])

    row = pl.BlockSpec((tr, width), lambda i: (i, 0))
    tab = pl.BlockSpec((tr, LANES), lambda i: (i, 0))
    return _call(body, name=name, out_shape=_sds((S, width), F32), grid=(S // tr,), in_specs=[row, tab, tab],
                 out_specs=row, dims=("parallel",))(x, cos_t, sin_t)


def _attn_mask(n, reps):
    row = lax.broadcasted_iota(jnp.int32, (reps * BLOCK, 2 * BLOCK), 0) & (BLOCK - 1)
    col = lax.broadcasted_iota(jnp.int32, (reps * BLOCK, 2 * BLOCK), 1)
    rel = BLOCK + row - col
    return (rel >= 0) & (rel < WINDOW) & (n * BLOCK + col - BLOCK >= 0)


def _per_head_column(values, reps):
    grp = lax.broadcasted_iota(jnp.int32, (reps * BLOCK, 1), 0) // BLOCK
    col = jnp.zeros((reps * BLOCK, 1), F32)
    for g, v in enumerate(values):
        col = jnp.where(grp == g, v, col)
    return col


def _stack_heads(ref, hk, qpk, keep, scale):
    parts = []
    for g in range(qpk):
        qc, qpar, _, kpar = _head_place(hk * qpk + g, qpk)
        x = ref[:, qc * LANES:(qc + 1) * LANES]
        if scale != 1.0:
            x = x * scale
        if qpar != kpar:
            x = pltpu.roll(x, HEAD_DIM, 1)
        parts.append(jnp.where(keep, x, 0.0).astype(BF16))
    return jnp.concatenate(parts, axis=0)


def _unstack_heads(vals, ref, hk, qpk, lane, dtype):
    pair = None
    for g in range(qpk):
        qc, qpar, _, kpar = _head_place(hk * qpk + g, qpk)
        v = vals[g * BLOCK:(g + 1) * BLOCK]
        if qpar != kpar:
            v = pltpu.roll(v, HEAD_DIM, 1)
        if qpar == 0:
            pair = v
        else:
            ref[:, qc * LANES:(qc + 1) * LANES] = jnp.where(lane < HEAD_DIM, pair, v).astype(dtype)


def _head_place(h, qpk):
    hk = h // qpk
    return h // 2, h % 2, hk // 2, hk % 2


def _attn_specs(S, D):
    nb = S // BLOCK
    kvw = KV_DIM // 2
    qsp = pl.BlockSpec((BLOCK, D), lambda n: (n, 0))
    prev = lambda n: jnp.maximum(n - 1, 0)
    kp = pl.BlockSpec((BLOCK, kvw), lambda n: (prev(n), 0))
    ko = pl.BlockSpec((BLOCK, kvw), lambda n: (n, 0))
    vp = pl.BlockSpec((BLOCK, kvw), lambda n: (prev(n), 1))
    vo = pl.BlockSpec((BLOCK, kvw), lambda n: (n, 1))
    stat = pl.BlockSpec((BLOCK, LANES), lambda n: (n, 0))
    smem = pl.BlockSpec(memory_space=pltpu.SMEM)
    return nb, kvw, qsp, kp, ko, vp, vo, stat, smem


def _attn_fwd(q, k, kv, sinks, name):
    S, D = q.shape
    nh = D // HEAD_DIM
    qpk = nh // N_KV_HEADS
    nb, kvw, qsp, kp, ko, vp, vo, stat, smem = _attn_specs(S, D)

    def body(q_ref, kp_ref, ko_ref, vp_ref, vo_ref, s_ref, o_ref, l_ref):
        n = pl.program_id(0)
        valid = _attn_mask(n, qpk)
        lane = lax.broadcasted_iota(jnp.int32, (BLOCK, LANES), 1)
        lacc = jnp.zeros((BLOCK, LANES), F32)
        for hk in range(N_KV_HEADS):
            kc, kpar = hk // 2, hk % 2
            kcols = slice(kc * LANES, (kc + 1) * LANES)
            k2 = jnp.concatenate([kp_ref[:, kcols], ko_ref[:, kcols]], axis=0).astype(BF16)
            v2 = jnp.concatenate([vp_ref[:, kcols], vo_ref[:, kcols]], axis=0).astype(BF16)
            keep = (lane >= kpar * HEAD_DIM) & (lane < (kpar + 1) * HEAD_DIM)
            qm = _stack_heads(q_ref, hk, qpk, keep, ATTN_SCALE)
            s = lax.dot_general(qm, k2, (((1,), (1,)), ((), ())), preferred_element_type=F32)
            s = jnp.where(valid, s, NEG_INF)
            sink = _per_head_column([s_ref[hk * qpk + g] for g in range(qpk)], qpk)
            m = jnp.maximum(jnp.max(s, axis=1, keepdims=True), sink)
            p = jnp.exp(s - m)
            den = jnp.sum(p, axis=1, keepdims=True) + jnp.exp(sink - m)
            of = jnp.dot(p.astype(BF16), v2, preferred_element_type=F32) / den
            lse = m + jnp.log(den)
            for g in range(qpk):
                lacc = jnp.where(lane == hk * qpk + g, lse[g * BLOCK:(g + 1) * BLOCK], lacc)
            _unstack_heads(of, o_ref, hk, qpk, lane, o_ref.dtype)
        l_ref[...] = lacc

    return _call(body, name=name, out_shape=(_sds((S, D), BF16), _sds((S, LANES), F32)), grid=(nb,),
                 in_specs=[qsp, kp, ko, vp, vo, smem], out_specs=(qsp, stat), dims=("parallel",))(q, k, k, kv, kv, sinks)


def _attn_bwd(q, k, kv, do, lse, sinks, name):
    S, D = q.shape
    nh = D // HEAD_DIM
    qpk = nh // N_KV_HEADS
    nb, kvw, qsp, kp, ko, vp, vo, stat, smem = _attn_specs(S, D)

    def body(q_ref, kp_ref, ko_ref, vp_ref, vo_ref, do_ref, l_ref, s_ref,
             dq_ref, dkp_ref, dko_ref, dvp_ref, dvo_ref, ds_ref):
        n = pl.program_id(0)

        @pl.when(n == 0)
        def _():
            ds_ref[...] = jnp.zeros_like(ds_ref)

        valid = _attn_mask(n, qpk)
        lane = lax.broadcasted_iota(jnp.int32, (BLOCK, LANES), 1)
        lane8 = lax.broadcasted_iota(jnp.int32, (8, LANES), 1)
        lv = l_ref[...]
        dsink = jnp.zeros((8, LANES), F32)
        for kc in range(N_KV_HEADS // 2):
            kcols = slice(kc * LANES, (kc + 1) * LANES)
            k2 = jnp.concatenate([kp_ref[:, kcols], ko_ref[:, kcols]], axis=0).astype(BF16)
            v2 = jnp.concatenate([vp_ref[:, kcols], vo_ref[:, kcols]], axis=0).astype(BF16)
            dk2 = jnp.zeros((2 * BLOCK, LANES), F32)
            dv2 = jnp.zeros((2 * BLOCK, LANES), F32)
            for kpar in range(2):
                hk = 2 * kc + kpar
                heads = [hk * qpk + g for g in range(qpk)]
                keep = (lane >= kpar * HEAD_DIM) & (lane < (kpar + 1) * HEAD_DIM)
                qm = _stack_heads(q_ref, hk, qpk, keep, ATTN_SCALE)
                gm = _stack_heads(do_ref, hk, qpk, keep, 1.0)
                s = lax.dot_general(qm, k2, (((1,), (1,)), ((), ())), preferred_element_type=F32)
                lh = jnp.concatenate([jnp.sum(jnp.where(lane == h, lv, 0.0), axis=1, keepdims=True) for h in heads], axis=0)
                p = jnp.where(valid, jnp.exp(s - lh), 0.0)
                dp = lax.dot_general(gm, v2, (((1,), (1,)), ((), ())), preferred_element_type=F32)
                delta = jnp.sum(p * dp, axis=1, keepdims=True)
                dsb = (p * (dp - delta)).astype(BF16)
                lost = jnp.exp(_per_head_column([s_ref[h] for h in heads], qpk) - lh) * delta
                for g, h in enumerate(heads):
                    dsink = dsink - jnp.where(lane8 == h, jnp.sum(lost[g * BLOCK:(g + 1) * BLOCK]), 0.0)
                dqf = jnp.dot(dsb, k2, preferred_element_type=F32) * ATTN_SCALE
                _unstack_heads(dqf, dq_ref, hk, qpk, lane, F32)
                dk2 = dk2 + lax.dot_general(dsb, qm, (((0,), (0,)), ((), ())), preferred_element_type=F32)
                dv2 = dv2 + lax.dot_general(p.astype(BF16), gm, (((0,), (0,)), ((), ())), preferred_element_type=F32)
            dkp_ref[:, kcols] = dk2[:BLOCK]
            dko_ref[:, kcols] = dk2[BLOCK:]
            dvp_ref[:, kcols] = dv2[:BLOCK]
            dvo_ref[:, kcols] = dv2[BLOCK:]
        ds_ref[...] += dsink

    kvo = pl.BlockSpec((BLOCK, kvw), lambda n: (n, 0))
    acc = pl.BlockSpec((8, LANES), lambda n: (0, 0))
    part = _sds((S, kvw), F32)
    return _call(body, name=name, out_shape=(_sds((S, D), F32), part, part, part, part, _sds((8, LANES), F32)),
                 grid=(nb,), in_specs=[qsp, kp, ko, vp, vo, qsp, stat, smem],
                 out_specs=(qsp, kvo, kvo, kvo, kvo, acc), dims=("arbitrary",))(q, k, k, kv, kv, do, lse, sinks)


def _kv_grad(parts, cos_t, sin_neg_t, name):
    S, kvw = parts[0][0].shape
    nb = S // BLOCK
    flat = [a for p in parts for a in p]
    nl = len(parts)

    def body(*refs):
        c_ref, s_ref, o_ref = refs[4 * nl], refs[4 * nl + 1], refs[4 * nl + 2]
        n = pl.program_id(0)
        last = n == nb - 1
        dk = jnp.zeros((BLOCK, kvw), F32)
        dv = jnp.zeros((BLOCK, kvw), F32)
        for li in range(nl):
            kn, kown, vn, vown = refs[4 * li:4 * li + 4]
            dk = dk + kown[...] + jnp.where(last, 0.0, kn[...])
            dv = dv + vown[...] + jnp.where(last, 0.0, vn[...])
        for cc in range(kvw // LANES):
            cols = slice(cc * LANES, (cc + 1) * LANES)
            o_ref[:, cols] = _rope_chunk(dk[:, cols], c_ref[...], s_ref[...])
        o_ref[:, kvw:] = dv

    own = pl.BlockSpec((BLOCK, kvw), lambda n: (n, 0))
    nxt = pl.BlockSpec((BLOCK, kvw), lambda n: (jnp.minimum(n + 1, nb - 1), 0))
    tab = pl.BlockSpec((BLOCK, LANES), lambda n: (n, 0))
    return _call(body, name=name, out_shape=_sds((S, 2 * kvw), F32), grid=(nb,),
                 in_specs=[nxt, own, nxt, own] * nl + [tab, tab],
                 out_specs=pl.BlockSpec((BLOCK, 2 * kvw), lambda n: (n, 0)), dims=("parallel",))(*flat, cos_t, sin_neg_t)


def _sum_blocks(name, qc, grid, out_shape, out_block, out_imap, ins, out_dtype=F32, into=None):
    nin = len(ins)

    def body(qc_ref, *refs):
        acc = refs[0][...].astype(F32)
        for r in refs[1:nin]:
            acc = acc + r[...].astype(F32)
        refs[-1][...] = acc.astype(refs[-1].dtype)

    in_specs = [pl.BlockSpec(b, m) for _, b, m in ins]
    operands = [a for a, _, _ in ins]
    aliases = None
    if into is not None:
        in_specs.append(ANY)
        operands.append(into)
        aliases = {1 + nin: 0}
    gs = pltpu.PrefetchScalarGridSpec(num_scalar_prefetch=1, grid=grid, in_specs=in_specs,
                                      out_specs=pl.BlockSpec(out_block, out_imap))
    return _call(body, name=name, out_shape=_sds(out_shape, out_dtype), grid_spec=gs,
                 dims=("parallel",) * len(grid), aliases=aliases)(qc, *operands)


def _adamw(w, g, m, v, name):
    shape = w.shape
    C = shape[-1]
    R = w.size // C
    tr = _tile(R, max(8, (1 << 18) // C))

    def body(w_ref, g_ref, m_ref, v_ref, d_ref, nm_ref, nv_ref):
        gv = g_ref[...]
        nm = ADAM_B1 * m_ref[...] + (1.0 - ADAM_B1) * gv
        nv = ADAM_B2 * v_ref[...] + (1.0 - ADAM_B2) * (gv * gv)
        m_hat = nm / (1.0 - ADAM_B1 ** ADAM_STEP)
        v_hat = nv / (1.0 - ADAM_B2 ** ADAM_STEP)
        d_ref[...] = -ADAM_LR * (m_hat / (jnp.sqrt(v_hat) + ADAM_EPS) + ADAM_WD * w_ref[...])
        nm_ref[...] = nm
        nv_ref[...] = nv

    blk = pl.BlockSpec((tr, C), lambda i: (i, 0))
    flat = _sds((R, C), F32)
    outs = _call(body, name=name, out_shape=(flat, flat, flat), grid=(R // tr,), in_specs=[blk] * 4,
                 out_specs=(blk, blk, blk), dims=("parallel",))(*[a.reshape(R, C) for a in (w, g, m, v)])
    return tuple(o.reshape(shape) for o in outs)


def _place():
    x, y, c = lax.axis_index("x"), lax.axis_index("y"), lax.axis_index("c")
    chips = [(1 - x, y), (x, 1 - y), (1 - x, 1 - y)]
    return x, y, c, chips


def _at(ref, nd, dims):
    idx = [slice(None)] * nd
    for d, v in dims.items():
        idx[d] = pl.ds(v[0], v[1]) if isinstance(v, tuple) else v
    return ref.at[tuple(idx)]


def _remote(src, dst, send_sem, recv_sem, dev):
    return pltpu.make_async_remote_copy(src_ref=src, dst_ref=dst, send_sem=send_sem, recv_sem=recv_sem,
                                        device_id=dev, device_id_type=MESH)


def _split_call(body, name, out_shape, in_specs, out_specs, aliases):
    return pl.pallas_call(body, name=name, out_shape=out_shape, in_specs=in_specs, out_specs=out_specs,
                          input_output_aliases=aliases,
                          compiler_params=pltpu.CompilerParams(has_side_effects=EFFECT))


def _hbm(a):
    return pltpu.with_memory_space_constraint(a, pltpu.HBM)


def _copies_start(srcs, lands, after, name, plan, ncopies):
    n, m = len(srcs), len(lands)

    def body(*refs):
        src, land = refs[:n], refs[n:n + m]
        send_sems, recv_sems, token = refs[n + m + 1], refs[n + m + 2], refs[-1]
        x, y, c, chips = _place()
        for k, (s, d, dev) in enumerate(plan(x, y, c, chips, src, land)):
            _remote(s, d, send_sems.at[k], recv_sems.at[k], dev).start()
        token[...] = jnp.zeros_like(token)

    thru = tuple(pltpu.HBM(a.shape, a.dtype) for a in list(srcs) + list(lands))
    outs = _split_call(
        body, name,
        out_shape=(pltpu.SemaphoreType.DMA((ncopies,)), pltpu.SemaphoreType.DMA((ncopies,))) + thru + (_sds((8, LANES), F32),),
        in_specs=(HBM,) * (n + m) + (ANY,), out_specs=(SEM, SEM) + (HBM,) * (n + m) + (VMEM,),
        aliases={i: 2 + i for i in range(n + m)})(*[_hbm(a) for a in srcs], *[_hbm(a) for a in lands], after)
    return dict(send=outs[0], recv=outs[1], srcs=outs[2:2 + n], lands=outs[2 + n:2 + n + m], token=outs[-1])


def _copies_wait(handle, after, name, plan):
    srcs, lands = handle['srcs'], handle['lands']
    n, m = len(srcs), len(lands)

    def body(*refs):
        src, land = refs[:n], refs[n:n + m]
        send_sems, recv_sems = refs[n + m], refs[n + m + 1]
        x, y, c, chips = _place()
        for k, (s, d, dev) in enumerate(plan(x, y, c, chips, src, land)):
            cp = _remote(s, d, send_sems.at[k], recv_sems.at[k], dev)
            cp.wait_send()
            cp.wait_recv()

    thru = tuple(pltpu.HBM(a.shape, a.dtype) for a in list(srcs) + list(lands))
    outs = _split_call(body, name, out_shape=thru, in_specs=(HBM,) * (n + m) + (SEM, SEM, ANY),
                       out_specs=(HBM,) * (n + m), aliases={i: i for i in range(n + m)})(
        *srcs, *lands, handle['send'], handle['recv'], after)
    return outs[:n], outs[n:]


def _gather_plan(x, y, c, chips, src, land, arriving):
    q = 2 * x + y
    peers = [(ch[0], ch[1], c) for ch in chips] + [(x, y, 1 - c)]
    slots = [2 * ch[0] + ch[1] for ch in chips] + [q]
    return [(s, d.at[slots[j] if arriving else q], peers[j]) for s, d in zip(src, land) for j in range(4)]


def _scatter_plan(shard_axes, shapes):
    def plan(x, y, c, chips, src, land):
        out = []
        for s, d, sd, shp in zip(src, land, shard_axes, shapes):
            ss = shp[sd] // N_SHARDS
            for j, ch in enumerate(chips):
                out.append((_at(s, len(shp), {sd: ((2 * ch[0] + ch[1]) * ss, ss)}), d.at[j], (ch[0], ch[1], c)))
        return out
    return plan


def _half_dims(shape, hd, c):
    hs = shape[hd] // 2
    return {hd: (c * hs, hs)}


def _swap_halves(grads, specs, name):
    n = len(grads)
    outs_shape = []
    for a, (sd, hd) in zip(grads, specs):
        shp = list(a.shape)
        shp[hd] //= 2
        outs_shape.append(_sds(shp, F32))

    def body(*refs):
        ins, outs = refs[:n], refs[n:2 * n]
        send_sems, recv_sems = refs[2 * n:]
        x, y, c, _ = _place()
        cps = []
        for ai, (sd, hd) in enumerate(specs):
            shp = grads[ai].shape
            cp = _remote(_at(ins[ai], len(shp), _half_dims(shp, hd, 1 - c)), outs[ai],
                         send_sems.at[ai], recv_sems.at[ai], (x, y, 1 - c))
            cp.start()
            cps.append(cp)
        for cp in cps:
            cp.wait()

    return _call(body, name=name, out_shape=tuple(outs_shape), in_specs=[ANY] * n, out_specs=tuple([ANY] * n),
                 scratch_shapes=[pltpu.SemaphoreType.DMA((n,)), pltpu.SemaphoreType.DMA((n,))])(*grads)


def _share_halves(arrs, half_axes, name):
    n = len(arrs)

    def body(*refs):
        ins, outs = refs[:n], refs[n:2 * n]
        send_sems, recv_sems = refs[2 * n:]
        x, y, c, _ = _place()

        def half(ref, ai, which):
            shp = arrs[ai].shape
            hs = shp[half_axes[ai]] // 2
            return _at(ref, len(shp), {half_axes[ai]: (which * hs, hs)})

        sends = []
        for ai in range(n):
            cp = _remote(half(ins[ai], ai, c), half(outs[ai], ai, c), send_sems.at[ai], recv_sems.at[ai], (x, y, 1 - c))
            cp.start()
            sends.append(cp)
        for ai in range(n):
            land = half(outs[ai], ai, 1 - c)
            _remote(land, land, send_sems.at[ai], recv_sems.at[ai], (x, y, c)).wait_recv()
        for cp in sends:
            cp.wait_send()

    return _call(body, name=name, out_shape=tuple(_sds(a.shape, a.dtype) for a in arrs), in_specs=[ANY] * n,
                 out_specs=tuple([ANY] * n), aliases={i: i for i in range(n)},
                 scratch_shapes=[pltpu.SemaphoreType.DMA((n,)), pltpu.SemaphoreType.DMA((n,))])(*arrs)


def _gather_small(v, name):
    R, C = v.shape

    def body(x_ref, out_ref, send_sems, recv_sems, local_sem):
        x, y, c, chips = _place()
        me, sibling = (x, y, c), (x, y, 1 - c)

        def rows(px, py, pc):
            return out_ref.at[pl.ds((4 * px + 2 * py + pc) * R, R), :]

        def copy(k, block, to, src=None):
            return _remote(rows(*block) if src is None else src, rows(*block), send_sems.at[k], recv_sems.at[k], to)

        mine = pltpu.make_async_copy(x_ref, rows(*me), local_sem)
        mine.start()
        first = [copy(0, me, sibling, src=x_ref)]
        first += [copy(1 + j, me, (ch[0], ch[1], c), src=x_ref) for j, ch in enumerate(chips)]
        for cp in first:
            cp.start()
        passed = [copy(4 + j, (ch[0], ch[1], c), sibling) for j, ch in enumerate(chips)]
        for j, ch in enumerate(chips):
            copy(1 + j, (ch[0], ch[1], c), me).wait_recv()
            passed[j].start()
        copy(0, sibling, me).wait_recv()
        for j, ch in enumerate(chips):
            copy(4 + j, (ch[0], ch[1], 1 - c), me).wait_recv()
        for cp in first + passed:
            cp.wait_send()
        mine.wait()

    vm = pl.BlockSpec(memory_space=pltpu.VMEM)
    return _call(body, name=name, out_shape=_sds((8 * R, C), v.dtype), in_specs=[vm], out_specs=vm,
                 scratch_shapes=[pltpu.SemaphoreType.DMA((7,)), pltpu.SemaphoreType.DMA((7,)),
                                 pltpu.SemaphoreType.DMA])(v)


def _sum8(g, name):
    _, R, C = g.shape

    def body(g_ref, o_ref):
        acc = g_ref[0]
        for d in range(1, 8):
            acc = acc + g_ref[d]
        o_ref[...] = acc

    return _call(body, name=name, out_shape=_sds((R, C), F32), in_specs=[pl.BlockSpec(memory_space=pltpu.VMEM)],
                 out_specs=pl.BlockSpec(memory_space=pltpu.VMEM))(g)


def _rope_tables(positions):
    inv_freq = 1.0 / (ROPE_THETA ** (jnp.arange(0, HEAD_DIM, 2, dtype=F32) / HEAD_DIM))
    ang = positions.astype(F32)[:, None] * inv_freq
    cosv, sinv = jnp.cos(ang), jnp.sin(ang)
    return jnp.tile(cosv, (1, 4)), jnp.tile(jnp.concatenate([-sinv, sinv], axis=1), (1, 2))


def _blocked(a):
    parts = jnp.split(a, 4, axis=-1)
    return jnp.concatenate([parts[0], parts[2], parts[1], parts[3]], axis=-1)


def _local_step(x, target, positions, P, weights_of, grads_done):
    S, D = x.shape
    depth = P['mix_pre_g'].shape[0]
    n_a = depth // 2
    cos_t, sin_t = _rope_tables(positions)
    row = lambda a, l: a[l][None]
    cb = [_blocked(P['ffn_conv_b'][l])[None] for l in range(depth)]
    sv, W = {}, {}
    kv = k_rot = None
    for l in range(depth):
        t = f"l{l}"
        W[l], zero = weights_of(l, x)
        wd = W[l]['w_in'].shape[2]
        sv[l, 'x_in'] = x
        g_pre = row(P['mix_pre_g'], l) + zero
        if l < n_a:
            d = _pool_fwd(x, g_pre, "pool_fwd_" + t)
            y, x = _pool_mm_fwd(d, W[l]['pool_w'], W[l]['pool_scale'], x, row(P['mix_post_g'], l), "pool_mm_fwd_" + t)
            sv[l, 'd'], sv[l, 'y'] = d, y
        else:
            j = l - n_a
            h = _rms_fwd(x, g_pre, BF16, "q_norm_" + t)
            if l == n_a:
                hkv = _rms_fwd(x, P['kv_norm_g'][None], BF16, "kv_norm")
                kv = _matmul(hkv, W[l]['w_kv'], 'nn', F32, "kv_proj", 512, 512, 1024)
                k_rot = _rope(kv, KV_DIM // 2, cos_t, sin_t, "k_rope")
                sv['hkv'] = hkv
            qraw = _matmul(h, W[l]['w_q'], 'nn', F32, "q_proj_" + t, 512, 1024, 1024)
            q = _rope(qraw, D, cos_t, sin_t, "q_rope_" + t)
            o, lse = _attn_fwd(q, k_rot, kv, P['sinks'][j], "attn_fwd_" + t)
            m = _matmul(o, W[l]['w_o'], 'nn', F32, "o_proj_" + t, 512, 1024, 1024)
            x = _res_rms_fwd(x, m, row(P['mix_post_g'], l), "mix_post_" + t)
            sv[l, 'h'], sv[l, 'q'], sv[l, 'o'], sv[l, 'lse'], sv[l, 'm'] = h, q, o, lse, m
        sv[l, 'x1'] = x
        h2 = _rms_fwd(x, row(P['ffn_pre_g'], l), BF16, "ffn_norm_" + t)
        u = _matmul(h2, W[l]['w_in'], 'nn', F32, "ffn_up_" + t, 512, wd, 1024, b_blocks=True)
        a = _conv_glu_fwd(u, W[l]['cw'], cb[l], "ffn_glu_" + t)
        f = _matmul(a, W[l]['w_out'], 'nn', F32, "ffn_down_" + t, 512, 1024, 2816)
        x = _res_rms_fwd(x, f, row(P['ffn_post_g'], l), "ffn_post_" + t)
        sv[l, 'h2'], sv[l, 'u'], sv[l, 'a'], sv[l, 'f'] = h2, u, a, f

    dx, sq = _loss_grad(x, target, "loss")
    kv_parts = []
    zero = 0.0
    for l in reversed(range(depth)):
        t = f"l{l}"
        G = {}
        wd = W[l]['w_in'].shape[2]
        df, G['ffn_post_g'] = _rms_bwd(sv[l, 'f'], row(P['ffn_post_g'], l) + zero, dx, None, "ffn_post_bwd_" + t)
        da = _matmul(df, W[l]['w_out'], 'nt', F32, "ffn_down_dx_" + t, 512, wd, 1024)
        G['ffn_w_out'] = _matmul(sv[l, 'a'], df, 'tn', F32, "ffn_down_dw_" + t, wd, 1024, 1024)
        duc, acc = _conv_glu_bwd1(sv[l, 'u'], da, W[l]['cw'], cb[l], "ffn_glu_bwd_" + t)
        G['ffn_conv_w'] = _blocked(acc[0:3])
        G['ffn_conv_b'] = _blocked(acc[3:4])
        du = _conv_bwd2(duc, W[l]['cw'], "ffn_conv_bwd_" + t)
        dh2 = _matmul(du, W[l]['w_in'], 'nt', F32, "ffn_up_dx_" + t, 512, 1024, wd, b_blocks=True)
        G['ffn_w_in'] = _matmul(sv[l, 'h2'], du, 'tn', F32, "ffn_up_dw_" + t, 1024, wd, 2048, out_perm=True)
        dx, G['ffn_pre_g'] = _rms_bwd(sv[l, 'x1'], row(P['ffn_pre_g'], l), dh2, dx, "ffn_norm_bwd_" + t)
        if l < n_a:
            dd, G['pool_w'], G['pool_scale'], G['mix_post_g'] = _pool_mm_bwd(
                dx, sv[l, 'y'], sv[l, 'd'], W[l]['pool_w'], W[l]['pool_scale'], row(P['mix_post_g'], l), "pool_mm_bwd_" + t)
            dx, G['mix_pre_g'] = _pool_bwd(dd, sv[l, 'x_in'], row(P['mix_pre_g'], l), dx, "pool_bwd_" + t)
        else:
            j = l - n_a
            dm, G['mix_post_g'] = _rms_bwd(sv[l, 'm'], row(P['mix_post_g'], l), dx, None, "mix_post_bwd_" + t)
            do = _matmul(dm, W[l]['w_o'], 'nt', F32, "o_proj_dx_" + t, 512, 1024, 1024)
            G['w_o'] = _matmul(sv[l, 'o'], dm, 'tn', F32, "o_proj_dw_" + t, 1024, 1024, 1024)
            dq, dkn, dko, dvn, dvo, dsk = _attn_bwd(sv[l, 'q'], k_rot, kv, do, sv[l, 'lse'], P['sinks'][j], "attn_bwd_" + t)
            G['sinks'] = dsk[0:1]
            kv_parts.append((dkn, dko, dvn, dvo))
            dqraw = _rope(dq, D, cos_t, -sin_t, "q_rope_bwd_" + t)
            dh = _matmul(dqraw, W[l]['w_q'], 'nt', F32, "q_proj_dx_" + t, 512, 1024, 1024)
            G['w_q'] = _matmul(sv[l, 'h'], dqraw, 'tn', F32, "q_proj_dw_" + t, 1024, 1024, 1024)
            dx, G['mix_pre_g'] = _rms_bwd(sv[l, 'x_in'], row(P['mix_pre_g'], l), dh, dx, "q_norm_bwd_" + t)
            if l == n_a:
                dkv = _kv_grad(kv_parts, cos_t, -sin_t, "kv_grad")
                dhkv = _matmul(dkv, W[l]['w_kv'], 'nt', F32, "kv_proj_dx", 512, 1024, 512)
                G['w_kv'] = _matmul(sv['hkv'], dkv, 'tn', F32, "kv_proj_dw", 1024, 512, 1024)
                dx, G['kv_norm_g'] = _rms_bwd(sv[l, 'x_in'], P['kv_norm_g'][None], dhkv, dx, "kv_norm_bwd")
        zero = grads_done(l, G, dx)
    return sq, dx


SMALL = ['mix_pre_g', 'mix_post_g', 'kv_norm_g', 'sinks', 'ffn_pre_g', 'ffn_post_g', 'ffn_conv_b', 'ffn_conv_w', 'pool_scale']
BIG = {'ffn_w_in': (1, 0, 1), 'ffn_w_out': (0, 1, 2), 'w_q': (0, 1, 2), 'w_o': (0, 1, 2), 'w_kv': (0, 1, 1),
       'pool_w': (1, 0, 1)}


def _reduce_start(pieces, qc, after, tag):
    arrs = [p[0] for p in pieces]
    specs = [(p[1], p[2]) for p in pieces]

    theirs = _swap_halves(arrs, specs, "grad_swap_halves_" + tag)
    sums = []
    for pi, (a, (sd, hd), r) in enumerate(zip(arrs, specs, theirs)):
        shp = r.shape
        nd = len(shp)
        if nd == 3:
            blk, grid = tuple(shp), (1,)
            mine = lambda i, s: (s[1], 0, 0)
            zero = lambda i, s: (0, 0, 0)
        elif hd == 0:
            tr = _tile(shp[0], max(16, (1 << 18) // shp[1]), 16)
            blk, grid = (tr, shp[1]), (shp[0] // tr,)
            nblk = shp[0] // tr
            mine = lambda i, s, nblk=nblk: (s[1] * nblk + i, 0)
            zero = lambda i, s: (i, 0)
        else:
            tr = _tile(shp[0], max(16, (1 << 18) // shp[1]), 16)
            blk, grid = (tr, shp[1]), (shp[0] // tr,)
            mine = lambda i, s: (i, s[1])
            zero = lambda i, s: (i, 0)
        sums.append(_sum_blocks(f"grad_chip_sum_{tag}_{pi}", qc, grid, shp, blk, zero, [(a, blk, mine), (r, blk, zero)],
                                out_dtype=BF16))

    lands = []
    for s_arr, (sd, hd) in zip(sums, specs):
        shp = list(s_arr.shape)
        shp[sd] //= N_SHARDS
        lands.append(lax.empty((3,) + tuple(shp), BF16))
    plan = _scatter_plan([sd for sd, _ in specs], [s.shape for s in sums])
    handle = _copies_start(sums, lands, after, "grad_scatter_start_" + tag, plan, 3 * len(sums))
    return dict(handle=handle, plan=plan, pieces=pieces, tag=tag)


def _reduce_finish(state, after, qc, outs, out_shapes):
    handle, pieces, tag = state['handle'], state['pieces'], state['tag']
    sums, recvd = _copies_wait(handle, after, "grad_scatter_wait_" + tag, state['plan'])
    for pi, ((a, sd, hd, oname, fixed, ohd), s_arr, r) in enumerate(zip(pieces, sums, recvd)):
        shp = r.shape[1:]
        nd = len(shp)
        lead = (fixed[0],) if fixed else ()
        none = (None,) if fixed else ()
        n_stack = out_shapes[oname][0]
        if nd == 3:
            blk, grid = tuple(shp), (1,)
            mine = lambda i, s: (0, s[0], 0)
            rk = [lambda i, s, k=k: (k, 0, 0, 0) for k in range(3)]
            oshape = (n_stack, 2 * shp[0]) + tuple(shp[1:])
            oblk = none + blk
            omap = lambda i, s, lead=lead: lead + (s[1], 0, 0)
        elif sd == 1:
            tr = _tile(shp[0], max(16, (1 << 18) // shp[1]), 16)
            blk, grid = (tr, shp[1]), (shp[0] // tr,)
            nblk = shp[0] // tr
            mine = lambda i, s: (i, s[0])
            rk = [lambda i, s, k=k: (k, i, 0) for k in range(3)]
            oshape = (n_stack, 2 * shp[0], shp[1])
            oblk = none + blk
            omap = lambda i, s, lead=lead, nblk=nblk: lead + (s[1] * nblk + i, 0)
        else:
            tr = _tile(shp[0], max(16, (1 << 18) // shp[1]), 16)
            blk, grid = (tr, shp[1]), (shp[0] // tr,)
            nblk = shp[0] // tr
            mine = lambda i, s, nblk=nblk: (s[0] * nblk + i, 0)
            rk = [lambda i, s, k=k: (k, i, 0) for k in range(3)]
            oshape = ((n_stack,) if fixed else ()) + (shp[0], 2 * shp[1])
            oblk = none + blk
            omap = lambda i, s, lead=lead: lead + (i, s[1])
        assert tuple(oshape) == tuple(out_shapes[oname]), (oname, oshape, out_shapes[oname])
        ins = [(s_arr, blk, mine)] + [(r, (None,) + blk, rk[k]) for k in range(3)]
        outs[oname] = _sum_blocks(f"grad_total_{tag}_{pi}", qc, grid, oshape, oblk, omap, ins, into=outs.get(oname))


def _pack_small(parts):
    rows, offs, r = [], [], 0
    for a in parts:
        flat = a.reshape(-1)
        nr = -(-flat.size // (8 * LANES)) * 8
        rows.append(jnp.pad(flat, (0, nr * LANES - flat.size)).reshape(nr, LANES))
        offs.append((r, nr, a.shape))
        r += nr
    return jnp.concatenate(rows, axis=0), offs


def _unpack_small(packed, offs):
    return [packed[r:r + nr].reshape(-1)[:math.prod(shape)].reshape(shape) for r, nr, shape in offs]


def kernel(x, positions, mix_pre_g, mix_post_g, pool_w, pool_scale, kv_norm_g, w_kv, w_q, w_o, sinks, ffn_pre_g, ffn_post_g, ffn_w_in, ffn_conv_w, ffn_conv_b, ffn_w_out, loss_target, m_mix_pre_g, m_mix_post_g, m_pool_w, m_pool_scale, m_kv_norm_g, m_w_kv, m_w_q, m_w_o, m_sinks, m_ffn_pre_g, m_ffn_post_g, m_ffn_w_in, m_ffn_conv_w, m_ffn_conv_b, m_ffn_w_out, v_mix_pre_g, v_mix_post_g, v_pool_w, v_pool_scale, v_kv_norm_g, v_w_kv, v_w_q, v_w_o, v_sinks, v_ffn_pre_g, v_ffn_post_g, v_ffn_w_in, v_ffn_conv_w, v_ffn_conv_b, v_ffn_w_out):
    w = dict(mix_pre_g=mix_pre_g, mix_post_g=mix_post_g, pool_w=pool_w, pool_scale=pool_scale, kv_norm_g=kv_norm_g,
             w_kv=w_kv, w_q=w_q, w_o=w_o, sinks=sinks, ffn_pre_g=ffn_pre_g, ffn_post_g=ffn_post_g, ffn_w_in=ffn_w_in,
             ffn_conv_w=ffn_conv_w, ffn_conv_b=ffn_conv_b, ffn_w_out=ffn_w_out)
    mom = dict(mix_pre_g=m_mix_pre_g, mix_post_g=m_mix_post_g, pool_w=m_pool_w, pool_scale=m_pool_scale,
               kv_norm_g=m_kv_norm_g, w_kv=m_w_kv, w_q=m_w_q, w_o=m_w_o, sinks=m_sinks, ffn_pre_g=m_ffn_pre_g,
               ffn_post_g=m_ffn_post_g, ffn_w_in=m_ffn_w_in, ffn_conv_w=m_ffn_conv_w, ffn_conv_b=m_ffn_conv_b,
               ffn_w_out=m_ffn_w_out)
    var = dict(mix_pre_g=v_mix_pre_g, mix_post_g=v_mix_post_g, pool_w=v_pool_w, pool_scale=v_pool_scale,
               kv_norm_g=v_kv_norm_g, w_kv=v_w_kv, w_q=v_w_q, w_o=v_w_o, sinks=v_sinks, ffn_pre_g=v_ffn_pre_g,
               ffn_post_g=v_ffn_post_g, ffn_w_in=v_ffn_w_in, ffn_conv_w=v_ffn_conv_w, ffn_conv_b=v_ffn_conv_b,
               ffn_w_out=v_ffn_w_out)
    depth = mix_pre_g.shape[0]
    q_chip = 2 * lax.axis_index("x") + lax.axis_index("y")
    qc = jnp.stack([q_chip, lax.axis_index("c")]).astype(jnp.int32)

    n_a = depth // 2
    D = x.shape[-1]
    gc = pool_w.shape[3]

    def layer_shards(l):
        items = [('w_in', ffn_w_in[l].astype(BF16)), ('w_out', ffn_w_out[l].astype(BF16))]
        if l < n_a:
            items.append(('pool_w', pool_w[l].astype(BF16)))
        else:
            items += [('w_q', w_q[l - n_a].astype(BF16)), ('w_o', w_o[l - n_a].astype(BF16))]
            if l == n_a:
                items.append(('w_kv', w_kv.astype(BF16)))
        if l == 0:
            items += [('conv_w', ffn_conv_w), ('pool_scale', pool_scale)]
        return items

    def start_gather(l, after):
        items = layer_shards(l)
        srcs = [a for _, a in items]
        lands = [lax.empty((N_SHARDS,) + a.shape, a.dtype) for a in srcs]
        plan = functools.partial(_gather_plan, arriving=False)
        return [n for n, _ in items], _copies_start(srcs, lands, after, f"weight_gather_start_l{l}", plan, 4 * len(srcs))

    pending = {0: start_gather(0, x)}
    shared = {}

    def weights_of(l, x_now):
        names, handle = pending.pop(l)
        _, lands = _copies_wait(handle, x_now, f"weight_gather_wait_l{l}", functools.partial(_gather_plan, arriving=True))
        got = dict(zip(names, lands))
        zero = 0.0
        if l + 1 < depth:
            pending[l + 1] = start_gather(l + 1, got['w_in'])
            zero = pending[l + 1][1]['token'][0, 0]
        if l == 0:
            shared['conv_w'] = got['conv_w']
            shared['pool_scale'] = got['pool_scale'].transpose(1, 0, 2).reshape(n_a, D)
        taps = jnp.concatenate([shared['conv_w'][p, l] for p in (0, 2, 1, 3)], axis=-1)
        Wl = dict(w_in=got['w_in'], w_out=got['w_out'].reshape(-1, D), cw=jnp.pad(taps, ((0, 5), (0, 0))))
        if l < n_a:
            Wl['pool_w'] = got['pool_w'].transpose(1, 0, 2, 3).reshape(-1, gc, gc)
            Wl['pool_scale'] = shared['pool_scale'][l][None]
        else:
            Wl['w_q'], Wl['w_o'] = got['w_q'].reshape(D, D), got['w_o'].reshape(D, D)
            if l == n_a:
                Wl['w_kv'] = got['w_kv'].reshape(D, -1)
        return Wl, zero

    big_shapes = {n: w[n].shape for n in BIG}
    big, G, scattering = {}, {}, {}

    def grads_done(l, Gl, dx_now):
        for n, g in Gl.items():
            G[n, l] = g
        fixed = lambda n: {0: (l if n.startswith('ffn') or n == 'pool_w' else l - n_a)} if len(big_shapes[n]) > 2 else {}
        pieces = [(Gl[n], BIG[n][0], BIG[n][1], n, fixed(n), BIG[n][2]) for n in BIG if n in Gl]
        scattering[l] = _reduce_start(pieces, qc, dx_now, f"l{l}")
        if l + 1 in scattering:
            _reduce_finish(scattering.pop(l + 1), dx_now, qc, big, big_shapes)
        return scattering[l]['handle']['token'][0, 0]

    P = {n: w[n] for n in ('mix_pre_g', 'mix_post_g', 'kv_norm_g', 'sinks', 'ffn_pre_g', 'ffn_post_g', 'ffn_conv_b')}
    sq, dx = _local_step(x[0], loss_target[0], positions[0], P, weights_of, grads_done)
    loss = 0.5 / D * lax.psum(jnp.sum(sq), ("x", "y", "c"))
    _reduce_finish(scattering.pop(0), dx, qc, big, big_shapes)
    names = list(big)
    big = dict(zip(names, _share_halves([big[n] for n in names], [BIG[n][2] for n in names], "grad_share_halves")))

    small_local = {
        'mix_pre_g': jnp.concatenate([G['mix_pre_g', l] for l in range(depth)], axis=0),
        'mix_post_g': jnp.concatenate([G['mix_post_g', l] for l in range(depth)], axis=0),
        'kv_norm_g': G['kv_norm_g', n_a][0],
        'sinks': jnp.concatenate([G['sinks', l][:, :sinks.shape[1]] for l in range(n_a, depth)], axis=0),
        'ffn_pre_g': jnp.concatenate([G['ffn_pre_g', l] for l in range(depth)], axis=0),
        'ffn_post_g': jnp.concatenate([G['ffn_post_g', l] for l in range(depth)], axis=0),
        'ffn_conv_b': jnp.concatenate([G['ffn_conv_b', l] for l in range(depth)], axis=0),
        'ffn_conv_w': jnp.stack([G['ffn_conv_w', l] for l in range(depth)], axis=0),
        'pool_scale': jnp.concatenate([G['pool_scale', l] for l in range(n_a)], axis=0),
    }
    packed, offs = _pack_small([small_local[n] for n in SMALL])
    gathered = _gather_small(packed, "small_grad_gather")
    summed = _sum8(gathered.reshape(8, packed.shape[0], LANES), "small_grad_sum")
    grads = dict(zip(SMALL, _unpack_small(summed, offs)))
    wd = ffn_conv_w.shape[2]
    grads['ffn_conv_w'] = lax.dynamic_slice_in_dim(grads['ffn_conv_w'], q_chip * wd, wd, axis=2)
    ps = pool_scale.shape[1]
    grads['pool_scale'] = lax.dynamic_slice_in_dim(grads['pool_scale'], q_chip * ps, ps, axis=1)
    grads.update(big)

    w_small, o_w = _pack_small([w[n] for n in SMALL])
    g_small, _ = _pack_small([grads[n] for n in SMALL])
    m_small, _ = _pack_small([mom[n] for n in SMALL])
    v_small, _ = _pack_small([var[n] for n in SMALL])
    upd = _adamw(w_small, g_small, m_small, v_small, "adamw_small")
    delta, new_m, new_v = ({n: a for n, a in zip(SMALL, _unpack_small(u, o_w))} for u in upd)
    for n in WEIGHTS:
        if n not in SMALL:
            delta[n], new_m[n], new_v[n] = _adamw(w[n], grads[n], mom[n], var[n], "adamw_" + n)

    return (loss, dx[None], *[grads[n] for n in WEIGHTS], *[delta[n] for n in WEIGHTS],
            *[new_m[n] for n in WEIGHTS], *[new_v[n] for n in WEIGHTS])
```

```python
import functools
import math

import jax
import jax.numpy as jnp
from jax import lax
from jax.experimental import pallas as pl
from jax.experimental.pallas import tpu as pltpu

F32 = jnp.float32
BF16 = jnp.bfloat16
MESH = pl.DeviceIdType.MESH
ANY = pl.BlockSpec(memory_space=pl.ANY)
HBM = pl.BlockSpec(memory_space=pltpu.HBM)
VMEM = pl.BlockSpec(memory_space=pltpu.VMEM)
SEM = pl.BlockSpec(memory_space=pltpu.SEMAPHORE)
EFFECT = pltpu.SideEffectType.DATAFLOW_SIDE_EFFECTING

HEAD_DIM = 64
N_KV_HEADS = 4
KV_DIM = 2 * N_KV_HEADS * HEAD_DIM
WINDOW = 128
BLOCK = 128
POOL_WINDOWS = (2, 4, 8, 16)
POOL_HALO = 16
CONV_HALO = 8
ROPE_THETA = 10000.0
ATTN_SCALE = 1.0 / math.sqrt(HEAD_DIM)
NEG_INF = -1e30
RMS_EPS = 1e-6
ADAM_LR, ADAM_B1, ADAM_B2, ADAM_EPS, ADAM_WD, ADAM_STEP = 0.001, 0.9, 0.999, 1e-08, 0.01, 10
N_SHARDS = 4
LANES = 128
VMEM_LIMIT_BYTES = 48 << 20

WEIGHTS = ['mix_pre_g', 'mix_post_g', 'pool_w', 'pool_scale', 'kv_norm_g', 'w_kv', 'w_q', 'w_o', 'sinks',
           'ffn_pre_g', 'ffn_post_g', 'ffn_w_in', 'ffn_conv_w', 'ffn_conv_b', 'ffn_w_out']


def _call(body, *, name, out_shape, grid=None, in_specs=None, out_specs=None, scratch_shapes=(), dims=None,
          grid_spec=None, aliases=None):
    params = pltpu.CompilerParams(dimension_semantics=dims, vmem_limit_bytes=VMEM_LIMIT_BYTES)
    kw = {} if aliases is None else dict(input_output_aliases=aliases)
    if grid_spec is not None:
        return pl.pallas_call(body, name=name, out_shape=out_shape, grid_spec=grid_spec, compiler_params=params, **kw)
    if grid is not None:
        kw['grid'] = grid
    return pl.pallas_call(body, name=name, out_shape=out_shape, in_specs=in_specs, out_specs=out_specs,
                          scratch_shapes=list(scratch_shapes), compiler_params=params, **kw)


def _tile(n, pref, mult=8):
    if n <= pref:
        return n
    for t in range(pref, 0, -1):
        if n % t == 0 and t % mult == 0:
            return t
    raise ValueError((n, pref, mult))


def _sds(shape, dtype):
    return jax.ShapeDtypeStruct(tuple(shape), dtype)


def _perm4(j):
    return (j % 2) * 2 + j // 2


def _matmul(a, b, mode, out_dtype, name, tm, tn, tk, b_blocks=False, out_perm=False):
    a2 = a.shape
    b2 = (b.shape[1], 4 * b.shape[2]) if b_blocks else b.shape
    if mode == 'nn':
        (M, K), (K2, N) = a2, b2
    elif mode == 'nt':
        (M, K), (N, K2) = a2, b2
    else:
        (K, M), (K2, N) = a2, b2
    assert K == K2, (name, a.shape, b.shape)
    tm, tn, tk = _tile(M, tm), _tile(N, tn, LANES), _tile(K, tk, LANES if mode != 'tn' else 16)
    assert M % tm == 0 and N % tn == 0 and K % tk == 0
    nk = K // tk
    grid = (N // tn, M // tm, nk)

    if mode == 'nn':
        a_spec = pl.BlockSpec((tm, tk), lambda j, i, k: (i, k))
        if b_blocks:
            assert tn == b.shape[2]
            b_spec = pl.BlockSpec((None, tk, tn), lambda j, i, k: (_perm4(j), k, 0))
        else:
            b_spec = pl.BlockSpec((tk, tn), lambda j, i, k: (k, j))
        dn = (((1,), (0,)), ((), ()))
    elif mode == 'nt':
        a_spec = pl.BlockSpec((tm, tk), lambda j, i, k: (i, k))
        if b_blocks:
            assert tk == b.shape[2]
            b_spec = pl.BlockSpec((None, tn, tk), lambda j, i, k: (_perm4(k), j, 0))
        else:
            b_spec = pl.BlockSpec((tn, tk), lambda j, i, k: (j, k))
        dn = (((1,), (1,)), ((), ()))
    else:
        a_spec = pl.BlockSpec((tk, tm), lambda j, i, k: (k, i))
        b_spec = pl.BlockSpec((tk, tn), lambda j, i, k: (k, j))
        dn = (((0,), (0,)), ((), ()))
    po = _perm4 if out_perm else (lambda j: j)
    o_spec = pl.BlockSpec((tm, tn), lambda j, i, k: (i, po(j)))

    def body(a_ref, b_ref, o_ref, *acc):
        prod = lax.dot_general(a_ref[...].astype(BF16), b_ref[...].astype(BF16), dn, preferred_element_type=F32)
        if nk == 1:
            o_ref[...] = prod.astype(o_ref.dtype)
        else:
            k = pl.program_id(2)

            @pl.when(k == 0)
            def _():
                acc[0][...] = prod

            @pl.when(k > 0)
            def _():
                acc[0][...] += prod

            @pl.when(k == nk - 1)
            def _():
                o_ref[...] = acc[0][...].astype(o_ref.dtype)

    scratch = [] if nk == 1 else [pltpu.VMEM((tm, tn), F32)]
    return _call(body, name=name, out_shape=_sds((M, N), out_dtype), grid=grid, in_specs=[a_spec, b_spec],
                 out_specs=o_spec, scratch_shapes=scratch, dims=("parallel", "parallel", "arbitrary"))(a, b)


def _rstd(x):
    return lax.rsqrt(jnp.mean(x * x, axis=-1, keepdims=True) + RMS_EPS)


def _rms_fwd(x, g, out_dtype, name):
    S, D = x.shape
    tr = _tile(S, 256)

    def body(x_ref, g_ref, o_ref):
        xv = x_ref[...]
        o_ref[...] = (xv * _rstd(xv) * g_ref[...]).astype(o_ref.dtype)

    row = pl.BlockSpec((tr, D), lambda i: (i, 0))
    vec = pl.BlockSpec((1, D), lambda i: (0, 0))
    return _call(body, name=name, out_shape=_sds((S, D), out_dtype), grid=(S // tr,), in_specs=[row, vec],
                 out_specs=row, dims=("parallel",))(x, g)


def _res_rms_fwd(x, f, g, name):
    S, D = x.shape
    tr = _tile(S, 256)

    def body(x_ref, f_ref, g_ref, o_ref):
        fv = f_ref[...]
        o_ref[...] = x_ref[...] + fv * _rstd(fv) * g_ref[...]

    row = pl.BlockSpec((tr, D), lambda i: (i, 0))
    vec = pl.BlockSpec((1, D), lambda i: (0, 0))
    return _call(body, name=name, out_shape=_sds((S, D), F32), grid=(S // tr,), in_specs=[row, row, vec],
                 out_specs=row, dims=("parallel",))(x, f, g)


def _rms_bwd_math(xin, g, dy):
    r = _rstd(xin)
    xh = xin * r
    gy = dy * g
    dx = r * (gy - xh * jnp.mean(gy * xh, axis=-1, keepdims=True))
    return dx, dy * xh


def _rms_bwd(xin, g, dy, res, name):
    S, D = xin.shape
    tr = _tile(S, 256)
    has_res = res is not None

    def body(*refs):
        if has_res:
            x_ref, g_ref, dy_ref, res_ref, dx_ref, dg_ref = refs
        else:
            x_ref, g_ref, dy_ref, dx_ref, dg_ref = refs
        dx, dgr = _rms_bwd_math(x_ref[...], g_ref[...], dy_ref[...])
        dx_ref[...] = dx + res_ref[...] if has_res else dx
        i = pl.program_id(0)

        @pl.when(i == 0)
        def _():
            dg_ref[...] = jnp.zeros_like(dg_ref)

        dg_ref[...] += jnp.sum(dgr, axis=0, keepdims=True)

    row = pl.BlockSpec((tr, D), lambda i: (i, 0))
    vec = pl.BlockSpec((1, D), lambda i: (0, 0))
    ins = [xin, g, dy] + ([res] if has_res else [])
    in_specs = [row, vec, row] + ([row] if has_res else [])
    return _call(body, name=name, out_shape=(_sds((S, D), F32), _sds((1, D), F32)), grid=(S // tr,),
                 in_specs=in_specs, out_specs=(row, vec), dims=("arbitrary",))(*ins)


def _loss_grad(y, target, name):
    S, D = y.shape
    tr = _tile(S, 256)

    def body(y_ref, t_ref, dy_ref, acc_ref):
        e = y_ref[...] - t_ref[...]
        dy_ref[...] = e * (1.0 / D)
        i = pl.program_id(0)

        @pl.when(i == 0)
        def _():
            acc_ref[...] = jnp.zeros_like(acc_ref)

        acc_ref[...] += jnp.sum(e * e, axis=0, keepdims=True)

    row = pl.BlockSpec((tr, D), lambda i: (i, 0))
    vec = pl.BlockSpec((1, D), lambda i: (0, 0))
    return _call(body, name=name, out_shape=(_sds((S, D), F32), _sds((1, D), F32)), grid=(S // tr,),
                 in_specs=[row, row], out_specs=(row, vec), dims=("arbitrary",))(y, target)


def _pool_counts(t0, rows):
    return t0 + lax.broadcasted_iota(jnp.int32, (rows, 1), 0)


def _pool_fwd(x, g, name):
    S, D = x.shape
    gc = D // len(POOL_WINDOWS)
    tp = _tile(S, 256)

    def body(x_ref, g_ref, d_ref, ext_ref):
        i = pl.program_id(0)

        @pl.when(i == 0)
        def _():
            ext_ref[pl.ds(0, POOL_HALO), :] = jnp.zeros((POOL_HALO, D), F32)

        xv = x_ref[...]
        ext_ref[pl.ds(POOL_HALO, tp), :] = xv * _rstd(xv) * g_ref[...]
        t = _pool_counts(i * tp, tp)
        for gi, w in enumerate(POOL_WINDOWS):
            cols = slice(gi * gc, (gi + 1) * gc)
            s = ext_ref[:, cols]
            h = s[POOL_HALO:]
            sh = 1
            while sh < w:
                s = s + pltpu.roll(s, sh, 0)
                sh *= 2
            cnt = jnp.minimum(t + 1, w).astype(F32)
            d_ref[:, cols] = (s[POOL_HALO:] / cnt - h).astype(d_ref.dtype)
        ext_ref[pl.ds(0, POOL_HALO), :] = ext_ref[pl.ds(tp, POOL_HALO), :]

    row = pl.BlockSpec((tp, D), lambda i: (i, 0))
    vec = pl.BlockSpec((1, D), lambda i: (0, 0))
    return _call(body, name=name, out_shape=_sds((S, D), BF16), grid=(S // tp,), in_specs=[row, vec],
                 out_specs=row, scratch_shapes=[pltpu.VMEM((tp + POOL_HALO, D), F32)], dims=("arbitrary",))(x, g)


def _pool_mm_fwd(d, wp, scale, x, gpost, name):
    S, D = x.shape
    ng = len(POOL_WINDOWS)
    gc = D // ng
    tp = _tile(S, 256)

    def body(d_ref, w_ref, sc_ref, x_ref, g_ref, y_ref, o_ref):
        for gi in range(ng):
            cols = slice(gi * gc, (gi + 1) * gc)
            y_ref[:, cols] = jnp.dot(d_ref[:, cols], w_ref[gi], preferred_element_type=F32)
        m = y_ref[...] * sc_ref[...]
        o_ref[...] = x_ref[...] + m * _rstd(m) * g_ref[...]

    row = pl.BlockSpec((tp, D), lambda i: (i, 0))
    vec = pl.BlockSpec((1, D), lambda i: (0, 0))
    wsp = pl.BlockSpec((ng, gc, gc), lambda i: (0, 0, 0))
    return _call(body, name=name, out_shape=(_sds((S, D), F32), _sds((S, D), F32)), grid=(S // tp,),
                 in_specs=[row, wsp, vec, row, vec], out_specs=(row, row), dims=("parallel",))(d, wp, scale, x, gpost)


def _pool_mm_bwd(dx, y, d, wp, scale, gpost, name):
    S, D = dx.shape
    ng = len(POOL_WINDOWS)
    gc = D // ng
    tp = _tile(S, 256)

    def body(dx_ref, y_ref, d_ref, w_ref, sc_ref, g_ref, dd_ref, dw_ref, dsc_ref, dg_ref):
        i = pl.program_id(0)

        @pl.when(i == 0)
        def _():
            dw_ref[...] = jnp.zeros_like(dw_ref)
            dsc_ref[...] = jnp.zeros_like(dsc_ref)
            dg_ref[...] = jnp.zeros_like(dg_ref)

        yv = y_ref[...]
        sc = sc_ref[...]
        dm, dgr = _rms_bwd_math(yv * sc, g_ref[...], dx_ref[...])
        dg_ref[...] += jnp.sum(dgr, axis=0, keepdims=True)
        dsc_ref[...] += jnp.sum(dm * yv, axis=0, keepdims=True)
        dyv = (dm * sc).astype(BF16)
        for gi in range(ng):
            cols = slice(gi * gc, (gi + 1) * gc)
            dyg = dyv[:, cols]
            dd_ref[:, cols] = lax.dot_general(dyg, w_ref[gi], (((1,), (1,)), ((), ())), preferred_element_type=F32)
            dw_ref[gi] += lax.dot_general(d_ref[:, cols], dyg, (((0,), (0,)), ((), ())), preferred_element_type=F32)

    row = pl.BlockSpec((tp, D), lambda i: (i, 0))
    vec = pl.BlockSpec((1, D), lambda i: (0, 0))
    wsp = pl.BlockSpec((ng, gc, gc), lambda i: (0, 0, 0))
    dwsp = pl.BlockSpec((ng, gc, gc), lambda i: (0, 0, 0))
    return _call(body, name=name,
                 out_shape=(_sds((S, D), F32), _sds((ng, gc, gc), F32), _sds((1, D), F32), _sds((1, D), F32)),
                 grid=(S // tp,), in_specs=[row, row, row, wsp, vec, vec], out_specs=(row, dwsp, vec, vec),
                 dims=("arbitrary",))(dx, y, d, wp, scale, gpost)


def _pool_bwd(dd, x, g, res, name):
    S, D = x.shape
    gc = D // len(POOL_WINDOWS)
    tp = _tile(S, 256)
    nt = S // tp

    def body(dd_ref, x_ref, g_ref, res_ref, dx_ref, dg_ref, ext_ref, dh_ref):
        i = pl.program_id(0)

        @pl.when(i == 0)
        def _():
            ext_ref[pl.ds(tp, POOL_HALO), :] = jnp.zeros((POOL_HALO, D), F32)
            dg_ref[...] = jnp.zeros_like(dg_ref)

        t = _pool_counts((nt - 1 - i) * tp, tp)
        for gi, w in enumerate(POOL_WINDOWS):
            cols = slice(gi * gc, (gi + 1) * gc)
            ddv = dd_ref[:, cols]
            ext_ref[pl.ds(0, tp), cols] = ddv / jnp.minimum(t + 1, w).astype(F32)
            s = ext_ref[:, cols]
            sh = 1
            while sh < w:
                s = s + pltpu.roll(s, tp + POOL_HALO - sh, 0)
                sh *= 2
            dh_ref[:, cols] = s[:tp] - ddv
        ext_ref[pl.ds(tp, POOL_HALO), :] = ext_ref[pl.ds(0, POOL_HALO), :]
        dx, dgr = _rms_bwd_math(x_ref[...], g_ref[...], dh_ref[...])
        dx_ref[...] = dx + res_ref[...]
        dg_ref[...] += jnp.sum(dgr, axis=0, keepdims=True)

    row = pl.BlockSpec((tp, D), lambda i: (nt - 1 - i, 0))
    vec = pl.BlockSpec((1, D), lambda i: (0, 0))
    return _call(body, name=name, out_shape=(_sds((S, D), F32), _sds((1, D), F32)), grid=(nt,),
                 in_specs=[row, row, vec, row], out_specs=(row, vec),
                 scratch_shapes=[pltpu.VMEM((tp + POOL_HALO, D), F32), pltpu.VMEM((tp, D), F32)],
                 dims=("arbitrary",))(dd, x, g, res)


def _gelu(x):
    return 0.5 * x * (1.0 + jnp.tanh(0.7978845608028654 * (x + 0.044715 * x * x * x)))


def _gelu_grad(x):
    th = jnp.tanh(0.7978845608028654 * (x + 0.044715 * x * x * x))
    return 0.5 * (1.0 + th) + 0.5 * x * (1.0 - th * th) * 0.7978845608028654 * (1.0 + 3.0 * 0.044715 * x * x)


def _conv_taps(ext_ref, cols, tt):
    e = ext_ref[:, cols]
    return e[CONV_HALO:], pltpu.roll(e, 1, 0)[CONV_HALO:], pltpu.roll(e, 2, 0)[CONV_HALO:]


def _conv_glu_fwd(u, cw, cb, name):
    S, F2 = u.shape
    wd = F2 // 4
    tt = _tile(S, 256)

    def body(u_ref, cw_ref, cb_ref, a_ref, ext_ref):
        it = pl.program_id(1)

        @pl.when(it == 0)
        def _():
            ext_ref[pl.ds(0, CONV_HALO), :] = jnp.zeros((CONV_HALO, 2 * wd), F32)

        ext_ref[pl.ds(CONV_HALO, tt), :] = u_ref[...]
        for cc in range(wd // LANES):
            act = []
            for half in range(2):
                cols = slice(half * wd + cc * LANES, half * wd + (cc + 1) * LANES)
                u0, u1, u2 = _conv_taps(ext_ref, cols, tt)
                act.append(cw_ref[2:3, cols] * u0 + cw_ref[1:2, cols] * u1 + cw_ref[0:1, cols] * u2 + cb_ref[:, cols])
            a_ref[:, cc * LANES:(cc + 1) * LANES] = (_gelu(act[0]) * act[1]).astype(a_ref.dtype)
        ext_ref[pl.ds(0, CONV_HALO), :] = ext_ref[pl.ds(tt, CONV_HALO), :]

    return _call(body, name=name, out_shape=_sds((S, F2 // 2), BF16), grid=(2, S // tt),
                 in_specs=[pl.BlockSpec((tt, 2 * wd), lambda h, t: (t, h)), pl.BlockSpec((8, 2 * wd), lambda h, t: (0, h)),
                           pl.BlockSpec((1, 2 * wd), lambda h, t: (0, h))],
                 out_specs=pl.BlockSpec((tt, wd), lambda h, t: (t, h)),
                 scratch_shapes=[pltpu.VMEM((tt + CONV_HALO, 2 * wd), F32)], dims=("parallel", "arbitrary"))(u, cw, cb)


def _conv_glu_bwd(u, da, cw, cb, name):
    S, F2 = u.shape
    wd = F2 // 4
    tt = _tile(S, 256)
    nt = S // tt
    n = tt + CONV_HALO

    def body(u_ref, uprev_ref, da_ref, cw_ref, cb_ref, du_ref, acc_ref, ext_ref, carry_ref):
        it = pl.program_id(1)

        @pl.when(it == 0)
        def _():
            carry_ref[...] = jnp.zeros_like(carry_ref)
            acc_ref[...] = jnp.zeros_like(acc_ref)

        @pl.when(it == nt - 1)
        def _():
            ext_ref[pl.ds(0, CONV_HALO), :] = jnp.zeros((CONV_HALO, 2 * wd), F32)

        @pl.when(it < nt - 1)
        def _():
            ext_ref[pl.ds(0, CONV_HALO), :] = uprev_ref[...]

        ext_ref[pl.ds(CONV_HALO, tt), :] = u_ref[...]
        for cc in range(wd // LANES):
            taps, act = [], []
            for half in range(2):
                cols = slice(half * wd + cc * LANES, half * wd + (cc + 1) * LANES)
                u0, u1, u2 = _conv_taps(ext_ref, cols, tt)
                taps.append((u0, u1, u2))
                act.append(cw_ref[2:3, cols] * u0 + cw_ref[1:2, cols] * u1 + cw_ref[0:1, cols] * u2 + cb_ref[:, cols])
            dav = da_ref[:, cc * LANES:(cc + 1) * LANES]
            dact = (dav * act[1] * _gelu_grad(act[0]), dav * _gelu(act[0]))
            for half in range(2):
                cols = slice(half * wd + cc * LANES, half * wd + (cc + 1) * LANES)
                u0, u1, u2 = taps[half]
                acc_ref[2:3, cols] += jnp.sum(dact[half] * u0, axis=0, keepdims=True)
                acc_ref[1:2, cols] += jnp.sum(dact[half] * u1, axis=0, keepdims=True)
                acc_ref[0:1, cols] += jnp.sum(dact[half] * u2, axis=0, keepdims=True)
                acc_ref[3:4, cols] += jnp.sum(dact[half], axis=0, keepdims=True)
                e = jnp.concatenate([dact[half], carry_ref[:, cols]], axis=0)
                du = (cw_ref[2:3, cols] * dact[half] + cw_ref[1:2, cols] * pltpu.roll(e, n - 1, 0)[:tt]
                      + cw_ref[0:1, cols] * pltpu.roll(e, n - 2, 0)[:tt])
                du_ref[:, cols] = du.astype(du_ref.dtype)
                carry_ref[:, cols] = dact[half][:CONV_HALO]

    rev = lambda t: nt - 1 - t
    per8 = tt // CONV_HALO
    wide = pl.BlockSpec((tt, 2 * wd), lambda h, t: (rev(t), h))
    prev = pl.BlockSpec((CONV_HALO, 2 * wd), lambda h, t: (jnp.maximum(rev(t) * per8 - 1, 0), h))
    acc = pl.BlockSpec((8, 2 * wd), lambda h, t: (0, h))
    return _call(body, name=name, out_shape=(_sds((S, F2), BF16), _sds((8, F2), F32)), grid=(2, nt),
                 in_specs=[wide, prev, pl.BlockSpec((tt, wd), lambda h, t: (rev(t), h)), acc,
                           pl.BlockSpec((1, 2 * wd), lambda h, t: (0, h))],
                 out_specs=(wide, acc),
                 scratch_shapes=[pltpu.VMEM((n, 2 * wd), F32), pltpu.VMEM((CONV_HALO, 2 * wd), F32)],
                 dims=("parallel", "arbitrary"))(u, u, da, cw, cb)


def _rope_chunk(x, cosv, sinv):
    lane = lax.broadcasted_iota(jnp.int32, x.shape, 1)
    partner = jnp.where(lane % HEAD_DIM < HEAD_DIM // 2, pltpu.roll(x, LANES - HEAD_DIM // 2, 1),
                        pltpu.roll(x, HEAD_DIM // 2, 1))
    return x * cosv + partner * sinv


def _rope(x, width, cos_t, sin_t, name):
    S = x.shape[0]
    tr = _tile(S, 256)

    def body(x_ref, c_ref, s_ref, o_ref):
        for cc in range(width // LANES):
            cols = slice(cc * LANES, (cc + 1) * LANES)
            o_ref[:, cols] = _rope_chunk(x_ref[:, cols], c_ref[...], s_ref[...])

    row = pl.BlockSpec((tr, width), lambda i: (i, 0))
    tab = pl.BlockSpec((tr, LANES), lambda i: (i, 0))
    return _call(body, name=name, out_shape=_sds((S, width), F32), grid=(S // tr,), in_specs=[row, tab, tab],
                 out_specs=row, dims=("parallel",))(x, cos_t, sin_t)


def _attn_mask(n, reps):
    row = lax.broadcasted_iota(jnp.int32, (reps * BLOCK, 2 * BLOCK), 0) & (BLOCK - 1)
    col = lax.broadcasted_iota(jnp.int32, (reps * BLOCK, 2 * BLOCK), 1)
    rel = BLOCK + row - col
    return (rel >= 0) & (rel < WINDOW) & (n * BLOCK + col - BLOCK >= 0)


def _per_head_column(values, reps):
    grp = lax.broadcasted_iota(jnp.int32, (reps * BLOCK, 1), 0) // BLOCK
    col = jnp.zeros((reps * BLOCK, 1), F32)
    for g, v in enumerate(values):
        col = jnp.where(grp == g, v, col)
    return col


def _stack_heads(ref, hk, qpk, keep, scale):
    parts = []
    for g in range(qpk):
        qc, qpar, _, kpar = _head_place(hk * qpk + g, qpk)
        x = ref[:, qc * LANES:(qc + 1) * LANES]
        if scale != 1.0:
            x = x * scale
        if qpar != kpar:
            x = pltpu.roll(x, HEAD_DIM, 1)
        parts.append(jnp.where(keep, x, 0.0).astype(BF16))
    return jnp.concatenate(parts, axis=0)


def _unstack_heads(vals, ref, hk, qpk, lane, dtype):
    pair = None
    for g in range(qpk):
        qc, qpar, _, kpar = _head_place(hk * qpk + g, qpk)
        v = vals[g * BLOCK:(g + 1) * BLOCK]
        if qpar != kpar:
            v = pltpu.roll(v, HEAD_DIM, 1)
        if qpar == 0:
            pair = v
        else:
            ref[:, qc * LANES:(qc + 1) * LANES] = jnp.where(lane < HEAD_DIM, pair, v).astype(dtype)


def _head_place(h, qpk):
    hk = h // qpk
    return h // 2, h % 2, hk // 2, hk % 2


def _attn_specs(S, D):
    nb = S // BLOCK
    kvw = KV_DIM // 2
    qsp = pl.BlockSpec((BLOCK, D), lambda n: (n, 0))
    prev = lambda n: jnp.maximum(n - 1, 0)
    kp = pl.BlockSpec((BLOCK, kvw), lambda n: (prev(n), 0))
    ko = pl.BlockSpec((BLOCK, kvw), lambda n: (n, 0))
    vp = pl.BlockSpec((BLOCK, kvw), lambda n: (prev(n), 1))
    vo = pl.BlockSpec((BLOCK, kvw), lambda n: (n, 1))
    stat = pl.BlockSpec((BLOCK, LANES), lambda n: (n, 0))
    smem = pl.BlockSpec(memory_space=pltpu.SMEM)
    return nb, kvw, qsp, kp, ko, vp, vo, stat, smem


def _attn_fwd(q, k, kv, sinks, name):
    S, D = q.shape
    nh = D // HEAD_DIM
    qpk = nh // N_KV_HEADS
    nb, kvw, qsp, kp, ko, vp, vo, stat, smem = _attn_specs(S, D)

    def body(q_ref, kp_ref, ko_ref, vp_ref, vo_ref, s_ref, o_ref, l_ref):
        n = pl.program_id(0)
        valid = _attn_mask(n, qpk)
        lane = lax.broadcasted_iota(jnp.int32, (BLOCK, LANES), 1)
        lacc = jnp.zeros((BLOCK, LANES), F32)
        for hk in range(N_KV_HEADS):
            kc, kpar = hk // 2, hk % 2
            kcols = slice(kc * LANES, (kc + 1) * LANES)
            k2 = jnp.concatenate([kp_ref[:, kcols], ko_ref[:, kcols]], axis=0).astype(BF16)
            v2 = jnp.concatenate([vp_ref[:, kcols], vo_ref[:, kcols]], axis=0).astype(BF16)
            keep = (lane >= kpar * HEAD_DIM) & (lane < (kpar + 1) * HEAD_DIM)
            qm = _stack_heads(q_ref, hk, qpk, keep, ATTN_SCALE)
            s = lax.dot_general(qm, k2, (((1,), (1,)), ((), ())), preferred_element_type=F32)
            s = jnp.where(valid, s, NEG_INF)
            sink = _per_head_column([s_ref[hk * qpk + g] for g in range(qpk)], qpk)
            m = jnp.maximum(jnp.max(s, axis=1, keepdims=True), sink)
            p = jnp.exp(s - m)
            den = jnp.sum(p, axis=1, keepdims=True) + jnp.exp(sink - m)
            of = jnp.dot(p.astype(BF16), v2, preferred_element_type=F32) / den
            lse = m + jnp.log(den)
            for g in range(qpk):
                lacc = jnp.where(lane == hk * qpk + g, lse[g * BLOCK:(g + 1) * BLOCK], lacc)
            _unstack_heads(of, o_ref, hk, qpk, lane, o_ref.dtype)
        l_ref[...] = lacc

    return _call(body, name=name, out_shape=(_sds((S, D), BF16), _sds((S, LANES), F32)), grid=(nb,),
                 in_specs=[qsp, kp, ko, vp, vo, smem], out_specs=(qsp, stat), dims=("parallel",))(q, k, k, kv, kv, sinks)


def _attn_bwd(q, k, kv, do, lse, sinks, name):
    S, D = q.shape
    nh = D // HEAD_DIM
    qpk = nh // N_KV_HEADS
    nb, kvw, qsp, kp, ko, vp, vo, stat, smem = _attn_specs(S, D)

    def body(q_ref, kp_ref, ko_ref, vp_ref, vo_ref, do_ref, l_ref, s_ref,
             dq_ref, dkp_ref, dko_ref, dvp_ref, dvo_ref, ds_ref):
        n = pl.program_id(0)

        @pl.when(n == 0)
        def _():
            ds_ref[...] = jnp.zeros_like(ds_ref)

        valid = _attn_mask(n, qpk)
        lane = lax.broadcasted_iota(jnp.int32, (BLOCK, LANES), 1)
        lane8 = lax.broadcasted_iota(jnp.int32, (8, LANES), 1)
        lv = l_ref[...]
        dsink = jnp.zeros((8, LANES), F32)
        for kc in range(N_KV_HEADS // 2):
            kcols = slice(kc * LANES, (kc + 1) * LANES)
            k2 = jnp.concatenate([kp_ref[:, kcols], ko_ref[:, kcols]], axis=0).astype(BF16)
            v2 = jnp.concatenate([vp_ref[:, kcols], vo_ref[:, kcols]], axis=0).astype(BF16)
            dk2 = jnp.zeros((2 * BLOCK, LANES), F32)
            dv2 = jnp.zeros((2 * BLOCK, LANES), F32)
            for kpar in range(2):
                hk = 2 * kc + kpar
                heads = [hk * qpk + g for g in range(qpk)]
                keep = (lane >= kpar * HEAD_DIM) & (lane < (kpar + 1) * HEAD_DIM)
                qm = _stack_heads(q_ref, hk, qpk, keep, ATTN_SCALE)
                gm = _stack_heads(do_ref, hk, qpk, keep, 1.0)
                s = lax.dot_general(qm, k2, (((1,), (1,)), ((), ())), preferred_element_type=F32)
                lh = jnp.concatenate([jnp.sum(jnp.where(lane == h, lv, 0.0), axis=1, keepdims=True) for h in heads], axis=0)
                p = jnp.where(valid, jnp.exp(s - lh), 0.0)
                dp = lax.dot_general(gm, v2, (((1,), (1,)), ((), ())), preferred_element_type=F32)
                delta = jnp.sum(p * dp, axis=1, keepdims=True)
                dsb = (p * (dp - delta)).astype(BF16)
                lost = jnp.exp(_per_head_column([s_ref[h] for h in heads], qpk) - lh) * delta
                for g, h in enumerate(heads):
                    dsink = dsink - jnp.where(lane8 == h, jnp.sum(lost[g * BLOCK:(g + 1) * BLOCK]), 0.0)
                dqf = jnp.dot(dsb, k2, preferred_element_type=F32) * ATTN_SCALE
                _unstack_heads(dqf, dq_ref, hk, qpk, lane, F32)
                dk2 = dk2 + lax.dot_general(dsb, qm, (((0,), (0,)), ((), ())), preferred_element_type=F32)
                dv2 = dv2 + lax.dot_general(p.astype(BF16), gm, (((0,), (0,)), ((), ())), preferred_element_type=F32)
            dkp_ref[:, kcols] = dk2[:BLOCK]
            dko_ref[:, kcols] = dk2[BLOCK:]
            dvp_ref[:, kcols] = dv2[:BLOCK]
            dvo_ref[:, kcols] = dv2[BLOCK:]
        ds_ref[...] += dsink

    kvo = pl.BlockSpec((BLOCK, kvw), lambda n: (n, 0))
    acc = pl.BlockSpec((8, LANES), lambda n: (0, 0))
    part = _sds((S, kvw), F32)
    return _call(body, name=name, out_shape=(_sds((S, D), F32), part, part, part, part, _sds((8, LANES), F32)),
                 grid=(nb,), in_specs=[qsp, kp, ko, vp, vo, qsp, stat, smem],
                 out_specs=(qsp, kvo, kvo, kvo, kvo, acc), dims=("arbitrary",))(q, k, k, kv, kv, do, lse, sinks)


def _kv_grad(parts, cos_t, sin_neg_t, name):
    S, kvw = parts[0][0].shape
    nb = S // BLOCK
    flat = [a for p in parts for a in p]
    nl = len(parts)

    def body(*refs):
        c_ref, s_ref, o_ref = refs[4 * nl], refs[4 * nl + 1], refs[4 * nl + 2]
        n = pl.program_id(0)
        last = n == nb - 1
        dk = jnp.zeros((BLOCK, kvw), F32)
        dv = jnp.zeros((BLOCK, kvw), F32)
        for li in range(nl):
            kn, kown, vn, vown = refs[4 * li:4 * li + 4]
            dk = dk + kown[...] + jnp.where(last, 0.0, kn[...])
            dv = dv + vown[...] + jnp.where(last, 0.0, vn[...])
        for cc in range(kvw // LANES):
            cols = slice(cc * LANES, (cc + 1) * LANES)
            o_ref[:, cols] = _rope_chunk(dk[:, cols], c_ref[...], s_ref[...])
        o_ref[:, kvw:] = dv

    own = pl.BlockSpec((BLOCK, kvw), lambda n: (n, 0))
    nxt = pl.BlockSpec((BLOCK, kvw), lambda n: (jnp.minimum(n + 1, nb - 1), 0))
    tab = pl.BlockSpec((BLOCK, LANES), lambda n: (n, 0))
    return _call(body, name=name, out_shape=_sds((S, 2 * kvw), F32), grid=(nb,),
                 in_specs=[nxt, own, nxt, own] * nl + [tab, tab],
                 out_specs=pl.BlockSpec((BLOCK, 2 * kvw), lambda n: (n, 0)), dims=("parallel",))(*flat, cos_t, sin_neg_t)


def _sum_blocks(name, qc, grid, out_shape, out_block, out_imap, ins, out_dtype=F32, into=None):
    nin = len(ins)

    def body(qc_ref, *refs):
        acc = refs[0][...].astype(F32)
        for r in refs[1:nin]:
            acc = acc + r[...].astype(F32)
        refs[-1][...] = acc.astype(refs[-1].dtype)

    in_specs = [pl.BlockSpec(b, m) for _, b, m in ins]
    operands = [a for a, _, _ in ins]
    aliases = None
    if into is not None:
        in_specs.append(ANY)
        operands.append(into)
        aliases = {1 + nin: 0}
    gs = pltpu.PrefetchScalarGridSpec(num_scalar_prefetch=1, grid=grid, in_specs=in_specs,
                                      out_specs=pl.BlockSpec(out_block, out_imap))
    return _call(body, name=name, out_shape=_sds(out_shape, out_dtype), grid_spec=gs,
                 dims=("parallel",) * len(grid), aliases=aliases)(qc, *operands)


def _adamw(w, g, m, v, name, part=None, into=None):
    shape = w.shape
    C = shape[-1]
    R = w.size // C
    k, cnt, nparts = part if part is not None else (0, 1, 1)
    tr = _tile(R // nparts, max(8, (1 << 18) // C))
    first = k * (R // nparts // tr)
    rows = cnt * (R // nparts)

    def body(w_ref, g_ref, m_ref, v_ref, *outs):
        go_ref, d_ref, nm_ref, nv_ref = outs[-4:]
        gv = g_ref[...]
        nm = ADAM_B1 * m_ref[...] + (1.0 - ADAM_B1) * gv
        nv = ADAM_B2 * v_ref[...] + (1.0 - ADAM_B2) * (gv * gv)
        m_hat = nm / (1.0 - ADAM_B1 ** ADAM_STEP)
        v_hat = nv / (1.0 - ADAM_B2 ** ADAM_STEP)
        go_ref[...] = gv
        d_ref[...] = -ADAM_LR * (m_hat / (jnp.sqrt(v_hat) + ADAM_EPS) + ADAM_WD * w_ref[...])
        nm_ref[...] = nm
        nv_ref[...] = nv

    blk = pl.BlockSpec((tr, C), lambda i: (first + i, 0))
    flat = _sds((R, C), F32)
    operands = [a.reshape(-1, C) for a in (w, g, m, v)]
    in_specs, aliases = [blk] * 4, None
    if g.size != w.size:
        assert g.size == rows * C, (name, g.shape, shape, part)
        in_specs[1] = pl.BlockSpec((tr, C), lambda i: (i, 0))
    if into is not None:
        operands += [a.reshape(R, C) for a in into]
        in_specs = in_specs + [ANY] * 4
        aliases = {4 + i: i for i in range(4)}
    outs = _call(body, name=name, out_shape=(flat,) * 4, grid=(rows // tr,), in_specs=in_specs,
                 out_specs=(blk,) * 4, dims=("parallel",), aliases=aliases)(*operands)
    return tuple(o.reshape(shape) for o in outs)


def _place():
    x, y, c = lax.axis_index("x"), lax.axis_index("y"), lax.axis_index("c")
    chips = [(1 - x, y), (x, 1 - y), (1 - x, 1 - y)]
    return x, y, c, chips


def _at(ref, nd, dims):
    idx = [slice(None)] * nd
    for d, v in dims.items():
        idx[d] = pl.ds(v[0], v[1]) if isinstance(v, tuple) else v
    return ref.at[tuple(idx)]


def _remote(src, dst, send_sem, recv_sem, dev):
    return pltpu.make_async_remote_copy(src_ref=src, dst_ref=dst, send_sem=send_sem, recv_sem=recv_sem,
                                        device_id=dev, device_id_type=MESH)


def _split_call(body, name, out_shape, in_specs, out_specs, aliases):
    return pl.pallas_call(body, name=name, out_shape=out_shape, in_specs=in_specs, out_specs=out_specs,
                          input_output_aliases=aliases,
                          compiler_params=pltpu.CompilerParams(has_side_effects=EFFECT))


def _hbm(a):
    return pltpu.with_memory_space_constraint(a, pltpu.HBM)


def _copies_start(srcs, lands, after, name, plan, ncopies):
    n, m = len(srcs), len(lands)

    def body(*refs):
        src, land = refs[:n], refs[n:n + m]
        send_sems, recv_sems, token = refs[n + m + 1], refs[n + m + 2], refs[-1]
        x, y, c, chips = _place()
        for k, (s, d, dev) in enumerate(plan(x, y, c, chips, src, land)):
            _remote(s, d, send_sems.at[k], recv_sems.at[k], dev).start()
        token[...] = jnp.zeros_like(token)

    thru = tuple(pltpu.HBM(a.shape, a.dtype) for a in list(srcs) + list(lands))
    outs = _split_call(
        body, name,
        out_shape=(pltpu.SemaphoreType.DMA((ncopies,)), pltpu.SemaphoreType.DMA((ncopies,))) + thru + (_sds((8, LANES), F32),),
        in_specs=(HBM,) * (n + m) + (ANY,), out_specs=(SEM, SEM) + (HBM,) * (n + m) + (VMEM,),
        aliases={i: 2 + i for i in range(n + m)})(*[_hbm(a) for a in srcs], *[_hbm(a) for a in lands], after)
    return dict(send=outs[0], recv=outs[1], srcs=outs[2:2 + n], lands=outs[2 + n:2 + n + m], token=outs[-1])


def _copies_wait(handle, after, name, plan):
    srcs, lands = handle['srcs'], handle['lands']
    n, m = len(srcs), len(lands)

    def body(*refs):
        src, land = refs[:n], refs[n:n + m]
        send_sems, recv_sems = refs[n + m], refs[n + m + 1]
        x, y, c, chips = _place()
        for k, (s, d, dev) in enumerate(plan(x, y, c, chips, src, land)):
            cp = _remote(s, d, send_sems.at[k], recv_sems.at[k], dev)
            cp.wait_send()
            cp.wait_recv()

    thru = tuple(pltpu.HBM(a.shape, a.dtype) for a in list(srcs) + list(lands))
    outs = _split_call(body, name, out_shape=thru, in_specs=(HBM,) * (n + m) + (SEM, SEM, ANY),
                       out_specs=(HBM,) * (n + m), aliases={i: i for i in range(n + m)})(
        *srcs, *lands, handle['send'], handle['recv'], after)
    return outs[:n], outs[n:]


def _gather_plan(x, y, c, chips, src, land, arriving):
    q = 2 * x + y
    peers = [(ch[0], ch[1], c) for ch in chips] + [(x, y, 1 - c)]
    slots = [2 * ch[0] + ch[1] for ch in chips] + [q]
    return [(s, d.at[slots[j] if arriving else q], peers[j]) for s, d in zip(src, land) for j in range(4)]


def _scatter_plan(shard_axes, shapes):
    def plan(x, y, c, chips, src, land):
        out = []
        for s, d, sd, shp in zip(src, land, shard_axes, shapes):
            ss = shp[sd] // N_SHARDS
            for j, ch in enumerate(chips):
                out.append((_at(s, len(shp), {sd: ((2 * ch[0] + ch[1]) * ss, ss)}), d.at[j], (ch[0], ch[1], c)))
        return out
    return plan


def _half_dims(shape, hd, c):
    hs = shape[hd] // 2
    return {hd: (c * hs, hs)}


def _swap_halves(grads, specs, name):
    n = len(grads)
    outs_shape = []
    for a, (sd, hd) in zip(grads, specs):
        shp = list(a.shape)
        shp[hd] //= 2
        outs_shape.append(_sds(shp, F32))

    def body(*refs):
        ins, outs = refs[:n], refs[n:2 * n]
        send_sems, recv_sems = refs[2 * n:]
        x, y, c, _ = _place()
        cps = []
        for ai, (sd, hd) in enumerate(specs):
            shp = grads[ai].shape
            cp = _remote(_at(ins[ai], len(shp), _half_dims(shp, hd, 1 - c)), outs[ai],
                         send_sems.at[ai], recv_sems.at[ai], (x, y, 1 - c))
            cp.start()
            cps.append(cp)
        for cp in cps:
            cp.wait()

    return _call(body, name=name, out_shape=tuple(outs_shape), in_specs=[ANY] * n, out_specs=tuple([ANY] * n),
                 scratch_shapes=[pltpu.SemaphoreType.DMA((n,)), pltpu.SemaphoreType.DMA((n,))])(*grads)


def _share_halves(arrs, half_axes, layers, name):
    n = len(arrs)

    def body(*refs):
        ins, outs = refs[:n], refs[n:2 * n]
        send_sems, recv_sems = refs[2 * n:]
        x, y, c, _ = _place()

        def half(ref, ai, which):
            shp = arrs[ai].shape
            hs = shp[half_axes[ai]] // 2
            dims = {half_axes[ai]: (which * hs, hs)}
            if layers[ai] is not None:
                dims[0] = layers[ai]
            return _at(ref, len(shp), dims)

        sends = []
        for ai in range(n):
            cp = _remote(half(ins[ai], ai, c), half(outs[ai], ai, c), send_sems.at[ai], recv_sems.at[ai], (x, y, 1 - c))
            cp.start()
            sends.append(cp)
        for ai in range(n):
            land = half(outs[ai], ai, 1 - c)
            _remote(land, land, send_sems.at[ai], recv_sems.at[ai], (x, y, c)).wait_recv()
        for cp in sends:
            cp.wait_send()

    return _call(body, name=name, out_shape=tuple(_sds(a.shape, a.dtype) for a in arrs), in_specs=[ANY] * n,
                 out_specs=tuple([ANY] * n), aliases={i: i for i in range(n)},
                 scratch_shapes=[pltpu.SemaphoreType.DMA((n,)), pltpu.SemaphoreType.DMA((n,))])(*arrs)


def _gather_small(v, name):
    R, C = v.shape

    def body(x_ref, out_ref, send_sems, recv_sems, local_sem):
        x, y, c, chips = _place()
        me, sibling = (x, y, c), (x, y, 1 - c)

        def rows(px, py, pc):
            return out_ref.at[pl.ds((4 * px + 2 * py + pc) * R, R), :]

        def copy(k, block, to, src=None):
            return _remote(rows(*block) if src is None else src, rows(*block), send_sems.at[k], recv_sems.at[k], to)

        mine = pltpu.make_async_copy(x_ref, rows(*me), local_sem)
        mine.start()
        first = [copy(0, me, sibling, src=x_ref)]
        first += [copy(1 + j, me, (ch[0], ch[1], c), src=x_ref) for j, ch in enumerate(chips)]
        for cp in first:
            cp.start()
        passed = [copy(4 + j, (ch[0], ch[1], c), sibling) for j, ch in enumerate(chips)]
        for j, ch in enumerate(chips):
            copy(1 + j, (ch[0], ch[1], c), me).wait_recv()
            passed[j].start()
        copy(0, sibling, me).wait_recv()
        for j, ch in enumerate(chips):
            copy(4 + j, (ch[0], ch[1], 1 - c), me).wait_recv()
        for cp in first + passed:
            cp.wait_send()
        mine.wait()

    vm = pl.BlockSpec(memory_space=pltpu.VMEM)
    return _call(body, name=name, out_shape=_sds((8 * R, C), v.dtype), in_specs=[vm], out_specs=vm,
                 scratch_shapes=[pltpu.SemaphoreType.DMA((7,)), pltpu.SemaphoreType.DMA((7,)),
                                 pltpu.SemaphoreType.DMA])(v)


def _sum8(g, name):
    _, R, C = g.shape

    def body(g_ref, o_ref):
        acc = g_ref[0]
        for d in range(1, 8):
            acc = acc + g_ref[d]
        o_ref[...] = acc

    return _call(body, name=name, out_shape=_sds((R, C), F32), in_specs=[pl.BlockSpec(memory_space=pltpu.VMEM)],
                 out_specs=pl.BlockSpec(memory_space=pltpu.VMEM))(g)


def _rope_tables(positions):
    inv_freq = 1.0 / (ROPE_THETA ** (jnp.arange(0, HEAD_DIM, 2, dtype=F32) / HEAD_DIM))
    ang = positions.astype(F32)[:, None] * inv_freq
    cosv, sinv = jnp.cos(ang), jnp.sin(ang)
    return jnp.tile(cosv, (1, 4)), jnp.tile(jnp.concatenate([-sinv, sinv], axis=1), (1, 2))


def _blocked(a):
    parts = jnp.split(a, 4, axis=-1)
    return jnp.concatenate([parts[0], parts[2], parts[1], parts[3]], axis=-1)


def _arrived(Wl, name, after):
    if callable(Wl[name]):
        Wl[name] = Wl[name](after)
    return Wl[name]


def _local_step(x, target, positions, P, weights_of, grads_done):
    S, D = x.shape
    depth = P['mix_pre_g'].shape[0]
    n_a = depth // 2
    cos_t, sin_t = _rope_tables(positions)
    row = lambda a, l: a[l][None]
    cb = [_blocked(P['ffn_conv_b'][l])[None] for l in range(depth)]
    sv, W = {}, {}
    kv = k_rot = None
    for l in range(depth):
        t = f"l{l}"
        W[l], zero = weights_of(l, x)
        sv[l, 'x_in'] = x
        g_pre = row(P['mix_pre_g'], l) + zero
        if l < n_a:
            d = _pool_fwd(x, g_pre, "pool_fwd_" + t)
            y, x = _pool_mm_fwd(d, W[l]['pool_w'], W[l]['pool_scale'], x, row(P['mix_post_g'], l), "pool_mm_fwd_" + t)
            sv[l, 'd'], sv[l, 'y'] = d, y
        else:
            j = l - n_a
            h = _rms_fwd(x, g_pre, BF16, "q_norm_" + t)
            if l == n_a:
                hkv = _rms_fwd(x, P['kv_norm_g'][None], BF16, "kv_norm")
                kv = _matmul(hkv, W[l]['w_kv'], 'nn', F32, "kv_proj", 512, 512, 1024)
                k_rot = _rope(kv, KV_DIM // 2, cos_t, sin_t, "k_rope")
                sv['hkv'] = hkv
            qraw = _matmul(h, W[l]['w_q'], 'nn', F32, "q_proj_" + t, 512, 1024, 1024)
            q = _rope(qraw, D, cos_t, sin_t, "q_rope_" + t)
            o, lse = _attn_fwd(q, k_rot, kv, P['sinks'][j], "attn_fwd_" + t)
            m = _matmul(o, W[l]['w_o'], 'nn', F32, "o_proj_" + t, 512, 1024, 1024)
            x = _res_rms_fwd(x, m, row(P['mix_post_g'], l), "mix_post_" + t)
            sv[l, 'h'], sv[l, 'q'], sv[l, 'o'], sv[l, 'lse'], sv[l, 'm'] = h, q, o, lse, m
        sv[l, 'x1'] = x
        h2 = _rms_fwd(x, row(P['ffn_pre_g'], l), BF16, "ffn_norm_" + t)
        w_in = _arrived(W[l], 'w_in', h2)
        u = _matmul(h2, w_in, 'nn', F32, "ffn_up_" + t, 512, w_in.shape[2], 1024, b_blocks=True)
        a = _conv_glu_fwd(u, W[l]['cw'], cb[l], "ffn_glu_" + t)
        f = _matmul(a, _arrived(W[l], 'w_out', a), 'nn', F32, "ffn_down_" + t, 512, 1024, 2816)
        x = _res_rms_fwd(x, f, row(P['ffn_post_g'], l), "ffn_post_" + t)
        sv[l, 'h2'], sv[l, 'u'], sv[l, 'a'], sv[l, 'f'] = h2, u, a, f

    dx, sq = _loss_grad(x, target, "loss")
    kv_parts = []
    zero = 0.0
    for l in reversed(range(depth)):
        t = f"l{l}"
        G = {}
        wd = W[l]['w_in'].shape[2]
        df, G['ffn_post_g'] = _rms_bwd(sv[l, 'f'], row(P['ffn_post_g'], l) + zero, dx, None, "ffn_post_bwd_" + t)
        da = _matmul(df, W[l]['w_out'], 'nt', F32, "ffn_down_dx_" + t, 512, wd, 1024)
        G['ffn_w_out'] = _matmul(sv[l, 'a'], df, 'tn', F32, "ffn_down_dw_" + t, wd, 1024, 1024)
        du, acc = _conv_glu_bwd(sv[l, 'u'], da, W[l]['cw'], cb[l], "ffn_glu_bwd_" + t)
        G['ffn_conv_w'] = _blocked(acc[0:3])
        G['ffn_conv_b'] = _blocked(acc[3:4])
        dh2 = _matmul(du, W[l]['w_in'], 'nt', F32, "ffn_up_dx_" + t, 512, 1024, wd, b_blocks=True)
        G['ffn_w_in'] = _matmul(sv[l, 'h2'], du, 'tn', F32, "ffn_up_dw_" + t, 1024, wd, 2048, out_perm=True)
        dx, G['ffn_pre_g'] = _rms_bwd(sv[l, 'x1'], row(P['ffn_pre_g'], l), dh2, dx, "ffn_norm_bwd_" + t)
        if l < n_a:
            dd, G['pool_w'], G['pool_scale'], G['mix_post_g'] = _pool_mm_bwd(
                dx, sv[l, 'y'], sv[l, 'd'], W[l]['pool_w'], W[l]['pool_scale'], row(P['mix_post_g'], l), "pool_mm_bwd_" + t)
            dx, G['mix_pre_g'] = _pool_bwd(dd, sv[l, 'x_in'], row(P['mix_pre_g'], l), dx, "pool_bwd_" + t)
        else:
            j = l - n_a
            dm, G['mix_post_g'] = _rms_bwd(sv[l, 'm'], row(P['mix_post_g'], l), dx, None, "mix_post_bwd_" + t)
            do = _matmul(dm, W[l]['w_o'], 'nt', F32, "o_proj_dx_" + t, 512, 1024, 1024)
            G['w_o'] = _matmul(sv[l, 'o'], dm, 'tn', F32, "o_proj_dw_" + t, 1024, 1024, 1024)
            dq, dkn, dko, dvn, dvo, dsk = _attn_bwd(sv[l, 'q'], k_rot, kv, do, sv[l, 'lse'], P['sinks'][j], "attn_bwd_" + t)
            G['sinks'] = dsk[0:1]
            kv_parts.append((dkn, dko, dvn, dvo))
            dqraw = _rope(dq, D, cos_t, -sin_t, "q_rope_bwd_" + t)
            dh = _matmul(dqraw, W[l]['w_q'], 'nt', F32, "q_proj_dx_" + t, 512, 1024, 1024)
            G['w_q'] = _matmul(sv[l, 'h'], dqraw, 'tn', F32, "q_proj_dw_" + t, 1024, 1024, 1024)
            dx, G['mix_pre_g'] = _rms_bwd(sv[l, 'x_in'], row(P['mix_pre_g'], l), dh, dx, "q_norm_bwd_" + t)
            if l == n_a:
                dkv = _kv_grad(kv_parts, cos_t, -sin_t, "kv_grad")
                dhkv = _matmul(dkv, W[l]['w_kv'], 'nt', F32, "kv_proj_dx", 512, 1024, 512)
                G['w_kv'] = _matmul(sv['hkv'], dkv, 'tn', F32, "kv_proj_dw", 1024, 512, 1024)
                dx, G['kv_norm_g'] = _rms_bwd(sv[l, 'x_in'], P['kv_norm_g'][None], dhkv, dx, "kv_norm_bwd")
        zero = grads_done(l, G, dx)
    return sq, dx


SMALL = ['mix_pre_g', 'mix_post_g', 'kv_norm_g', 'sinks', 'ffn_pre_g', 'ffn_post_g', 'ffn_conv_b', 'ffn_conv_w', 'pool_scale']
BIG = {'ffn_w_in': (1, 0, 1), 'ffn_w_out': (0, 1, 2), 'w_q': (0, 1, 2), 'w_o': (0, 1, 2), 'w_kv': (0, 1, 1),
       'pool_w': (1, 0, 1)}


def _reduce_start(pieces, qc, after, tag):
    arrs = [p[0] for p in pieces]
    specs = [(p[1], p[2]) for p in pieces]

    theirs = _swap_halves(arrs, specs, "grad_swap_halves_" + tag)
    sums = []
    for pi, (a, (sd, hd), r) in enumerate(zip(arrs, specs, theirs)):
        shp = r.shape
        nd = len(shp)
        if nd == 3:
            blk, grid = tuple(shp), (1,)
            mine = lambda i, s: (s[1], 0, 0)
            zero = lambda i, s: (0, 0, 0)
        elif hd == 0:
            tr = _tile(shp[0], max(16, (1 << 18) // shp[1]), 16)
            blk, grid = (tr, shp[1]), (shp[0] // tr,)
            nblk = shp[0] // tr
            mine = lambda i, s, nblk=nblk: (s[1] * nblk + i, 0)
            zero = lambda i, s: (i, 0)
        else:
            tr = _tile(shp[0], max(16, (1 << 18) // shp[1]), 16)
            blk, grid = (tr, shp[1]), (shp[0] // tr,)
            mine = lambda i, s: (i, s[1])
            zero = lambda i, s: (i, 0)
        sums.append(_sum_blocks(f"grad_chip_sum_{tag}_{pi}", qc, grid, shp, blk, zero, [(a, blk, mine), (r, blk, zero)],
                                out_dtype=BF16))

    lands = []
    for s_arr, (sd, hd) in zip(sums, specs):
        shp = list(s_arr.shape)
        shp[sd] //= N_SHARDS
        lands.append(lax.empty((3,) + tuple(shp), BF16))
    plan = _scatter_plan([sd for sd, _ in specs], [s.shape for s in sums])
    handle = _copies_start(sums, lands, after, "grad_scatter_start_" + tag, plan, 3 * len(sums))
    return dict(handle=handle, plan=plan, pieces=pieces, tag=tag)


def _reduce_finish(state, after, qc, outs, out_shapes):
    handle, pieces, tag = state['handle'], state['pieces'], state['tag']
    sums, recvd = _copies_wait(handle, after, "grad_scatter_wait_" + tag, state['plan'])
    for pi, ((a, sd, hd, oname, fixed, ohd), s_arr, r) in enumerate(zip(pieces, sums, recvd)):
        shp = r.shape[1:]
        nd = len(shp)
        lead = (fixed[0],) if fixed else ()
        none = (None,) if fixed else ()
        n_stack = out_shapes[oname][0]
        if nd == 3:
            blk, grid = tuple(shp), (1,)
            mine = lambda i, s: (0, s[0], 0)
            rk = [lambda i, s, k=k: (k, 0, 0, 0) for k in range(3)]
            oshape = (n_stack, 2 * shp[0]) + tuple(shp[1:])
            oblk = none + blk
            omap = lambda i, s, lead=lead: lead + (s[1], 0, 0)
        elif sd == 1:
            tr = _tile(shp[0], max(16, (1 << 18) // shp[1]), 16)
            blk, grid = (tr, shp[1]), (shp[0] // tr,)
            nblk = shp[0] // tr
            mine = lambda i, s: (i, s[0])
            rk = [lambda i, s, k=k: (k, i, 0) for k in range(3)]
            oshape = (n_stack, 2 * shp[0], shp[1])
            oblk = none + blk
            omap = lambda i, s, lead=lead, nblk=nblk: lead + (s[1] * nblk + i, 0)
        else:
            tr = _tile(shp[0], max(16, (1 << 18) // shp[1]), 16)
            blk, grid = (tr, shp[1]), (shp[0] // tr,)
            nblk = shp[0] // tr
            mine = lambda i, s, nblk=nblk: (s[0] * nblk + i, 0)
            rk = [lambda i, s, k=k: (k, i, 0) for k in range(3)]
            oshape = ((n_stack,) if fixed else ()) + (shp[0], 2 * shp[1])
            oblk = none + blk
            omap = lambda i, s, lead=lead: lead + (i, s[1])
        assert tuple(oshape) == tuple(out_shapes[oname]), (oname, oshape, out_shapes[oname])
        ins = [(s_arr, blk, mine)] + [(r, (None,) + blk, rk[k]) for k in range(3)]
        outs[oname] = _sum_blocks(f"grad_total_{tag}_{pi}", qc, grid, oshape, oblk, omap, ins, into=outs.get(oname))


def _pack_small(parts):
    rows, offs, r = [], [], 0
    for a in parts:
        flat = a.reshape(-1)
        nr = -(-flat.size // (8 * LANES)) * 8
        rows.append(jnp.pad(flat, (0, nr * LANES - flat.size)).reshape(nr, LANES))
        offs.append((r, nr, a.shape))
        r += nr
    return jnp.concatenate(rows, axis=0), offs


def _unpack_small(packed, offs):
    return [packed[r:r + nr].reshape(-1)[:math.prod(shape)].reshape(shape) for r, nr, shape in offs]


def kernel(x, positions, mix_pre_g, mix_post_g, pool_w, pool_scale, kv_norm_g, w_kv, w_q, w_o, sinks, ffn_pre_g, ffn_post_g, ffn_w_in, ffn_conv_w, ffn_conv_b, ffn_w_out, loss_target, m_mix_pre_g, m_mix_post_g, m_pool_w, m_pool_scale, m_kv_norm_g, m_w_kv, m_w_q, m_w_o, m_sinks, m_ffn_pre_g, m_ffn_post_g, m_ffn_w_in, m_ffn_conv_w, m_ffn_conv_b, m_ffn_w_out, v_mix_pre_g, v_mix_post_g, v_pool_w, v_pool_scale, v_kv_norm_g, v_w_kv, v_w_q, v_w_o, v_sinks, v_ffn_pre_g, v_ffn_post_g, v_ffn_w_in, v_ffn_conv_w, v_ffn_conv_b, v_ffn_w_out):
    w = dict(mix_pre_g=mix_pre_g, mix_post_g=mix_post_g, pool_w=pool_w, pool_scale=pool_scale, kv_norm_g=kv_norm_g,
             w_kv=w_kv, w_q=w_q, w_o=w_o, sinks=sinks, ffn_pre_g=ffn_pre_g, ffn_post_g=ffn_post_g, ffn_w_in=ffn_w_in,
             ffn_conv_w=ffn_conv_w, ffn_conv_b=ffn_conv_b, ffn_w_out=ffn_w_out)
    mom = dict(mix_pre_g=m_mix_pre_g, mix_post_g=m_mix_post_g, pool_w=m_pool_w, pool_scale=m_pool_scale,
               kv_norm_g=m_kv_norm_g, w_kv=m_w_kv, w_q=m_w_q, w_o=m_w_o, sinks=m_sinks, ffn_pre_g=m_ffn_pre_g,
               ffn_post_g=m_ffn_post_g, ffn_w_in=m_ffn_w_in, ffn_conv_w=m_ffn_conv_w, ffn_conv_b=m_ffn_conv_b,
               ffn_w_out=m_ffn_w_out)
    var = dict(mix_pre_g=v_mix_pre_g, mix_post_g=v_mix_post_g, pool_w=v_pool_w, pool_scale=v_pool_scale,
               kv_norm_g=v_kv_norm_g, w_kv=v_w_kv, w_q=v_w_q, w_o=v_w_o, sinks=v_sinks, ffn_pre_g=v_ffn_pre_g,
               ffn_post_g=v_ffn_post_g, ffn_w_in=v_ffn_w_in, ffn_conv_w=v_ffn_conv_w, ffn_conv_b=v_ffn_conv_b,
               ffn_w_out=v_ffn_w_out)
    depth = mix_pre_g.shape[0]
    q_chip = 2 * lax.axis_index("x") + lax.axis_index("y")
    qc = jnp.stack([q_chip, lax.axis_index("c")]).astype(jnp.int32)

    n_a = depth // 2
    D = x.shape[-1]
    gc = pool_w.shape[3]

    def layer_shards(l):
        first = []
        if l < n_a:
            first.append(('pool_w', pool_w[l].astype(BF16)))
        else:
            first += [('w_q', w_q[l - n_a].astype(BF16)), ('w_o', w_o[l - n_a].astype(BF16))]
            if l == n_a:
                first.append(('w_kv', w_kv.astype(BF16)))
        ffn = [('w_in', ffn_w_in[l].astype(BF16)), ('w_out', ffn_w_out[l].astype(BF16))]
        if l == 0:
            return [first + [('conv_w', ffn_conv_w), ('pool_scale', pool_scale)], ffn[:1], ffn[1:]]
        return [first + ffn]

    def start_gather(l, after):
        started = []
        for gi, items in enumerate(layer_shards(l)):
            srcs = [a for _, a in items]
            lands = [lax.empty((N_SHARDS,) + a.shape, a.dtype) for a in srcs]
            handle = _copies_start(srcs, lands, after, f"weight_gather_start_l{l}_{gi}",
                                   functools.partial(_gather_plan, arriving=False), 4 * len(srcs))
            started.append(([n for n, _ in items], handle))
            after = handle['token']
        return started

    def wait_gather(l, gi, started, after):
        names, handle = started
        _, lands = _copies_wait(handle, after, f"weight_gather_wait_l{l}_{gi}", functools.partial(_gather_plan, arriving=True))
        return dict(zip(names, lands))

    pending = {0: start_gather(0, x)}
    shared = {}

    def weights_of(l, x_now):
        groups = pending.pop(l)
        got = wait_gather(l, 0, groups[0], x_now)
        lazy = {n: functools.partial(wait_gather, l, gi, grp) for gi, grp in enumerate(groups) if gi > 0 for n in grp[0]}
        zero = 0.0
        if l + 1 < depth:
            pending[l + 1] = start_gather(l + 1, next(iter(got.values())))
            zero = pending[l + 1][-1][1]['token'][0, 0]
        if l == 0:
            shared['conv_w'] = got['conv_w']
            shared['pool_scale'] = got['pool_scale'].transpose(1, 0, 2).reshape(n_a, D)
        taps = jnp.concatenate([shared['conv_w'][p, l] for p in (0, 2, 1, 3)], axis=-1)
        Wl = dict(cw=jnp.pad(taps, ((0, 5), (0, 0))))
        Wl['w_in'] = got['w_in'] if 'w_in' in got else (lambda after: lazy['w_in'](after)['w_in'])
        Wl['w_out'] = (got['w_out'].reshape(-1, D) if 'w_out' in got
                       else (lambda after: lazy['w_out'](after)['w_out'].reshape(-1, D)))
        if l < n_a:
            Wl['pool_w'] = got['pool_w'].transpose(1, 0, 2, 3).reshape(-1, gc, gc)
            Wl['pool_scale'] = shared['pool_scale'][l][None]
        else:
            Wl['w_q'], Wl['w_o'] = got['w_q'].reshape(D, D), got['w_o'].reshape(D, D)
            if l == n_a:
                Wl['w_kv'] = got['w_kv'].reshape(D, -1)
        return Wl, zero

    big_shapes = {n: w[n].shape for n in BIG}
    big, G, scattering = {}, {}, {}

    def grads_done(l, Gl, dx_now):
        for n, g in Gl.items():
            G[n, l] = g
        fixed = lambda n: {0: (l if n.startswith('ffn') or n == 'pool_w' else l - n_a)} if len(big_shapes[n]) > 2 else {}
        pieces = [(Gl[n], BIG[n][0], BIG[n][1], n, fixed(n), BIG[n][2]) for n in BIG if n in Gl]
        scattering[l] = _reduce_start(pieces, qc, dx_now, f"l{l}")
        if l + 1 in scattering:
            _reduce_finish(scattering.pop(l + 1), dx_now, qc, big, big_shapes)
        return scattering[l]['handle']['token'][0, 0]

    P = {n: w[n] for n in ('mix_pre_g', 'mix_post_g', 'kv_norm_g', 'sinks', 'ffn_pre_g', 'ffn_post_g', 'ffn_conv_b')}
    sq, dx = _local_step(x[0], loss_target[0], positions[0], P, weights_of, grads_done)
    loss = 0.5 / D * lax.psum(jnp.sum(sq), ("x", "y", "c"))

    late = {'ffn_w_in': (1, depth - 1), 'ffn_w_out': (1, depth - 1), 'pool_w': (1, n_a - 1), 'w_q': None, 'w_o': None,
            'w_kv': None}
    names = list(big)
    whole = _share_halves([big[n] for n in names], [BIG[n][2] for n in names], [late[n] for n in names],
                          "grad_share_halves_late")
    upd = {}
    for n, g in zip(names, whole):
        part = None if late[n] is None else late[n] + (w[n].shape[0],)
        upd[n] = _adamw(w[n], g, mom[n], var[n], "adamw_late_" + n, part=part)

    small_local = {
        'mix_pre_g': jnp.concatenate([G['mix_pre_g', l] for l in range(depth)], axis=0),
        'mix_post_g': jnp.concatenate([G['mix_post_g', l] for l in range(depth)], axis=0),
        'kv_norm_g': G['kv_norm_g', n_a][0],
        'sinks': jnp.concatenate([G['sinks', l][:, :sinks.shape[1]] for l in range(n_a, depth)], axis=0),
        'ffn_pre_g': jnp.concatenate([G['ffn_pre_g', l] for l in range(depth)], axis=0),
        'ffn_post_g': jnp.concatenate([G['ffn_post_g', l] for l in range(depth)], axis=0),
        'ffn_conv_b': jnp.concatenate([G['ffn_conv_b', l] for l in range(depth)], axis=0),
        'ffn_conv_w': jnp.stack([G['ffn_conv_w', l] for l in range(depth)], axis=0),
        'pool_scale': jnp.concatenate([G['pool_scale', l] for l in range(n_a)], axis=0),
    }
    packed, offs = _pack_small([small_local[n] for n in SMALL])
    gathered = _gather_small(packed, "small_grad_gather")
    summed = _sum8(gathered.reshape(8, packed.shape[0], LANES), "small_grad_sum")
    grads = dict(zip(SMALL, _unpack_small(summed, offs)))
    wd = ffn_conv_w.shape[2]
    grads['ffn_conv_w'] = lax.dynamic_slice_in_dim(grads['ffn_conv_w'], q_chip * wd, wd, axis=2)
    ps = pool_scale.shape[1]
    grads['pool_scale'] = lax.dynamic_slice_in_dim(grads['pool_scale'], q_chip * ps, ps, axis=1)

    w_small, o_w = _pack_small([w[n] for n in SMALL])
    g_small, _ = _pack_small([grads[n] for n in SMALL])
    m_small, _ = _pack_small([mom[n] for n in SMALL])
    v_small, _ = _pack_small([var[n] for n in SMALL])
    upd_small = _adamw(w_small, g_small, m_small, v_small, "adamw_small")
    delta, new_m, new_v = ({n: a for n, a in zip(SMALL, _unpack_small(u, o_w))} for u in upd_small[1:])

    first, shapes0 = {}, {n: (1,) + w[n].shape[1:] for n in ('ffn_w_in', 'ffn_w_out', 'pool_w')}
    _reduce_finish(scattering.pop(0), upd_small[1], qc, first, shapes0)
    names0 = list(first)
    whole0 = _share_halves([first[n] for n in names0], [BIG[n][2] for n in names0], [None] * len(names0),
                           "grad_share_halves_l0")
    for n, g in zip(names0, whole0):
        upd[n] = _adamw(w[n], g, mom[n], var[n], "adamw_l0_" + n, part=(0, 1, w[n].shape[0]), into=upd[n])
    for n in upd:
        grads[n], delta[n], new_m[n], new_v[n] = upd[n]

    return (loss, dx[None], *[grads[n] for n in WEIGHTS], *[delta[n] for n in WEIGHTS],
            *[new_m[n] for n in WEIGHTS], *[new_v[n] for n in WEIGHTS])
```

```python
import functools
import math

import jax
import jax.numpy as jnp
from jax import lax
from jax.experimental import pallas as pl
from jax.experimental.pallas import tpu as pltpu

F32 = jnp.float32
BF16 = jnp.bfloat16
MESH = pl.DeviceIdType.MESH
ANY = pl.BlockSpec(memory_space=pl.ANY)
HBM = pl.BlockSpec(memory_space=pltpu.HBM)
VMEM = pl.BlockSpec(memory_space=pltpu.VMEM)
SEM = pl.BlockSpec(memory_space=pltpu.SEMAPHORE)
EFFECT = pltpu.SideEffectType.DATAFLOW_SIDE_EFFECTING

HEAD_DIM = 64
N_KV_HEADS = 4
KV_DIM = 2 * N_KV_HEADS * HEAD_DIM
WINDOW = 128
BLOCK = 128
POOL_WINDOWS = (2, 4, 8, 16)
POOL_HALO = 16
CONV_HALO = 8
ROPE_THETA = 10000.0
ATTN_SCALE = 1.0 / math.sqrt(HEAD_DIM)
NEG_INF = -1e30
RMS_EPS = 1e-6
ADAM_LR, ADAM_B1, ADAM_B2, ADAM_EPS, ADAM_WD, ADAM_STEP = 0.001, 0.9, 0.999, 1e-08, 0.01, 10
N_SHARDS = 4
LANES = 128
VMEM_LIMIT_BYTES = 48 << 20

WEIGHTS = ['mix_pre_g', 'mix_post_g', 'pool_w', 'pool_scale', 'kv_norm_g', 'w_kv', 'w_q', 'w_o', 'sinks',
           'ffn_pre_g', 'ffn_post_g', 'ffn_w_in', 'ffn_conv_w', 'ffn_conv_b', 'ffn_w_out']


def _call(body, *, name, out_shape, grid=None, in_specs=None, out_specs=None, scratch_shapes=(), dims=None,
          grid_spec=None, aliases=None):
    params = pltpu.CompilerParams(dimension_semantics=dims, vmem_limit_bytes=VMEM_LIMIT_BYTES)
    kw = {} if aliases is None else dict(input_output_aliases=aliases)
    if grid_spec is not None:
        return pl.pallas_call(body, name=name, out_shape=out_shape, grid_spec=grid_spec, compiler_params=params, **kw)
    if grid is not None:
        kw['grid'] = grid
    return pl.pallas_call(body, name=name, out_shape=out_shape, in_specs=in_specs, out_specs=out_specs,
                          scratch_shapes=list(scratch_shapes), compiler_params=params, **kw)


def _tile(n, pref, mult=8):
    if n <= pref:
        return n
    for t in range(pref, 0, -1):
        if n % t == 0 and t % mult == 0:
            return t
    raise ValueError((n, pref, mult))


def _sds(shape, dtype):
    return jax.ShapeDtypeStruct(tuple(shape), dtype)


def _perm4(j):
    return (j % 2) * 2 + j // 2


def _matmul(a, b, mode, out_dtype, name, tm, tn, tk, b_blocks=False, out_perm=False):
    a2 = a.shape
    b2 = (b.shape[1], 4 * b.shape[2]) if b_blocks else b.shape
    if mode == 'nn':
        (M, K), (K2, N) = a2, b2
    elif mode == 'nt':
        (M, K), (N, K2) = a2, b2
    else:
        (K, M), (K2, N) = a2, b2
    assert K == K2, (name, a.shape, b.shape)
    tm, tn, tk = _tile(M, tm), _tile(N, tn, LANES), _tile(K, tk, LANES if mode != 'tn' else 16)
    assert M % tm == 0 and N % tn == 0 and K % tk == 0
    nk = K // tk
    grid = (N // tn, M // tm, nk)

    if mode == 'nn':
        a_spec = pl.BlockSpec((tm, tk), lambda j, i, k: (i, k))
        if b_blocks:
            assert tn == b.shape[2]
            b_spec = pl.BlockSpec((None, tk, tn), lambda j, i, k: (_perm4(j), k, 0))
        else:
            b_spec = pl.BlockSpec((tk, tn), lambda j, i, k: (k, j))
        dn = (((1,), (0,)), ((), ()))
    elif mode == 'nt':
        a_spec = pl.BlockSpec((tm, tk), lambda j, i, k: (i, k))
        if b_blocks:
            assert tk == b.shape[2]
            b_spec = pl.BlockSpec((None, tn, tk), lambda j, i, k: (_perm4(k), j, 0))
        else:
            b_spec = pl.BlockSpec((tn, tk), lambda j, i, k: (j, k))
        dn = (((1,), (1,)), ((), ()))
    else:
        a_spec = pl.BlockSpec((tk, tm), lambda j, i, k: (k, i))
        b_spec = pl.BlockSpec((tk, tn), lambda j, i, k: (k, j))
        dn = (((0,), (0,)), ((), ()))
    po = _perm4 if out_perm else (lambda j: j)
    o_spec = pl.BlockSpec((tm, tn), lambda j, i, k: (i, po(j)))

    def body(a_ref, b_ref, o_ref, *acc):
        prod = lax.dot_general(a_ref[...].astype(BF16), b_ref[...].astype(BF16), dn, preferred_element_type=F32)
        if nk == 1:
            o_ref[...] = prod.astype(o_ref.dtype)
        else:
            k = pl.program_id(2)

            @pl.when(k == 0)
            def _():
                acc[0][...] = prod

            @pl.when(k > 0)
            def _():
                acc[0][...] += prod

            @pl.when(k == nk - 1)
            def _():
                o_ref[...] = acc[0][...].astype(o_ref.dtype)

    scratch = [] if nk == 1 else [pltpu.VMEM((tm, tn), F32)]
    return _call(body, name=name, out_shape=_sds((M, N), out_dtype), grid=grid, in_specs=[a_spec, b_spec],
                 out_specs=o_spec, scratch_shapes=scratch, dims=("parallel", "parallel", "arbitrary"))(a, b)


def _rstd(x):
    return lax.rsqrt(jnp.mean(x * x, axis=-1, keepdims=True) + RMS_EPS)


def _rms_fwd(x, g, out_dtype, name):
    S, D = x.shape
    tr = _tile(S, 256)

    def body(x_ref, g_ref, o_ref):
        xv = x_ref[...]
        o_ref[...] = (xv * _rstd(xv) * g_ref[...]).astype(o_ref.dtype)

    row = pl.BlockSpec((tr, D), lambda i: (i, 0))
    vec = pl.BlockSpec((1, D), lambda i: (0, 0))
    return _call(body, name=name, out_shape=_sds((S, D), out_dtype), grid=(S // tr,), in_specs=[row, vec],
                 out_specs=row, dims=("parallel",))(x, g)


def _res_rms_fwd(x, f, g, name):
    S, D = x.shape
    tr = _tile(S, 256)

    def body(x_ref, f_ref, g_ref, o_ref):
        fv = f_ref[...]
        o_ref[...] = x_ref[...] + fv * _rstd(fv) * g_ref[...]

    row = pl.BlockSpec((tr, D), lambda i: (i, 0))
    vec = pl.BlockSpec((1, D), lambda i: (0, 0))
    return _call(body, name=name, out_shape=_sds((S, D), F32), grid=(S // tr,), in_specs=[row, row, vec],
                 out_specs=row, dims=("parallel",))(x, f, g)


def _rms_bwd_math(xin, g, dy):
    r = _rstd(xin)
    xh = xin * r
    gy = dy * g
    dx = r * (gy - xh * jnp.mean(gy * xh, axis=-1, keepdims=True))
    return dx, dy * xh


def _rms_bwd(xin, g, dy, res, name):
    S, D = xin.shape
    tr = _tile(S, 256)
    has_res = res is not None

    def body(*refs):
        if has_res:
            x_ref, g_ref, dy_ref, res_ref, dx_ref, dg_ref = refs
        else:
            x_ref, g_ref, dy_ref, dx_ref, dg_ref = refs
        dx, dgr = _rms_bwd_math(x_ref[...], g_ref[...], dy_ref[...])
        dx_ref[...] = dx + res_ref[...] if has_res else dx
        i = pl.program_id(0)

        @pl.when(i == 0)
        def _():
            dg_ref[...] = jnp.zeros_like(dg_ref)

        dg_ref[...] += jnp.sum(dgr, axis=0, keepdims=True)

    row = pl.BlockSpec((tr, D), lambda i: (i, 0))
    vec = pl.BlockSpec((1, D), lambda i: (0, 0))
    ins = [xin, g, dy] + ([res] if has_res else [])
    in_specs = [row, vec, row] + ([row] if has_res else [])
    return _call(body, name=name, out_shape=(_sds((S, D), F32), _sds((1, D), F32)), grid=(S // tr,),
                 in_specs=in_specs, out_specs=(row, vec), dims=("arbitrary",))(*ins)


def _loss_grad(y, target, name):
    S, D = y.shape
    tr = _tile(S, 256)

    def body(y_ref, t_ref, dy_ref, acc_ref):
        e = y_ref[...] - t_ref[...]
        dy_ref[...] = e * (1.0 / D)
        i = pl.program_id(0)

        @pl.when(i == 0)
        def _():
            acc_ref[...] = jnp.zeros_like(acc_ref)

        acc_ref[...] += jnp.sum(e * e, axis=0, keepdims=True)

    row = pl.BlockSpec((tr, D), lambda i: (i, 0))
    vec = pl.BlockSpec((1, D), lambda i: (0, 0))
    return _call(body, name=name, out_shape=(_sds((S, D), F32), _sds((1, D), F32)), grid=(S // tr,),
                 in_specs=[row, row], out_specs=(row, vec), dims=("arbitrary",))(y, target)


def _pool_counts(t0, rows):
    return t0 + lax.broadcasted_iota(jnp.int32, (rows, 1), 0)


def _pool_fwd(x, g, name):
    S, D = x.shape
    gc = D // len(POOL_WINDOWS)
    tp = _tile(S, 256)

    def body(x_ref, g_ref, d_ref, ext_ref):
        i = pl.program_id(0)

        @pl.when(i == 0)
        def _():
            ext_ref[pl.ds(0, POOL_HALO), :] = jnp.zeros((POOL_HALO, D), F32)

        xv = x_ref[...]
        ext_ref[pl.ds(POOL_HALO, tp), :] = xv * _rstd(xv) * g_ref[...]
        t = _pool_counts(i * tp, tp)
        for gi, w in enumerate(POOL_WINDOWS):
            cols = slice(gi * gc, (gi + 1) * gc)
            s = ext_ref[:, cols]
            h = s[POOL_HALO:]
            sh = 1
            while sh < w:
                s = s + pltpu.roll(s, sh, 0)
                sh *= 2
            cnt = jnp.minimum(t + 1, w).astype(F32)
            d_ref[:, cols] = (s[POOL_HALO:] / cnt - h).astype(d_ref.dtype)
        ext_ref[pl.ds(0, POOL_HALO), :] = ext_ref[pl.ds(tp, POOL_HALO), :]

    row = pl.BlockSpec((tp, D), lambda i: (i, 0))
    vec = pl.BlockSpec((1, D), lambda i: (0, 0))
    return _call(body, name=name, out_shape=_sds((S, D), BF16), grid=(S // tp,), in_specs=[row, vec],
                 out_specs=row, scratch_shapes=[pltpu.VMEM((tp + POOL_HALO, D), F32)], dims=("arbitrary",))(x, g)


def _pool_mm_fwd(d, wp, scale, x, gpost, name):
    S, D = x.shape
    ng = len(POOL_WINDOWS)
    gc = D // ng
    tp = _tile(S, 256)

    def body(d_ref, w_ref, sc_ref, x_ref, g_ref, y_ref, o_ref):
        for gi in range(ng):
            cols = slice(gi * gc, (gi + 1) * gc)
            y_ref[:, cols] = jnp.dot(d_ref[:, cols], w_ref[gi], preferred_element_type=F32)
        m = y_ref[...] * sc_ref[...]
        o_ref[...] = x_ref[...] + m * _rstd(m) * g_ref[...]

    row = pl.BlockSpec((tp, D), lambda i: (i, 0))
    vec = pl.BlockSpec((1, D), lambda i: (0, 0))
    wsp = pl.BlockSpec((ng, gc, gc), lambda i: (0, 0, 0))
    return _call(body, name=name, out_shape=(_sds((S, D), F32), _sds((S, D), F32)), grid=(S // tp,),
                 in_specs=[row, wsp, vec, row, vec], out_specs=(row, row), dims=("parallel",))(d, wp, scale, x, gpost)


def _pool_mm_bwd(dx, y, d, wp, scale, gpost, name):
    S, D = dx.shape
    ng = len(POOL_WINDOWS)
    gc = D // ng
    tp = _tile(S, 256)

    def body(dx_ref, y_ref, d_ref, w_ref, sc_ref, g_ref, dd_ref, dw_ref, dsc_ref, dg_ref):
        i = pl.program_id(0)

        @pl.when(i == 0)
        def _():
            dw_ref[...] = jnp.zeros_like(dw_ref)
            dsc_ref[...] = jnp.zeros_like(dsc_ref)
            dg_ref[...] = jnp.zeros_like(dg_ref)

        yv = y_ref[...]
        sc = sc_ref[...]
        dm, dgr = _rms_bwd_math(yv * sc, g_ref[...], dx_ref[...])
        dg_ref[...] += jnp.sum(dgr, axis=0, keepdims=True)
        dsc_ref[...] += jnp.sum(dm * yv, axis=0, keepdims=True)
        dyv = (dm * sc).astype(BF16)
        for gi in range(ng):
            cols = slice(gi * gc, (gi + 1) * gc)
            dyg = dyv[:, cols]
            dd_ref[:, cols] = lax.dot_general(dyg, w_ref[gi], (((1,), (1,)), ((), ())), preferred_element_type=F32)
            dw_ref[gi] += lax.dot_general(d_ref[:, cols], dyg, (((0,), (0,)), ((), ())), preferred_element_type=F32)

    row = pl.BlockSpec((tp, D), lambda i: (i, 0))
    vec = pl.BlockSpec((1, D), lambda i: (0, 0))
    wsp = pl.BlockSpec((ng, gc, gc), lambda i: (0, 0, 0))
    dwsp = pl.BlockSpec((ng, gc, gc), lambda i: (0, 0, 0))
    return _call(body, name=name,
                 out_shape=(_sds((S, D), F32), _sds((ng, gc, gc), F32), _sds((1, D), F32), _sds((1, D), F32)),
                 grid=(S // tp,), in_specs=[row, row, row, wsp, vec, vec], out_specs=(row, dwsp, vec, vec),
                 dims=("arbitrary",))(dx, y, d, wp, scale, gpost)


def _pool_bwd(dd, x, g, res, name):
    S, D = x.shape
    gc = D // len(POOL_WINDOWS)
    tp = _tile(S, 256)
    nt = S // tp

    def body(dd_ref, x_ref, g_ref, res_ref, dx_ref, dg_ref, ext_ref, dh_ref):
        i = pl.program_id(0)

        @pl.when(i == 0)
        def _():
            ext_ref[pl.ds(tp, POOL_HALO), :] = jnp.zeros((POOL_HALO, D), F32)
            dg_ref[...] = jnp.zeros_like(dg_ref)

        t = _pool_counts((nt - 1 - i) * tp, tp)
        for gi, w in enumerate(POOL_WINDOWS):
            cols = slice(gi * gc, (gi + 1) * gc)
            ddv = dd_ref[:, cols]
            ext_ref[pl.ds(0, tp), cols] = ddv / jnp.minimum(t + 1, w).astype(F32)
            s = ext_ref[:, cols]
            sh = 1
            while sh < w:
                s = s + pltpu.roll(s, tp + POOL_HALO - sh, 0)
                sh *= 2
            dh_ref[:, cols] = s[:tp] - ddv
        ext_ref[pl.ds(tp, POOL_HALO), :] = ext_ref[pl.ds(0, POOL_HALO), :]
        dx, dgr = _rms_bwd_math(x_ref[...], g_ref[...], dh_ref[...])
        dx_ref[...] = dx + res_ref[...]
        dg_ref[...] += jnp.sum(dgr, axis=0, keepdims=True)

    row = pl.BlockSpec((tp, D), lambda i: (nt - 1 - i, 0))
    vec = pl.BlockSpec((1, D), lambda i: (0, 0))
    return _call(body, name=name, out_shape=(_sds((S, D), F32), _sds((1, D), F32)), grid=(nt,),
                 in_specs=[row, row, vec, row], out_specs=(row, vec),
                 scratch_shapes=[pltpu.VMEM((tp + POOL_HALO, D), F32), pltpu.VMEM((tp, D), F32)],
                 dims=("arbitrary",))(dd, x, g, res)


def _gelu(x):
    return 0.5 * x * (1.0 + jnp.tanh(0.7978845608028654 * (x + 0.044715 * x * x * x)))


def _gelu_grad(x):
    th = jnp.tanh(0.7978845608028654 * (x + 0.044715 * x * x * x))
    return 0.5 * (1.0 + th) + 0.5 * x * (1.0 - th * th) * 0.7978845608028654 * (1.0 + 3.0 * 0.044715 * x * x)


def _conv_taps(ext_ref, cols, tt):
    e = ext_ref[:, cols]
    return e[CONV_HALO:], pltpu.roll(e, 1, 0)[CONV_HALO:], pltpu.roll(e, 2, 0)[CONV_HALO:]


def _conv_glu_fwd(u, cw, cb, name):
    S, F2 = u.shape
    wd = F2 // 4
    tt = _tile(S, 256)

    def body(u_ref, cw_ref, cb_ref, a_ref, ext_ref):
        it = pl.program_id(1)

        @pl.when(it == 0)
        def _():
            ext_ref[pl.ds(0, CONV_HALO), :] = jnp.zeros((CONV_HALO, 2 * wd), F32)

        ext_ref[pl.ds(CONV_HALO, tt), :] = u_ref[...]
        for cc in range(wd // LANES):
            act = []
            for half in range(2):
                cols = slice(half * wd + cc * LANES, half * wd + (cc + 1) * LANES)
                u0, u1, u2 = _conv_taps(ext_ref, cols, tt)
                act.append(cw_ref[2:3, cols] * u0 + cw_ref[1:2, cols] * u1 + cw_ref[0:1, cols] * u2 + cb_ref[:, cols])
            a_ref[:, cc * LANES:(cc + 1) * LANES] = (_gelu(act[0]) * act[1]).astype(a_ref.dtype)
        ext_ref[pl.ds(0, CONV_HALO), :] = ext_ref[pl.ds(tt, CONV_HALO), :]

    return _call(body, name=name, out_shape=_sds((S, F2 // 2), BF16), grid=(2, S // tt),
                 in_specs=[pl.BlockSpec((tt, 2 * wd), lambda h, t: (t, h)), pl.BlockSpec((8, 2 * wd), lambda h, t: (0, h)),
                           pl.BlockSpec((1, 2 * wd), lambda h, t: (0, h))],
                 out_specs=pl.BlockSpec((tt, wd), lambda h, t: (t, h)),
                 scratch_shapes=[pltpu.VMEM((tt + CONV_HALO, 2 * wd), F32)], dims=("parallel", "arbitrary"))(u, cw, cb)


def _conv_glu_bwd(u, da, cw, cb, name):
    S, F2 = u.shape
    wd = F2 // 4
    tt = _tile(S, 256)
    nt = S // tt
    n = tt + CONV_HALO

    def body(u_ref, uprev_ref, da_ref, cw_ref, cb_ref, du_ref, acc_ref, ext_ref, carry_ref):
        it = pl.program_id(1)

        @pl.when(it == 0)
        def _():
            carry_ref[...] = jnp.zeros_like(carry_ref)
            acc_ref[...] = jnp.zeros_like(acc_ref)

        @pl.when(it == nt - 1)
        def _():
            ext_ref[pl.ds(0, CONV_HALO), :] = jnp.zeros((CONV_HALO, 2 * wd), F32)

        @pl.when(it < nt - 1)
        def _():
            ext_ref[pl.ds(0, CONV_HALO), :] = uprev_ref[...]

        ext_ref[pl.ds(CONV_HALO, tt), :] = u_ref[...]
        for cc in range(wd // LANES):
            taps, act = [], []
            for half in range(2):
                cols = slice(half * wd + cc * LANES, half * wd + (cc + 1) * LANES)
                u0, u1, u2 = _conv_taps(ext_ref, cols, tt)
                taps.append((u0, u1, u2))
                act.append(cw_ref[2:3, cols] * u0 + cw_ref[1:2, cols] * u1 + cw_ref[0:1, cols] * u2 + cb_ref[:, cols])
            dav = da_ref[:, cc * LANES:(cc + 1) * LANES]
            dact = (dav * act[1] * _gelu_grad(act[0]), dav * _gelu(act[0]))
            for half in range(2):
                cols = slice(half * wd + cc * LANES, half * wd + (cc + 1) * LANES)
                u0, u1, u2 = taps[half]
                acc_ref[2:3, cols] += jnp.sum(dact[half] * u0, axis=0, keepdims=True)
                acc_ref[1:2, cols] += jnp.sum(dact[half] * u1, axis=0, keepdims=True)
                acc_ref[0:1, cols] += jnp.sum(dact[half] * u2, axis=0, keepdims=True)
                acc_ref[3:4, cols] += jnp.sum(dact[half], axis=0, keepdims=True)
                e = jnp.concatenate([dact[half], carry_ref[:, cols]], axis=0)
                du = (cw_ref[2:3, cols] * dact[half] + cw_ref[1:2, cols] * pltpu.roll(e, n - 1, 0)[:tt]
                      + cw_ref[0:1, cols] * pltpu.roll(e, n - 2, 0)[:tt])
                du_ref[:, cols] = du.astype(du_ref.dtype)
                carry_ref[:, cols] = dact[half][:CONV_HALO]

    rev = lambda t: nt - 1 - t
    per8 = tt // CONV_HALO
    wide = pl.BlockSpec((tt, 2 * wd), lambda h, t: (rev(t), h))
    prev = pl.BlockSpec((CONV_HALO, 2 * wd), lambda h, t: (jnp.maximum(rev(t) * per8 - 1, 0), h))
    acc = pl.BlockSpec((8, 2 * wd), lambda h, t: (0, h))
    return _call(body, name=name, out_shape=(_sds((S, F2), BF16), _sds((8, F2), F32)), grid=(2, nt),
                 in_specs=[wide, prev, pl.BlockSpec((tt, wd), lambda h, t: (rev(t), h)), acc,
                           pl.BlockSpec((1, 2 * wd), lambda h, t: (0, h))],
                 out_specs=(wide, acc),
                 scratch_shapes=[pltpu.VMEM((n, 2 * wd), F32), pltpu.VMEM((CONV_HALO, 2 * wd), F32)],
                 dims=("parallel", "arbitrary"))(u, u, da, cw, cb)


def _rope_chunk(x, cosv, sinv):
    lane = lax.broadcasted_iota(jnp.int32, x.shape, 1)
    partner = jnp.where(lane % HEAD_DIM < HEAD_DIM // 2, pltpu.roll(x, LANES - HEAD_DIM // 2, 1),
                        pltpu.roll(x, HEAD_DIM // 2, 1))
    return x * cosv + partner * sinv


def _rope(x, width, cos_t, sin_t, name):
    S = x.shape[0]
    tr = _tile(S, 256)

    def body(x_ref, c_ref, s_ref, o_ref):
        for cc in range(width // LANES):
            cols = slice(cc * LANES, (cc + 1) * LANES)
            o_ref[:, cols] = _rope_chunk(x_ref[:, cols], c_ref[...], s_ref[...])

    row = pl.BlockSpec((tr, width), lambda i: (i, 0))
    tab = pl.BlockSpec((tr, LANES), lambda i: (i, 0))
    return _call(body, name=name, out_shape=_sds((S, width), F32), grid=(S // tr,), in_specs=[row, tab, tab],
                 out_specs=row, dims=("parallel",))(x, cos_t, sin_t)


def _attn_mask(n, reps):
    row = lax.broadcasted_iota(jnp.int32, (reps * BLOCK, 2 * BLOCK), 0) & (BLOCK - 1)
    col = lax.broadcasted_iota(jnp.int32, (reps * BLOCK, 2 * BLOCK), 1)
    rel = BLOCK + row - col
    return (rel >= 0) & (rel < WINDOW) & (n * BLOCK + col - BLOCK >= 0)


def _per_head_column(values, reps):
    grp = lax.broadcasted_iota(jnp.int32, (reps * BLOCK, 1), 0) // BLOCK
    col = jnp.zeros((reps * BLOCK, 1), F32)
    for g, v in enumerate(values):
        col = jnp.where(grp == g, v, col)
    return col


def _stack_heads(ref, hk, qpk, keep, scale):
    parts = []
    for g in range(qpk):
        qc, qpar, _, kpar = _head_place(hk * qpk + g, qpk)
        x = ref[:, qc * LANES:(qc + 1) * LANES]
        if scale != 1.0:
            x = x * scale
        if qpar != kpar:
            x = pltpu.roll(x, HEAD_DIM, 1)
        parts.append(jnp.where(keep, x, 0.0).astype(BF16))
    return jnp.concatenate(parts, axis=0)


def _unstack_heads(vals, ref, hk, qpk, lane, dtype):
    pair = None
    for g in range(qpk):
        qc, qpar, _, kpar = _head_place(hk * qpk + g, qpk)
        v = vals[g * BLOCK:(g + 1) * BLOCK]
        if qpar != kpar:
            v = pltpu.roll(v, HEAD_DIM, 1)
        if qpar == 0:
            pair = v
        else:
            ref[:, qc * LANES:(qc + 1) * LANES] = jnp.where(lane < HEAD_DIM, pair, v).astype(dtype)


def _head_place(h, qpk):
    hk = h // qpk
    return h // 2, h % 2, hk // 2, hk % 2


def _attn_specs(S, D):
    nb = S // BLOCK
    kvw = KV_DIM // 2
    qsp = pl.BlockSpec((BLOCK, D), lambda n: (n, 0))
    prev = lambda n: jnp.maximum(n - 1, 0)
    kp = pl.BlockSpec((BLOCK, kvw), lambda n: (prev(n), 0))
    ko = pl.BlockSpec((BLOCK, kvw), lambda n: (n, 0))
    vp = pl.BlockSpec((BLOCK, kvw), lambda n: (prev(n), 1))
    vo = pl.BlockSpec((BLOCK, kvw), lambda n: (n, 1))
    stat = pl.BlockSpec((BLOCK, LANES), lambda n: (n, 0))
    smem = pl.BlockSpec(memory_space=pltpu.SMEM)
    return nb, kvw, qsp, kp, ko, vp, vo, stat, smem


def _attn_fwd(q, k, kv, sinks, name):
    S, D = q.shape
    nh = D // HEAD_DIM
    qpk = nh // N_KV_HEADS
    nb, kvw, qsp, kp, ko, vp, vo, stat, smem = _attn_specs(S, D)

    def body(q_ref, kp_ref, ko_ref, vp_ref, vo_ref, s_ref, o_ref, l_ref):
        n = pl.program_id(0)
        valid = _attn_mask(n, qpk)
        lane = lax.broadcasted_iota(jnp.int32, (BLOCK, LANES), 1)
        lacc = jnp.zeros((BLOCK, LANES), F32)
        for hk in range(N_KV_HEADS):
            kc, kpar = hk // 2, hk % 2
            kcols = slice(kc * LANES, (kc + 1) * LANES)
            k2 = jnp.concatenate([kp_ref[:, kcols], ko_ref[:, kcols]], axis=0).astype(BF16)
            v2 = jnp.concatenate([vp_ref[:, kcols], vo_ref[:, kcols]], axis=0).astype(BF16)
            keep = (lane >= kpar * HEAD_DIM) & (lane < (kpar + 1) * HEAD_DIM)
            qm = _stack_heads(q_ref, hk, qpk, keep, ATTN_SCALE)
            s = lax.dot_general(qm, k2, (((1,), (1,)), ((), ())), preferred_element_type=F32)
            s = jnp.where(valid, s, NEG_INF)
            sink = _per_head_column([s_ref[hk * qpk + g] for g in range(qpk)], qpk)
            m = jnp.maximum(jnp.max(s, axis=1, keepdims=True), sink)
            p = jnp.exp(s - m)
            den = jnp.sum(p, axis=1, keepdims=True) + jnp.exp(sink - m)
            of = jnp.dot(p.astype(BF16), v2, preferred_element_type=F32) / den
            lse = m + jnp.log(den)
            for g in range(qpk):
                lacc = jnp.where(lane == hk * qpk + g, lse[g * BLOCK:(g + 1) * BLOCK], lacc)
            _unstack_heads(of, o_ref, hk, qpk, lane, o_ref.dtype)
        l_ref[...] = lacc

    return _call(body, name=name, out_shape=(_sds((S, D), BF16), _sds((S, LANES), F32)), grid=(nb,),
                 in_specs=[qsp, kp, ko, vp, vo, smem], out_specs=(qsp, stat), dims=("parallel",))(q, k, k, kv, kv, sinks)


def _attn_bwd(q, k, kv, do, lse, sinks, name):
    S, D = q.shape
    nh = D // HEAD_DIM
    qpk = nh // N_KV_HEADS
    nb, kvw, qsp, kp, ko, vp, vo, stat, smem = _attn_specs(S, D)

    def body(q_ref, kp_ref, ko_ref, vp_ref, vo_ref, do_ref, l_ref, s_ref,
             dq_ref, dkp_ref, dko_ref, dvp_ref, dvo_ref, ds_ref):
        n = pl.program_id(0)

        @pl.when(n == 0)
        def _():
            ds_ref[...] = jnp.zeros_like(ds_ref)

        valid = _attn_mask(n, qpk)
        lane = lax.broadcasted_iota(jnp.int32, (BLOCK, LANES), 1)
        lane8 = lax.broadcasted_iota(jnp.int32, (8, LANES), 1)
        lv = l_ref[...]
        dsink = jnp.zeros((8, LANES), F32)
        for kc in range(N_KV_HEADS // 2):
            kcols = slice(kc * LANES, (kc + 1) * LANES)
            k2 = jnp.concatenate([kp_ref[:, kcols], ko_ref[:, kcols]], axis=0).astype(BF16)
            v2 = jnp.concatenate([vp_ref[:, kcols], vo_ref[:, kcols]], axis=0).astype(BF16)
            dk2 = jnp.zeros((2 * BLOCK, LANES), F32)
            dv2 = jnp.zeros((2 * BLOCK, LANES), F32)
            for kpar in range(2):
                hk = 2 * kc + kpar
                heads = [hk * qpk + g for g in range(qpk)]
                keep = (lane >= kpar * HEAD_DIM) & (lane < (kpar + 1) * HEAD_DIM)
                qm = _stack_heads(q_ref, hk, qpk, keep, ATTN_SCALE)
                gm = _stack_heads(do_ref, hk, qpk, keep, 1.0)
                s = lax.dot_general(qm, k2, (((1,), (1,)), ((), ())), preferred_element_type=F32)
                lh = jnp.concatenate([jnp.sum(jnp.where(lane == h, lv, 0.0), axis=1, keepdims=True) for h in heads], axis=0)
                p = jnp.where(valid, jnp.exp(s - lh), 0.0)
                dp = lax.dot_general(gm, v2, (((1,), (1,)), ((), ())), preferred_element_type=F32)
                delta = jnp.sum(p * dp, axis=1, keepdims=True)
                dsb = (p * (dp - delta)).astype(BF16)
                lost = jnp.exp(_per_head_column([s_ref[h] for h in heads], qpk) - lh) * delta
                for g, h in enumerate(heads):
                    dsink = dsink - jnp.where(lane8 == h, jnp.sum(lost[g * BLOCK:(g + 1) * BLOCK]), 0.0)
                dqf = jnp.dot(dsb, k2, preferred_element_type=F32) * ATTN_SCALE
                _unstack_heads(dqf, dq_ref, hk, qpk, lane, F32)
                dk2 = dk2 + lax.dot_general(dsb, qm, (((0,), (0,)), ((), ())), preferred_element_type=F32)
                dv2 = dv2 + lax.dot_general(p.astype(BF16), gm, (((0,), (0,)), ((), ())), preferred_element_type=F32)
            dkp_ref[:, kcols] = dk2[:BLOCK]
            dko_ref[:, kcols] = dk2[BLOCK:]
            dvp_ref[:, kcols] = dv2[:BLOCK]
            dvo_ref[:, kcols] = dv2[BLOCK:]
        ds_ref[...] += dsink

    kvo = pl.BlockSpec((BLOCK, kvw), lambda n: (n, 0))
    acc = pl.BlockSpec((8, LANES), lambda n: (0, 0))
    part = _sds((S, kvw), F32)
    return _call(body, name=name, out_shape=(_sds((S, D), F32), part, part, part, part, _sds((8, LANES), F32)),
                 grid=(nb,), in_specs=[qsp, kp, ko, vp, vo, qsp, stat, smem],
                 out_specs=(qsp, kvo, kvo, kvo, kvo, acc), dims=("arbitrary",))(q, k, k, kv, kv, do, lse, sinks)


def _kv_grad(parts, cos_t, sin_neg_t, name):
    S, kvw = parts[0][0].shape
    nb = S // BLOCK
    flat = [a for p in parts for a in p]
    nl = len(parts)

    def body(*refs):
        c_ref, s_ref, o_ref = refs[4 * nl], refs[4 * nl + 1], refs[4 * nl + 2]
        n = pl.program_id(0)
        last = n == nb - 1
        dk = jnp.zeros((BLOCK, kvw), F32)
        dv = jnp.zeros((BLOCK, kvw), F32)
        for li in range(nl):
            kn, kown, vn, vown = refs[4 * li:4 * li + 4]
            dk = dk + kown[...] + jnp.where(last, 0.0, kn[...])
            dv = dv + vown[...] + jnp.where(last, 0.0, vn[...])
        for cc in range(kvw // LANES):
            cols = slice(cc * LANES, (cc + 1) * LANES)
            o_ref[:, cols] = _rope_chunk(dk[:, cols], c_ref[...], s_ref[...])
        o_ref[:, kvw:] = dv

    own = pl.BlockSpec((BLOCK, kvw), lambda n: (n, 0))
    nxt = pl.BlockSpec((BLOCK, kvw), lambda n: (jnp.minimum(n + 1, nb - 1), 0))
    tab = pl.BlockSpec((BLOCK, LANES), lambda n: (n, 0))
    return _call(body, name=name, out_shape=_sds((S, 2 * kvw), F32), grid=(nb,),
                 in_specs=[nxt, own, nxt, own] * nl + [tab, tab],
                 out_specs=pl.BlockSpec((BLOCK, 2 * kvw), lambda n: (n, 0)), dims=("parallel",))(*flat, cos_t, sin_neg_t)


def _sum_blocks(name, qc, grid, out_shape, out_block, out_imap, ins, out_dtype=F32, into=None):
    nin = len(ins)

    def body(qc_ref, *refs):
        acc = refs[0][...].astype(F32)
        for r in refs[1:nin]:
            acc = acc + r[...].astype(F32)
        refs[-1][...] = acc.astype(refs[-1].dtype)

    in_specs = [pl.BlockSpec(b, m) for _, b, m in ins]
    operands = [a for a, _, _ in ins]
    aliases = None
    if into is not None:
        in_specs.append(ANY)
        operands.append(into)
        aliases = {1 + nin: 0}
    gs = pltpu.PrefetchScalarGridSpec(num_scalar_prefetch=1, grid=grid, in_specs=in_specs,
                                      out_specs=pl.BlockSpec(out_block, out_imap))
    return _call(body, name=name, out_shape=_sds(out_shape, out_dtype), grid_spec=gs,
                 dims=("parallel",) * len(grid), aliases=aliases)(qc, *operands)


def _adamw(w, g, m, v, name, part=None, into=None):
    shape = w.shape
    C = shape[-1]
    R = w.size // C
    k, cnt, nparts = part if part is not None else (0, 1, 1)
    tr = _tile(R // nparts, max(8, (1 << 18) // C))
    first = k * (R // nparts // tr)
    rows = cnt * (R // nparts)

    def body(w_ref, g_ref, m_ref, v_ref, *outs):
        go_ref, d_ref, nm_ref, nv_ref = outs[-4:]
        gv = g_ref[...]
        nm = ADAM_B1 * m_ref[...] + (1.0 - ADAM_B1) * gv
        nv = ADAM_B2 * v_ref[...] + (1.0 - ADAM_B2) * (gv * gv)
        m_hat = nm / (1.0 - ADAM_B1 ** ADAM_STEP)
        v_hat = nv / (1.0 - ADAM_B2 ** ADAM_STEP)
        go_ref[...] = gv
        d_ref[...] = -ADAM_LR * (m_hat / (jnp.sqrt(v_hat) + ADAM_EPS) + ADAM_WD * w_ref[...])
        nm_ref[...] = nm
        nv_ref[...] = nv

    blk = pl.BlockSpec((tr, C), lambda i: (first + i, 0))
    flat = _sds((R, C), F32)
    operands = [a.reshape(-1, C) for a in (w, g, m, v)]
    in_specs, aliases = [blk] * 4, None
    if g.size != w.size:
        assert g.size == rows * C, (name, g.shape, shape, part)
        in_specs[1] = pl.BlockSpec((tr, C), lambda i: (i, 0))
    if into is not None:
        operands += [a.reshape(R, C) for a in into]
        in_specs = in_specs + [ANY] * 4
        aliases = {4 + i: i for i in range(4)}
    outs = _call(body, name=name, out_shape=(flat,) * 4, grid=(rows // tr,), in_specs=in_specs,
                 out_specs=(blk,) * 4, dims=("parallel",), aliases=aliases)(*operands)
    return tuple(o.reshape(shape) for o in outs)


def _place():
    x, y, c = lax.axis_index("x"), lax.axis_index("y"), lax.axis_index("c")
    chips = [(1 - x, y), (x, 1 - y), (1 - x, 1 - y)]
    return x, y, c, chips


def _at(ref, nd, dims):
    idx = [slice(None)] * nd
    for d, v in dims.items():
        idx[d] = pl.ds(v[0], v[1]) if isinstance(v, tuple) else v
    return ref.at[tuple(idx)]


def _remote(src, dst, send_sem, recv_sem, dev):
    return pltpu.make_async_remote_copy(src_ref=src, dst_ref=dst, send_sem=send_sem, recv_sem=recv_sem,
                                        device_id=dev, device_id_type=MESH)


def _split_call(body, name, out_shape, in_specs, out_specs, aliases):
    return pl.pallas_call(body, name=name, out_shape=out_shape, in_specs=in_specs, out_specs=out_specs,
                          input_output_aliases=aliases,
                          compiler_params=pltpu.CompilerParams(has_side_effects=EFFECT))


def _hbm(a):
    return pltpu.with_memory_space_constraint(a, pltpu.HBM)


def _copies_start(srcs, lands, after, name, plan, ncopies):
    n, m = len(srcs), len(lands)

    def body(*refs):
        src, land = refs[:n], refs[n:n + m]
        send_sems, recv_sems, token = refs[n + m + 1], refs[n + m + 2], refs[-1]
        x, y, c, chips = _place()
        for k, (s, d, dev) in enumerate(plan(x, y, c, chips, src, land)):
            _remote(s, d, send_sems.at[k], recv_sems.at[k], dev).start()
        token[...] = jnp.zeros_like(token)

    thru = tuple(pltpu.HBM(a.shape, a.dtype) for a in list(srcs) + list(lands))
    outs = _split_call(
        body, name,
        out_shape=(pltpu.SemaphoreType.DMA((ncopies,)), pltpu.SemaphoreType.DMA((ncopies,))) + thru + (_sds((8, LANES), F32),),
        in_specs=(HBM,) * (n + m) + (ANY,), out_specs=(SEM, SEM) + (HBM,) * (n + m) + (VMEM,),
        aliases={i: 2 + i for i in range(n + m)})(*[_hbm(a) for a in srcs], *[_hbm(a) for a in lands], after)
    return dict(send=outs[0], recv=outs[1], srcs=outs[2:2 + n], lands=outs[2 + n:2 + n + m], token=outs[-1])


def _copies_wait(handle, after, name, plan):
    srcs, lands = handle['srcs'], handle['lands']
    n, m = len(srcs), len(lands)

    def body(*refs):
        src, land = refs[:n], refs[n:n + m]
        send_sems, recv_sems = refs[n + m], refs[n + m + 1]
        x, y, c, chips = _place()
        for k, (s, d, dev) in enumerate(plan(x, y, c, chips, src, land)):
            cp = _remote(s, d, send_sems.at[k], recv_sems.at[k], dev)
            cp.wait_send()
            cp.wait_recv()

    thru = tuple(pltpu.HBM(a.shape, a.dtype) for a in list(srcs) + list(lands))
    outs = _split_call(body, name, out_shape=thru, in_specs=(HBM,) * (n + m) + (SEM, SEM, ANY),
                       out_specs=(HBM,) * (n + m), aliases={i: i for i in range(n + m)})(
        *srcs, *lands, handle['send'], handle['recv'], after)
    return outs[:n], outs[n:]


def _gather_plan(x, y, c, chips, src, land, arriving):
    q = 2 * x + y
    peers = [(ch[0], ch[1], c) for ch in chips] + [(x, y, 1 - c)]
    slots = [2 * ch[0] + ch[1] for ch in chips] + [q]
    return [(s, d.at[slots[j] if arriving else q], peers[j]) for s, d in zip(src, land) for j in range(4)]


def _gather_half_plan(x, y, c, chips, src, land, arriving):
    q = 2 * x + y
    out = []
    for s, d in zip(src, land):
        hs = s.shape[0] // 2
        rows = pl.ds(c * hs, hs)
        for ch in chips:
            out.append((s.at[rows], d.at[2 * ch[0] + ch[1] if arriving else q, rows], (ch[0], ch[1], c)))
        out.append((s, d.at[q], (x, y, 1 - c)))
    return out


def _exchange_plan(x, y, c, chips, src, land, arriving):
    out = []
    for d in land:
        hs = d.shape[1] // 2
        for ch in chips:
            slot = 2 * ch[0] + ch[1]
            out.append((d.at[slot, pl.ds(c * hs, hs)], d.at[slot, pl.ds(((1 - c) if arriving else c) * hs, hs)], (x, y, 1 - c)))
    return out


def _scatter_plan(shard_axes, shapes):
    def plan(x, y, c, chips, src, land):
        out = []
        for s, d, sd, shp in zip(src, land, shard_axes, shapes):
            ss = shp[sd] // N_SHARDS
            for j, ch in enumerate(chips):
                out.append((_at(s, len(shp), {sd: ((2 * ch[0] + ch[1]) * ss, ss)}), d.at[j], (ch[0], ch[1], c)))
        return out
    return plan


def _half_dims(shape, hd, c):
    hs = shape[hd] // 2
    return {hd: (c * hs, hs)}


def _swap_halves(grads, specs, name):
    n = len(grads)
    outs_shape = []
    for a, (sd, hd) in zip(grads, specs):
        shp = list(a.shape)
        shp[hd] //= 2
        outs_shape.append(_sds(shp, F32))

    def body(*refs):
        ins, outs = refs[:n], refs[n:2 * n]
        send_sems, recv_sems = refs[2 * n:]
        x, y, c, _ = _place()
        cps = []
        for ai, (sd, hd) in enumerate(specs):
            shp = grads[ai].shape
            cp = _remote(_at(ins[ai], len(shp), _half_dims(shp, hd, 1 - c)), outs[ai],
                         send_sems.at[ai], recv_sems.at[ai], (x, y, 1 - c))
            cp.start()
            cps.append(cp)
        for cp in cps:
            cp.wait()

    return _call(body, name=name, out_shape=tuple(outs_shape), in_specs=[ANY] * n, out_specs=tuple([ANY] * n),
                 scratch_shapes=[pltpu.SemaphoreType.DMA((n,)), pltpu.SemaphoreType.DMA((n,))])(*grads)


def _share_halves(arrs, half_axes, layers, after, name):
    n = len(arrs)

    def body(*refs):
        ins, outs = refs[:n], refs[n + 1:2 * n + 1]
        send_sems, recv_sems = refs[2 * n + 1:]
        x, y, c, _ = _place()

        def half(ref, ai, which):
            shp = arrs[ai].shape
            hs = shp[half_axes[ai]] // 2
            dims = {half_axes[ai]: (which * hs, hs)}
            if layers[ai] is not None:
                dims[0] = layers[ai]
            return _at(ref, len(shp), dims)

        sends = []
        for ai in range(n):
            cp = _remote(half(ins[ai], ai, c), half(outs[ai], ai, c), send_sems.at[ai], recv_sems.at[ai], (x, y, 1 - c))
            cp.start()
            sends.append(cp)
        for ai in range(n):
            land = half(outs[ai], ai, 1 - c)
            _remote(land, land, send_sems.at[ai], recv_sems.at[ai], (x, y, c)).wait_recv()
        for cp in sends:
            cp.wait_send()

    return _call(body, name=name, out_shape=tuple(_sds(a.shape, a.dtype) for a in arrs), in_specs=[ANY] * (n + 1),
                 out_specs=tuple([ANY] * n), aliases={i: i for i in range(n)},
                 scratch_shapes=[pltpu.SemaphoreType.DMA((n,)), pltpu.SemaphoreType.DMA((n,))])(*arrs, after)


def _gather_small(v, name):
    R, C = v.shape

    def body(x_ref, out_ref, send_sems, recv_sems, local_sem):
        x, y, c, chips = _place()
        me, sibling = (x, y, c), (x, y, 1 - c)

        def rows(px, py, pc):
            return out_ref.at[pl.ds((4 * px + 2 * py + pc) * R, R), :]

        def copy(k, block, to, src=None):
            return _remote(rows(*block) if src is None else src, rows(*block), send_sems.at[k], recv_sems.at[k], to)

        mine = pltpu.make_async_copy(x_ref, rows(*me), local_sem)
        mine.start()
        first = [copy(0, me, sibling, src=x_ref)]
        first += [copy(1 + j, me, (ch[0], ch[1], c), src=x_ref) for j, ch in enumerate(chips)]
        for cp in first:
            cp.start()
        passed = [copy(4 + j, (ch[0], ch[1], c), sibling) for j, ch in enumerate(chips)]
        for j, ch in enumerate(chips):
            copy(1 + j, (ch[0], ch[1], c), me).wait_recv()
            passed[j].start()
        copy(0, sibling, me).wait_recv()
        for j, ch in enumerate(chips):
            copy(4 + j, (ch[0], ch[1], 1 - c), me).wait_recv()
        for cp in first + passed:
            cp.wait_send()
        mine.wait()

    vm = pl.BlockSpec(memory_space=pltpu.VMEM)
    return _call(body, name=name, out_shape=_sds((8 * R, C), v.dtype), in_specs=[vm], out_specs=vm,
                 scratch_shapes=[pltpu.SemaphoreType.DMA((7,)), pltpu.SemaphoreType.DMA((7,)),
                                 pltpu.SemaphoreType.DMA])(v)


def _sum8(g, name):
    _, R, C = g.shape

    def body(g_ref, o_ref):
        acc = g_ref[0]
        for d in range(1, 8):
            acc = acc + g_ref[d]
        o_ref[...] = acc

    return _call(body, name=name, out_shape=_sds((R, C), F32), in_specs=[pl.BlockSpec(memory_space=pltpu.VMEM)],
                 out_specs=pl.BlockSpec(memory_space=pltpu.VMEM))(g)


def _rope_tables(positions):
    inv_freq = 1.0 / (ROPE_THETA ** (jnp.arange(0, HEAD_DIM, 2, dtype=F32) / HEAD_DIM))
    ang = positions.astype(F32)[:, None] * inv_freq
    cosv, sinv = jnp.cos(ang), jnp.sin(ang)
    return jnp.tile(cosv, (1, 4)), jnp.tile(jnp.concatenate([-sinv, sinv], axis=1), (1, 2))


def _blocked(a):
    parts = jnp.split(a, 4, axis=-1)
    return jnp.concatenate([parts[0], parts[2], parts[1], parts[3]], axis=-1)


def _arrived(Wl, name, after):
    if callable(Wl[name]):
        Wl[name] = Wl[name](after)
    return Wl[name]


def _local_step(x, target, positions, P, weights_of, grads_done):
    S, D = x.shape
    depth = P['mix_pre_g'].shape[0]
    n_a = depth // 2
    cos_t, sin_t = _rope_tables(positions)
    row = lambda a, l: a[l][None]
    cb = [_blocked(P['ffn_conv_b'][l])[None] for l in range(depth)]
    sv, W = {}, {}
    kv = k_rot = None
    for l in range(depth):
        t = f"l{l}"
        W[l], zero = weights_of(l, x)
        sv[l, 'x_in'] = x
        g_pre = row(P['mix_pre_g'], l) + zero
        if l < n_a:
            d = _pool_fwd(x, g_pre, "pool_fwd_" + t)
            y, x = _pool_mm_fwd(d, W[l]['pool_w'], W[l]['pool_scale'], x, row(P['mix_post_g'], l), "pool_mm_fwd_" + t)
            sv[l, 'd'], sv[l, 'y'] = d, y
        else:
            j = l - n_a
            h = _rms_fwd(x, g_pre, BF16, "q_norm_" + t)
            if l == n_a:
                hkv = _rms_fwd(x, P['kv_norm_g'][None], BF16, "kv_norm")
                kv = _matmul(hkv, W[l]['w_kv'], 'nn', F32, "kv_proj", 512, 512, 1024)
                k_rot = _rope(kv, KV_DIM // 2, cos_t, sin_t, "k_rope")
                sv['hkv'] = hkv
            qraw = _matmul(h, W[l]['w_q'], 'nn', F32, "q_proj_" + t, 512, 1024, 1024)
            q = _rope(qraw, D, cos_t, sin_t, "q_rope_" + t)
            o, lse = _attn_fwd(q, k_rot, kv, P['sinks'][j], "attn_fwd_" + t)
            m = _matmul(o, W[l]['w_o'], 'nn', F32, "o_proj_" + t, 512, 1024, 1024)
            x = _res_rms_fwd(x, m, row(P['mix_post_g'], l), "mix_post_" + t)
            sv[l, 'h'], sv[l, 'q'], sv[l, 'o'], sv[l, 'lse'], sv[l, 'm'] = h, q, o, lse, m
        sv[l, 'x1'] = x
        if 'pre_ffn' in W[l]:
            W[l].pop('pre_ffn')(x)
        h2 = _rms_fwd(x, row(P['ffn_pre_g'], l), BF16, "ffn_norm_" + t)
        w_in = _arrived(W[l], 'w_in', h2)
        u = _matmul(h2, w_in, 'nn', F32, "ffn_up_" + t, 512, w_in.shape[2], 1024, b_blocks=True)
        a = _conv_glu_fwd(u, W[l]['cw'], cb[l], "ffn_glu_" + t)
        f = _matmul(a, _arrived(W[l], 'w_out', a), 'nn', F32, "ffn_down_" + t, 512, 1024, 2816)
        x = _res_rms_fwd(x, f, row(P['ffn_post_g'], l) + W[l].pop('tie', 0.0), "ffn_post_" + t)
        sv[l, 'h2'], sv[l, 'u'], sv[l, 'a'], sv[l, 'f'] = h2, u, a, f

    dx, sq = _loss_grad(x, target, "loss")
    kv_parts = []
    zero = 0.0
    for l in reversed(range(depth)):
        t = f"l{l}"
        G = {}
        wd = W[l]['w_in'].shape[2]
        df, G['ffn_post_g'] = _rms_bwd(sv[l, 'f'], row(P['ffn_post_g'], l) + zero, dx, None, "ffn_post_bwd_" + t)
        da = _matmul(df, W[l]['w_out'], 'nt', F32, "ffn_down_dx_" + t, 512, wd, 1024)
        G['ffn_w_out'] = _matmul(sv[l, 'a'], df, 'tn', F32, "ffn_down_dw_" + t, wd, 1024, 1024)
        du, acc = _conv_glu_bwd(sv[l, 'u'], da, W[l]['cw'], cb[l], "ffn_glu_bwd_" + t)
        G['ffn_conv_w'] = _blocked(acc[0:3])
        G['ffn_conv_b'] = _blocked(acc[3:4])
        dh2 = _matmul(du, W[l]['w_in'], 'nt', F32, "ffn_up_dx_" + t, 512, 1024, wd, b_blocks=True)
        G['ffn_w_in'] = _matmul(sv[l, 'h2'], du, 'tn', F32, "ffn_up_dw_" + t, 1024, wd, 2048, out_perm=True)
        dx, G['ffn_pre_g'] = _rms_bwd(sv[l, 'x1'], row(P['ffn_pre_g'], l), dh2, dx, "ffn_norm_bwd_" + t)
        if l < n_a:
            dd, G['pool_w'], G['pool_scale'], G['mix_post_g'] = _pool_mm_bwd(
                dx, sv[l, 'y'], sv[l, 'd'], W[l]['pool_w'], W[l]['pool_scale'], row(P['mix_post_g'], l), "pool_mm_bwd_" + t)
            dx, G['mix_pre_g'] = _pool_bwd(dd, sv[l, 'x_in'], row(P['mix_pre_g'], l), dx, "pool_bwd_" + t)
        else:
            j = l - n_a
            dm, G['mix_post_g'] = _rms_bwd(sv[l, 'm'], row(P['mix_post_g'], l), dx, None, "mix_post_bwd_" + t)
            do = _matmul(dm, W[l]['w_o'], 'nt', F32, "o_proj_dx_" + t, 512, 1024, 1024)
            G['w_o'] = _matmul(sv[l, 'o'], dm, 'tn', F32, "o_proj_dw_" + t, 1024, 1024, 1024)
            dq, dkn, dko, dvn, dvo, dsk = _attn_bwd(sv[l, 'q'], k_rot, kv, do, sv[l, 'lse'], P['sinks'][j], "attn_bwd_" + t)
            G['sinks'] = dsk[0:1]
            kv_parts.append((dkn, dko, dvn, dvo))
            dqraw = _rope(dq, D, cos_t, -sin_t, "q_rope_bwd_" + t)
            dh = _matmul(dqraw, W[l]['w_q'], 'nt', F32, "q_proj_dx_" + t, 512, 1024, 1024)
            G['w_q'] = _matmul(sv[l, 'h'], dqraw, 'tn', F32, "q_proj_dw_" + t, 1024, 1024, 1024)
            dx, G['mix_pre_g'] = _rms_bwd(sv[l, 'x_in'], row(P['mix_pre_g'], l), dh, dx, "q_norm_bwd_" + t)
            if l == n_a:
                dkv = _kv_grad(kv_parts, cos_t, -sin_t, "kv_grad")
                dhkv = _matmul(dkv, W[l]['w_kv'], 'nt', F32, "kv_proj_dx", 512, 1024, 512)
                G['w_kv'] = _matmul(sv['hkv'], dkv, 'tn', F32, "kv_proj_dw", 1024, 512, 1024)
                dx, G['kv_norm_g'] = _rms_bwd(sv[l, 'x_in'], P['kv_norm_g'][None], dhkv, dx, "kv_norm_bwd")
        zero = grads_done(l, G, dx)
    return sq, dx


SMALL = ['mix_pre_g', 'mix_post_g', 'kv_norm_g', 'sinks', 'ffn_pre_g', 'ffn_post_g', 'ffn_conv_b', 'ffn_conv_w', 'pool_scale']
BIG = {'ffn_w_in': (1, 0, 1), 'ffn_w_out': (0, 1, 2), 'w_q': (0, 1, 2), 'w_o': (0, 1, 2), 'w_kv': (0, 1, 1),
       'pool_w': (1, 0, 1)}


def _reduce_start(pieces, qc, after, tag):
    arrs = [p[0] for p in pieces]
    specs = [(p[1], p[2]) for p in pieces]

    theirs = _swap_halves(arrs, specs, "grad_swap_halves_" + tag)
    sums = []
    for pi, (a, (sd, hd), r) in enumerate(zip(arrs, specs, theirs)):
        shp = r.shape
        nd = len(shp)
        if nd == 3:
            blk, grid = tuple(shp), (1,)
            mine = lambda i, s: (s[1], 0, 0)
            zero = lambda i, s: (0, 0, 0)
        elif hd == 0:
            tr = _tile(shp[0], max(16, (1 << 18) // shp[1]), 16)
            blk, grid = (tr, shp[1]), (shp[0] // tr,)
            nblk = shp[0] // tr
            mine = lambda i, s, nblk=nblk: (s[1] * nblk + i, 0)
            zero = lambda i, s: (i, 0)
        else:
            tr = _tile(shp[0], max(16, (1 << 18) // shp[1]), 16)
            blk, grid = (tr, shp[1]), (shp[0] // tr,)
            mine = lambda i, s: (i, s[1])
            zero = lambda i, s: (i, 0)
        sums.append(_sum_blocks(f"grad_chip_sum_{tag}_{pi}", qc, grid, shp, blk, zero, [(a, blk, mine), (r, blk, zero)],
                                out_dtype=BF16))

    lands = []
    for s_arr, (sd, hd) in zip(sums, specs):
        shp = list(s_arr.shape)
        shp[sd] //= N_SHARDS
        lands.append(lax.empty((3,) + tuple(shp), BF16))
    plan = _scatter_plan([sd for sd, _ in specs], [s.shape for s in sums])
    handle = _copies_start(sums, lands, after, "grad_scatter_start_" + tag, plan, 3 * len(sums))
    return dict(handle=handle, plan=plan, pieces=pieces, tag=tag)


def _reduce_finish(state, after, qc, outs, out_shapes):
    handle, pieces, tag = state['handle'], state['pieces'], state['tag']
    sums, recvd = _copies_wait(handle, after, "grad_scatter_wait_" + tag, state['plan'])
    for pi, ((a, sd, hd, oname, fixed, ohd), s_arr, r) in enumerate(zip(pieces, sums, recvd)):
        shp = r.shape[1:]
        nd = len(shp)
        lead = (fixed[0],) if fixed else ()
        none = (None,) if fixed else ()
        n_stack = out_shapes[oname][0]
        if nd == 3:
            blk, grid = tuple(shp), (1,)
            mine = lambda i, s: (0, s[0], 0)
            rk = [lambda i, s, k=k: (k, 0, 0, 0) for k in range(3)]
            oshape = (n_stack, 2 * shp[0]) + tuple(shp[1:])
            oblk = none + blk
            omap = lambda i, s, lead=lead: lead + (s[1], 0, 0)
        elif sd == 1:
            tr = _tile(shp[0], max(16, (1 << 18) // shp[1]), 16)
            blk, grid = (tr, shp[1]), (shp[0] // tr,)
            nblk = shp[0] // tr
            mine = lambda i, s: (i, s[0])
            rk = [lambda i, s, k=k: (k, i, 0) for k in range(3)]
            oshape = (n_stack, 2 * shp[0], shp[1])
            oblk = none + blk
            omap = lambda i, s, lead=lead, nblk=nblk: lead + (s[1] * nblk + i, 0)
        else:
            tr = _tile(shp[0], max(16, (1 << 18) // shp[1]), 16)
            blk, grid = (tr, shp[1]), (shp[0] // tr,)
            nblk = shp[0] // tr
            mine = lambda i, s, nblk=nblk: (s[0] * nblk + i, 0)
            rk = [lambda i, s, k=k: (k, i, 0) for k in range(3)]
            oshape = ((n_stack,) if fixed else ()) + (shp[0], 2 * shp[1])
            oblk = none + blk
            omap = lambda i, s, lead=lead: lead + (i, s[1])
        assert tuple(oshape) == tuple(out_shapes[oname]), (oname, oshape, out_shapes[oname])
        ins = [(s_arr, blk, mine)] + [(r, (None,) + blk, rk[k]) for k in range(3)]
        outs[oname] = _sum_blocks(f"grad_total_{tag}_{pi}", qc, grid, oshape, oblk, omap, ins, into=outs.get(oname))


def _pack_small(parts):
    rows, offs, r = [], [], 0
    for a in parts:
        flat = a.reshape(-1)
        nr = -(-flat.size // (8 * LANES)) * 8
        rows.append(jnp.pad(flat, (0, nr * LANES - flat.size)).reshape(nr, LANES))
        offs.append((r, nr, a.shape))
        r += nr
    return jnp.concatenate(rows, axis=0), offs


def _unpack_small(packed, offs):
    return [packed[r:r + nr].reshape(-1)[:math.prod(shape)].reshape(shape) for r, nr, shape in offs]


def kernel(x, positions, mix_pre_g, mix_post_g, pool_w, pool_scale, kv_norm_g, w_kv, w_q, w_o, sinks, ffn_pre_g, ffn_post_g, ffn_w_in, ffn_conv_w, ffn_conv_b, ffn_w_out, loss_target, m_mix_pre_g, m_mix_post_g, m_pool_w, m_pool_scale, m_kv_norm_g, m_w_kv, m_w_q, m_w_o, m_sinks, m_ffn_pre_g, m_ffn_post_g, m_ffn_w_in, m_ffn_conv_w, m_ffn_conv_b, m_ffn_w_out, v_mix_pre_g, v_mix_post_g, v_pool_w, v_pool_scale, v_kv_norm_g, v_w_kv, v_w_q, v_w_o, v_sinks, v_ffn_pre_g, v_ffn_post_g, v_ffn_w_in, v_ffn_conv_w, v_ffn_conv_b, v_ffn_w_out):
    w = dict(mix_pre_g=mix_pre_g, mix_post_g=mix_post_g, pool_w=pool_w, pool_scale=pool_scale, kv_norm_g=kv_norm_g,
             w_kv=w_kv, w_q=w_q, w_o=w_o, sinks=sinks, ffn_pre_g=ffn_pre_g, ffn_post_g=ffn_post_g, ffn_w_in=ffn_w_in,
             ffn_conv_w=ffn_conv_w, ffn_conv_b=ffn_conv_b, ffn_w_out=ffn_w_out)
    mom = dict(mix_pre_g=m_mix_pre_g, mix_post_g=m_mix_post_g, pool_w=m_pool_w, pool_scale=m_pool_scale,
               kv_norm_g=m_kv_norm_g, w_kv=m_w_kv, w_q=m_w_q, w_o=m_w_o, sinks=m_sinks, ffn_pre_g=m_ffn_pre_g,
               ffn_post_g=m_ffn_post_g, ffn_w_in=m_ffn_w_in, ffn_conv_w=m_ffn_conv_w, ffn_conv_b=m_ffn_conv_b,
               ffn_w_out=m_ffn_w_out)
    var = dict(mix_pre_g=v_mix_pre_g, mix_post_g=v_mix_post_g, pool_w=v_pool_w, pool_scale=v_pool_scale,
               kv_norm_g=v_kv_norm_g, w_kv=v_w_kv, w_q=v_w_q, w_o=v_w_o, sinks=v_sinks, ffn_pre_g=v_ffn_pre_g,
               ffn_post_g=v_ffn_post_g, ffn_w_in=v_ffn_w_in, ffn_conv_w=v_ffn_conv_w, ffn_conv_b=v_ffn_conv_b,
               ffn_w_out=v_ffn_w_out)
    depth = mix_pre_g.shape[0]
    q_chip = 2 * lax.axis_index("x") + lax.axis_index("y")
    qc = jnp.stack([q_chip, lax.axis_index("c")]).astype(jnp.int32)

    n_a = depth // 2
    D = x.shape[-1]
    gc = pool_w.shape[3]

    def layer_shards(l):
        first = []
        if l < n_a:
            first.append(('pool_w', pool_w[l].astype(BF16)))
        else:
            first += [('w_q', w_q[l - n_a].astype(BF16)), ('w_o', w_o[l - n_a].astype(BF16))]
            if l == n_a:
                first.append(('w_kv', w_kv.astype(BF16)))
        ffn = [('w_in', ffn_w_in[l].astype(BF16)), ('w_out', ffn_w_out[l].astype(BF16))]
        if l == 0:
            return [first + [('conv_w', ffn_conv_w), ('pool_scale', pool_scale)], ffn[:1], ffn[1:]]
        return [first + ffn]

    def start_group(items, after, name, plan):
        srcs = [a for _, a in items]
        lands = [lax.empty((N_SHARDS,) + a.shape, a.dtype) for a in srcs]
        handle = _copies_start(srcs, lands, after, name, functools.partial(plan, arriving=False), 4 * len(srcs))
        return [n for n, _ in items], handle, plan

    def wait_group(started, after, name):
        names, handle, plan = started
        _, lands = _copies_wait(handle, after, name, functools.partial(plan, arriving=True))
        return dict(zip(names, lands))

    def start_exchange(got, after, name):
        names = list(got)
        handle = _copies_start([], [got[n] for n in names], after, name,
                               functools.partial(_exchange_plan, arriving=False), 3 * len(names))
        return names, handle, _exchange_plan

    groups0 = layer_shards(0)
    small0 = start_group(groups0[0], x, "weight_gather_start_l0_small", _gather_plan)
    in0 = start_group(groups0[1], small0[1]['token'], "weight_gather_start_l0_in", _gather_half_plan)
    out0 = start_group(groups0[2], in0[1]['token'], "weight_gather_start_l0_out", _gather_half_plan)
    pending, shared, steps = {}, {}, {}

    def start_next(l, after):
        pending[l + 1] = start_group(layer_shards(l + 1)[0], after, f"weight_gather_start_l{l + 1}", _gather_plan)
        return pending[l + 1][1]['token'][0, 0]

    def pre_ffn0(after):
        landed = wait_group(in0, after, "weight_gather_wait_l0_in")
        steps['in'] = start_exchange(landed, after, "weight_exchange_start_l0_in")

    def w_in0(after):
        return wait_group(steps['in'], after, "weight_exchange_wait_l0_in")['w_in']

    def w_out0(Wl, after):
        landed = wait_group(out0, after, "weight_gather_wait_l0_out")
        both = wait_group(start_exchange(landed, after, "weight_exchange_start_l0_out"), after,
                          "weight_exchange_wait_l0_out")
        Wl['tie'] = start_next(0, both['w_out'])
        return both['w_out'].reshape(-1, D)

    def weights_of(l, x_now):
        zero = 0.0
        if l == 0:
            got = wait_group(small0, out0[1]['token'], "weight_gather_wait_l0_small")
            shared['conv_w'] = got['conv_w']
            shared['pool_scale'] = got['pool_scale'].transpose(1, 0, 2).reshape(n_a, D)
        else:
            got = wait_group(pending.pop(l), x_now, f"weight_gather_wait_l{l}")
            if l + 1 < depth:
                zero = start_next(l, got['w_in'])
        taps = jnp.concatenate([shared['conv_w'][p, l] for p in (0, 2, 1, 3)], axis=-1)
        Wl = dict(cw=jnp.pad(taps, ((0, 5), (0, 0))))
        if l == 0:
            Wl['pre_ffn'], Wl['w_in'], Wl['w_out'] = pre_ffn0, w_in0, functools.partial(w_out0, Wl)
        else:
            Wl['w_in'], Wl['w_out'] = got['w_in'], got['w_out'].reshape(-1, D)
        if l < n_a:
            Wl['pool_w'] = got['pool_w'].transpose(1, 0, 2, 3).reshape(-1, gc, gc)
            Wl['pool_scale'] = shared['pool_scale'][l][None]
        else:
            Wl['w_q'], Wl['w_o'] = got['w_q'].reshape(D, D), got['w_o'].reshape(D, D)
            if l == n_a:
                Wl['w_kv'] = got['w_kv'].reshape(D, -1)
        return Wl, zero

    big_shapes = {n: w[n].shape for n in BIG}
    big, G, scattering = {}, {}, {}

    def grads_done(l, Gl, dx_now):
        for n, g in Gl.items():
            G[n, l] = g
        fixed = lambda n: {0: (l if n.startswith('ffn') or n == 'pool_w' else l - n_a)} if len(big_shapes[n]) > 2 else {}
        pieces = [(Gl[n], BIG[n][0], BIG[n][1], n, fixed(n), BIG[n][2]) for n in BIG if n in Gl]
        scattering[l] = _reduce_start(pieces, qc, dx_now, f"l{l}")
        if l + 1 in scattering:
            _reduce_finish(scattering.pop(l + 1), dx_now, qc, big, big_shapes)
        return scattering[l]['handle']['token'][0, 0]

    P = {n: w[n] for n in ('mix_pre_g', 'mix_post_g', 'kv_norm_g', 'sinks', 'ffn_pre_g', 'ffn_post_g', 'ffn_conv_b')}
    sq, dx = _local_step(x[0], loss_target[0], positions[0], P, weights_of, grads_done)
    loss = 0.5 / D * lax.psum(jnp.sum(sq), ("x", "y", "c"))

    late = {'ffn_w_in': (1, depth - 1), 'ffn_w_out': (1, depth - 1), 'pool_w': (1, n_a - 1), 'w_q': None, 'w_o': None,
            'w_kv': None}
    names = list(big)
    in_flight = scattering[0]['handle']['token']
    whole = _share_halves([big[n] for n in names], [BIG[n][2] for n in names], [late[n] for n in names], in_flight,
                          "grad_share_halves_late")
    upd = {}
    for n, g in zip(names, whole):
        part = None if late[n] is None else late[n] + (w[n].shape[0],)
        upd[n] = _adamw(w[n], g, mom[n], var[n], "adamw_late_" + n, part=part)

    small_local = {
        'mix_pre_g': jnp.concatenate([G['mix_pre_g', l] for l in range(depth)], axis=0),
        'mix_post_g': jnp.concatenate([G['mix_post_g', l] for l in range(depth)], axis=0),
        'kv_norm_g': G['kv_norm_g', n_a][0],
        'sinks': jnp.concatenate([G['sinks', l][:, :sinks.shape[1]] for l in range(n_a, depth)], axis=0),
        'ffn_pre_g': jnp.concatenate([G['ffn_pre_g', l] for l in range(depth)], axis=0),
        'ffn_post_g': jnp.concatenate([G['ffn_post_g', l] for l in range(depth)], axis=0),
        'ffn_conv_b': jnp.concatenate([G['ffn_conv_b', l] for l in range(depth)], axis=0),
        'ffn_conv_w': jnp.stack([G['ffn_conv_w', l] for l in range(depth)], axis=0),
        'pool_scale': jnp.concatenate([G['pool_scale', l] for l in range(n_a)], axis=0),
    }
    packed, offs = _pack_small([small_local[n] for n in SMALL])
    gathered = _gather_small(packed + in_flight[0, 0], "small_grad_gather")
    summed = _sum8(gathered.reshape(8, packed.shape[0], LANES), "small_grad_sum")
    grads = dict(zip(SMALL, _unpack_small(summed, offs)))
    wd = ffn_conv_w.shape[2]
    grads['ffn_conv_w'] = lax.dynamic_slice_in_dim(grads['ffn_conv_w'], q_chip * wd, wd, axis=2)
    ps = pool_scale.shape[1]
    grads['pool_scale'] = lax.dynamic_slice_in_dim(grads['pool_scale'], q_chip * ps, ps, axis=1)

    w_small, o_w = _pack_small([w[n] for n in SMALL])
    g_small, _ = _pack_small([grads[n] for n in SMALL])
    m_small, _ = _pack_small([mom[n] for n in SMALL])
    v_small, _ = _pack_small([var[n] for n in SMALL])
    upd_small = _adamw(w_small, g_small, m_small, v_small, "adamw_small")
    delta, new_m, new_v = ({n: a for n, a in zip(SMALL, _unpack_small(u, o_w))} for u in upd_small[1:])

    first, shapes0 = {}, {n: (1,) + w[n].shape[1:] for n in ('ffn_w_in', 'ffn_w_out', 'pool_w')}
    done = jnp.stack([upd_small[1][0, 0]] + [upd[n][1].reshape(-1)[0] for n in names])
    _reduce_finish(scattering.pop(0), done, qc, first, shapes0)
    names0 = list(first)
    whole0 = _share_halves([first[n] for n in names0], [BIG[n][2] for n in names0], [None] * len(names0), done,
                           "grad_share_halves_l0")
    for n, g in zip(names0, whole0):
        upd[n] = _adamw(w[n], g, mom[n], var[n], "adamw_l0_" + n, part=(0, 1, w[n].shape[0]), into=upd[n])
    for n in upd:
        grads[n], delta[n], new_m[n], new_v[n] = upd[n]

    return (loss, dx[None], *[grads[n] for n in WEIGHTS], *[delta[n] for n in WEIGHTS],
            *[new_m[n] for n in WEIGHTS], *[new_v[n] for n in WEIGHTS])
```

```python
import functools
import math

import jax
import jax.numpy as jnp
from jax import lax
from jax.experimental import pallas as pl
from jax.experimental.pallas import tpu as pltpu

F32 = jnp.float32
BF16 = jnp.bfloat16
MESH = pl.DeviceIdType.MESH
ANY = pl.BlockSpec(memory_space=pl.ANY)
HBM = pl.BlockSpec(memory_space=pltpu.HBM)
VMEM = pl.BlockSpec(memory_space=pltpu.VMEM)
SEM = pl.BlockSpec(memory_space=pltpu.SEMAPHORE)
EFFECT = pltpu.SideEffectType.DATAFLOW_SIDE_EFFECTING

HEAD_DIM = 64
N_KV_HEADS = 4
KV_DIM = 2 * N_KV_HEADS * HEAD_DIM
WINDOW = 128
BLOCK = 128
POOL_WINDOWS = (2, 4, 8, 16)
POOL_HALO = 16
CONV_HALO = 8
ROPE_THETA = 10000.0
ATTN_SCALE = 1.0 / math.sqrt(HEAD_DIM)
NEG_INF = -1e30
RMS_EPS = 1e-6
ADAM_LR, ADAM_B1, ADAM_B2, ADAM_EPS, ADAM_WD, ADAM_STEP = 0.001, 0.9, 0.999, 1e-08, 0.01, 10
N_SHARDS = 4
LANES = 128
VMEM_LIMIT_BYTES = 48 << 20

WEIGHTS = ['mix_pre_g', 'mix_post_g', 'pool_w', 'pool_scale', 'kv_norm_g', 'w_kv', 'w_q', 'w_o', 'sinks',
           'ffn_pre_g', 'ffn_post_g', 'ffn_w_in', 'ffn_conv_w', 'ffn_conv_b', 'ffn_w_out']


def _call(body, *, name, out_shape, grid=None, in_specs=None, out_specs=None, scratch_shapes=(), dims=None,
          grid_spec=None, aliases=None):
    params = pltpu.CompilerParams(dimension_semantics=dims, vmem_limit_bytes=VMEM_LIMIT_BYTES)
    kw = {} if aliases is None else dict(input_output_aliases=aliases)
    if grid_spec is not None:
        return pl.pallas_call(body, name=name, out_shape=out_shape, grid_spec=grid_spec, compiler_params=params, **kw)
    if grid is not None:
        kw['grid'] = grid
    return pl.pallas_call(body, name=name, out_shape=out_shape, in_specs=in_specs, out_specs=out_specs,
                          scratch_shapes=list(scratch_shapes), compiler_params=params, **kw)


def _tile(n, pref, mult=8):
    if n <= pref:
        return n
    for t in range(pref, 0, -1):
        if n % t == 0 and t % mult == 0:
            return t
    raise ValueError((n, pref, mult))


def _sds(shape, dtype):
    return jax.ShapeDtypeStruct(tuple(shape), dtype)


def _perm4(j):
    return (j % 2) * 2 + j // 2


def _matmul(a, b, mode, out_dtype, name, tm, tn, tk, b_blocks=False, out_perm=False):
    a2 = a.shape
    b2 = (b.shape[1], 4 * b.shape[2]) if b_blocks else b.shape
    if mode == 'nn':
        (M, K), (K2, N) = a2, b2
    elif mode == 'nt':
        (M, K), (N, K2) = a2, b2
    else:
        (K, M), (K2, N) = a2, b2
    assert K == K2, (name, a.shape, b.shape)
    tm, tn, tk = _tile(M, tm), _tile(N, tn, LANES), _tile(K, tk, LANES if mode != 'tn' else 16)
    assert M % tm == 0 and N % tn == 0 and K % tk == 0
    nk = K // tk
    grid = (N // tn, M // tm, nk)

    if mode == 'nn':
        a_spec = pl.BlockSpec((tm, tk), lambda j, i, k: (i, k))
        if b_blocks:
            assert tn == b.shape[2]
            b_spec = pl.BlockSpec((None, tk, tn), lambda j, i, k: (_perm4(j), k, 0))
        else:
            b_spec = pl.BlockSpec((tk, tn), lambda j, i, k: (k, j))
        dn = (((1,), (0,)), ((), ()))
    elif mode == 'nt':
        a_spec = pl.BlockSpec((tm, tk), lambda j, i, k: (i, k))
        if b_blocks:
            assert tk == b.shape[2]
            b_spec = pl.BlockSpec((None, tn, tk), lambda j, i, k: (_perm4(k), j, 0))
        else:
            b_spec = pl.BlockSpec((tn, tk), lambda j, i, k: (j, k))
        dn = (((1,), (1,)), ((), ()))
    else:
        a_spec = pl.BlockSpec((tk, tm), lambda j, i, k: (k, i))
        b_spec = pl.BlockSpec((tk, tn), lambda j, i, k: (k, j))
        dn = (((0,), (0,)), ((), ()))
    po = _perm4 if out_perm else (lambda j: j)
    o_spec = pl.BlockSpec((tm, tn), lambda j, i, k: (i, po(j)))

    def body(a_ref, b_ref, o_ref, *acc):
        prod = lax.dot_general(a_ref[...].astype(BF16), b_ref[...].astype(BF16), dn, preferred_element_type=F32)
        if nk == 1:
            o_ref[...] = prod.astype(o_ref.dtype)
        else:
            k = pl.program_id(2)

            @pl.when(k == 0)
            def _():
                acc[0][...] = prod

            @pl.when(k > 0)
            def _():
                acc[0][...] += prod

            @pl.when(k == nk - 1)
            def _():
                o_ref[...] = acc[0][...].astype(o_ref.dtype)

    scratch = [] if nk == 1 else [pltpu.VMEM((tm, tn), F32)]
    return _call(body, name=name, out_shape=_sds((M, N), out_dtype), grid=grid, in_specs=[a_spec, b_spec],
                 out_specs=o_spec, scratch_shapes=scratch, dims=("parallel", "parallel", "arbitrary"))(a, b)


def _rstd(x):
    return lax.rsqrt(jnp.mean(x * x, axis=-1, keepdims=True) + RMS_EPS)


def _rms_fwd(x, g, out_dtype, name):
    S, D = x.shape
    tr = _tile(S, 256)

    def body(x_ref, g_ref, o_ref):
        xv = x_ref[...]
        o_ref[...] = (xv * _rstd(xv) * g_ref[...]).astype(o_ref.dtype)

    row = pl.BlockSpec((tr, D), lambda i: (i, 0))
    vec = pl.BlockSpec((1, D), lambda i: (0, 0))
    return _call(body, name=name, out_shape=_sds((S, D), out_dtype), grid=(S // tr,), in_specs=[row, vec],
                 out_specs=row, dims=("parallel",))(x, g)


def _res_rms_fwd(x, f, g, name):
    S, D = x.shape
    tr = _tile(S, 256)

    def body(x_ref, f_ref, g_ref, o_ref):
        fv = f_ref[...]
        o_ref[...] = x_ref[...] + fv * _rstd(fv) * g_ref[...]

    row = pl.BlockSpec((tr, D), lambda i: (i, 0))
    vec = pl.BlockSpec((1, D), lambda i: (0, 0))
    return _call(body, name=name, out_shape=_sds((S, D), F32), grid=(S // tr,), in_specs=[row, row, vec],
                 out_specs=row, dims=("parallel",))(x, f, g)


def _rms_bwd_math(xin, g, dy):
    r = _rstd(xin)
    xh = xin * r
    gy = dy * g
    dx = r * (gy - xh * jnp.mean(gy * xh, axis=-1, keepdims=True))
    return dx, dy * xh


def _rms_bwd(xin, g, dy, res, name):
    S, D = xin.shape
    tr = _tile(S, 256)
    has_res = res is not None

    def body(*refs):
        if has_res:
            x_ref, g_ref, dy_ref, res_ref, dx_ref, dg_ref = refs
        else:
            x_ref, g_ref, dy_ref, dx_ref, dg_ref = refs
        dx, dgr = _rms_bwd_math(x_ref[...], g_ref[...], dy_ref[...])
        dx_ref[...] = dx + res_ref[...] if has_res else dx
        i = pl.program_id(0)

        @pl.when(i == 0)
        def _():
            dg_ref[...] = jnp.zeros_like(dg_ref)

        dg_ref[...] += jnp.sum(dgr, axis=0, keepdims=True)

    row = pl.BlockSpec((tr, D), lambda i: (i, 0))
    vec = pl.BlockSpec((1, D), lambda i: (0, 0))
    ins = [xin, g, dy] + ([res] if has_res else [])
    in_specs = [row, vec, row] + ([row] if has_res else [])
    return _call(body, name=name, out_shape=(_sds((S, D), F32), _sds((1, D), F32)), grid=(S // tr,),
                 in_specs=in_specs, out_specs=(row, vec), dims=("arbitrary",))(*ins)


def _loss_grad(y, target, name):
    S, D = y.shape
    tr = _tile(S, 256)

    def body(y_ref, t_ref, dy_ref, acc_ref):
        e = y_ref[...] - t_ref[...]
        dy_ref[...] = e * (1.0 / D)
        i = pl.program_id(0)

        @pl.when(i == 0)
        def _():
            acc_ref[...] = jnp.zeros_like(acc_ref)

        acc_ref[...] += jnp.sum(e * e, axis=0, keepdims=True)

    row = pl.BlockSpec((tr, D), lambda i: (i, 0))
    vec = pl.BlockSpec((1, D), lambda i: (0, 0))
    return _call(body, name=name, out_shape=(_sds((S, D), F32), _sds((1, D), F32)), grid=(S // tr,),
                 in_specs=[row, row], out_specs=(row, vec), dims=("arbitrary",))(y, target)


def _pool_counts(t0, rows):
    return t0 + lax.broadcasted_iota(jnp.int32, (rows, 1), 0)


def _pool_fwd(x, g, name):
    S, D = x.shape
    gc = D // len(POOL_WINDOWS)
    tp = _tile(S, 256)

    def body(x_ref, g_ref, d_ref, ext_ref):
        i = pl.program_id(0)

        @pl.when(i == 0)
        def _():
            ext_ref[pl.ds(0, POOL_HALO), :] = jnp.zeros((POOL_HALO, D), F32)

        xv = x_ref[...]
        ext_ref[pl.ds(POOL_HALO, tp), :] = xv * _rstd(xv) * g_ref[...]
        t = _pool_counts(i * tp, tp)
        for gi, w in enumerate(POOL_WINDOWS):
            cols = slice(gi * gc, (gi + 1) * gc)
            s = ext_ref[:, cols]
            h = s[POOL_HALO:]
            sh = 1
            while sh < w:
                s = s + pltpu.roll(s, sh, 0)
                sh *= 2
            cnt = jnp.minimum(t + 1, w).astype(F32)
            d_ref[:, cols] = (s[POOL_HALO:] / cnt - h).astype(d_ref.dtype)
        ext_ref[pl.ds(0, POOL_HALO), :] = ext_ref[pl.ds(tp, POOL_HALO), :]

    row = pl.BlockSpec((tp, D), lambda i: (i, 0))
    vec = pl.BlockSpec((1, D), lambda i: (0, 0))
    return _call(body, name=name, out_shape=_sds((S, D), BF16), grid=(S // tp,), in_specs=[row, vec],
                 out_specs=row, scratch_shapes=[pltpu.VMEM((tp + POOL_HALO, D), F32)], dims=("arbitrary",))(x, g)


def _pool_mm_fwd(d, wp, scale, x, gpost, name):
    S, D = x.shape
    ng = len(POOL_WINDOWS)
    gc = D // ng
    tp = _tile(S, 256)

    def body(d_ref, w_ref, sc_ref, x_ref, g_ref, y_ref, o_ref):
        for gi in range(ng):
            cols = slice(gi * gc, (gi + 1) * gc)
            y_ref[:, cols] = jnp.dot(d_ref[:, cols], w_ref[gi], preferred_element_type=F32)
        m = y_ref[...] * sc_ref[...]
        o_ref[...] = x_ref[...] + m * _rstd(m) * g_ref[...]

    row = pl.BlockSpec((tp, D), lambda i: (i, 0))
    vec = pl.BlockSpec((1, D), lambda i: (0, 0))
    wsp = pl.BlockSpec((ng, gc, gc), lambda i: (0, 0, 0))
    return _call(body, name=name, out_shape=(_sds((S, D), F32), _sds((S, D), F32)), grid=(S // tp,),
                 in_specs=[row, wsp, vec, row, vec], out_specs=(row, row), dims=("parallel",))(d, wp, scale, x, gpost)


def _pool_mm_bwd(dx, y, d, wp, scale, gpost, name):
    S, D = dx.shape
    ng = len(POOL_WINDOWS)
    gc = D // ng
    tp = _tile(S, 256)

    def body(dx_ref, y_ref, d_ref, w_ref, sc_ref, g_ref, dd_ref, dw_ref, dsc_ref, dg_ref):
        i = pl.program_id(0)

        @pl.when(i == 0)
        def _():
            dw_ref[...] = jnp.zeros_like(dw_ref)
            dsc_ref[...] = jnp.zeros_like(dsc_ref)
            dg_ref[...] = jnp.zeros_like(dg_ref)

        yv = y_ref[...]
        sc = sc_ref[...]
        dm, dgr = _rms_bwd_math(yv * sc, g_ref[...], dx_ref[...])
        dg_ref[...] += jnp.sum(dgr, axis=0, keepdims=True)
        dsc_ref[...] += jnp.sum(dm * yv, axis=0, keepdims=True)
        dyv = (dm * sc).astype(BF16)
        for gi in range(ng):
            cols = slice(gi * gc, (gi + 1) * gc)
            dyg = dyv[:, cols]
            dd_ref[:, cols] = lax.dot_general(dyg, w_ref[gi], (((1,), (1,)), ((), ())), preferred_element_type=F32)
            dw_ref[gi] += lax.dot_general(d_ref[:, cols], dyg, (((0,), (0,)), ((), ())), preferred_element_type=F32)

    row = pl.BlockSpec((tp, D), lambda i: (i, 0))
    vec = pl.BlockSpec((1, D), lambda i: (0, 0))
    wsp = pl.BlockSpec((ng, gc, gc), lambda i: (0, 0, 0))
    dwsp = pl.BlockSpec((ng, gc, gc), lambda i: (0, 0, 0))
    return _call(body, name=name,
                 out_shape=(_sds((S, D), F32), _sds((ng, gc, gc), F32), _sds((1, D), F32), _sds((1, D), F32)),
                 grid=(S // tp,), in_specs=[row, row, row, wsp, vec, vec], out_specs=(row, dwsp, vec, vec),
                 dims=("arbitrary",))(dx, y, d, wp, scale, gpost)


def _pool_bwd(dd, x, g, res, name):
    S, D = x.shape
    gc = D // len(POOL_WINDOWS)
    tp = _tile(S, 256)
    nt = S // tp

    def body(dd_ref, x_ref, g_ref, res_ref, dx_ref, dg_ref, ext_ref, dh_ref):
        i = pl.program_id(0)

        @pl.when(i == 0)
        def _():
            ext_ref[pl.ds(tp, POOL_HALO), :] = jnp.zeros((POOL_HALO, D), F32)
            dg_ref[...] = jnp.zeros_like(dg_ref)

        t = _pool_counts((nt - 1 - i) * tp, tp)
        for gi, w in enumerate(POOL_WINDOWS):
            cols = slice(gi * gc, (gi + 1) * gc)
            ddv = dd_ref[:, cols]
            ext_ref[pl.ds(0, tp), cols] = ddv / jnp.minimum(t + 1, w).astype(F32)
            s = ext_ref[:, cols]
            sh = 1
            while sh < w:
                s = s + pltpu.roll(s, tp + POOL_HALO - sh, 0)
                sh *= 2
            dh_ref[:, cols] = s[:tp] - ddv
        ext_ref[pl.ds(tp, POOL_HALO), :] = ext_ref[pl.ds(0, POOL_HALO), :]
        dx, dgr = _rms_bwd_math(x_ref[...], g_ref[...], dh_ref[...])
        dx_ref[...] = dx + res_ref[...]
        dg_ref[...] += jnp.sum(dgr, axis=0, keepdims=True)

    row = pl.BlockSpec((tp, D), lambda i: (nt - 1 - i, 0))
    vec = pl.BlockSpec((1, D), lambda i: (0, 0))
    return _call(body, name=name, out_shape=(_sds((S, D), F32), _sds((1, D), F32)), grid=(nt,),
                 in_specs=[row, row, vec, row], out_specs=(row, vec),
                 scratch_shapes=[pltpu.VMEM((tp + POOL_HALO, D), F32), pltpu.VMEM((tp, D), F32)],
                 dims=("arbitrary",))(dd, x, g, res)


def _gelu(x):
    return 0.5 * x * (1.0 + jnp.tanh(0.7978845608028654 * (x + 0.044715 * x * x * x)))


def _gelu_grad(x):
    th = jnp.tanh(0.7978845608028654 * (x + 0.044715 * x * x * x))
    return 0.5 * (1.0 + th) + 0.5 * x * (1.0 - th * th) * 0.7978845608028654 * (1.0 + 3.0 * 0.044715 * x * x)


def _conv_taps(ext_ref, cols, tt):
    e = ext_ref[:, cols]
    return e[CONV_HALO:], pltpu.roll(e, 1, 0)[CONV_HALO:], pltpu.roll(e, 2, 0)[CONV_HALO:]


def _conv_glu_fwd(u, cw, cb, name):
    S, F2 = u.shape
    wd = F2 // 4
    tt = _tile(S, 256)

    def body(u_ref, cw_ref, cb_ref, a_ref, ext_ref):
        it = pl.program_id(1)

        @pl.when(it == 0)
        def _():
            ext_ref[pl.ds(0, CONV_HALO), :] = jnp.zeros((CONV_HALO, 2 * wd), F32)

        ext_ref[pl.ds(CONV_HALO, tt), :] = u_ref[...]
        for cc in range(wd // LANES):
            act = []
            for half in range(2):
                cols = slice(half * wd + cc * LANES, half * wd + (cc + 1) * LANES)
                u0, u1, u2 = _conv_taps(ext_ref, cols, tt)
                act.append(cw_ref[2:3, cols] * u0 + cw_ref[1:2, cols] * u1 + cw_ref[0:1, cols] * u2 + cb_ref[:, cols])
            a_ref[:, cc * LANES:(cc + 1) * LANES] = (_gelu(act[0]) * act[1]).astype(a_ref.dtype)
        ext_ref[pl.ds(0, CONV_HALO), :] = ext_ref[pl.ds(tt, CONV_HALO), :]

    return _call(body, name=name, out_shape=_sds((S, F2 // 2), BF16), grid=(2, S // tt),
                 in_specs=[pl.BlockSpec((tt, 2 * wd), lambda h, t: (t, h)), pl.BlockSpec((8, 2 * wd), lambda h, t: (0, h)),
                           pl.BlockSpec((1, 2 * wd), lambda h, t: (0, h))],
                 out_specs=pl.BlockSpec((tt, wd), lambda h, t: (t, h)),
                 scratch_shapes=[pltpu.VMEM((tt + CONV_HALO, 2 * wd), F32)], dims=("parallel", "arbitrary"))(u, cw, cb)


def _conv_glu_bwd(u, da, cw, cb, name):
    S, F2 = u.shape
    wd = F2 // 4
    tt = _tile(S, 256)
    nt = S // tt
    n = tt + CONV_HALO

    def body(u_ref, uprev_ref, da_ref, cw_ref, cb_ref, du_ref, acc_ref, ext_ref, carry_ref):
        it = pl.program_id(1)

        @pl.when(it == 0)
        def _():
            carry_ref[...] = jnp.zeros_like(carry_ref)
            acc_ref[...] = jnp.zeros_like(acc_ref)

        @pl.when(it == nt - 1)
        def _():
            ext_ref[pl.ds(0, CONV_HALO), :] = jnp.zeros((CONV_HALO, 2 * wd), F32)

        @pl.when(it < nt - 1)
        def _():
            ext_ref[pl.ds(0, CONV_HALO), :] = uprev_ref[...]

        ext_ref[pl.ds(CONV_HALO, tt), :] = u_ref[...]
        for cc in range(wd // LANES):
            taps, act = [], []
            for half in range(2):
                cols = slice(half * wd + cc * LANES, half * wd + (cc + 1) * LANES)
                u0, u1, u2 = _conv_taps(ext_ref, cols, tt)
                taps.append((u0, u1, u2))
                act.append(cw_ref[2:3, cols] * u0 + cw_ref[1:2, cols] * u1 + cw_ref[0:1, cols] * u2 + cb_ref[:, cols])
            dav = da_ref[:, cc * LANES:(cc + 1) * LANES]
            dact = (dav * act[1] * _gelu_grad(act[0]), dav * _gelu(act[0]))
            for half in range(2):
                cols = slice(half * wd + cc * LANES, half * wd + (cc + 1) * LANES)
                u0, u1, u2 = taps[half]
                acc_ref[2:3, cols] += jnp.sum(dact[half] * u0, axis=0, keepdims=True)
                acc_ref[1:2, cols] += jnp.sum(dact[half] * u1, axis=0, keepdims=True)
                acc_ref[0:1, cols] += jnp.sum(dact[half] * u2, axis=0, keepdims=True)
                acc_ref[3:4, cols] += jnp.sum(dact[half], axis=0, keepdims=True)
                e = jnp.concatenate([dact[half], carry_ref[:, cols]], axis=0)
                du = (cw_ref[2:3, cols] * dact[half] + cw_ref[1:2, cols] * pltpu.roll(e, n - 1, 0)[:tt]
                      + cw_ref[0:1, cols] * pltpu.roll(e, n - 2, 0)[:tt])
                du_ref[:, cols] = du.astype(du_ref.dtype)
                carry_ref[:, cols] = dact[half][:CONV_HALO]

    rev = lambda t: nt - 1 - t
    per8 = tt // CONV_HALO
    wide = pl.BlockSpec((tt, 2 * wd), lambda h, t: (rev(t), h))
    prev = pl.BlockSpec((CONV_HALO, 2 * wd), lambda h, t: (jnp.maximum(rev(t) * per8 - 1, 0), h))
    acc = pl.BlockSpec((8, 2 * wd), lambda h, t: (0, h))
    return _call(body, name=name, out_shape=(_sds((S, F2), BF16), _sds((8, F2), F32)), grid=(2, nt),
                 in_specs=[wide, prev, pl.BlockSpec((tt, wd), lambda h, t: (rev(t), h)), acc,
                           pl.BlockSpec((1, 2 * wd), lambda h, t: (0, h))],
                 out_specs=(wide, acc),
                 scratch_shapes=[pltpu.VMEM((n, 2 * wd), F32), pltpu.VMEM((CONV_HALO, 2 * wd), F32)],
                 dims=("parallel", "arbitrary"))(u, u, da, cw, cb)


def _rope_chunk(x, cosv, sinv):
    lane = lax.broadcasted_iota(jnp.int32, x.shape, 1)
    partner = jnp.where(lane % HEAD_DIM < HEAD_DIM // 2, pltpu.roll(x, LANES - HEAD_DIM // 2, 1),
                        pltpu.roll(x, HEAD_DIM // 2, 1))
    return x * cosv + partner * sinv


def _rope(x, width, cos_t, sin_t, name):
    S = x.shape[0]
    tr = _tile(S, 256)

    def body(x_ref, c_ref, s_ref, o_ref):
        for cc in range(width // LANES):
            cols = slice(cc * LANES, (cc + 1) * LANES)
            o_ref[:, cols] = _rope_chunk(x_ref[:, cols], c_ref[...], s_ref[...])

    row = pl.BlockSpec((tr, width), lambda i: (i, 0))
    tab = pl.BlockSpec((tr, LANES), lambda i: (i, 0))
    return _call(body, name=name, out_shape=_sds((S, width), F32), grid=(S // tr,), in_specs=[row, tab, tab],
                 out_specs=row, dims=("parallel",))(x, cos_t, sin_t)


def _attn_mask(n, reps):
    row = lax.broadcasted_iota(jnp.int32, (reps * BLOCK, 2 * BLOCK), 0) & (BLOCK - 1)
    col = lax.broadcasted_iota(jnp.int32, (reps * BLOCK, 2 * BLOCK), 1)
    rel = BLOCK + row - col
    return (rel >= 0) & (rel < WINDOW) & (n * BLOCK + col - BLOCK >= 0)


def _per_head_column(values, reps):
    grp = lax.broadcasted_iota(jnp.int32, (reps * BLOCK, 1), 0) // BLOCK
    col = jnp.zeros((reps * BLOCK, 1), F32)
    for g, v in enumerate(values):
        col = jnp.where(grp == g, v, col)
    return col


def _stack_heads(ref, hk, qpk, keep, scale):
    parts = []
    for g in range(qpk):
        qc, qpar, _, kpar = _head_place(hk * qpk + g, qpk)
        x = ref[:, qc * LANES:(qc + 1) * LANES]
        if scale != 1.0:
            x = x * scale
        if qpar != kpar:
            x = pltpu.roll(x, HEAD_DIM, 1)
        parts.append(jnp.where(keep, x, 0.0).astype(BF16))
    return jnp.concatenate(parts, axis=0)


def _unstack_heads(vals, ref, hk, qpk, lane, dtype):
    pair = None
    for g in range(qpk):
        qc, qpar, _, kpar = _head_place(hk * qpk + g, qpk)
        v = vals[g * BLOCK:(g + 1) * BLOCK]
        if qpar != kpar:
            v = pltpu.roll(v, HEAD_DIM, 1)
        if qpar == 0:
            pair = v
        else:
            ref[:, qc * LANES:(qc + 1) * LANES] = jnp.where(lane < HEAD_DIM, pair, v).astype(dtype)


def _head_place(h, qpk):
    hk = h // qpk
    return h // 2, h % 2, hk // 2, hk % 2


def _attn_specs(S, D):
    nb = S // BLOCK
    kvw = KV_DIM // 2
    qsp = pl.BlockSpec((BLOCK, D), lambda n: (n, 0))
    prev = lambda n: jnp.maximum(n - 1, 0)
    kp = pl.BlockSpec((BLOCK, kvw), lambda n: (prev(n), 0))
    ko = pl.BlockSpec((BLOCK, kvw), lambda n: (n, 0))
    vp = pl.BlockSpec((BLOCK, kvw), lambda n: (prev(n), 1))
    vo = pl.BlockSpec((BLOCK, kvw), lambda n: (n, 1))
    stat = pl.BlockSpec((BLOCK, LANES), lambda n: (n, 0))
    smem = pl.BlockSpec(memory_space=pltpu.SMEM)
    return nb, kvw, qsp, kp, ko, vp, vo, stat, smem


def _attn_fwd(q, k, kv, sinks, name):
    S, D = q.shape
    nh = D // HEAD_DIM
    qpk = nh // N_KV_HEADS
    nb, kvw, qsp, kp, ko, vp, vo, stat, smem = _attn_specs(S, D)

    def body(q_ref, kp_ref, ko_ref, vp_ref, vo_ref, s_ref, o_ref, l_ref):
        n = pl.program_id(0)
        valid = _attn_mask(n, qpk)
        lane = lax.broadcasted_iota(jnp.int32, (BLOCK, LANES), 1)
        lacc = jnp.zeros((BLOCK, LANES), F32)
        for hk in range(N_KV_HEADS):
            kc, kpar = hk // 2, hk % 2
            kcols = slice(kc * LANES, (kc + 1) * LANES)
            k2 = jnp.concatenate([kp_ref[:, kcols], ko_ref[:, kcols]], axis=0).astype(BF16)
            v2 = jnp.concatenate([vp_ref[:, kcols], vo_ref[:, kcols]], axis=0).astype(BF16)
            keep = (lane >= kpar * HEAD_DIM) & (lane < (kpar + 1) * HEAD_DIM)
            qm = _stack_heads(q_ref, hk, qpk, keep, ATTN_SCALE)
            s = lax.dot_general(qm, k2, (((1,), (1,)), ((), ())), preferred_element_type=F32)
            s = jnp.where(valid, s, NEG_INF)
            sink = _per_head_column([s_ref[hk * qpk + g] for g in range(qpk)], qpk)
            m = jnp.maximum(jnp.max(s, axis=1, keepdims=True), sink)
            p = jnp.exp(s - m)
            den = jnp.sum(p, axis=1, keepdims=True) + jnp.exp(sink - m)
            of = jnp.dot(p.astype(BF16), v2, preferred_element_type=F32) / den
            lse = m + jnp.log(den)
            for g in range(qpk):
                lacc = jnp.where(lane == hk * qpk + g, lse[g * BLOCK:(g + 1) * BLOCK], lacc)
            _unstack_heads(of, o_ref, hk, qpk, lane, o_ref.dtype)
        l_ref[...] = lacc

    return _call(body, name=name, out_shape=(_sds((S, D), BF16), _sds((S, LANES), F32)), grid=(nb,),
                 in_specs=[qsp, kp, ko, vp, vo, smem], out_specs=(qsp, stat), dims=("parallel",))(q, k, k, kv, kv, sinks)


def _attn_bwd(q, k, kv, do, lse, sinks, name):
    S, D = q.shape
    nh = D // HEAD_DIM
    qpk = nh // N_KV_HEADS
    nb, kvw, qsp, kp, ko, vp, vo, stat, smem = _attn_specs(S, D)

    def body(q_ref, kp_ref, ko_ref, vp_ref, vo_ref, do_ref, l_ref, s_ref,
             dq_ref, dkp_ref, dko_ref, dvp_ref, dvo_ref, ds_ref):
        n = pl.program_id(0)

        @pl.when(n == 0)
        def _():
            ds_ref[...] = jnp.zeros_like(ds_ref)

        valid = _attn_mask(n, qpk)
        lane = lax.broadcasted_iota(jnp.int32, (BLOCK, LANES), 1)
        lane8 = lax.broadcasted_iota(jnp.int32, (8, LANES), 1)
        lv = l_ref[...]
        dsink = jnp.zeros((8, LANES), F32)
        for kc in range(N_KV_HEADS // 2):
            kcols = slice(kc * LANES, (kc + 1) * LANES)
            k2 = jnp.concatenate([kp_ref[:, kcols], ko_ref[:, kcols]], axis=0).astype(BF16)
            v2 = jnp.concatenate([vp_ref[:, kcols], vo_ref[:, kcols]], axis=0).astype(BF16)
            dk2 = jnp.zeros((2 * BLOCK, LANES), F32)
            dv2 = jnp.zeros((2 * BLOCK, LANES), F32)
            for kpar in range(2):
                hk = 2 * kc + kpar
                heads = [hk * qpk + g for g in range(qpk)]
                keep = (lane >= kpar * HEAD_DIM) & (lane < (kpar + 1) * HEAD_DIM)
                qm = _stack_heads(q_ref, hk, qpk, keep, ATTN_SCALE)
                gm = _stack_heads(do_ref, hk, qpk, keep, 1.0)
                s = lax.dot_general(qm, k2, (((1,), (1,)), ((), ())), preferred_element_type=F32)
                lh = jnp.concatenate([jnp.sum(jnp.where(lane == h, lv, 0.0), axis=1, keepdims=True) for h in heads], axis=0)
                p = jnp.where(valid, jnp.exp(s - lh), 0.0)
                dp = lax.dot_general(gm, v2, (((1,), (1,)), ((), ())), preferred_element_type=F32)
                delta = jnp.sum(p * dp, axis=1, keepdims=True)
                dsb = (p * (dp - delta)).astype(BF16)
                lost = jnp.exp(_per_head_column([s_ref[h] for h in heads], qpk) - lh) * delta
                for g, h in enumerate(heads):
                    dsink = dsink - jnp.where(lane8 == h, jnp.sum(lost[g * BLOCK:(g + 1) * BLOCK]), 0.0)
                dqf = jnp.dot(dsb, k2, preferred_element_type=F32) * ATTN_SCALE
                _unstack_heads(dqf, dq_ref, hk, qpk, lane, F32)
                dk2 = dk2 + lax.dot_general(dsb, qm, (((0,), (0,)), ((), ())), preferred_element_type=F32)
                dv2 = dv2 + lax.dot_general(p.astype(BF16), gm, (((0,), (0,)), ((), ())), preferred_element_type=F32)
            dkp_ref[:, kcols] = dk2[:BLOCK]
            dko_ref[:, kcols] = dk2[BLOCK:]
            dvp_ref[:, kcols] = dv2[:BLOCK]
            dvo_ref[:, kcols] = dv2[BLOCK:]
        ds_ref[...] += dsink

    kvo = pl.BlockSpec((BLOCK, kvw), lambda n: (n, 0))
    acc = pl.BlockSpec((8, LANES), lambda n: (0, 0))
    part = _sds((S, kvw), F32)
    return _call(body, name=name, out_shape=(_sds((S, D), F32), part, part, part, part, _sds((8, LANES), F32)),
                 grid=(nb,), in_specs=[qsp, kp, ko, vp, vo, qsp, stat, smem],
                 out_specs=(qsp, kvo, kvo, kvo, kvo, acc), dims=("arbitrary",))(q, k, k, kv, kv, do, lse, sinks)


def _kv_grad(parts, cos_t, sin_neg_t, name):
    S, kvw = parts[0][0].shape
    nb = S // BLOCK
    flat = [a for p in parts for a in p]
    nl = len(parts)

    def body(*refs):
        c_ref, s_ref, o_ref = refs[4 * nl], refs[4 * nl + 1], refs[4 * nl + 2]
        n = pl.program_id(0)
        last = n == nb - 1
        dk = jnp.zeros((BLOCK, kvw), F32)
        dv = jnp.zeros((BLOCK, kvw), F32)
        for li in range(nl):
            kn, kown, vn, vown = refs[4 * li:4 * li + 4]
            dk = dk + kown[...] + jnp.where(last, 0.0, kn[...])
            dv = dv + vown[...] + jnp.where(last, 0.0, vn[...])
        for cc in range(kvw // LANES):
            cols = slice(cc * LANES, (cc + 1) * LANES)
            o_ref[:, cols] = _rope_chunk(dk[:, cols], c_ref[...], s_ref[...])
        o_ref[:, kvw:] = dv

    own = pl.BlockSpec((BLOCK, kvw), lambda n: (n, 0))
    nxt = pl.BlockSpec((BLOCK, kvw), lambda n: (jnp.minimum(n + 1, nb - 1), 0))
    tab = pl.BlockSpec((BLOCK, LANES), lambda n: (n, 0))
    return _call(body, name=name, out_shape=_sds((S, 2 * kvw), F32), grid=(nb,),
                 in_specs=[nxt, own, nxt, own] * nl + [tab, tab],
                 out_specs=pl.BlockSpec((BLOCK, 2 * kvw), lambda n: (n, 0)), dims=("parallel",))(*flat, cos_t, sin_neg_t)


def _sum_blocks(name, qc, grid, out_shape, out_block, out_imap, ins, out_dtype=F32, into=None):
    nin = len(ins)

    def body(qc_ref, *refs):
        acc = refs[0][...].astype(F32)
        for r in refs[1:nin]:
            acc = acc + r[...].astype(F32)
        refs[-1][...] = acc.astype(refs[-1].dtype)

    in_specs = [pl.BlockSpec(b, m) for _, b, m in ins]
    operands = [a for a, _, _ in ins]
    aliases = None
    if into is not None:
        in_specs.append(ANY)
        operands.append(into)
        aliases = {1 + nin: 0}
    gs = pltpu.PrefetchScalarGridSpec(num_scalar_prefetch=1, grid=grid, in_specs=in_specs,
                                      out_specs=pl.BlockSpec(out_block, out_imap))
    return _call(body, name=name, out_shape=_sds(out_shape, out_dtype), grid_spec=gs,
                 dims=("parallel",) * len(grid), aliases=aliases)(qc, *operands)


def _adamw(w, g, m, v, name, part=None, into=None):
    shape = w.shape
    C = shape[-1]
    R = w.size // C
    k, cnt, nparts = part if part is not None else (0, 1, 1)
    tr = _tile(R // nparts, max(8, (1 << 18) // C))
    first = k * (R // nparts // tr)
    rows = cnt * (R // nparts)

    def body(w_ref, g_ref, m_ref, v_ref, *outs):
        go_ref, d_ref, nm_ref, nv_ref = outs[-4:]
        gv = g_ref[...]
        nm = ADAM_B1 * m_ref[...] + (1.0 - ADAM_B1) * gv
        nv = ADAM_B2 * v_ref[...] + (1.0 - ADAM_B2) * (gv * gv)
        m_hat = nm / (1.0 - ADAM_B1 ** ADAM_STEP)
        v_hat = nv / (1.0 - ADAM_B2 ** ADAM_STEP)
        go_ref[...] = gv
        d_ref[...] = -ADAM_LR * (m_hat / (jnp.sqrt(v_hat) + ADAM_EPS) + ADAM_WD * w_ref[...])
        nm_ref[...] = nm
        nv_ref[...] = nv

    blk = pl.BlockSpec((tr, C), lambda i: (first + i, 0))
    flat = _sds((R, C), F32)
    operands = [a.reshape(-1, C) for a in (w, g, m, v)]
    in_specs, aliases = [blk] * 4, None
    if g.size != w.size:
        assert g.size == rows * C, (name, g.shape, shape, part)
        in_specs[1] = pl.BlockSpec((tr, C), lambda i: (i, 0))
    if into is not None:
        operands += [a.reshape(R, C) for a in into]
        in_specs = in_specs + [ANY] * 4
        aliases = {4 + i: i for i in range(4)}
    outs = _call(body, name=name, out_shape=(flat,) * 4, grid=(rows // tr,), in_specs=in_specs,
                 out_specs=(blk,) * 4, dims=("parallel",), aliases=aliases)(*operands)
    return tuple(o.reshape(shape) for o in outs)


def _place():
    x, y, c = lax.axis_index("x"), lax.axis_index("y"), lax.axis_index("c")
    chips = [(1 - x, y), (x, 1 - y), (1 - x, 1 - y)]
    return x, y, c, chips


def _at(ref, nd, dims):
    idx = [slice(None)] * nd
    for d, v in dims.items():
        idx[d] = pl.ds(v[0], v[1]) if isinstance(v, tuple) else v
    return ref.at[tuple(idx)]


def _remote(src, dst, send_sem, recv_sem, dev):
    return pltpu.make_async_remote_copy(src_ref=src, dst_ref=dst, send_sem=send_sem, recv_sem=recv_sem,
                                        device_id=dev, device_id_type=MESH)


def _split_call(body, name, out_shape, in_specs, out_specs, aliases):
    return pl.pallas_call(body, name=name, out_shape=out_shape, in_specs=in_specs, out_specs=out_specs,
                          input_output_aliases=aliases,
                          compiler_params=pltpu.CompilerParams(has_side_effects=EFFECT))


def _hbm(a):
    return pltpu.with_memory_space_constraint(a, pltpu.HBM)


def _copies_start(srcs, lands, after, name, plan, ncopies):
    n, m = len(srcs), len(lands)

    def body(*refs):
        src, land = refs[:n], refs[n:n + m]
        send_sems, recv_sems, token = refs[n + m + 1], refs[n + m + 2], refs[-1]
        x, y, c, chips = _place()
        for k, (s, d, dev) in enumerate(plan(x, y, c, chips, src, land)):
            _remote(s, d, send_sems.at[k], recv_sems.at[k], dev).start()
        token[...] = jnp.zeros_like(token)

    thru = tuple(pltpu.HBM(a.shape, a.dtype) for a in list(srcs) + list(lands))
    outs = _split_call(
        body, name,
        out_shape=(pltpu.SemaphoreType.DMA((ncopies,)), pltpu.SemaphoreType.DMA((ncopies,))) + thru + (_sds((8, LANES), F32),),
        in_specs=(HBM,) * (n + m) + (ANY,), out_specs=(SEM, SEM) + (HBM,) * (n + m) + (VMEM,),
        aliases={i: 2 + i for i in range(n + m)})(*[_hbm(a) for a in srcs], *[_hbm(a) for a in lands], after)
    return dict(send=outs[0], recv=outs[1], srcs=outs[2:2 + n], lands=outs[2 + n:2 + n + m], token=outs[-1])


def _copies_wait(handle, after, name, plan):
    srcs, lands = handle['srcs'], handle['lands']
    n, m = len(srcs), len(lands)

    def body(*refs):
        src, land = refs[:n], refs[n:n + m]
        send_sems, recv_sems = refs[n + m], refs[n + m + 1]
        x, y, c, chips = _place()
        for k, (s, d, dev) in enumerate(plan(x, y, c, chips, src, land)):
            cp = _remote(s, d, send_sems.at[k], recv_sems.at[k], dev)
            cp.wait_send()
            cp.wait_recv()

    thru = tuple(pltpu.HBM(a.shape, a.dtype) for a in list(srcs) + list(lands))
    outs = _split_call(body, name, out_shape=thru, in_specs=(HBM,) * (n + m) + (SEM, SEM, ANY),
                       out_specs=(HBM,) * (n + m), aliases={i: i for i in range(n + m)})(
        *srcs, *lands, handle['send'], handle['recv'], after)
    return outs[:n], outs[n:]


def _gather_plan(x, y, c, chips, src, land, arriving):
    q = 2 * x + y
    peers = [(ch[0], ch[1], c) for ch in chips] + [(x, y, 1 - c)]
    slots = [2 * ch[0] + ch[1] for ch in chips] + [q]
    return [(s, d.at[slots[j] if arriving else q], peers[j]) for s, d in zip(src, land) for j in range(4)]


def _gather_half_plan(x, y, c, chips, src, land, arriving):
    q = 2 * x + y
    out = []
    for s, d in zip(src, land):
        hs = s.shape[0] // 2
        rows = pl.ds(c * hs, hs)
        for ch in chips:
            out.append((s.at[rows], d.at[2 * ch[0] + ch[1] if arriving else q, rows], (ch[0], ch[1], c)))
        out.append((s, d.at[q], (x, y, 1 - c)))
    return out


def _exchange_plan(x, y, c, chips, src, land, arriving):
    out = []
    for d in land:
        hs = d.shape[1] // 2
        for ch in chips:
            slot = 2 * ch[0] + ch[1]
            out.append((d.at[slot, pl.ds(c * hs, hs)], d.at[slot, pl.ds(((1 - c) if arriving else c) * hs, hs)], (x, y, 1 - c)))
    return out


def _scatter_plan(shard_axes, shapes):
    def plan(x, y, c, chips, src, land):
        out = []
        for s, d, sd, shp in zip(src, land, shard_axes, shapes):
            ss = shp[sd] // N_SHARDS
            for j, ch in enumerate(chips):
                out.append((_at(s, len(shp), {sd: ((2 * ch[0] + ch[1]) * ss, ss)}), d.at[j], (ch[0], ch[1], c)))
        return out
    return plan


def _half_dims(shape, hd, c):
    hs = shape[hd] // 2
    return {hd: (c * hs, hs)}


def _swap_halves(grads, specs, name):
    n = len(grads)
    outs_shape = []
    for a, (sd, hd) in zip(grads, specs):
        shp = list(a.shape)
        shp[hd] //= 2
        outs_shape.append(_sds(shp, F32))

    def body(*refs):
        ins, outs = refs[:n], refs[n:2 * n]
        send_sems, recv_sems = refs[2 * n:]
        x, y, c, _ = _place()
        cps = []
        for ai, (sd, hd) in enumerate(specs):
            shp = grads[ai].shape
            cp = _remote(_at(ins[ai], len(shp), _half_dims(shp, hd, 1 - c)), outs[ai],
                         send_sems.at[ai], recv_sems.at[ai], (x, y, 1 - c))
            cp.start()
            cps.append(cp)
        for cp in cps:
            cp.wait()

    return _call(body, name=name, out_shape=tuple(outs_shape), in_specs=[ANY] * n, out_specs=tuple([ANY] * n),
                 scratch_shapes=[pltpu.SemaphoreType.DMA((n,)), pltpu.SemaphoreType.DMA((n,))])(*grads)


def _share_halves(arrs, half_axes, layers, after, name):
    n = len(arrs)

    def body(*refs):
        ins, outs = refs[:n], refs[n + 1:2 * n + 1]
        send_sems, recv_sems = refs[2 * n + 1:]
        x, y, c, _ = _place()

        def half(ref, ai, which):
            shp = arrs[ai].shape
            hs = shp[half_axes[ai]] // 2
            dims = {half_axes[ai]: (which * hs, hs)}
            if layers[ai] is not None:
                dims[0] = layers[ai]
            return _at(ref, len(shp), dims)

        sends = []
        for ai in range(n):
            cp = _remote(half(ins[ai], ai, c), half(outs[ai], ai, c), send_sems.at[ai], recv_sems.at[ai], (x, y, 1 - c))
            cp.start()
            sends.append(cp)
        for ai in range(n):
            land = half(outs[ai], ai, 1 - c)
            _remote(land, land, send_sems.at[ai], recv_sems.at[ai], (x, y, c)).wait_recv()
        for cp in sends:
            cp.wait_send()

    return _call(body, name=name, out_shape=tuple(_sds(a.shape, a.dtype) for a in arrs), in_specs=[ANY] * (n + 1),
                 out_specs=tuple([ANY] * n), aliases={i: i for i in range(n)},
                 scratch_shapes=[pltpu.SemaphoreType.DMA((n,)), pltpu.SemaphoreType.DMA((n,))])(*arrs, after)


def _gather_small(v, name):
    R, C = v.shape

    def body(x_ref, out_ref, send_sems, recv_sems, local_sem):
        x, y, c, chips = _place()
        me, sibling = (x, y, c), (x, y, 1 - c)

        def rows(px, py, pc):
            return out_ref.at[pl.ds((4 * px + 2 * py + pc) * R, R), :]

        def copy(k, block, to, src=None):
            return _remote(rows(*block) if src is None else src, rows(*block), send_sems.at[k], recv_sems.at[k], to)

        mine = pltpu.make_async_copy(x_ref, rows(*me), local_sem)
        mine.start()
        first = [copy(0, me, sibling, src=x_ref)]
        first += [copy(1 + j, me, (ch[0], ch[1], c), src=x_ref) for j, ch in enumerate(chips)]
        for cp in first:
            cp.start()
        passed = [copy(4 + j, (ch[0], ch[1], c), sibling) for j, ch in enumerate(chips)]
        for j, ch in enumerate(chips):
            copy(1 + j, (ch[0], ch[1], c), me).wait_recv()
            passed[j].start()
        copy(0, sibling, me).wait_recv()
        for j, ch in enumerate(chips):
            copy(4 + j, (ch[0], ch[1], 1 - c), me).wait_recv()
        for cp in first + passed:
            cp.wait_send()
        mine.wait()

    vm = pl.BlockSpec(memory_space=pltpu.VMEM)
    return _call(body, name=name, out_shape=_sds((8 * R, C), v.dtype), in_specs=[vm], out_specs=vm,
                 scratch_shapes=[pltpu.SemaphoreType.DMA((7,)), pltpu.SemaphoreType.DMA((7,)),
                                 pltpu.SemaphoreType.DMA])(v)


def _sum8(g, name):
    _, R, C = g.shape

    def body(g_ref, o_ref):
        acc = g_ref[0]
        for d in range(1, 8):
            acc = acc + g_ref[d]
        o_ref[...] = acc

    return _call(body, name=name, out_shape=_sds((R, C), F32), in_specs=[pl.BlockSpec(memory_space=pltpu.VMEM)],
                 out_specs=pl.BlockSpec(memory_space=pltpu.VMEM))(g)


def _rope_tables(positions):
    inv_freq = 1.0 / (ROPE_THETA ** (jnp.arange(0, HEAD_DIM, 2, dtype=F32) / HEAD_DIM))
    ang = positions.astype(F32)[:, None] * inv_freq
    cosv, sinv = jnp.cos(ang), jnp.sin(ang)
    return jnp.tile(cosv, (1, 4)), jnp.tile(jnp.concatenate([-sinv, sinv], axis=1), (1, 2))


def _blocked(a):
    parts = jnp.split(a, 4, axis=-1)
    return jnp.concatenate([parts[0], parts[2], parts[1], parts[3]], axis=-1)


def _arrived(Wl, name, after):
    if callable(Wl[name]):
        Wl[name] = Wl[name](after)
    return Wl[name]


def _local_step(x, target, positions, P, weights_of, ffn_grads_done, grads_done):
    S, D = x.shape
    depth = P['mix_pre_g'].shape[0]
    n_a = depth // 2
    cos_t, sin_t = _rope_tables(positions)
    row = lambda a, l: a[l][None]
    cb = [_blocked(P['ffn_conv_b'][l])[None] for l in range(depth)]
    sv, W = {}, {}
    kv = k_rot = None
    for l in range(depth):
        t = f"l{l}"
        W[l], zero = weights_of(l, x)
        sv[l, 'x_in'] = x
        g_pre = row(P['mix_pre_g'], l) + zero
        if l < n_a:
            d = _pool_fwd(x, g_pre, "pool_fwd_" + t)
            y, x = _pool_mm_fwd(d, W[l]['pool_w'], W[l]['pool_scale'], x, row(P['mix_post_g'], l), "pool_mm_fwd_" + t)
            sv[l, 'd'], sv[l, 'y'] = d, y
        else:
            j = l - n_a
            h = _rms_fwd(x, g_pre, BF16, "q_norm_" + t)
            if l == n_a:
                hkv = _rms_fwd(x, P['kv_norm_g'][None], BF16, "kv_norm")
                kv = _matmul(hkv, W[l]['w_kv'], 'nn', F32, "kv_proj", 512, 512, 1024)
                k_rot = _rope(kv, KV_DIM // 2, cos_t, sin_t, "k_rope")
                sv['hkv'] = hkv
            qraw = _matmul(h, W[l]['w_q'], 'nn', F32, "q_proj_" + t, 512, 1024, 1024)
            q = _rope(qraw, D, cos_t, sin_t, "q_rope_" + t)
            o, lse = _attn_fwd(q, k_rot, kv, P['sinks'][j], "attn_fwd_" + t)
            m = _matmul(o, W[l]['w_o'], 'nn', F32, "o_proj_" + t, 512, 1024, 1024)
            x = _res_rms_fwd(x, m, row(P['mix_post_g'], l), "mix_post_" + t)
            sv[l, 'h'], sv[l, 'q'], sv[l, 'o'], sv[l, 'lse'], sv[l, 'm'] = h, q, o, lse, m
        sv[l, 'x1'] = x
        if 'pre_ffn' in W[l]:
            W[l].pop('pre_ffn')(x)
        h2 = _rms_fwd(x, row(P['ffn_pre_g'], l), BF16, "ffn_norm_" + t)
        w_in = _arrived(W[l], 'w_in', h2)
        u = _matmul(h2, w_in, 'nn', F32, "ffn_up_" + t, 512, w_in.shape[2], 1024, b_blocks=True)
        a = _conv_glu_fwd(u, W[l]['cw'], cb[l] + W[l].pop('tie', 0.0), "ffn_glu_" + t)
        f = _matmul(a, _arrived(W[l], 'w_out', a), 'nn', F32, "ffn_down_" + t, 512, 1024, 2816)
        x = _res_rms_fwd(x, f, row(P['ffn_post_g'], l), "ffn_post_" + t)
        sv[l, 'h2'], sv[l, 'u'], sv[l, 'a'], sv[l, 'f'] = h2, u, a, f

    dx, sq = _loss_grad(x, target, "loss")
    kv_parts = []
    zero = 0.0
    for l in reversed(range(depth)):
        t = f"l{l}"
        G = {}
        wd = W[l]['w_in'].shape[2]
        df, G['ffn_post_g'] = _rms_bwd(sv[l, 'f'], row(P['ffn_post_g'], l) + zero, dx, None, "ffn_post_bwd_" + t)
        da = _matmul(df, W[l]['w_out'], 'nt', F32, "ffn_down_dx_" + t, 512, wd, 1024)
        G['ffn_w_out'] = _matmul(sv[l, 'a'], df, 'tn', F32, "ffn_down_dw_" + t, wd, 1024, 1024)
        du, acc = _conv_glu_bwd(sv[l, 'u'], da, W[l]['cw'], cb[l], "ffn_glu_bwd_" + t)
        G['ffn_conv_w'] = _blocked(acc[0:3])
        G['ffn_conv_b'] = _blocked(acc[3:4])
        dh2 = _matmul(du, W[l]['w_in'], 'nt', F32, "ffn_up_dx_" + t, 512, 1024, wd, b_blocks=True)
        G['ffn_w_in'] = _matmul(sv[l, 'h2'], du, 'tn', F32, "ffn_up_dw_" + t, 1024, wd, 2048, out_perm=True)
        zero = ffn_grads_done(l, G, dh2)
        dx, G['ffn_pre_g'] = _rms_bwd(sv[l, 'x1'], row(P['ffn_pre_g'], l) + zero, dh2, dx, "ffn_norm_bwd_" + t)
        if l < n_a:
            dd, G['pool_w'], G['pool_scale'], G['mix_post_g'] = _pool_mm_bwd(
                dx, sv[l, 'y'], sv[l, 'd'], W[l]['pool_w'], W[l]['pool_scale'], row(P['mix_post_g'], l), "pool_mm_bwd_" + t)
            dx, G['mix_pre_g'] = _pool_bwd(dd, sv[l, 'x_in'], row(P['mix_pre_g'], l), dx, "pool_bwd_" + t)
        else:
            j = l - n_a
            dm, G['mix_post_g'] = _rms_bwd(sv[l, 'm'], row(P['mix_post_g'], l), dx, None, "mix_post_bwd_" + t)
            do = _matmul(dm, W[l]['w_o'], 'nt', F32, "o_proj_dx_" + t, 512, 1024, 1024)
            G['w_o'] = _matmul(sv[l, 'o'], dm, 'tn', F32, "o_proj_dw_" + t, 1024, 1024, 1024)
            dq, dkn, dko, dvn, dvo, dsk = _attn_bwd(sv[l, 'q'], k_rot, kv, do, sv[l, 'lse'], P['sinks'][j], "attn_bwd_" + t)
            G['sinks'] = dsk[0:1]
            kv_parts.append((dkn, dko, dvn, dvo))
            dqraw = _rope(dq, D, cos_t, -sin_t, "q_rope_bwd_" + t)
            dh = _matmul(dqraw, W[l]['w_q'], 'nt', F32, "q_proj_dx_" + t, 512, 1024, 1024)
            G['w_q'] = _matmul(sv[l, 'h'], dqraw, 'tn', F32, "q_proj_dw_" + t, 1024, 1024, 1024)
            dx, G['mix_pre_g'] = _rms_bwd(sv[l, 'x_in'], row(P['mix_pre_g'], l), dh, dx, "q_norm_bwd_" + t)
            if l == n_a:
                dkv = _kv_grad(kv_parts, cos_t, -sin_t, "kv_grad")
                dhkv = _matmul(dkv, W[l]['w_kv'], 'nt', F32, "kv_proj_dx", 512, 1024, 512)
                G['w_kv'] = _matmul(sv['hkv'], dkv, 'tn', F32, "kv_proj_dw", 1024, 512, 1024)
                dx, G['kv_norm_g'] = _rms_bwd(sv[l, 'x_in'], P['kv_norm_g'][None], dhkv, dx, "kv_norm_bwd")
        zero = grads_done(l, G, dx)
    return sq, dx


SMALL = ['mix_pre_g', 'mix_post_g', 'kv_norm_g', 'sinks', 'ffn_pre_g', 'ffn_post_g', 'ffn_conv_b', 'ffn_conv_w', 'pool_scale']
BIG = {'ffn_w_in': (1, 0, 1), 'ffn_w_out': (0, 1, 2), 'w_q': (0, 1, 2), 'w_o': (0, 1, 2), 'w_kv': (0, 1, 1),
       'pool_w': (1, 0, 1)}


def _swap_plan(half_axes, shapes):
    def plan(x, y, c, chips, src, land):
        return [(_at(s, len(shp), _half_dims(shp, hd, 1 - c)), d, (x, y, 1 - c))
                for s, d, hd, shp in zip(src, land, half_axes, shapes)]
    return plan


def _swap_start(pieces, after, name):
    arrs = [p[0] for p in pieces]
    plan = _swap_plan([p[2] for p in pieces], [a.shape for a in arrs])
    lands = []
    for a, p in zip(arrs, pieces):
        shp = list(a.shape)
        shp[p[2]] //= 2
        lands.append(lax.empty(tuple(shp), F32))
    return pieces, _copies_start(arrs, lands, after, name, plan, len(arrs)), plan


def _swap_wait(started, after, name):
    pieces, handle, plan = started
    arrs, theirs = _copies_wait(handle, after, name, plan)
    return [(a,) + p[1:] for a, p in zip(arrs, pieces)], list(theirs)


def _reduce_start(pieces, theirs, qc, after, tag):
    arrs = [p[0] for p in pieces]
    specs = [(p[1], p[2]) for p in pieces]
    sums = []
    for pi, (a, (sd, hd), r) in enumerate(zip(arrs, specs, theirs)):
        shp = r.shape
        nd = len(shp)
        if nd == 3:
            blk, grid = tuple(shp), (1,)
            mine = lambda i, s: (s[1], 0, 0)
            zero = lambda i, s: (0, 0, 0)
        elif hd == 0:
            tr = _tile(shp[0], max(16, (1 << 18) // shp[1]), 16)
            blk, grid = (tr, shp[1]), (shp[0] // tr,)
            nblk = shp[0] // tr
            mine = lambda i, s, nblk=nblk: (s[1] * nblk + i, 0)
            zero = lambda i, s: (i, 0)
        else:
            tr = _tile(shp[0], max(16, (1 << 18) // shp[1]), 16)
            blk, grid = (tr, shp[1]), (shp[0] // tr,)
            mine = lambda i, s: (i, s[1])
            zero = lambda i, s: (i, 0)
        sums.append(_sum_blocks(f"grad_chip_sum_{tag}_{pi}", qc, grid, shp, blk, zero, [(a, blk, mine), (r, blk, zero)],
                                out_dtype=BF16))

    lands = []
    for s_arr, (sd, hd) in zip(sums, specs):
        shp = list(s_arr.shape)
        shp[sd] //= N_SHARDS
        lands.append(lax.empty((3,) + tuple(shp), BF16))
    plan = _scatter_plan([sd for sd, _ in specs], [s.shape for s in sums])
    handle = _copies_start(sums, lands, after, "grad_scatter_start_" + tag, plan, 3 * len(sums))
    return dict(handle=handle, plan=plan, pieces=pieces, tag=tag)


def _reduce_finish(state, after, qc, outs, out_shapes):
    handle, pieces, tag = state['handle'], state['pieces'], state['tag']
    sums, recvd = _copies_wait(handle, after, "grad_scatter_wait_" + tag, state['plan'])
    for pi, ((a, sd, hd, oname, fixed, ohd), s_arr, r) in enumerate(zip(pieces, sums, recvd)):
        shp = r.shape[1:]
        nd = len(shp)
        lead = (fixed[0],) if fixed else ()
        none = (None,) if fixed else ()
        n_stack = out_shapes[oname][0]
        if nd == 3:
            blk, grid = tuple(shp), (1,)
            mine = lambda i, s: (0, s[0], 0)
            rk = [lambda i, s, k=k: (k, 0, 0, 0) for k in range(3)]
            oshape = (n_stack, 2 * shp[0]) + tuple(shp[1:])
            oblk = none + blk
            omap = lambda i, s, lead=lead: lead + (s[1], 0, 0)
        elif sd == 1:
            tr = _tile(shp[0], max(16, (1 << 18) // shp[1]), 16)
            blk, grid = (tr, shp[1]), (shp[0] // tr,)
            nblk = shp[0] // tr
            mine = lambda i, s: (i, s[0])
            rk = [lambda i, s, k=k: (k, i, 0) for k in range(3)]
            oshape = (n_stack, 2 * shp[0], shp[1])
            oblk = none + blk
            omap = lambda i, s, lead=lead, nblk=nblk: lead + (s[1] * nblk + i, 0)
        else:
            tr = _tile(shp[0], max(16, (1 << 18) // shp[1]), 16)
            blk, grid = (tr, shp[1]), (shp[0] // tr,)
            nblk = shp[0] // tr
            mine = lambda i, s, nblk=nblk: (s[0] * nblk + i, 0)
            rk = [lambda i, s, k=k: (k, i, 0) for k in range(3)]
            oshape = ((n_stack,) if fixed else ()) + (shp[0], 2 * shp[1])
            oblk = none + blk
            omap = lambda i, s, lead=lead: lead + (i, s[1])
        assert tuple(oshape) == tuple(out_shapes[oname]), (oname, oshape, out_shapes[oname])
        ins = [(s_arr, blk, mine)] + [(r, (None,) + blk, rk[k]) for k in range(3)]
        outs[oname] = _sum_blocks(f"grad_total_{tag}_{pi}", qc, grid, oshape, oblk, omap, ins, into=outs.get(oname))


def _pack_small(parts):
    rows, offs, r = [], [], 0
    for a in parts:
        flat = a.reshape(-1)
        nr = -(-flat.size // (8 * LANES)) * 8
        rows.append(jnp.pad(flat, (0, nr * LANES - flat.size)).reshape(nr, LANES))
        offs.append((r, nr, a.shape))
        r += nr
    return jnp.concatenate(rows, axis=0), offs


def _unpack_small(packed, offs):
    return [packed[r:r + nr].reshape(-1)[:math.prod(shape)].reshape(shape) for r, nr, shape in offs]


def kernel(x, positions, mix_pre_g, mix_post_g, pool_w, pool_scale, kv_norm_g, w_kv, w_q, w_o, sinks, ffn_pre_g, ffn_post_g, ffn_w_in, ffn_conv_w, ffn_conv_b, ffn_w_out, loss_target, m_mix_pre_g, m_mix_post_g, m_pool_w, m_pool_scale, m_kv_norm_g, m_w_kv, m_w_q, m_w_o, m_sinks, m_ffn_pre_g, m_ffn_post_g, m_ffn_w_in, m_ffn_conv_w, m_ffn_conv_b, m_ffn_w_out, v_mix_pre_g, v_mix_post_g, v_pool_w, v_pool_scale, v_kv_norm_g, v_w_kv, v_w_q, v_w_o, v_sinks, v_ffn_pre_g, v_ffn_post_g, v_ffn_w_in, v_ffn_conv_w, v_ffn_conv_b, v_ffn_w_out):
    w = dict(mix_pre_g=mix_pre_g, mix_post_g=mix_post_g, pool_w=pool_w, pool_scale=pool_scale, kv_norm_g=kv_norm_g,
             w_kv=w_kv, w_q=w_q, w_o=w_o, sinks=sinks, ffn_pre_g=ffn_pre_g, ffn_post_g=ffn_post_g, ffn_w_in=ffn_w_in,
             ffn_conv_w=ffn_conv_w, ffn_conv_b=ffn_conv_b, ffn_w_out=ffn_w_out)
    mom = dict(mix_pre_g=m_mix_pre_g, mix_post_g=m_mix_post_g, pool_w=m_pool_w, pool_scale=m_pool_scale,
               kv_norm_g=m_kv_norm_g, w_kv=m_w_kv, w_q=m_w_q, w_o=m_w_o, sinks=m_sinks, ffn_pre_g=m_ffn_pre_g,
               ffn_post_g=m_ffn_post_g, ffn_w_in=m_ffn_w_in, ffn_conv_w=m_ffn_conv_w, ffn_conv_b=m_ffn_conv_b,
               ffn_w_out=m_ffn_w_out)
    var = dict(mix_pre_g=v_mix_pre_g, mix_post_g=v_mix_post_g, pool_w=v_pool_w, pool_scale=v_pool_scale,
               kv_norm_g=v_kv_norm_g, w_kv=v_w_kv, w_q=v_w_q, w_o=v_w_o, sinks=v_sinks, ffn_pre_g=v_ffn_pre_g,
               ffn_post_g=v_ffn_post_g, ffn_w_in=v_ffn_w_in, ffn_conv_w=v_ffn_conv_w, ffn_conv_b=v_ffn_conv_b,
               ffn_w_out=v_ffn_w_out)
    depth = mix_pre_g.shape[0]
    q_chip = 2 * lax.axis_index("x") + lax.axis_index("y")
    qc = jnp.stack([q_chip, lax.axis_index("c")]).astype(jnp.int32)

    n_a = depth // 2
    D = x.shape[-1]
    gc = pool_w.shape[3]

    def layer_shards(l):
        first = []
        if l < n_a:
            first.append(('pool_w', pool_w[l].astype(BF16)))
        else:
            first += [('w_q', w_q[l - n_a].astype(BF16)), ('w_o', w_o[l - n_a].astype(BF16))]
            if l == n_a:
                first.append(('w_kv', w_kv.astype(BF16)))
        ffn = [('w_in', ffn_w_in[l].astype(BF16)), ('w_out', ffn_w_out[l].astype(BF16))]
        if l == 0:
            return [first + [('conv_w', ffn_conv_w), ('pool_scale', pool_scale)], ffn[:1], ffn[1:]]
        return [first + ffn]

    def start_group(items, after, name, plan):
        srcs = [a for _, a in items]
        lands = [lax.empty((N_SHARDS,) + a.shape, a.dtype) for a in srcs]
        handle = _copies_start(srcs, lands, after, name, functools.partial(plan, arriving=False), 4 * len(srcs))
        return [n for n, _ in items], handle, plan

    def wait_group(started, after, name):
        names, handle, plan = started
        _, lands = _copies_wait(handle, after, name, functools.partial(plan, arriving=True))
        return dict(zip(names, lands))

    def start_exchange(got, after, name):
        names = list(got)
        handle = _copies_start([], [got[n] for n in names], after, name,
                               functools.partial(_exchange_plan, arriving=False), 3 * len(names))
        return names, handle, _exchange_plan

    groups0 = layer_shards(0)
    small0 = start_group(groups0[0], x, "weight_gather_start_l0_small", _gather_plan)
    in0 = start_group(groups0[1], small0[1]['token'], "weight_gather_start_l0_in", _gather_half_plan)
    out0 = start_group(groups0[2], in0[1]['token'], "weight_gather_start_l0_out", _gather_half_plan)
    pending, shared, steps = {}, {}, {}

    def start_next(l, after):
        pending[l + 1] = start_group(layer_shards(l + 1)[0], after, f"weight_gather_start_l{l + 1}", _gather_plan)
        return pending[l + 1][1]['token'][0, 0]

    def pre_ffn0(after):
        landed = wait_group(in0, after, "weight_gather_wait_l0_in")
        steps['in'] = start_exchange(landed, after, "weight_exchange_start_l0_in")

    def w_in0(Wl, after):
        both = wait_group(steps['in'], after, "weight_exchange_wait_l0_in")
        Wl['tie'] = start_next(0, both['w_in'])
        return both['w_in']

    def w_out0(after):
        landed = wait_group(out0, after, "weight_gather_wait_l0_out")
        both = wait_group(start_exchange(landed, after, "weight_exchange_start_l0_out"), after,
                          "weight_exchange_wait_l0_out")
        return both['w_out'].reshape(-1, D)

    def weights_of(l, x_now):
        zero = 0.0
        if l == 0:
            got = wait_group(small0, out0[1]['token'], "weight_gather_wait_l0_small")
            shared['conv_w'] = got['conv_w']
            shared['pool_scale'] = got['pool_scale'].transpose(1, 0, 2).reshape(n_a, D)
        else:
            got = wait_group(pending.pop(l), x_now, f"weight_gather_wait_l{l}")
            if l + 1 < depth:
                zero = start_next(l, got['w_in'])
        taps = jnp.concatenate([shared['conv_w'][p, l] for p in (0, 2, 1, 3)], axis=-1)
        Wl = dict(cw=jnp.pad(taps, ((0, 5), (0, 0))))
        if l == 0:
            Wl['pre_ffn'], Wl['w_in'], Wl['w_out'] = pre_ffn0, functools.partial(w_in0, Wl), w_out0
        else:
            Wl['w_in'], Wl['w_out'] = got['w_in'], got['w_out'].reshape(-1, D)
        if l < n_a:
            Wl['pool_w'] = got['pool_w'].transpose(1, 0, 2, 3).reshape(-1, gc, gc)
            Wl['pool_scale'] = shared['pool_scale'][l][None]
        else:
            Wl['w_q'], Wl['w_o'] = got['w_q'].reshape(D, D), got['w_o'].reshape(D, D)
            if l == n_a:
                Wl['w_kv'] = got['w_kv'].reshape(D, -1)
        return Wl, zero

    big_shapes = {n: w[n].shape for n in BIG}
    big, G, scattering = {}, {}, {}

    swapping = {}

    def piece(n, g, l):
        lead = {0: (l if n.startswith('ffn') or n == 'pool_w' else l - n_a)} if len(big_shapes[n]) > 2 else {}
        return (g, BIG[n][0], BIG[n][1], n, lead, BIG[n][2])

    def ffn_grads_done(l, Gl, after):
        swapping[l] = _swap_start([piece(n, Gl[n], l) for n in ('ffn_w_in', 'ffn_w_out')], after, f"grad_swap_start_l{l}")
        return swapping[l][1]['token'][0, 0]

    def grads_done(l, Gl, dx_now):
        for n, g in Gl.items():
            if n not in BIG:
                G[n, l] = g
        pieces, theirs = _swap_wait(swapping.pop(l), dx_now, f"grad_swap_wait_l{l}")
        rest = [piece(n, Gl[n], l) for n in BIG if n in Gl and not n.startswith('ffn')]
        theirs += list(_swap_halves([p[0] for p in rest], [(p[1], p[2]) for p in rest], f"grad_swap_halves_l{l}"))
        scattering[l] = _reduce_start(pieces + rest, theirs, qc, dx_now, f"l{l}")
        if l + 1 in scattering:
            _reduce_finish(scattering.pop(l + 1), dx_now, qc, big, big_shapes)
        return scattering[l]['handle']['token'][0, 0]

    P = {n: w[n] for n in ('mix_pre_g', 'mix_post_g', 'kv_norm_g', 'sinks', 'ffn_pre_g', 'ffn_post_g', 'ffn_conv_b')}
    sq, dx = _local_step(x[0], loss_target[0], positions[0], P, weights_of, ffn_grads_done, grads_done)
    loss = 0.5 / D * lax.psum(jnp.sum(sq), ("x", "y", "c"))

    late = {'ffn_w_in': (1, depth - 1), 'ffn_w_out': (1, depth - 1), 'pool_w': (1, n_a - 1), 'w_q': None, 'w_o': None,
            'w_kv': None}
    names = list(big)
    in_flight = scattering[0]['handle']['token']
    whole = _share_halves([big[n] for n in names], [BIG[n][2] for n in names], [late[n] for n in names], in_flight,
                          "grad_share_halves_late")
    upd = {}
    for n, g in zip(names, whole):
        part = None if late[n] is None else late[n] + (w[n].shape[0],)
        upd[n] = _adamw(w[n], g, mom[n], var[n], "adamw_late_" + n, part=part)

    small_local = {
        'mix_pre_g': jnp.concatenate([G['mix_pre_g', l] for l in range(depth)], axis=0),
        'mix_post_g': jnp.concatenate([G['mix_post_g', l] for l in range(depth)], axis=0),
        'kv_norm_g': G['kv_norm_g', n_a][0],
        'sinks': jnp.concatenate([G['sinks', l][:, :sinks.shape[1]] for l in range(n_a, depth)], axis=0),
        'ffn_pre_g': jnp.concatenate([G['ffn_pre_g', l] for l in range(depth)], axis=0),
        'ffn_post_g': jnp.concatenate([G['ffn_post_g', l] for l in range(depth)], axis=0),
        'ffn_conv_b': jnp.concatenate([G['ffn_conv_b', l] for l in range(depth)], axis=0),
        'ffn_conv_w': jnp.stack([G['ffn_conv_w', l] for l in range(depth)], axis=0),
        'pool_scale': jnp.concatenate([G['pool_scale', l] for l in range(n_a)], axis=0),
    }
    packed, offs = _pack_small([small_local[n] for n in SMALL])
    gathered = _gather_small(packed + in_flight[0, 0], "small_grad_gather")
    summed = _sum8(gathered.reshape(8, packed.shape[0], LANES), "small_grad_sum")
    grads = dict(zip(SMALL, _unpack_small(summed, offs)))
    wd = ffn_conv_w.shape[2]
    grads['ffn_conv_w'] = lax.dynamic_slice_in_dim(grads['ffn_conv_w'], q_chip * wd, wd, axis=2)
    ps = pool_scale.shape[1]
    grads['pool_scale'] = lax.dynamic_slice_in_dim(grads['pool_scale'], q_chip * ps, ps, axis=1)

    delta, new_m, new_v = {}, {}, {}
    for n in SMALL:
        upd[n] = _adamw(w[n], grads[n], mom[n], var[n], "adamw_" + n)

    first, shapes0 = {}, {n: (1,) + w[n].shape[1:] for n in ('ffn_w_in', 'ffn_w_out', 'pool_w')}
    done = jnp.stack([upd[n][1][(-1,) * upd[n][1].ndim] for n in upd])
    _reduce_finish(scattering.pop(0), done, qc, first, shapes0)
    names0 = list(first)
    whole0 = _share_halves([first[n] for n in names0], [BIG[n][2] for n in names0], [None] * len(names0), done,
                           "grad_share_halves_l0")
    for n, g in zip(names0, whole0):
        upd[n] = _adamw(w[n], g, mom[n], var[n], "adamw_l0_" + n, part=(0, 1, w[n].shape[0]), into=upd[n])
    for n in upd:
        grads[n], delta[n], new_m[n], new_v[n] = upd[n]

    return (loss, dx[None], *[grads[n] for n in WEIGHTS], *[delta[n] for n in WEIGHTS],
            *[new_m[n] for n in WEIGHTS], *[new_v[n] for n in WEIGHTS])
```

```python
import functools
import math

import jax
import jax.numpy as jnp
from jax import lax
from jax.experimental import pallas as pl
from jax.experimental.pallas import tpu as pltpu

F32 = jnp.float32
BF16 = jnp.bfloat16
MESH = pl.DeviceIdType.MESH
ANY = pl.BlockSpec(memory_space=pl.ANY)
HBM = pl.BlockSpec(memory_space=pltpu.HBM)
VMEM = pl.BlockSpec(memory_space=pltpu.VMEM)
SEM = pl.BlockSpec(memory_space=pltpu.SEMAPHORE)
EFFECT = pltpu.SideEffectType.DATAFLOW_SIDE_EFFECTING

HEAD_DIM = 64
N_KV_HEADS = 4
KV_DIM = 2 * N_KV_HEADS * HEAD_DIM
WINDOW = 128
BLOCK = 128
POOL_WINDOWS = (2, 4, 8, 16)
POOL_HALO = 16
CONV_HALO = 8
ROPE_THETA = 10000.0
ATTN_SCALE = 1.0 / math.sqrt(HEAD_DIM)
NEG_INF = -1e30
RMS_EPS = 1e-6
ADAM_LR, ADAM_B1, ADAM_B2, ADAM_EPS, ADAM_WD, ADAM_STEP = 0.001, 0.9, 0.999, 1e-08, 0.01, 10
N_SHARDS = 4
LANES = 128
VMEM_LIMIT_BYTES = 48 << 20

WEIGHTS = ['mix_pre_g', 'mix_post_g', 'pool_w', 'pool_scale', 'kv_norm_g', 'w_kv', 'w_q', 'w_o', 'sinks',
           'ffn_pre_g', 'ffn_post_g', 'ffn_w_in', 'ffn_conv_w', 'ffn_conv_b', 'ffn_w_out']


def _call(body, *, name, out_shape, grid=None, in_specs=None, out_specs=None, scratch_shapes=(), dims=None,
          grid_spec=None, aliases=None):
    params = pltpu.CompilerParams(dimension_semantics=dims, vmem_limit_bytes=VMEM_LIMIT_BYTES)
    kw = {} if aliases is None else dict(input_output_aliases=aliases)
    if grid_spec is not None:
        return pl.pallas_call(body, name=name, out_shape=out_shape, grid_spec=grid_spec, compiler_params=params, **kw)
    if grid is not None:
        kw['grid'] = grid
    return pl.pallas_call(body, name=name, out_shape=out_shape, in_specs=in_specs, out_specs=out_specs,
                          scratch_shapes=list(scratch_shapes), compiler_params=params, **kw)


def _tile(n, pref, mult=8):
    if n <= pref:
        return n
    for t in range(pref, 0, -1):
        if n % t == 0 and t % mult == 0:
            return t
    raise ValueError((n, pref, mult))


def _sds(shape, dtype):
    return jax.ShapeDtypeStruct(tuple(shape), dtype)


def _perm4(j):
    return (j % 2) * 2 + j // 2


def _matmul(a, b, mode, out_dtype, name, tm, tn, tk, b_blocks=False, out_perm=False):
    a2 = a.shape
    b2 = (b.shape[1], 4 * b.shape[2]) if b_blocks else b.shape
    if mode == 'nn':
        (M, K), (K2, N) = a2, b2
    elif mode == 'nt':
        (M, K), (N, K2) = a2, b2
    else:
        (K, M), (K2, N) = a2, b2
    assert K == K2, (name, a.shape, b.shape)
    tm, tn, tk = _tile(M, tm), _tile(N, tn, LANES), _tile(K, tk, LANES if mode != 'tn' else 16)
    assert M % tm == 0 and N % tn == 0 and K % tk == 0
    nk = K // tk
    grid = (N // tn, M // tm, nk)

    if mode == 'nn':
        a_spec = pl.BlockSpec((tm, tk), lambda j, i, k: (i, k))
        if b_blocks:
            assert tn == b.shape[2]
            b_spec = pl.BlockSpec((None, tk, tn), lambda j, i, k: (_perm4(j), k, 0))
        else:
            b_spec = pl.BlockSpec((tk, tn), lambda j, i, k: (k, j))
        dn = (((1,), (0,)), ((), ()))
    elif mode == 'nt':
        a_spec = pl.BlockSpec((tm, tk), lambda j, i, k: (i, k))
        if b_blocks:
            assert tk == b.shape[2]
            b_spec = pl.BlockSpec((None, tn, tk), lambda j, i, k: (_perm4(k), j, 0))
        else:
            b_spec = pl.BlockSpec((tn, tk), lambda j, i, k: (j, k))
        dn = (((1,), (1,)), ((), ()))
    else:
        a_spec = pl.BlockSpec((tk, tm), lambda j, i, k: (k, i))
        b_spec = pl.BlockSpec((tk, tn), lambda j, i, k: (k, j))
        dn = (((0,), (0,)), ((), ()))
    po = _perm4 if out_perm else (lambda j: j)
    o_spec = pl.BlockSpec((tm, tn), lambda j, i, k: (i, po(j)))

    def body(a_ref, b_ref, o_ref, *acc):
        prod = lax.dot_general(a_ref[...].astype(BF16), b_ref[...].astype(BF16), dn, preferred_element_type=F32)
        if nk == 1:
            o_ref[...] = prod.astype(o_ref.dtype)
        else:
            k = pl.program_id(2)

            @pl.when(k == 0)
            def _():
                acc[0][...] = prod

            @pl.when(k > 0)
            def _():
                acc[0][...] += prod

            @pl.when(k == nk - 1)
            def _():
                o_ref[...] = acc[0][...].astype(o_ref.dtype)

    scratch = [] if nk == 1 else [pltpu.VMEM((tm, tn), F32)]
    return _call(body, name=name, out_shape=_sds((M, N), out_dtype), grid=grid, in_specs=[a_spec, b_spec],
                 out_specs=o_spec, scratch_shapes=scratch, dims=("parallel", "parallel", "arbitrary"))(a, b)


def _rstd(x):
    return lax.rsqrt(jnp.mean(x * x, axis=-1, keepdims=True) + RMS_EPS)


def _rms_fwd(x, g, out_dtype, name):
    S, D = x.shape
    tr = _tile(S, 256)

    def body(x_ref, g_ref, o_ref):
        xv = x_ref[...]
        o_ref[...] = (xv * _rstd(xv) * g_ref[...]).astype(o_ref.dtype)

    row = pl.BlockSpec((tr, D), lambda i: (i, 0))
    vec = pl.BlockSpec((1, D), lambda i: (0, 0))
    return _call(body, name=name, out_shape=_sds((S, D), out_dtype), grid=(S // tr,), in_specs=[row, vec],
                 out_specs=row, dims=("parallel",))(x, g)


def _res_rms_fwd(x, f, g, name):
    S, D = x.shape
    tr = _tile(S, 256)

    def body(x_ref, f_ref, g_ref, o_ref):
        fv = f_ref[...]
        o_ref[...] = x_ref[...] + fv * _rstd(fv) * g_ref[...]

    row = pl.BlockSpec((tr, D), lambda i: (i, 0))
    vec = pl.BlockSpec((1, D), lambda i: (0, 0))
    return _call(body, name=name, out_shape=_sds((S, D), F32), grid=(S // tr,), in_specs=[row, row, vec],
                 out_specs=row, dims=("parallel",))(x, f, g)


def _rms_bwd_math(xin, g, dy):
    r = _rstd(xin)
    xh = xin * r
    gy = dy * g
    dx = r * (gy - xh * jnp.mean(gy * xh, axis=-1, keepdims=True))
    return dx, dy * xh


def _rms_bwd(xin, g, dy, res, name):
    S, D = xin.shape
    tr = _tile(S, 256)
    has_res = res is not None

    def body(*refs):
        if has_res:
            x_ref, g_ref, dy_ref, res_ref, dx_ref, dg_ref = refs
        else:
            x_ref, g_ref, dy_ref, dx_ref, dg_ref = refs
        dx, dgr = _rms_bwd_math(x_ref[...], g_ref[...], dy_ref[...])
        dx_ref[...] = dx + res_ref[...] if has_res else dx
        i = pl.program_id(0)

        @pl.when(i == 0)
        def _():
            dg_ref[...] = jnp.zeros_like(dg_ref)

        dg_ref[...] += jnp.sum(dgr, axis=0, keepdims=True)

    row = pl.BlockSpec((tr, D), lambda i: (i, 0))
    vec = pl.BlockSpec((1, D), lambda i: (0, 0))
    ins = [xin, g, dy] + ([res] if has_res else [])
    in_specs = [row, vec, row] + ([row] if has_res else [])
    return _call(body, name=name, out_shape=(_sds((S, D), F32), _sds((1, D), F32)), grid=(S // tr,),
                 in_specs=in_specs, out_specs=(row, vec), dims=("arbitrary",))(*ins)


def _loss_grad(y, target, name):
    S, D = y.shape
    tr = _tile(S, 256)

    def body(y_ref, t_ref, dy_ref, acc_ref):
        e = y_ref[...] - t_ref[...]
        dy_ref[...] = e * (1.0 / D)
        i = pl.program_id(0)

        @pl.when(i == 0)
        def _():
            acc_ref[...] = jnp.zeros_like(acc_ref)

        acc_ref[...] += jnp.sum(e * e, axis=0, keepdims=True)

    row = pl.BlockSpec((tr, D), lambda i: (i, 0))
    vec = pl.BlockSpec((1, D), lambda i: (0, 0))
    return _call(body, name=name, out_shape=(_sds((S, D), F32), _sds((1, D), F32)), grid=(S // tr,),
                 in_specs=[row, row], out_specs=(row, vec), dims=("arbitrary",))(y, target)


def _pool_counts(t0, rows):
    return t0 + lax.broadcasted_iota(jnp.int32, (rows, 1), 0)


def _pool_fwd(x, g, name):
    S, D = x.shape
    gc = D // len(POOL_WINDOWS)
    tp = _tile(S, 256)

    def body(x_ref, g_ref, d_ref, ext_ref):
        i = pl.program_id(0)

        @pl.when(i == 0)
        def _():
            ext_ref[pl.ds(0, POOL_HALO), :] = jnp.zeros((POOL_HALO, D), F32)

        xv = x_ref[...]
        ext_ref[pl.ds(POOL_HALO, tp), :] = xv * _rstd(xv) * g_ref[...]
        t = _pool_counts(i * tp, tp)
        for gi, w in enumerate(POOL_WINDOWS):
            cols = slice(gi * gc, (gi + 1) * gc)
            s = ext_ref[:, cols]
            h = s[POOL_HALO:]
            sh = 1
            while sh < w:
                s = s + pltpu.roll(s, sh, 0)
                sh *= 2
            cnt = jnp.minimum(t + 1, w).astype(F32)
            d_ref[:, cols] = (s[POOL_HALO:] / cnt - h).astype(d_ref.dtype)
        ext_ref[pl.ds(0, POOL_HALO), :] = ext_ref[pl.ds(tp, POOL_HALO), :]

    row = pl.BlockSpec((tp, D), lambda i: (i, 0))
    vec = pl.BlockSpec((1, D), lambda i: (0, 0))
    return _call(body, name=name, out_shape=_sds((S, D), BF16), grid=(S // tp,), in_specs=[row, vec],
                 out_specs=row, scratch_shapes=[pltpu.VMEM((tp + POOL_HALO, D), F32)], dims=("arbitrary",))(x, g)


def _pool_mm_fwd(d, wp, scale, x, gpost, name):
    S, D = x.shape
    ng = len(POOL_WINDOWS)
    gc = D // ng
    tp = _tile(S, 256)

    def body(d_ref, w_ref, sc_ref, x_ref, g_ref, y_ref, o_ref):
        for gi in range(ng):
            cols = slice(gi * gc, (gi + 1) * gc)
            y_ref[:, cols] = jnp.dot(d_ref[:, cols], w_ref[gi], preferred_element_type=F32)
        m = y_ref[...] * sc_ref[...]
        o_ref[...] = x_ref[...] + m * _rstd(m) * g_ref[...]

    row = pl.BlockSpec((tp, D), lambda i: (i, 0))
    vec = pl.BlockSpec((1, D), lambda i: (0, 0))
    wsp = pl.BlockSpec((ng, gc, gc), lambda i: (0, 0, 0))
    return _call(body, name=name, out_shape=(_sds((S, D), F32), _sds((S, D), F32)), grid=(S // tp,),
                 in_specs=[row, wsp, vec, row, vec], out_specs=(row, row), dims=("parallel",))(d, wp, scale, x, gpost)


def _pool_mm_bwd(dx, y, d, wp, scale, gpost, name):
    S, D = dx.shape
    ng = len(POOL_WINDOWS)
    gc = D // ng
    tp = _tile(S, 256)

    def body(dx_ref, y_ref, d_ref, w_ref, sc_ref, g_ref, dd_ref, dw_ref, dsc_ref, dg_ref):
        i = pl.program_id(0)

        @pl.when(i == 0)
        def _():
            dw_ref[...] = jnp.zeros_like(dw_ref)
            dsc_ref[...] = jnp.zeros_like(dsc_ref)
            dg_ref[...] = jnp.zeros_like(dg_ref)

        yv = y_ref[...]
        sc = sc_ref[...]
        dm, dgr = _rms_bwd_math(yv * sc, g_ref[...], dx_ref[...])
        dg_ref[...] += jnp.sum(dgr, axis=0, keepdims=True)
        dsc_ref[...] += jnp.sum(dm * yv, axis=0, keepdims=True)
        dyv = (dm * sc).astype(BF16)
        for gi in range(ng):
            cols = slice(gi * gc, (gi + 1) * gc)
            dyg = dyv[:, cols]
            dd_ref[:, cols] = lax.dot_general(dyg, w_ref[gi], (((1,), (1,)), ((), ())), preferred_element_type=F32)
            dw_ref[gi] += lax.dot_general(d_ref[:, cols], dyg, (((0,), (0,)), ((), ())), preferred_element_type=F32)

    row = pl.BlockSpec((tp, D), lambda i: (i, 0))
    vec = pl.BlockSpec((1, D), lambda i: (0, 0))
    wsp = pl.BlockSpec((ng, gc, gc), lambda i: (0, 0, 0))
    dwsp = pl.BlockSpec((ng, gc, gc), lambda i: (0, 0, 0))
    return _call(body, name=name,
                 out_shape=(_sds((S, D), F32), _sds((ng, gc, gc), F32), _sds((1, D), F32), _sds((1, D), F32)),
                 grid=(S // tp,), in_specs=[row, row, row, wsp, vec, vec], out_specs=(row, dwsp, vec, vec),
                 dims=("arbitrary",))(dx, y, d, wp, scale, gpost)


def _pool_bwd(dd, x, g, res, name):
    S, D = x.shape
    gc = D // len(POOL_WINDOWS)
    tp = _tile(S, 256)
    nt = S // tp

    def body(dd_ref, x_ref, g_ref, res_ref, dx_ref, dg_ref, ext_ref, dh_ref):
        i = pl.program_id(0)

        @pl.when(i == 0)
        def _():
            ext_ref[pl.ds(tp, POOL_HALO), :] = jnp.zeros((POOL_HALO, D), F32)
            dg_ref[...] = jnp.zeros_like(dg_ref)

        t = _pool_counts((nt - 1 - i) * tp, tp)
        for gi, w in enumerate(POOL_WINDOWS):
            cols = slice(gi * gc, (gi + 1) * gc)
            ddv = dd_ref[:, cols]
            ext_ref[pl.ds(0, tp), cols] = ddv / jnp.minimum(t + 1, w).astype(F32)
            s = ext_ref[:, cols]
            sh = 1
            while sh < w:
                s = s + pltpu.roll(s, tp + POOL_HALO - sh, 0)
                sh *= 2
            dh_ref[:, cols] = s[:tp] - ddv
        ext_ref[pl.ds(tp, POOL_HALO), :] = ext_ref[pl.ds(0, POOL_HALO), :]
        dx, dgr = _rms_bwd_math(x_ref[...], g_ref[...], dh_ref[...])
        dx_ref[...] = dx + res_ref[...]
        dg_ref[...] += jnp.sum(dgr, axis=0, keepdims=True)

    row = pl.BlockSpec((tp, D), lambda i: (nt - 1 - i, 0))
    vec = pl.BlockSpec((1, D), lambda i: (0, 0))
    return _call(body, name=name, out_shape=(_sds((S, D), F32), _sds((1, D), F32)), grid=(nt,),
                 in_specs=[row, row, vec, row], out_specs=(row, vec),
                 scratch_shapes=[pltpu.VMEM((tp + POOL_HALO, D), F32), pltpu.VMEM((tp, D), F32)],
                 dims=("arbitrary",))(dd, x, g, res)


def _gelu(x):
    return 0.5 * x * (1.0 + jnp.tanh(0.7978845608028654 * (x + 0.044715 * x * x * x)))


def _gelu_grad(x):
    th = jnp.tanh(0.7978845608028654 * (x + 0.044715 * x * x * x))
    return 0.5 * (1.0 + th) + 0.5 * x * (1.0 - th * th) * 0.7978845608028654 * (1.0 + 3.0 * 0.044715 * x * x)


def _conv_taps(ext_ref, cols, tt):
    e = ext_ref[:, cols]
    return e[CONV_HALO:], pltpu.roll(e, 1, 0)[CONV_HALO:], pltpu.roll(e, 2, 0)[CONV_HALO:]


def _conv_glu_fwd(u, cw, cb, name):
    S, F2 = u.shape
    wd = F2 // 4
    tt = _tile(S, 256)

    def body(u_ref, cw_ref, cb_ref, a_ref, ext_ref):
        it = pl.program_id(1)

        @pl.when(it == 0)
        def _():
            ext_ref[pl.ds(0, CONV_HALO), :] = jnp.zeros((CONV_HALO, 2 * wd), F32)

        ext_ref[pl.ds(CONV_HALO, tt), :] = u_ref[...]
        for cc in range(wd // LANES):
            act = []
            for half in range(2):
                cols = slice(half * wd + cc * LANES, half * wd + (cc + 1) * LANES)
                u0, u1, u2 = _conv_taps(ext_ref, cols, tt)
                act.append(cw_ref[2:3, cols] * u0 + cw_ref[1:2, cols] * u1 + cw_ref[0:1, cols] * u2 + cb_ref[:, cols])
            a_ref[:, cc * LANES:(cc + 1) * LANES] = (_gelu(act[0]) * act[1]).astype(a_ref.dtype)
        ext_ref[pl.ds(0, CONV_HALO), :] = ext_ref[pl.ds(tt, CONV_HALO), :]

    return _call(body, name=name, out_shape=_sds((S, F2 // 2), BF16), grid=(2, S // tt),
                 in_specs=[pl.BlockSpec((tt, 2 * wd), lambda h, t: (t, h)), pl.BlockSpec((8, 2 * wd), lambda h, t: (0, h)),
                           pl.BlockSpec((1, 2 * wd), lambda h, t: (0, h))],
                 out_specs=pl.BlockSpec((tt, wd), lambda h, t: (t, h)),
                 scratch_shapes=[pltpu.VMEM((tt + CONV_HALO, 2 * wd), F32)], dims=("parallel", "arbitrary"))(u, cw, cb)


def _conv_glu_bwd(u, da, cw, cb, name):
    S, F2 = u.shape
    wd = F2 // 4
    tt = _tile(S, 256)
    nt = S // tt
    n = tt + CONV_HALO

    def body(u_ref, uprev_ref, da_ref, cw_ref, cb_ref, du_ref, acc_ref, ext_ref, carry_ref):
        it = pl.program_id(1)

        @pl.when(it == 0)
        def _():
            carry_ref[...] = jnp.zeros_like(carry_ref)
            acc_ref[...] = jnp.zeros_like(acc_ref)

        @pl.when(it == nt - 1)
        def _():
            ext_ref[pl.ds(0, CONV_HALO), :] = jnp.zeros((CONV_HALO, 2 * wd), F32)

        @pl.when(it < nt - 1)
        def _():
            ext_ref[pl.ds(0, CONV_HALO), :] = uprev_ref[...]

        ext_ref[pl.ds(CONV_HALO, tt), :] = u_ref[...]
        for cc in range(wd // LANES):
            taps, act = [], []
            for half in range(2):
                cols = slice(half * wd + cc * LANES, half * wd + (cc + 1) * LANES)
                u0, u1, u2 = _conv_taps(ext_ref, cols, tt)
                taps.append((u0, u1, u2))
                act.append(cw_ref[2:3, cols] * u0 + cw_ref[1:2, cols] * u1 + cw_ref[0:1, cols] * u2 + cb_ref[:, cols])
            dav = da_ref[:, cc * LANES:(cc + 1) * LANES]
            dact = (dav * act[1] * _gelu_grad(act[0]), dav * _gelu(act[0]))
            for half in range(2):
                cols = slice(half * wd + cc * LANES, half * wd + (cc + 1) * LANES)
                u0, u1, u2 = taps[half]
                acc_ref[2:3, cols] += jnp.sum(dact[half] * u0, axis=0, keepdims=True)
                acc_ref[1:2, cols] += jnp.sum(dact[half] * u1, axis=0, keepdims=True)
                acc_ref[0:1, cols] += jnp.sum(dact[half] * u2, axis=0, keepdims=True)
                acc_ref[3:4, cols] += jnp.sum(dact[half], axis=0, keepdims=True)
                e = jnp.concatenate([dact[half], carry_ref[:, cols]], axis=0)
                du = (cw_ref[2:3, cols] * dact[half] + cw_ref[1:2, cols] * pltpu.roll(e, n - 1, 0)[:tt]
                      + cw_ref[0:1, cols] * pltpu.roll(e, n - 2, 0)[:tt])
                du_ref[:, cols] = du.astype(du_ref.dtype)
                carry_ref[:, cols] = dact[half][:CONV_HALO]

    rev = lambda t: nt - 1 - t
    per8 = tt // CONV_HALO
    wide = pl.BlockSpec((tt, 2 * wd), lambda h, t: (rev(t), h))
    prev = pl.BlockSpec((CONV_HALO, 2 * wd), lambda h, t: (jnp.maximum(rev(t) * per8 - 1, 0), h))
    acc = pl.BlockSpec((8, 2 * wd), lambda h, t: (0, h))
    return _call(body, name=name, out_shape=(_sds((S, F2), BF16), _sds((8, F2), F32)), grid=(2, nt),
                 in_specs=[wide, prev, pl.BlockSpec((tt, wd), lambda h, t: (rev(t), h)), acc,
                           pl.BlockSpec((1, 2 * wd), lambda h, t: (0, h))],
                 out_specs=(wide, acc),
                 scratch_shapes=[pltpu.VMEM((n, 2 * wd), F32), pltpu.VMEM((CONV_HALO, 2 * wd), F32)],
                 dims=("parallel", "arbitrary"))(u, u, da, cw, cb)


def _rope_chunk(x, cosv, sinv):
    lane = lax.broadcasted_iota(jnp.int32, x.shape, 1)
    partner = jnp.where(lane % HEAD_DIM < HEAD_DIM // 2, pltpu.roll(x, LANES - HEAD_DIM // 2, 1),
                        pltpu.roll(x, HEAD_DIM // 2, 1))
    return x * cosv + partner * sinv


def _rope(x, width, cos_t, sin_t, name):
    S = x.shape[0]
    tr = _tile(S, 256)

    def body(x_ref, c_ref, s_ref, o_ref):
        for cc in range(width // LANES):
            cols = slice(cc * LANES, (cc + 1) * LANES)
            o_ref[:, cols] = _rope_chunk(x_ref[:, cols], c_ref[...], s_ref[...])

    row = pl.BlockSpec((tr, width), lambda i: (i, 0))
    tab = pl.BlockSpec((tr, LANES), lambda i: (i, 0))
    return _call(body, name=name, out_shape=_sds((S, width), F32), grid=(S // tr,), in_specs=[row, tab, tab],
                 out_specs=row, dims=("parallel",))(x, cos_t, sin_t)


def _attn_mask(n, reps):
    row = lax.broadcasted_iota(jnp.int32, (reps * BLOCK, 2 * BLOCK), 0) & (BLOCK - 1)
    col = lax.broadcasted_iota(jnp.int32, (reps * BLOCK, 2 * BLOCK), 1)
    rel = BLOCK + row - col
    return (rel >= 0) & (rel < WINDOW) & (n * BLOCK + col - BLOCK >= 0)


def _per_head_column(values, reps):
    grp = lax.broadcasted_iota(jnp.int32, (reps * BLOCK, 1), 0) // BLOCK
    col = jnp.zeros((reps * BLOCK, 1), F32)
    for g, v in enumerate(values):
        col = jnp.where(grp == g, v, col)
    return col


def _stack_heads(ref, heads, qpk, lane, scale):
    parts = []
    for h in heads:
        qc, qpar, _, kpar = _head_place(h, qpk)
        x = ref[:, qc * LANES:(qc + 1) * LANES]
        if scale != 1.0:
            x = x * scale
        if qpar != kpar:
            x = pltpu.roll(x, HEAD_DIM, 1)
        keep = (lane >= kpar * HEAD_DIM) & (lane < (kpar + 1) * HEAD_DIM)
        parts.append(jnp.where(keep, x, 0.0).astype(BF16))
    return jnp.concatenate(parts, axis=0)


def _unstack_heads(vals, ref, heads, qpk, lane, dtype):
    pair = None
    for g, h in enumerate(heads):
        qc, qpar, _, kpar = _head_place(h, qpk)
        v = vals[g * BLOCK:(g + 1) * BLOCK]
        if qpar != kpar:
            v = pltpu.roll(v, HEAD_DIM, 1)
        if qpar == 0:
            pair = v
        else:
            ref[:, qc * LANES:(qc + 1) * LANES] = jnp.where(lane < HEAD_DIM, pair, v).astype(dtype)


def _head_place(h, qpk):
    hk = h // qpk
    return h // 2, h % 2, hk // 2, hk % 2


def _attn_specs(S, D):
    nb = S // BLOCK
    kvw = KV_DIM // 2
    qsp = pl.BlockSpec((BLOCK, D), lambda n: (n, 0))
    prev = lambda n: jnp.maximum(n - 1, 0)
    kp = pl.BlockSpec((BLOCK, kvw), lambda n: (prev(n), 0))
    ko = pl.BlockSpec((BLOCK, kvw), lambda n: (n, 0))
    vp = pl.BlockSpec((BLOCK, kvw), lambda n: (prev(n), 1))
    vo = pl.BlockSpec((BLOCK, kvw), lambda n: (n, 1))
    stat = pl.BlockSpec((BLOCK, LANES), lambda n: (n, 0))
    smem = pl.BlockSpec(memory_space=pltpu.SMEM)
    return nb, kvw, qsp, kp, ko, vp, vo, stat, smem


def _attn_fwd(q, k, kv, sinks, name):
    S, D = q.shape
    nh = D // HEAD_DIM
    qpk = nh // N_KV_HEADS
    nb, kvw, qsp, kp, ko, vp, vo, stat, smem = _attn_specs(S, D)

    def body(q_ref, kp_ref, ko_ref, vp_ref, vo_ref, s_ref, o_ref, l_ref):
        n = pl.program_id(0)
        valid = _attn_mask(n, 2 * qpk)
        lane = lax.broadcasted_iota(jnp.int32, (BLOCK, LANES), 1)
        lacc = jnp.zeros((BLOCK, LANES), F32)
        for kc in range(N_KV_HEADS // 2):
            heads = list(range(2 * kc * qpk, 2 * (kc + 1) * qpk))
            kcols = slice(kc * LANES, (kc + 1) * LANES)
            k2 = jnp.concatenate([kp_ref[:, kcols], ko_ref[:, kcols]], axis=0).astype(BF16)
            v2 = jnp.concatenate([vp_ref[:, kcols], vo_ref[:, kcols]], axis=0).astype(BF16)
            qm = _stack_heads(q_ref, heads, qpk, lane, ATTN_SCALE)
            s = lax.dot_general(qm, k2, (((1,), (1,)), ((), ())), preferred_element_type=F32)
            s = jnp.where(valid, s, NEG_INF)
            sink = _per_head_column([s_ref[h] for h in heads], len(heads))
            m = jnp.maximum(jnp.max(s, axis=1, keepdims=True), sink)
            p = jnp.exp(s - m)
            den = jnp.sum(p, axis=1, keepdims=True) + jnp.exp(sink - m)
            of = jnp.dot(p.astype(BF16), v2, preferred_element_type=F32) / den
            lse = m + jnp.log(den)
            for g, h in enumerate(heads):
                lacc = jnp.where(lane == h, lse[g * BLOCK:(g + 1) * BLOCK], lacc)
            _unstack_heads(of, o_ref, heads, qpk, lane, o_ref.dtype)
        l_ref[...] = lacc

    return _call(body, name=name, out_shape=(_sds((S, D), BF16), _sds((S, LANES), F32)), grid=(nb,),
                 in_specs=[qsp, kp, ko, vp, vo, smem], out_specs=(qsp, stat), dims=("parallel",))(q, k, k, kv, kv, sinks)


def _attn_bwd(q, k, kv, do, lse, sinks, name):
    S, D = q.shape
    nh = D // HEAD_DIM
    qpk = nh // N_KV_HEADS
    nb, kvw, qsp, kp, ko, vp, vo, stat, smem = _attn_specs(S, D)

    def body(q_ref, kp_ref, ko_ref, vp_ref, vo_ref, do_ref, l_ref, s_ref,
             dq_ref, dkp_ref, dko_ref, dvp_ref, dvo_ref, ds_ref):
        n = pl.program_id(0)

        @pl.when(n == 0)
        def _():
            ds_ref[...] = jnp.zeros_like(ds_ref)

        valid = _attn_mask(n, 2 * qpk)
        lane = lax.broadcasted_iota(jnp.int32, (BLOCK, LANES), 1)
        lane8 = lax.broadcasted_iota(jnp.int32, (8, LANES), 1)
        lv = l_ref[...]
        dsink = jnp.zeros((8, LANES), F32)
        for kc in range(N_KV_HEADS // 2):
            heads = list(range(2 * kc * qpk, 2 * (kc + 1) * qpk))
            kcols = slice(kc * LANES, (kc + 1) * LANES)
            k2 = jnp.concatenate([kp_ref[:, kcols], ko_ref[:, kcols]], axis=0).astype(BF16)
            v2 = jnp.concatenate([vp_ref[:, kcols], vo_ref[:, kcols]], axis=0).astype(BF16)
            qm = _stack_heads(q_ref, heads, qpk, lane, ATTN_SCALE)
            gm = _stack_heads(do_ref, heads, qpk, lane, 1.0)
            s = lax.dot_general(qm, k2, (((1,), (1,)), ((), ())), preferred_element_type=F32)
            lh = jnp.concatenate([jnp.sum(jnp.where(lane == h, lv, 0.0), axis=1, keepdims=True) for h in heads], axis=0)
            p = jnp.where(valid, jnp.exp(s - lh), 0.0)
            dp = lax.dot_general(gm, v2, (((1,), (1,)), ((), ())), preferred_element_type=F32)
            delta = jnp.sum(p * dp, axis=1, keepdims=True)
            dsb = (p * (dp - delta)).astype(BF16)
            lost = jnp.exp(_per_head_column([s_ref[h] for h in heads], len(heads)) - lh) * delta
            for g, h in enumerate(heads):
                dsink = dsink - jnp.where(lane8 == h, jnp.sum(lost[g * BLOCK:(g + 1) * BLOCK]), 0.0)
            dqf = jnp.dot(dsb, k2, preferred_element_type=F32) * ATTN_SCALE
            _unstack_heads(dqf, dq_ref, heads, qpk, lane, F32)
            dk2 = lax.dot_general(dsb, qm, (((0,), (0,)), ((), ())), preferred_element_type=F32)
            dv2 = lax.dot_general(p.astype(BF16), gm, (((0,), (0,)), ((), ())), preferred_element_type=F32)
            dkp_ref[:, kcols] = dk2[:BLOCK]
            dko_ref[:, kcols] = dk2[BLOCK:]
            dvp_ref[:, kcols] = dv2[:BLOCK]
            dvo_ref[:, kcols] = dv2[BLOCK:]
        ds_ref[...] += dsink

    kvo = pl.BlockSpec((BLOCK, kvw), lambda n: (n, 0))
    acc = pl.BlockSpec((8, LANES), lambda n: (0, 0))
    part = _sds((S, kvw), F32)
    return _call(body, name=name, out_shape=(_sds((S, D), F32), part, part, part, part, _sds((8, LANES), F32)),
                 grid=(nb,), in_specs=[qsp, kp, ko, vp, vo, qsp, stat, smem],
                 out_specs=(qsp, kvo, kvo, kvo, kvo, acc), dims=("arbitrary",))(q, k, k, kv, kv, do, lse, sinks)


def _kv_grad(parts, cos_t, sin_neg_t, name):
    S, kvw = parts[0][0].shape
    nb = S // BLOCK
    flat = [a for p in parts for a in p]
    nl = len(parts)

    def body(*refs):
        c_ref, s_ref, o_ref = refs[4 * nl], refs[4 * nl + 1], refs[4 * nl + 2]
        n = pl.program_id(0)
        last = n == nb - 1
        dk = jnp.zeros((BLOCK, kvw), F32)
        dv = jnp.zeros((BLOCK, kvw), F32)
        for li in range(nl):
            kn, kown, vn, vown = refs[4 * li:4 * li + 4]
            dk = dk + kown[...] + jnp.where(last, 0.0, kn[...])
            dv = dv + vown[...] + jnp.where(last, 0.0, vn[...])
        for cc in range(kvw // LANES):
            cols = slice(cc * LANES, (cc + 1) * LANES)
            o_ref[:, cols] = _rope_chunk(dk[:, cols], c_ref[...], s_ref[...])
        o_ref[:, kvw:] = dv

    own = pl.BlockSpec((BLOCK, kvw), lambda n: (n, 0))
    nxt = pl.BlockSpec((BLOCK, kvw), lambda n: (jnp.minimum(n + 1, nb - 1), 0))
    tab = pl.BlockSpec((BLOCK, LANES), lambda n: (n, 0))
    return _call(body, name=name, out_shape=_sds((S, 2 * kvw), F32), grid=(nb,),
                 in_specs=[nxt, own, nxt, own] * nl + [tab, tab],
                 out_specs=pl.BlockSpec((BLOCK, 2 * kvw), lambda n: (n, 0)), dims=("parallel",))(*flat, cos_t, sin_neg_t)


def _sum_blocks(name, qc, grid, out_shape, out_block, out_imap, ins, out_dtype=F32, into=None):
    nin = len(ins)

    def body(qc_ref, *refs):
        acc = refs[0][...].astype(F32)
        for r in refs[1:nin]:
            acc = acc + r[...].astype(F32)
        refs[-1][...] = acc.astype(refs[-1].dtype)

    in_specs = [pl.BlockSpec(b, m) for _, b, m in ins]
    operands = [a for a, _, _ in ins]
    aliases = None
    if into is not None:
        in_specs.append(ANY)
        operands.append(into)
        aliases = {1 + nin: 0}
    gs = pltpu.PrefetchScalarGridSpec(num_scalar_prefetch=1, grid=grid, in_specs=in_specs,
                                      out_specs=pl.BlockSpec(out_block, out_imap))
    return _call(body, name=name, out_shape=_sds(out_shape, out_dtype), grid_spec=gs,
                 dims=("parallel",) * len(grid), aliases=aliases)(qc, *operands)


def _adamw(w, g, m, v, name, part=None, into=None):
    shape = w.shape
    C = shape[-1]
    R = w.size // C
    k, cnt, nparts = part if part is not None else (0, 1, 1)
    tr = _tile(R // nparts, max(8, (1 << 18) // C))
    first = k * (R // nparts // tr)
    rows = cnt * (R // nparts)

    def body(w_ref, g_ref, m_ref, v_ref, *outs):
        go_ref, d_ref, nm_ref, nv_ref = outs[-4:]
        gv = g_ref[...]
        nm = ADAM_B1 * m_ref[...] + (1.0 - ADAM_B1) * gv
        nv = ADAM_B2 * v_ref[...] + (1.0 - ADAM_B2) * (gv * gv)
        m_hat = nm / (1.0 - ADAM_B1 ** ADAM_STEP)
        v_hat = nv / (1.0 - ADAM_B2 ** ADAM_STEP)
        go_ref[...] = gv
        d_ref[...] = -ADAM_LR * (m_hat / (jnp.sqrt(v_hat) + ADAM_EPS) + ADAM_WD * w_ref[...])
        nm_ref[...] = nm
        nv_ref[...] = nv

    blk = pl.BlockSpec((tr, C), lambda i: (first + i, 0))
    flat = _sds((R, C), F32)
    operands = [a.reshape(-1, C) for a in (w, g, m, v)]
    in_specs, aliases = [blk] * 4, None
    if g.size != w.size:
        assert g.size == rows * C, (name, g.shape, shape, part)
        in_specs[1] = pl.BlockSpec((tr, C), lambda i: (i, 0))
    if into is not None:
        operands += [a.reshape(R, C) for a in into]
        in_specs = in_specs + [ANY] * 4
        aliases = {4 + i: i for i in range(4)}
    outs = _call(body, name=name, out_shape=(flat,) * 4, grid=(rows // tr,), in_specs=in_specs,
                 out_specs=(blk,) * 4, dims=("parallel",), aliases=aliases)(*operands)
    return tuple(o.reshape(shape) for o in outs)


def _place():
    x, y, c = lax.axis_index("x"), lax.axis_index("y"), lax.axis_index("c")
    chips = [(1 - x, y), (x, 1 - y), (1 - x, 1 - y)]
    return x, y, c, chips


def _at(ref, nd, dims):
    idx = [slice(None)] * nd
    for d, v in dims.items():
        idx[d] = pl.ds(v[0], v[1]) if isinstance(v, tuple) else v
    return ref.at[tuple(idx)]


def _remote(src, dst, send_sem, recv_sem, dev):
    return pltpu.make_async_remote_copy(src_ref=src, dst_ref=dst, send_sem=send_sem, recv_sem=recv_sem,
                                        device_id=dev, device_id_type=MESH)


def _split_call(body, name, out_shape, in_specs, out_specs, aliases):
    return pl.pallas_call(body, name=name, out_shape=out_shape, in_specs=in_specs, out_specs=out_specs,
                          input_output_aliases=aliases,
                          compiler_params=pltpu.CompilerParams(has_side_effects=EFFECT))


def _hbm(a):
    return pltpu.with_memory_space_constraint(a, pltpu.HBM)


def _copies_start(srcs, lands, after, name, plan, ncopies):
    n, m = len(srcs), len(lands)

    def body(*refs):
        src, land = refs[:n], refs[n:n + m]
        send_sems, recv_sems, token = refs[n + m + 1], refs[n + m + 2], refs[-1]
        x, y, c, chips = _place()
        for k, (s, d, dev) in enumerate(plan(x, y, c, chips, src, land)):
            _remote(s, d, send_sems.at[k], recv_sems.at[k], dev).start()
        token[...] = jnp.zeros_like(token)

    thru = tuple(pltpu.HBM(a.shape, a.dtype) for a in list(srcs) + list(lands))
    outs = _split_call(
        body, name,
        out_shape=(pltpu.SemaphoreType.DMA((ncopies,)), pltpu.SemaphoreType.DMA((ncopies,))) + thru + (_sds((8, LANES), F32),),
        in_specs=(HBM,) * (n + m) + (ANY,), out_specs=(SEM, SEM) + (HBM,) * (n + m) + (VMEM,),
        aliases={i: 2 + i for i in range(n + m)})(*[_hbm(a) for a in srcs], *[_hbm(a) for a in lands], after)
    return dict(send=outs[0], recv=outs[1], srcs=outs[2:2 + n], lands=outs[2 + n:2 + n + m], token=outs[-1])


def _copies_wait(handle, after, name, plan):
    srcs, lands = handle['srcs'], handle['lands']
    n, m = len(srcs), len(lands)

    def body(*refs):
        src, land = refs[:n], refs[n:n + m]
        send_sems, recv_sems = refs[n + m], refs[n + m + 1]
        x, y, c, chips = _place()
        for k, (s, d, dev) in enumerate(plan(x, y, c, chips, src, land)):
            cp = _remote(s, d, send_sems.at[k], recv_sems.at[k], dev)
            cp.wait_send()
            cp.wait_recv()

    thru = tuple(pltpu.HBM(a.shape, a.dtype) for a in list(srcs) + list(lands))
    outs = _split_call(body, name, out_shape=thru, in_specs=(HBM,) * (n + m) + (SEM, SEM, ANY),
                       out_specs=(HBM,) * (n + m), aliases={i: i for i in range(n + m)})(
        *srcs, *lands, handle['send'], handle['recv'], after)
    return outs[:n], outs[n:]


def _gather_plan(x, y, c, chips, src, land, arriving):
    q = 2 * x + y
    peers = [(ch[0], ch[1], c) for ch in chips] + [(x, y, 1 - c)]
    slots = [2 * ch[0] + ch[1] for ch in chips] + [q]
    return [(s, d.at[slots[j] if arriving else q], peers[j]) for s, d in zip(src, land) for j in range(4)]


def _gather_half_plan(x, y, c, chips, src, land, arriving):
    q = 2 * x + y
    out = []
    for s, d in zip(src, land):
        hs = s.shape[0] // 2
        rows = pl.ds(c * hs, hs)
        for ch in chips:
            out.append((s.at[rows], d.at[2 * ch[0] + ch[1] if arriving else q, rows], (ch[0], ch[1], c)))
        out.append((s, d.at[q], (x, y, 1 - c)))
    return out


def _exchange_plan(x, y, c, chips, src, land, arriving):
    out = []
    for d in land:
        hs = d.shape[1] // 2
        for ch in chips:
            slot = 2 * ch[0] + ch[1]
            out.append((d.at[slot, pl.ds(c * hs, hs)], d.at[slot, pl.ds(((1 - c) if arriving else c) * hs, hs)], (x, y, 1 - c)))
    return out


def _scatter_plan(shard_axes, shapes):
    def plan(x, y, c, chips, src, land):
        out = []
        for s, d, sd, shp in zip(src, land, shard_axes, shapes):
            ss = shp[sd] // N_SHARDS
            for j, ch in enumerate(chips):
                out.append((_at(s, len(shp), {sd: ((2 * ch[0] + ch[1]) * ss, ss)}), d.at[j], (ch[0], ch[1], c)))
        return out
    return plan


def _half_dims(shape, hd, c):
    hs = shape[hd] // 2
    return {hd: (c * hs, hs)}


def _swap_halves(grads, specs, name):
    n = len(grads)
    outs_shape = []
    for a, (sd, hd) in zip(grads, specs):
        shp = list(a.shape)
        shp[hd] //= 2
        outs_shape.append(_sds(shp, F32))

    def body(*refs):
        ins, outs = refs[:n], refs[n:2 * n]
        send_sems, recv_sems = refs[2 * n:]
        x, y, c, _ = _place()
        cps = []
        for ai, (sd, hd) in enumerate(specs):
            shp = grads[ai].shape
            cp = _remote(_at(ins[ai], len(shp), _half_dims(shp, hd, 1 - c)), outs[ai],
                         send_sems.at[ai], recv_sems.at[ai], (x, y, 1 - c))
            cp.start()
            cps.append(cp)
        for cp in cps:
            cp.wait()

    return _call(body, name=name, out_shape=tuple(outs_shape), in_specs=[ANY] * n, out_specs=tuple([ANY] * n),
                 scratch_shapes=[pltpu.SemaphoreType.DMA((n,)), pltpu.SemaphoreType.DMA((n,))])(*grads)


def _share_halves(arrs, half_axes, layers, after, name):
    n = len(arrs)

    def body(*refs):
        ins, outs = refs[:n], refs[n + 1:2 * n + 1]
        send_sems, recv_sems = refs[2 * n + 1:]
        x, y, c, _ = _place()

        def half(ref, ai, which):
            shp = arrs[ai].shape
            hs = shp[half_axes[ai]] // 2
            dims = {half_axes[ai]: (which * hs, hs)}
            if layers[ai] is not None:
                dims[0] = layers[ai]
            return _at(ref, len(shp), dims)

        sends = []
        for ai in range(n):
            cp = _remote(half(ins[ai], ai, c), half(outs[ai], ai, c), send_sems.at[ai], recv_sems.at[ai], (x, y, 1 - c))
            cp.start()
            sends.append(cp)
        for ai in range(n):
            land = half(outs[ai], ai, 1 - c)
            _remote(land, land, send_sems.at[ai], recv_sems.at[ai], (x, y, c)).wait_recv()
        for cp in sends:
            cp.wait_send()

    return _call(body, name=name, out_shape=tuple(_sds(a.shape, a.dtype) for a in arrs), in_specs=[ANY] * (n + 1),
                 out_specs=tuple([ANY] * n), aliases={i: i for i in range(n)},
                 scratch_shapes=[pltpu.SemaphoreType.DMA((n,)), pltpu.SemaphoreType.DMA((n,))])(*arrs, after)


def _gather_small(v, name):
    R, C = v.shape

    def body(x_ref, out_ref, send_sems, recv_sems, local_sem):
        x, y, c, chips = _place()
        me, sibling = (x, y, c), (x, y, 1 - c)

        def rows(px, py, pc):
            return out_ref.at[pl.ds((4 * px + 2 * py + pc) * R, R), :]

        def copy(k, block, to, src=None):
            return _remote(rows(*block) if src is None else src, rows(*block), send_sems.at[k], recv_sems.at[k], to)

        mine = pltpu.make_async_copy(x_ref, rows(*me), local_sem)
        mine.start()
        first = [copy(0, me, sibling, src=x_ref)]
        first += [copy(1 + j, me, (ch[0], ch[1], c), src=x_ref) for j, ch in enumerate(chips)]
        for cp in first:
            cp.start()
        passed = [copy(4 + j, (ch[0], ch[1], c), sibling) for j, ch in enumerate(chips)]
        for j, ch in enumerate(chips):
            copy(1 + j, (ch[0], ch[1], c), me).wait_recv()
            passed[j].start()
        copy(0, sibling, me).wait_recv()
        for j, ch in enumerate(chips):
            copy(4 + j, (ch[0], ch[1], 1 - c), me).wait_recv()
        for cp in first + passed:
            cp.wait_send()
        mine.wait()

    vm = pl.BlockSpec(memory_space=pltpu.VMEM)
    return _call(body, name=name, out_shape=_sds((8 * R, C), v.dtype), in_specs=[vm], out_specs=vm,
                 scratch_shapes=[pltpu.SemaphoreType.DMA((7,)), pltpu.SemaphoreType.DMA((7,)),
                                 pltpu.SemaphoreType.DMA])(v)


def _sum8(g, name):
    _, R, C = g.shape

    def body(g_ref, o_ref):
        acc = g_ref[0]
        for d in range(1, 8):
            acc = acc + g_ref[d]
        o_ref[...] = acc

    return _call(body, name=name, out_shape=_sds((R, C), F32), in_specs=[pl.BlockSpec(memory_space=pltpu.VMEM)],
                 out_specs=pl.BlockSpec(memory_space=pltpu.VMEM))(g)


def _rope_tables(positions):
    inv_freq = 1.0 / (ROPE_THETA ** (jnp.arange(0, HEAD_DIM, 2, dtype=F32) / HEAD_DIM))
    ang = positions.astype(F32)[:, None] * inv_freq
    cosv, sinv = jnp.cos(ang), jnp.sin(ang)
    return jnp.tile(cosv, (1, 4)), jnp.tile(jnp.concatenate([-sinv, sinv], axis=1), (1, 2))


def _blocked(a):
    parts = jnp.split(a, 4, axis=-1)
    return jnp.concatenate([parts[0], parts[2], parts[1], parts[3]], axis=-1)


def _arrived(Wl, name, after):
    if callable(Wl[name]):
        Wl[name] = Wl[name](after)
    return Wl[name]


def _local_step(x, target, positions, P, weights_of, ffn_grads_done, grads_done):
    S, D = x.shape
    depth = P['mix_pre_g'].shape[0]
    n_a = depth // 2
    cos_t, sin_t = _rope_tables(positions)
    row = lambda a, l: a[l][None]
    cb = [_blocked(P['ffn_conv_b'][l])[None] for l in range(depth)]
    sv, W = {}, {}
    kv = k_rot = None
    for l in range(depth):
        t = f"l{l}"
        W[l], zero = weights_of(l, x)
        sv[l, 'x_in'] = x
        g_pre = row(P['mix_pre_g'], l) + zero
        if l < n_a:
            d = _pool_fwd(x, g_pre, "pool_fwd_" + t)
            y, x = _pool_mm_fwd(d, W[l]['pool_w'], W[l]['pool_scale'], x, row(P['mix_post_g'], l), "pool_mm_fwd_" + t)
            sv[l, 'd'], sv[l, 'y'] = d, y
        else:
            j = l - n_a
            h = _rms_fwd(x, g_pre, BF16, "q_norm_" + t)
            if l == n_a:
                hkv = _rms_fwd(x, P['kv_norm_g'][None], BF16, "kv_norm")
                kv = _matmul(hkv, W[l]['w_kv'], 'nn', F32, "kv_proj", 512, 512, 1024)
                k_rot = _rope(kv, KV_DIM // 2, cos_t, sin_t, "k_rope")
                sv['hkv'] = hkv
            qraw = _matmul(h, W[l]['w_q'], 'nn', F32, "q_proj_" + t, 512, 1024, 1024)
            q = _rope(qraw, D, cos_t, sin_t, "q_rope_" + t)
            o, lse = _attn_fwd(q, k_rot, kv, P['sinks'][j], "attn_fwd_" + t)
            m = _matmul(o, W[l]['w_o'], 'nn', F32, "o_proj_" + t, 512, 1024, 1024)
            x = _res_rms_fwd(x, m, row(P['mix_post_g'], l), "mix_post_" + t)
            sv[l, 'h'], sv[l, 'q'], sv[l, 'o'], sv[l, 'lse'], sv[l, 'm'] = h, q, o, lse, m
        sv[l, 'x1'] = x
        if 'pre_ffn' in W[l]:
            W[l].pop('pre_ffn')(x)
        h2 = _rms_fwd(x, row(P['ffn_pre_g'], l), BF16, "ffn_norm_" + t)
        w_in = _arrived(W[l], 'w_in', h2)
        u = _matmul(h2, w_in, 'nn', F32, "ffn_up_" + t, 512, w_in.shape[2], 1024, b_blocks=True)
        a = _conv_glu_fwd(u, W[l]['cw'], cb[l] + W[l].pop('tie', 0.0), "ffn_glu_" + t)
        f = _matmul(a, _arrived(W[l], 'w_out', a), 'nn', F32, "ffn_down_" + t, 512, 1024, 2816)
        x = _res_rms_fwd(x, f, row(P['ffn_post_g'], l), "ffn_post_" + t)
        sv[l, 'h2'], sv[l, 'u'], sv[l, 'a'], sv[l, 'f'] = h2, u, a, f

    dx, sq = _loss_grad(x, target, "loss")
    kv_parts = []
    zero = 0.0
    for l in reversed(range(depth)):
        t = f"l{l}"
        G = {}
        wd = W[l]['w_in'].shape[2]
        df, G['ffn_post_g'] = _rms_bwd(sv[l, 'f'], row(P['ffn_post_g'], l) + zero, dx, None, "ffn_post_bwd_" + t)
        da = _matmul(df, W[l]['w_out'], 'nt', F32, "ffn_down_dx_" + t, 512, wd, 1024)
        G['ffn_w_out'] = _matmul(sv[l, 'a'], df, 'tn', F32, "ffn_down_dw_" + t, wd, 1024, 1024)
        du, acc = _conv_glu_bwd(sv[l, 'u'], da, W[l]['cw'], cb[l], "ffn_glu_bwd_" + t)
        G['ffn_conv_w'] = _blocked(acc[0:3])
        G['ffn_conv_b'] = _blocked(acc[3:4])
        dh2 = _matmul(du, W[l]['w_in'], 'nt', F32, "ffn_up_dx_" + t, 512, 1024, wd, b_blocks=True)
        G['ffn_w_in'] = _matmul(sv[l, 'h2'], du, 'tn', F32, "ffn_up_dw_" + t, 1024, wd, 2048, out_perm=True)
        zero = ffn_grads_done(l, G, dh2)
        dx, G['ffn_pre_g'] = _rms_bwd(sv[l, 'x1'], row(P['ffn_pre_g'], l) + zero, dh2, dx, "ffn_norm_bwd_" + t)
        if l < n_a:
            dd, G['pool_w'], G['pool_scale'], G['mix_post_g'] = _pool_mm_bwd(
                dx, sv[l, 'y'], sv[l, 'd'], W[l]['pool_w'], W[l]['pool_scale'], row(P['mix_post_g'], l), "pool_mm_bwd_" + t)
            dx, G['mix_pre_g'] = _pool_bwd(dd, sv[l, 'x_in'], row(P['mix_pre_g'], l), dx, "pool_bwd_" + t)
        else:
            j = l - n_a
            dm, G['mix_post_g'] = _rms_bwd(sv[l, 'm'], row(P['mix_post_g'], l), dx, None, "mix_post_bwd_" + t)
            do = _matmul(dm, W[l]['w_o'], 'nt', F32, "o_proj_dx_" + t, 512, 1024, 1024)
            G['w_o'] = _matmul(sv[l, 'o'], dm, 'tn', F32, "o_proj_dw_" + t, 1024, 1024, 1024)
            dq, dkn, dko, dvn, dvo, dsk = _attn_bwd(sv[l, 'q'], k_rot, kv, do, sv[l, 'lse'], P['sinks'][j], "attn_bwd_" + t)
            G['sinks'] = dsk[0:1]
            kv_parts.append((dkn, dko, dvn, dvo))
            dqraw = _rope(dq, D, cos_t, -sin_t, "q_rope_bwd_" + t)
            dh = _matmul(dqraw, W[l]['w_q'], 'nt', F32, "q_proj_dx_" + t, 512, 1024, 1024)
            G['w_q'] = _matmul(sv[l, 'h'], dqraw, 'tn', F32, "q_proj_dw_" + t, 1024, 1024, 1024)
            dx, G['mix_pre_g'] = _rms_bwd(sv[l, 'x_in'], row(P['mix_pre_g'], l), dh, dx, "q_norm_bwd_" + t)
            if l == n_a:
                dkv = _kv_grad(kv_parts, cos_t, -sin_t, "kv_grad")
                dhkv = _matmul(dkv, W[l]['w_kv'], 'nt', F32, "kv_proj_dx", 512, 1024, 512)
                G['w_kv'] = _matmul(sv['hkv'], dkv, 'tn', F32, "kv_proj_dw", 1024, 512, 1024)
                dx, G['kv_norm_g'] = _rms_bwd(sv[l, 'x_in'], P['kv_norm_g'][None], dhkv, dx, "kv_norm_bwd")
        zero = grads_done(l, G, dx)
    return sq, dx


SMALL = ['mix_pre_g', 'mix_post_g', 'kv_norm_g', 'sinks', 'ffn_pre_g', 'ffn_post_g', 'ffn_conv_b', 'ffn_conv_w', 'pool_scale']
BIG = {'ffn_w_in': (1, 0, 1), 'ffn_w_out': (0, 1, 2), 'w_q': (0, 1, 2), 'w_o': (0, 1, 2), 'w_kv': (0, 1, 1),
       'pool_w': (1, 0, 1)}


def _swap_plan(half_axes, shapes):
    def plan(x, y, c, chips, src, land):
        return [(_at(s, len(shp), _half_dims(shp, hd, 1 - c)), d, (x, y, 1 - c))
                for s, d, hd, shp in zip(src, land, half_axes, shapes)]
    return plan


def _swap_start(pieces, after, name):
    arrs = [p[0] for p in pieces]
    plan = _swap_plan([p[2] for p in pieces], [a.shape for a in arrs])
    lands = []
    for a, p in zip(arrs, pieces):
        shp = list(a.shape)
        shp[p[2]] //= 2
        lands.append(lax.empty(tuple(shp), F32))
    return pieces, _copies_start(arrs, lands, after, name, plan, len(arrs)), plan


def _swap_wait(started, after, name):
    pieces, handle, plan = started
    arrs, theirs = _copies_wait(handle, after, name, plan)
    return [(a,) + p[1:] for a, p in zip(arrs, pieces)], list(theirs)


def _reduce_start(pieces, theirs, qc, after, tag):
    arrs = [p[0] for p in pieces]
    specs = [(p[1], p[2]) for p in pieces]
    sums = []
    for pi, (a, (sd, hd), r) in enumerate(zip(arrs, specs, theirs)):
        shp = r.shape
        nd = len(shp)
        if nd == 3:
            blk, grid = tuple(shp), (1,)
            mine = lambda i, s: (s[1], 0, 0)
            zero = lambda i, s: (0, 0, 0)
        elif hd == 0:
            tr = _tile(shp[0], max(16, (1 << 18) // shp[1]), 16)
            blk, grid = (tr, shp[1]), (shp[0] // tr,)
            nblk = shp[0] // tr
            mine = lambda i, s, nblk=nblk: (s[1] * nblk + i, 0)
            zero = lambda i, s: (i, 0)
        else:
            tr = _tile(shp[0], max(16, (1 << 18) // shp[1]), 16)
            blk, grid = (tr, shp[1]), (shp[0] // tr,)
            mine = lambda i, s: (i, s[1])
            zero = lambda i, s: (i, 0)
        sums.append(_sum_blocks(f"grad_chip_sum_{tag}_{pi}", qc, grid, shp, blk, zero, [(a, blk, mine), (r, blk, zero)],
                                out_dtype=BF16))

    lands = []
    for s_arr, (sd, hd) in zip(sums, specs):
        shp = list(s_arr.shape)
        shp[sd] //= N_SHARDS
        lands.append(lax.empty((3,) + tuple(shp), BF16))
    plan = _scatter_plan([sd for sd, _ in specs], [s.shape for s in sums])
    handle = _copies_start(sums, lands, after, "grad_scatter_start_" + tag, plan, 3 * len(sums))
    return dict(handle=handle, plan=plan, pieces=pieces, tag=tag)


def _reduce_finish(state, after, qc, outs, out_shapes):
    handle, pieces, tag = state['handle'], state['pieces'], state['tag']
    sums, recvd = _copies_wait(handle, after, "grad_scatter_wait_" + tag, state['plan'])
    for pi, ((a, sd, hd, oname, fixed, ohd), s_arr, r) in enumerate(zip(pieces, sums, recvd)):
        shp = r.shape[1:]
        nd = len(shp)
        lead = (fixed[0],) if fixed else ()
        none = (None,) if fixed else ()
        n_stack = out_shapes[oname][0]
        if nd == 3:
            blk, grid = tuple(shp), (1,)
            mine = lambda i, s: (0, s[0], 0)
            rk = [lambda i, s, k=k: (k, 0, 0, 0) for k in range(3)]
            oshape = (n_stack, 2 * shp[0]) + tuple(shp[1:])
            oblk = none + blk
            omap = lambda i, s, lead=lead: lead + (s[1], 0, 0)
        elif sd == 1:
            tr = _tile(shp[0], max(16, (1 << 18) // shp[1]), 16)
            blk, grid = (tr, shp[1]), (shp[0] // tr,)
            nblk = shp[0] // tr
            mine = lambda i, s: (i, s[0])
            rk = [lambda i, s, k=k: (k, i, 0) for k in range(3)]
            oshape = (n_stack, 2 * shp[0], shp[1])
            oblk = none + blk
            omap = lambda i, s, lead=lead, nblk=nblk: lead + (s[1] * nblk + i, 0)
        else:
            tr = _tile(shp[0], max(16, (1 << 18) // shp[1]), 16)
            blk, grid = (tr, shp[1]), (shp[0] // tr,)
            nblk = shp[0] // tr
            mine = lambda i, s, nblk=nblk: (s[0] * nblk + i, 0)
            rk = [lambda i, s, k=k: (k, i, 0) for k in range(3)]
            oshape = ((n_stack,) if fixed else ()) + (shp[0], 2 * shp[1])
            oblk = none + blk
            omap = lambda i, s, lead=lead: lead + (i, s[1])
        assert tuple(oshape) == tuple(out_shapes[oname]), (oname, oshape, out_shapes[oname])
        ins = [(s_arr, blk, mine)] + [(r, (None,) + blk, rk[k]) for k in range(3)]
        outs[oname] = _sum_blocks(f"grad_total_{tag}_{pi}", qc, grid, oshape, oblk, omap, ins, into=outs.get(oname))


def _pack_small(parts):
    rows, offs, r = [], [], 0
    for a in parts:
        flat = a.reshape(-1)
        nr = -(-flat.size // (8 * LANES)) * 8
        rows.append(jnp.pad(flat, (0, nr * LANES - flat.size)).reshape(nr, LANES))
        offs.append((r, nr, a.shape))
        r += nr
    return jnp.concatenate(rows, axis=0), offs


def _unpack_small(packed, offs):
    return [packed[r:r + nr].reshape(-1)[:math.prod(shape)].reshape(shape) for r, nr, shape in offs]


def kernel(x, positions, mix_pre_g, mix_post_g, pool_w, pool_scale, kv_norm_g, w_kv, w_q, w_o, sinks, ffn_pre_g, ffn_post_g, ffn_w_in, ffn_conv_w, ffn_conv_b, ffn_w_out, loss_target, m_mix_pre_g, m_mix_post_g, m_pool_w, m_pool_scale, m_kv_norm_g, m_w_kv, m_w_q, m_w_o, m_sinks, m_ffn_pre_g, m_ffn_post_g, m_ffn_w_in, m_ffn_conv_w, m_ffn_conv_b, m_ffn_w_out, v_mix_pre_g, v_mix_post_g, v_pool_w, v_pool_scale, v_kv_norm_g, v_w_kv, v_w_q, v_w_o, v_sinks, v_ffn_pre_g, v_ffn_post_g, v_ffn_w_in, v_ffn_conv_w, v_ffn_conv_b, v_ffn_w_out):
    w = dict(mix_pre_g=mix_pre_g, mix_post_g=mix_post_g, pool_w=pool_w, pool_scale=pool_scale, kv_norm_g=kv_norm_g,
             w_kv=w_kv, w_q=w_q, w_o=w_o, sinks=sinks, ffn_pre_g=ffn_pre_g, ffn_post_g=ffn_post_g, ffn_w_in=ffn_w_in,
             ffn_conv_w=ffn_conv_w, ffn_conv_b=ffn_conv_b, ffn_w_out=ffn_w_out)
    mom = dict(mix_pre_g=m_mix_pre_g, mix_post_g=m_mix_post_g, pool_w=m_pool_w, pool_scale=m_pool_scale,
               kv_norm_g=m_kv_norm_g, w_kv=m_w_kv, w_q=m_w_q, w_o=m_w_o, sinks=m_sinks, ffn_pre_g=m_ffn_pre_g,
               ffn_post_g=m_ffn_post_g, ffn_w_in=m_ffn_w_in, ffn_conv_w=m_ffn_conv_w, ffn_conv_b=m_ffn_conv_b,
               ffn_w_out=m_ffn_w_out)
    var = dict(mix_pre_g=v_mix_pre_g, mix_post_g=v_mix_post_g, pool_w=v_pool_w, pool_scale=v_pool_scale,
               kv_norm_g=v_kv_norm_g, w_kv=v_w_kv, w_q=v_w_q, w_o=v_w_o, sinks=v_sinks, ffn_pre_g=v_ffn_pre_g,
               ffn_post_g=v_ffn_post_g, ffn_w_in=v_ffn_w_in, ffn_conv_w=v_ffn_conv_w, ffn_conv_b=v_ffn_conv_b,
               ffn_w_out=v_ffn_w_out)
    depth = mix_pre_g.shape[0]
    q_chip = 2 * lax.axis_index("x") + lax.axis_index("y")
    qc = jnp.stack([q_chip, lax.axis_index("c")]).astype(jnp.int32)

    n_a = depth // 2
    D = x.shape[-1]
    gc = pool_w.shape[3]

    def layer_shards(l):
        first = []
        if l < n_a:
            first.append(('pool_w', pool_w[l].astype(BF16)))
        else:
            first += [('w_q', w_q[l - n_a].astype(BF16)), ('w_o', w_o[l - n_a].astype(BF16))]
            if l == n_a:
                first.append(('w_kv', w_kv.astype(BF16)))
        ffn = [('w_in', ffn_w_in[l].astype(BF16)), ('w_out', ffn_w_out[l].astype(BF16))]
        if l == 0:
            return [first + [('conv_w', ffn_conv_w), ('pool_scale', pool_scale)], ffn[:1], ffn[1:]]
        return [first + ffn]

    def start_group(items, after, name, plan):
        srcs = [a for _, a in items]
        lands = [lax.empty((N_SHARDS,) + a.shape, a.dtype) for a in srcs]
        handle = _copies_start(srcs, lands, after, name, functools.partial(plan, arriving=False), 4 * len(srcs))
        return [n for n, _ in items], handle, plan

    def wait_group(started, after, name):
        names, handle, plan = started
        _, lands = _copies_wait(handle, after, name, functools.partial(plan, arriving=True))
        return dict(zip(names, lands))

    def start_exchange(got, after, name):
        names = list(got)
        handle = _copies_start([], [got[n] for n in names], after, name,
                               functools.partial(_exchange_plan, arriving=False), 3 * len(names))
        return names, handle, _exchange_plan

    groups0 = layer_shards(0)
    small0 = start_group(groups0[0], x, "weight_gather_start_l0_small", _gather_plan)
    in0 = start_group(groups0[1], small0[1]['token'], "weight_gather_start_l0_in", _gather_half_plan)
    out0 = start_group(groups0[2], in0[1]['token'], "weight_gather_start_l0_out", _gather_half_plan)
    pending, shared, steps = {}, {}, {}

    def start_next(l, after):
        pending[l + 1] = start_group(layer_shards(l + 1)[0], after, f"weight_gather_start_l{l + 1}", _gather_plan)
        return pending[l + 1][1]['token'][0, 0]

    def pre_ffn0(after):
        landed = wait_group(in0, after, "weight_gather_wait_l0_in")
        steps['in'] = start_exchange(landed, after, "weight_exchange_start_l0_in")

    def w_in0(Wl, after):
        both = wait_group(steps['in'], after, "weight_exchange_wait_l0_in")
        Wl['tie'] = start_next(0, both['w_in'])
        return both['w_in']

    def w_out0(after):
        landed = wait_group(out0, after, "weight_gather_wait_l0_out")
        both = wait_group(start_exchange(landed, after, "weight_exchange_start_l0_out"), after,
                          "weight_exchange_wait_l0_out")
        return both['w_out'].reshape(-1, D)

    def weights_of(l, x_now):
        zero = 0.0
        if l == 0:
            got = wait_group(small0, out0[1]['token'], "weight_gather_wait_l0_small")
            shared['conv_w'] = got['conv_w']
            shared['pool_scale'] = got['pool_scale'].transpose(1, 0, 2).reshape(n_a, D)
        else:
            got = wait_group(pending.pop(l), x_now, f"weight_gather_wait_l{l}")
            if l + 1 < depth:
                zero = start_next(l, got['w_in'])
        taps = jnp.concatenate([shared['conv_w'][p, l] for p in (0, 2, 1, 3)], axis=-1)
        Wl = dict(cw=jnp.pad(taps, ((0, 5), (0, 0))))
        if l == 0:
            Wl['pre_ffn'], Wl['w_in'], Wl['w_out'] = pre_ffn0, functools.partial(w_in0, Wl), w_out0
        else:
            Wl['w_in'], Wl['w_out'] = got['w_in'], got['w_out'].reshape(-1, D)
        if l < n_a:
            Wl['pool_w'] = got['pool_w'].transpose(1, 0, 2, 3).reshape(-1, gc, gc)
            Wl['pool_scale'] = shared['pool_scale'][l][None]
        else:
            Wl['w_q'], Wl['w_o'] = got['w_q'].reshape(D, D), got['w_o'].reshape(D, D)
            if l == n_a:
                Wl['w_kv'] = got['w_kv'].reshape(D, -1)
        return Wl, zero

    big_shapes = {n: w[n].shape for n in BIG}
    big, G, scattering = {}, {}, {}

    swapping = {}

    def piece(n, g, l):
        lead = {0: (l if n.startswith('ffn') or n == 'pool_w' else l - n_a)} if len(big_shapes[n]) > 2 else {}
        return (g, BIG[n][0], BIG[n][1], n, lead, BIG[n][2])

    def ffn_grads_done(l, Gl, after):
        swapping[l] = _swap_start([piece(n, Gl[n], l) for n in ('ffn_w_in', 'ffn_w_out')], after, f"grad_swap_start_l{l}")
        return swapping[l][1]['token'][0, 0]

    small = {}

    def gather_small_grads():
        local = {
            'mix_pre_g': jnp.concatenate([G['mix_pre_g', l] for l in range(depth)], axis=0),
            'mix_post_g': jnp.concatenate([G['mix_post_g', l] for l in range(depth)], axis=0),
            'kv_norm_g': G['kv_norm_g', n_a][0],
            'sinks': jnp.concatenate([G['sinks', l][:, :sinks.shape[1]] for l in range(n_a, depth)], axis=0),
            'ffn_pre_g': jnp.concatenate([G['ffn_pre_g', l] for l in range(depth)], axis=0),
            'ffn_post_g': jnp.concatenate([G['ffn_post_g', l] for l in range(depth)], axis=0),
            'ffn_conv_b': jnp.concatenate([G['ffn_conv_b', l] for l in range(depth)], axis=0),
            'ffn_conv_w': jnp.stack([G['ffn_conv_w', l] for l in range(depth)], axis=0),
            'pool_scale': jnp.concatenate([G['pool_scale', l] for l in range(n_a)], axis=0),
        }
        packed, small['offs'] = _pack_small([local[n] for n in SMALL])
        small['gathered'] = _gather_small(packed, "small_grad_gather").reshape(8, packed.shape[0], LANES)

    def grads_done(l, Gl, dx_now):
        for n, g in Gl.items():
            if n not in BIG:
                G[n, l] = g
        after = dx_now
        if l == 0:
            gather_small_grads()
            after = small['gathered']
        pieces, theirs = _swap_wait(swapping.pop(l), after, f"grad_swap_wait_l{l}")
        rest = [piece(n, Gl[n], l) for n in BIG if n in Gl and not n.startswith('ffn')]
        theirs += list(_swap_halves([p[0] for p in rest], [(p[1], p[2]) for p in rest], f"grad_swap_halves_l{l}"))
        scattering[l] = _reduce_start(pieces + rest, theirs, qc, after, f"l{l}")
        if l + 1 in scattering:
            _reduce_finish(scattering.pop(l + 1), dx_now, qc, big, big_shapes)
        return scattering[l]['handle']['token'][0, 0]

    P = {n: w[n] for n in ('mix_pre_g', 'mix_post_g', 'kv_norm_g', 'sinks', 'ffn_pre_g', 'ffn_post_g', 'ffn_conv_b')}
    sq, dx = _local_step(x[0], loss_target[0], positions[0], P, weights_of, ffn_grads_done, grads_done)
    loss = 0.5 / D * lax.psum(jnp.sum(sq), ("x", "y", "c"))

    late = {'ffn_w_in': (1, depth - 1), 'ffn_w_out': (1, depth - 1), 'pool_w': (1, n_a - 1), 'w_q': None, 'w_o': None,
            'w_kv': None}
    names = list(big)
    in_flight = scattering[0]['handle']['token']
    whole = _share_halves([big[n] for n in names], [BIG[n][2] for n in names], [late[n] for n in names], in_flight,
                          "grad_share_halves_late")
    upd = {}
    for n, g in zip(names, whole):
        part = None if late[n] is None else late[n] + (w[n].shape[0],)
        upd[n] = _adamw(w[n], g, mom[n], var[n], "adamw_late_" + n, part=part)

    summed = _sum8(small['gathered'] + in_flight[0, 0], "small_grad_sum")
    grads = dict(zip(SMALL, _unpack_small(summed, small['offs'])))
    wd = ffn_conv_w.shape[2]
    grads['ffn_conv_w'] = lax.dynamic_slice_in_dim(grads['ffn_conv_w'], q_chip * wd, wd, axis=2)
    ps = pool_scale.shape[1]
    grads['pool_scale'] = lax.dynamic_slice_in_dim(grads['pool_scale'], q_chip * ps, ps, axis=1)

    delta, new_m, new_v = {}, {}, {}
    for n in SMALL:
        upd[n] = _adamw(w[n], grads[n], mom[n], var[n], "adamw_" + n)

    first, shapes0 = {}, {n: (1,) + w[n].shape[1:] for n in ('ffn_w_in', 'ffn_w_out', 'pool_w')}
    done = jnp.stack([upd[n][1][(-1,) * upd[n][1].ndim] for n in upd])
    _reduce_finish(scattering.pop(0), done, qc, first, shapes0)
    names0 = list(first)
    whole0 = _share_halves([first[n] for n in names0], [BIG[n][2] for n in names0], [None] * len(names0), done,
                           "grad_share_halves_l0")
    for n, g in zip(names0, whole0):
        upd[n] = _adamw(w[n], g, mom[n], var[n], "adamw_l0_" + n, part=(0, 1, w[n].shape[0]), into=upd[n])
    for n in upd:
        grads[n], delta[n], new_m[n], new_v[n] = upd[n]

    return (loss, dx[None], *[grads[n] for n in WEIGHTS], *[delta[n] for n in WEIGHTS],
            *[new_m[n] for n in WEIGHTS], *[new_v[n] for n in WEIGHTS])
```

```python
import functools
import math

import jax
import jax.numpy as jnp
from jax import lax
from jax.experimental import pallas as pl
from jax.experimental.pallas import tpu as pltpu

F32 = jnp.float32
BF16 = jnp.bfloat16
MESH = pl.DeviceIdType.MESH
ANY = pl.BlockSpec(memory_space=pl.ANY)
HBM = pl.BlockSpec(memory_space=pltpu.HBM)
VMEM = pl.BlockSpec(memory_space=pltpu.VMEM)
SEM = pl.BlockSpec(memory_space=pltpu.SEMAPHORE)
EFFECT = pltpu.SideEffectType.DATAFLOW_SIDE_EFFECTING

HEAD_DIM = 64
N_KV_HEADS = 4
KV_DIM = 2 * N_KV_HEADS * HEAD_DIM
WINDOW = 128
BLOCK = 128
POOL_WINDOWS = (2, 4, 8, 16)
POOL_HALO = 16
CONV_HALO = 8
ROPE_THETA = 10000.0
ATTN_SCALE = 1.0 / math.sqrt(HEAD_DIM)
NEG_INF = -1e30
RMS_EPS = 1e-6
ADAM_LR, ADAM_B1, ADAM_B2, ADAM_EPS, ADAM_WD, ADAM_STEP = 0.001, 0.9, 0.999, 1e-08, 0.01, 10
N_SHARDS = 4
LANES = 128
VMEM_LIMIT_BYTES = 48 << 20
ROW_TILE = 512

WEIGHTS = ['mix_pre_g', 'mix_post_g', 'pool_w', 'pool_scale', 'kv_norm_g', 'w_kv', 'w_q', 'w_o', 'sinks',
           'ffn_pre_g', 'ffn_post_g', 'ffn_w_in', 'ffn_conv_w', 'ffn_conv_b', 'ffn_w_out']


def _call(body, *, name, out_shape, grid=None, in_specs=None, out_specs=None, scratch_shapes=(), dims=None,
          grid_spec=None, aliases=None):
    params = pltpu.CompilerParams(dimension_semantics=dims, vmem_limit_bytes=VMEM_LIMIT_BYTES)
    kw = {} if aliases is None else dict(input_output_aliases=aliases)
    if grid_spec is not None:
        return pl.pallas_call(body, name=name, out_shape=out_shape, grid_spec=grid_spec, compiler_params=params, **kw)
    if grid is not None:
        kw['grid'] = grid
    return pl.pallas_call(body, name=name, out_shape=out_shape, in_specs=in_specs, out_specs=out_specs,
                          scratch_shapes=list(scratch_shapes), compiler_params=params, **kw)


def _tile(n, pref, mult=8):
    if n <= pref:
        return n
    for t in range(pref, 0, -1):
        if n % t == 0 and t % mult == 0:
            return t
    raise ValueError((n, pref, mult))


def _sds(shape, dtype):
    return jax.ShapeDtypeStruct(tuple(shape), dtype)


def _perm4(j):
    return (j % 2) * 2 + j // 2


def _matmul(a, b, mode, out_dtype, name, tm, tn, tk, b_blocks=False, out_perm=False):
    a2 = a.shape
    b2 = (b.shape[1], 4 * b.shape[2]) if b_blocks else b.shape
    if mode == 'nn':
        (M, K), (K2, N) = a2, b2
    elif mode == 'nt':
        (M, K), (N, K2) = a2, b2
    else:
        (K, M), (K2, N) = a2, b2
    assert K == K2, (name, a.shape, b.shape)
    tm, tn, tk = _tile(M, tm), _tile(N, tn, LANES), _tile(K, tk, LANES if mode != 'tn' else 16)
    assert M % tm == 0 and N % tn == 0 and K % tk == 0
    nk = K // tk
    grid = (N // tn, M // tm, nk)

    if mode == 'nn':
        a_spec = pl.BlockSpec((tm, tk), lambda j, i, k: (i, k))
        if b_blocks:
            assert tn == b.shape[2]
            b_spec = pl.BlockSpec((None, tk, tn), lambda j, i, k: (_perm4(j), k, 0))
        else:
            b_spec = pl.BlockSpec((tk, tn), lambda j, i, k: (k, j))
        dn = (((1,), (0,)), ((), ()))
    elif mode == 'nt':
        a_spec = pl.BlockSpec((tm, tk), lambda j, i, k: (i, k))
        if b_blocks:
            assert tk == b.shape[2]
            b_spec = pl.BlockSpec((None, tn, tk), lambda j, i, k: (_perm4(k), j, 0))
        else:
            b_spec = pl.BlockSpec((tn, tk), lambda j, i, k: (j, k))
        dn = (((1,), (1,)), ((), ()))
    else:
        a_spec = pl.BlockSpec((tk, tm), lambda j, i, k: (k, i))
        b_spec = pl.BlockSpec((tk, tn), lambda j, i, k: (k, j))
        dn = (((0,), (0,)), ((), ()))
    po = _perm4 if out_perm else (lambda j: j)
    o_spec = pl.BlockSpec((tm, tn), lambda j, i, k: (i, po(j)))

    def body(a_ref, b_ref, o_ref, *acc):
        prod = lax.dot_general(a_ref[...].astype(BF16), b_ref[...].astype(BF16), dn, preferred_element_type=F32)
        if nk == 1:
            o_ref[...] = prod.astype(o_ref.dtype)
        else:
            k = pl.program_id(2)

            @pl.when(k == 0)
            def _():
                acc[0][...] = prod

            @pl.when(k > 0)
            def _():
                acc[0][...] += prod

            @pl.when(k == nk - 1)
            def _():
                o_ref[...] = acc[0][...].astype(o_ref.dtype)

    scratch = [] if nk == 1 else [pltpu.VMEM((tm, tn), F32)]
    return _call(body, name=name, out_shape=_sds((M, N), out_dtype), grid=grid, in_specs=[a_spec, b_spec],
                 out_specs=o_spec, scratch_shapes=scratch, dims=("parallel", "parallel", "arbitrary"))(a, b)


def _rstd(x):
    return lax.rsqrt(jnp.mean(x * x, axis=-1, keepdims=True) + RMS_EPS)


def _rms_fwd(x, g, out_dtype, name):
    S, D = x.shape
    tr = _tile(S, ROW_TILE)

    def body(x_ref, g_ref, o_ref):
        xv = x_ref[...]
        o_ref[...] = (xv * _rstd(xv) * g_ref[...]).astype(o_ref.dtype)

    row = pl.BlockSpec((tr, D), lambda i: (i, 0))
    vec = pl.BlockSpec((1, D), lambda i: (0, 0))
    return _call(body, name=name, out_shape=_sds((S, D), out_dtype), grid=(S // tr,), in_specs=[row, vec],
                 out_specs=row, dims=("parallel",))(x, g)


def _res_rms_fwd(x, f, g, name):
    S, D = x.shape
    tr = _tile(S, ROW_TILE)

    def body(x_ref, f_ref, g_ref, o_ref):
        fv = f_ref[...]
        o_ref[...] = x_ref[...] + fv * _rstd(fv) * g_ref[...]

    row = pl.BlockSpec((tr, D), lambda i: (i, 0))
    vec = pl.BlockSpec((1, D), lambda i: (0, 0))
    return _call(body, name=name, out_shape=_sds((S, D), F32), grid=(S // tr,), in_specs=[row, row, vec],
                 out_specs=row, dims=("parallel",))(x, f, g)


def _rms_bwd_math(xin, g, dy):
    r = _rstd(xin)
    xh = xin * r
    gy = dy * g
    dx = r * (gy - xh * jnp.mean(gy * xh, axis=-1, keepdims=True))
    return dx, dy * xh


def _rms_bwd(xin, g, dy, res, name):
    S, D = xin.shape
    tr = _tile(S, ROW_TILE)
    has_res = res is not None

    def body(*refs):
        if has_res:
            x_ref, g_ref, dy_ref, res_ref, dx_ref, dg_ref = refs
        else:
            x_ref, g_ref, dy_ref, dx_ref, dg_ref = refs
        dx, dgr = _rms_bwd_math(x_ref[...], g_ref[...], dy_ref[...])
        dx_ref[...] = dx + res_ref[...] if has_res else dx
        i = pl.program_id(0)

        @pl.when(i == 0)
        def _():
            dg_ref[...] = jnp.zeros_like(dg_ref)

        dg_ref[...] += jnp.sum(dgr, axis=0, keepdims=True)

    row = pl.BlockSpec((tr, D), lambda i: (i, 0))
    vec = pl.BlockSpec((1, D), lambda i: (0, 0))
    ins = [xin, g, dy] + ([res] if has_res else [])
    in_specs = [row, vec, row] + ([row] if has_res else [])
    return _call(body, name=name, out_shape=(_sds((S, D), F32), _sds((1, D), F32)), grid=(S // tr,),
                 in_specs=in_specs, out_specs=(row, vec), dims=("arbitrary",))(*ins)


def _loss_grad(y, target, name):
    S, D = y.shape
    tr = _tile(S, ROW_TILE)

    def body(y_ref, t_ref, dy_ref, acc_ref):
        e = y_ref[...] - t_ref[...]
        dy_ref[...] = e * (1.0 / D)
        i = pl.program_id(0)

        @pl.when(i == 0)
        def _():
            acc_ref[...] = jnp.zeros_like(acc_ref)

        acc_ref[...] += jnp.sum(e * e, axis=0, keepdims=True)

    row = pl.BlockSpec((tr, D), lambda i: (i, 0))
    vec = pl.BlockSpec((1, D), lambda i: (0, 0))
    return _call(body, name=name, out_shape=(_sds((S, D), F32), _sds((1, D), F32)), grid=(S // tr,),
                 in_specs=[row, row], out_specs=(row, vec), dims=("arbitrary",))(y, target)


def _pool_counts(t0, rows):
    return t0 + lax.broadcasted_iota(jnp.int32, (rows, 1), 0)


def _pool_fwd(x, g, name):
    S, D = x.shape
    gc = D // len(POOL_WINDOWS)
    tp = _tile(S, ROW_TILE)

    def body(x_ref, g_ref, d_ref, ext_ref):
        i = pl.program_id(0)

        @pl.when(i == 0)
        def _():
            ext_ref[pl.ds(0, POOL_HALO), :] = jnp.zeros((POOL_HALO, D), F32)

        xv = x_ref[...]
        ext_ref[pl.ds(POOL_HALO, tp), :] = xv * _rstd(xv) * g_ref[...]
        t = _pool_counts(i * tp, tp)
        for gi, w in enumerate(POOL_WINDOWS):
            cols = slice(gi * gc, (gi + 1) * gc)
            s = ext_ref[:, cols]
            h = s[POOL_HALO:]
            sh = 1
            while sh < w:
                s = s + pltpu.roll(s, sh, 0)
                sh *= 2
            cnt = jnp.minimum(t + 1, w).astype(F32)
            d_ref[:, cols] = (s[POOL_HALO:] / cnt - h).astype(d_ref.dtype)
        ext_ref[pl.ds(0, POOL_HALO), :] = ext_ref[pl.ds(tp, POOL_HALO), :]

    row = pl.BlockSpec((tp, D), lambda i: (i, 0))
    vec = pl.BlockSpec((1, D), lambda i: (0, 0))
    return _call(body, name=name, out_shape=_sds((S, D), BF16), grid=(S // tp,), in_specs=[row, vec],
                 out_specs=row, scratch_shapes=[pltpu.VMEM((tp + POOL_HALO, D), F32)], dims=("arbitrary",))(x, g)


def _pool_mm_fwd(d, wp, scale, x, gpost, name):
    S, D = x.shape
    ng = len(POOL_WINDOWS)
    gc = D // ng
    tp = _tile(S, ROW_TILE)

    def body(d_ref, w_ref, sc_ref, x_ref, g_ref, y_ref, o_ref):
        for gi in range(ng):
            cols = slice(gi * gc, (gi + 1) * gc)
            y_ref[:, cols] = jnp.dot(d_ref[:, cols], w_ref[gi], preferred_element_type=F32)
        m = y_ref[...] * sc_ref[...]
        o_ref[...] = x_ref[...] + m * _rstd(m) * g_ref[...]

    row = pl.BlockSpec((tp, D), lambda i: (i, 0))
    vec = pl.BlockSpec((1, D), lambda i: (0, 0))
    wsp = pl.BlockSpec((ng, gc, gc), lambda i: (0, 0, 0))
    return _call(body, name=name, out_shape=(_sds((S, D), F32), _sds((S, D), F32)), grid=(S // tp,),
                 in_specs=[row, wsp, vec, row, vec], out_specs=(row, row), dims=("parallel",))(d, wp, scale, x, gpost)


def _pool_mm_bwd(dx, y, d, wp, scale, gpost, name):
    S, D = dx.shape
    ng = len(POOL_WINDOWS)
    gc = D // ng
    tp = _tile(S, ROW_TILE)

    def body(dx_ref, y_ref, d_ref, w_ref, sc_ref, g_ref, dd_ref, dw_ref, dsc_ref, dg_ref):
        i = pl.program_id(0)

        @pl.when(i == 0)
        def _():
            dw_ref[...] = jnp.zeros_like(dw_ref)
            dsc_ref[...] = jnp.zeros_like(dsc_ref)
            dg_ref[...] = jnp.zeros_like(dg_ref)

        yv = y_ref[...]
        sc = sc_ref[...]
        dm, dgr = _rms_bwd_math(yv * sc, g_ref[...], dx_ref[...])
        dg_ref[...] += jnp.sum(dgr, axis=0, keepdims=True)
        dsc_ref[...] += jnp.sum(dm * yv, axis=0, keepdims=True)
        dyv = (dm * sc).astype(BF16)
        for gi in range(ng):
            cols = slice(gi * gc, (gi + 1) * gc)
            dyg = dyv[:, cols]
            dd_ref[:, cols] = lax.dot_general(dyg, w_ref[gi], (((1,), (1,)), ((), ())), preferred_element_type=F32)
            dw_ref[gi] += lax.dot_general(d_ref[:, cols], dyg, (((0,), (0,)), ((), ())), preferred_element_type=F32)

    row = pl.BlockSpec((tp, D), lambda i: (i, 0))
    vec = pl.BlockSpec((1, D), lambda i: (0, 0))
    wsp = pl.BlockSpec((ng, gc, gc), lambda i: (0, 0, 0))
    dwsp = pl.BlockSpec((ng, gc, gc), lambda i: (0, 0, 0))
    return _call(body, name=name,
                 out_shape=(_sds((S, D), F32), _sds((ng, gc, gc), F32), _sds((1, D), F32), _sds((1, D), F32)),
                 grid=(S // tp,), in_specs=[row, row, row, wsp, vec, vec], out_specs=(row, dwsp, vec, vec),
                 dims=("arbitrary",))(dx, y, d, wp, scale, gpost)


def _pool_bwd(dd, x, g, res, name):
    S, D = x.shape
    gc = D // len(POOL_WINDOWS)
    tp = _tile(S, ROW_TILE)
    nt = S // tp

    def body(dd_ref, x_ref, g_ref, res_ref, dx_ref, dg_ref, ext_ref, dh_ref):
        i = pl.program_id(0)

        @pl.when(i == 0)
        def _():
            ext_ref[pl.ds(tp, POOL_HALO), :] = jnp.zeros((POOL_HALO, D), F32)
            dg_ref[...] = jnp.zeros_like(dg_ref)

        t = _pool_counts((nt - 1 - i) * tp, tp)
        for gi, w in enumerate(POOL_WINDOWS):
            cols = slice(gi * gc, (gi + 1) * gc)
            ddv = dd_ref[:, cols]
            ext_ref[pl.ds(0, tp), cols] = ddv / jnp.minimum(t + 1, w).astype(F32)
            s = ext_ref[:, cols]
            sh = 1
            while sh < w:
                s = s + pltpu.roll(s, tp + POOL_HALO - sh, 0)
                sh *= 2
            dh_ref[:, cols] = s[:tp] - ddv
        ext_ref[pl.ds(tp, POOL_HALO), :] = ext_ref[pl.ds(0, POOL_HALO), :]
        dx, dgr = _rms_bwd_math(x_ref[...], g_ref[...], dh_ref[...])
        dx_ref[...] = dx + res_ref[...]
        dg_ref[...] += jnp.sum(dgr, axis=0, keepdims=True)

    row = pl.BlockSpec((tp, D), lambda i: (nt - 1 - i, 0))
    vec = pl.BlockSpec((1, D), lambda i: (0, 0))
    return _call(body, name=name, out_shape=(_sds((S, D), F32), _sds((1, D), F32)), grid=(nt,),
                 in_specs=[row, row, vec, row], out_specs=(row, vec),
                 scratch_shapes=[pltpu.VMEM((tp + POOL_HALO, D), F32), pltpu.VMEM((tp, D), F32)],
                 dims=("arbitrary",))(dd, x, g, res)


def _gelu(x):
    return 0.5 * x * (1.0 + jnp.tanh(0.7978845608028654 * (x + 0.044715 * x * x * x)))


def _gelu_grad(x):
    th = jnp.tanh(0.7978845608028654 * (x + 0.044715 * x * x * x))
    return 0.5 * (1.0 + th) + 0.5 * x * (1.0 - th * th) * 0.7978845608028654 * (1.0 + 3.0 * 0.044715 * x * x)


def _conv_taps(ext_ref, cols, tt):
    return tuple(ext_ref[pl.ds(CONV_HALO - k, tt), cols] for k in range(3))


def _conv_glu_fwd(u, cw, cb, name):
    S, F2 = u.shape
    wd = F2 // 4
    tt = _tile(S, ROW_TILE)

    def body(u_ref, cw_ref, cb_ref, a_ref, act_ref, ext_ref):
        it = pl.program_id(1)

        @pl.when(it == 0)
        def _():
            ext_ref[pl.ds(0, CONV_HALO), :] = jnp.zeros((CONV_HALO, 2 * wd), F32)

        ext_ref[pl.ds(CONV_HALO, tt), :] = u_ref[...]
        for cc in range(wd // LANES):
            act = []
            for half in range(2):
                cols = slice(half * wd + cc * LANES, half * wd + (cc + 1) * LANES)
                u0, u1, u2 = _conv_taps(ext_ref, cols, tt)
                act.append(cw_ref[2:3, cols] * u0 + cw_ref[1:2, cols] * u1 + cw_ref[0:1, cols] * u2 + cb_ref[:, cols])
                act_ref[:, cols] = act[half]
            a_ref[:, cc * LANES:(cc + 1) * LANES] = (_gelu(act[0]) * act[1]).astype(a_ref.dtype)
        ext_ref[pl.ds(0, CONV_HALO), :] = ext_ref[pl.ds(tt, CONV_HALO), :]

    wide = pl.BlockSpec((tt, 2 * wd), lambda h, t: (t, h))
    return _call(body, name=name, out_shape=(_sds((S, F2 // 2), BF16), _sds((S, F2), F32)), grid=(2, S // tt),
                 in_specs=[wide, pl.BlockSpec((8, 2 * wd), lambda h, t: (0, h)), pl.BlockSpec((1, 2 * wd), lambda h, t: (0, h))],
                 out_specs=(pl.BlockSpec((tt, wd), lambda h, t: (t, h)), wide),
                 scratch_shapes=[pltpu.VMEM((tt + CONV_HALO, 2 * wd), F32)], dims=("parallel", "arbitrary"))(u, cw, cb)


def _conv_glu_bwd(u, act, da, cw, name):
    S, F2 = u.shape
    wd = F2 // 4
    tt = _tile(S, ROW_TILE)
    nt = S // tt
    n = tt + CONV_HALO

    def body(u_ref, act_ref, da_ref, cw_ref, du_ref, acc_ref, carry_ref):
        it = pl.program_id(1)

        @pl.when(it == 0)
        def _():
            carry_ref[...] = jnp.zeros_like(carry_ref)
            acc_ref[...] = jnp.zeros_like(acc_ref)

        for cc in range(wd // LANES):
            gate = act_ref[:, cc * LANES:(cc + 1) * LANES]
            val = act_ref[:, wd + cc * LANES:wd + (cc + 1) * LANES]
            dav = da_ref[:, cc * LANES:(cc + 1) * LANES]
            dact = (dav * val * _gelu_grad(gate), dav * _gelu(gate))
            for half in range(2):
                cols = slice(half * wd + cc * LANES, half * wd + (cc + 1) * LANES)
                d0 = dact[half]
                e = jnp.concatenate([d0, carry_ref[:, cols]], axis=0)
                d1, d2 = pltpu.roll(e, n - 1, 0)[:tt], pltpu.roll(e, n - 2, 0)[:tt]
                du_ref[:, cols] = (cw_ref[2:3, cols] * d0 + cw_ref[1:2, cols] * d1 + cw_ref[0:1, cols] * d2).astype(du_ref.dtype)
                u0 = u_ref[:, cols]
                acc_ref[2:3, cols] += jnp.sum(d0 * u0, axis=0, keepdims=True)
                acc_ref[1:2, cols] += jnp.sum(d1 * u0, axis=0, keepdims=True)
                acc_ref[0:1, cols] += jnp.sum(d2 * u0, axis=0, keepdims=True)
                acc_ref[3:4, cols] += jnp.sum(d0, axis=0, keepdims=True)
                carry_ref[:, cols] = d0[:CONV_HALO]

    wide = pl.BlockSpec((tt, 2 * wd), lambda h, t: (nt - 1 - t, h))
    acc = pl.BlockSpec((8, 2 * wd), lambda h, t: (0, h))
    return _call(body, name=name, out_shape=(_sds((S, F2), BF16), _sds((8, F2), F32)), grid=(2, nt),
                 in_specs=[wide, wide, pl.BlockSpec((tt, wd), lambda h, t: (nt - 1 - t, h)), acc],
                 out_specs=(wide, acc), scratch_shapes=[pltpu.VMEM((CONV_HALO, 2 * wd), F32)],
                 dims=("parallel", "arbitrary"))(u, act, da, cw)


def _rope_chunk(x, cosv, sinv):
    lane = lax.broadcasted_iota(jnp.int32, x.shape, 1)
    partner = jnp.where(lane % HEAD_DIM < HEAD_DIM // 2, pltpu.roll(x, LANES - HEAD_DIM // 2, 1),
                        pltpu.roll(x, HEAD_DIM // 2, 1))
    return x * cosv + partner * sinv


def _rope(x, width, cos_t, sin_t, name):
    S = x.shape[0]
    tr = _tile(S, ROW_TILE)

    def body(x_ref, c_ref, s_ref, o_ref):
        for cc in range(width // LANES):
            cols = slice(cc * LANES, (cc + 1) * LANES)
            o_ref[:, cols] = _rope_chunk(x_ref[:, cols], c_ref[...], s_ref[...])

    row = pl.BlockSpec((tr, width), lambda i: (i, 0))
    tab = pl.BlockSpec((tr, LANES), lambda i: (i, 0))
    return _call(body, name=name, out_shape=_sds((S, width), F32), grid=(S // tr,), in_specs=[row, tab, tab],
                 out_specs=row, dims=("parallel",))(x, cos_t, sin_t)


def _attn_mask(n, reps):
    row = lax.broadcasted_iota(jnp.int32, (reps * BLOCK, 2 * BLOCK), 0) & (BLOCK - 1)
    col = lax.broadcasted_iota(jnp.int32, (reps * BLOCK, 2 * BLOCK), 1)
    rel = BLOCK + row - col
    return (rel >= 0) & (rel < WINDOW) & (n * BLOCK + col - BLOCK >= 0)


def _per_head_column(values, reps):
    grp = lax.broadcasted_iota(jnp.int32, (reps * BLOCK, 1), 0) // BLOCK
    col = jnp.zeros((reps * BLOCK, 1), F32)
    for g, v in enumerate(values):
        col = jnp.where(grp == g, v, col)
    return col


def _stack_heads(ref, heads, qpk, lane, scale):
    parts = []
    for h in heads:
        qc, qpar, _, kpar = _head_place(h, qpk)
        x = ref[:, qc * LANES:(qc + 1) * LANES]
        if scale != 1.0:
            x = x * scale
        if qpar != kpar:
            x = pltpu.roll(x, HEAD_DIM, 1)
        keep = (lane >= kpar * HEAD_DIM) & (lane < (kpar + 1) * HEAD_DIM)
        parts.append(jnp.where(keep, x, 0.0).astype(BF16))
    return jnp.concatenate(parts, axis=0)


def _unstack_heads(vals, ref, heads, qpk, lane, dtype):
    pair = None
    for g, h in enumerate(heads):
        qc, qpar, _, kpar = _head_place(h, qpk)
        v = vals[g * BLOCK:(g + 1) * BLOCK]
        if qpar != kpar:
            v = pltpu.roll(v, HEAD_DIM, 1)
        if qpar == 0:
            pair = v
        else:
            ref[:, qc * LANES:(qc + 1) * LANES] = jnp.where(lane < HEAD_DIM, pair, v).astype(dtype)


def _head_place(h, qpk):
    hk = h // qpk
    return h // 2, h % 2, hk // 2, hk % 2


def _attn_specs(S, D):
    nb = S // BLOCK
    kvw = KV_DIM // 2
    qsp = pl.BlockSpec((BLOCK, D), lambda n: (n, 0))
    prev = lambda n: jnp.maximum(n - 1, 0)
    kp = pl.BlockSpec((BLOCK, kvw), lambda n: (prev(n), 0))
    ko = pl.BlockSpec((BLOCK, kvw), lambda n: (n, 0))
    vp = pl.BlockSpec((BLOCK, kvw), lambda n: (prev(n), 1))
    vo = pl.BlockSpec((BLOCK, kvw), lambda n: (n, 1))
    stat = pl.BlockSpec((BLOCK, LANES), lambda n: (n, 0))
    smem = pl.BlockSpec(memory_space=pltpu.SMEM)
    return nb, kvw, qsp, kp, ko, vp, vo, stat, smem


def _attn_fwd(q, k, kv, sinks, name):
    S, D = q.shape
    nh = D // HEAD_DIM
    qpk = nh // N_KV_HEADS
    nb, kvw, qsp, kp, ko, vp, vo, stat, smem = _attn_specs(S, D)

    def body(q_ref, kp_ref, ko_ref, vp_ref, vo_ref, s_ref, o_ref, l_ref):
        n = pl.program_id(0)
        valid = _attn_mask(n, 2 * qpk)
        lane = lax.broadcasted_iota(jnp.int32, (BLOCK, LANES), 1)
        lacc = jnp.zeros((BLOCK, LANES), F32)
        for kc in range(N_KV_HEADS // 2):
            heads = list(range(2 * kc * qpk, 2 * (kc + 1) * qpk))
            kcols = slice(kc * LANES, (kc + 1) * LANES)
            k2 = jnp.concatenate([kp_ref[:, kcols], ko_ref[:, kcols]], axis=0).astype(BF16)
            v2 = jnp.concatenate([vp_ref[:, kcols], vo_ref[:, kcols]], axis=0).astype(BF16)
            qm = _stack_heads(q_ref, heads, qpk, lane, ATTN_SCALE)
            s = lax.dot_general(qm, k2, (((1,), (1,)), ((), ())), preferred_element_type=F32)
            s = jnp.where(valid, s, NEG_INF)
            sink = _per_head_column([s_ref[h] for h in heads], len(heads))
            m = jnp.maximum(jnp.max(s, axis=1, keepdims=True), sink)
            p = jnp.exp(s - m)
            den = jnp.sum(p, axis=1, keepdims=True) + jnp.exp(sink - m)
            of = jnp.dot(p.astype(BF16), v2, preferred_element_type=F32) / den
            lse = m + jnp.log(den)
            for g, h in enumerate(heads):
                lacc = jnp.where(lane == h, lse[g * BLOCK:(g + 1) * BLOCK], lacc)
            _unstack_heads(of, o_ref, heads, qpk, lane, o_ref.dtype)
        l_ref[...] = lacc

    return _call(body, name=name, out_shape=(_sds((S, D), BF16), _sds((S, LANES), F32)), grid=(nb,),
                 in_specs=[qsp, kp, ko, vp, vo, smem], out_specs=(qsp, stat), dims=("parallel",))(q, k, k, kv, kv, sinks)


def _attn_bwd(q, k, kv, do, lse, sinks, name):
    S, D = q.shape
    nh = D // HEAD_DIM
    qpk = nh // N_KV_HEADS
    nb, kvw, qsp, kp, ko, vp, vo, stat, smem = _attn_specs(S, D)

    def body(q_ref, kp_ref, ko_ref, vp_ref, vo_ref, do_ref, l_ref, s_ref,
             dq_ref, dkp_ref, dko_ref, dvp_ref, dvo_ref, ds_ref):
        n = pl.program_id(0)

        @pl.when(n == 0)
        def _():
            ds_ref[...] = jnp.zeros_like(ds_ref)

        valid = _attn_mask(n, 2 * qpk)
        lane = lax.broadcasted_iota(jnp.int32, (BLOCK, LANES), 1)
        lane8 = lax.broadcasted_iota(jnp.int32, (8, LANES), 1)
        lv = l_ref[...]
        dsink = jnp.zeros((8, LANES), F32)
        for kc in range(N_KV_HEADS // 2):
            heads = list(range(2 * kc * qpk, 2 * (kc + 1) * qpk))
            kcols = slice(kc * LANES, (kc + 1) * LANES)
            k2 = jnp.concatenate([kp_ref[:, kcols], ko_ref[:, kcols]], axis=0).astype(BF16)
            v2 = jnp.concatenate([vp_ref[:, kcols], vo_ref[:, kcols]], axis=0).astype(BF16)
            qm = _stack_heads(q_ref, heads, qpk, lane, ATTN_SCALE)
            gm = _stack_heads(do_ref, heads, qpk, lane, 1.0)
            s = lax.dot_general(qm, k2, (((1,), (1,)), ((), ())), preferred_element_type=F32)
            lh = jnp.concatenate([jnp.sum(jnp.where(lane == h, lv, 0.0), axis=1, keepdims=True) for h in heads], axis=0)
            p = jnp.where(valid, jnp.exp(s - lh), 0.0)
            dp = lax.dot_general(gm, v2, (((1,), (1,)), ((), ())), preferred_element_type=F32)
            delta = jnp.sum(p * dp, axis=1, keepdims=True)
            dsb = (p * (dp - delta)).astype(BF16)
            lost = jnp.exp(_per_head_column([s_ref[h] for h in heads], len(heads)) - lh) * delta
            for g, h in enumerate(heads):
                dsink = dsink - jnp.where(lane8 == h, jnp.sum(lost[g * BLOCK:(g + 1) * BLOCK]), 0.0)
            dqf = jnp.dot(dsb, k2, preferred_element_type=F32) * ATTN_SCALE
            _unstack_heads(dqf, dq_ref, heads, qpk, lane, F32)
            dk2 = lax.dot_general(dsb, qm, (((0,), (0,)), ((), ())), preferred_element_type=F32)
            dv2 = lax.dot_general(p.astype(BF16), gm, (((0,), (0,)), ((), ())), preferred_element_type=F32)
            dkp_ref[:, kcols] = dk2[:BLOCK]
            dko_ref[:, kcols] = dk2[BLOCK:]
            dvp_ref[:, kcols] = dv2[:BLOCK]
            dvo_ref[:, kcols] = dv2[BLOCK:]
        ds_ref[...] += dsink

    kvo = pl.BlockSpec((BLOCK, kvw), lambda n: (n, 0))
    acc = pl.BlockSpec((8, LANES), lambda n: (0, 0))
    part = _sds((S, kvw), F32)
    return _call(body, name=name, out_shape=(_sds((S, D), F32), part, part, part, part, _sds((8, LANES), F32)),
                 grid=(nb,), in_specs=[qsp, kp, ko, vp, vo, qsp, stat, smem],
                 out_specs=(qsp, kvo, kvo, kvo, kvo, acc), dims=("arbitrary",))(q, k, k, kv, kv, do, lse, sinks)


def _kv_grad(parts, cos_t, sin_neg_t, name):
    S, kvw = parts[0][0].shape
    nb = S // BLOCK
    flat = [a for p in parts for a in p]
    nl = len(parts)

    def body(*refs):
        c_ref, s_ref, o_ref = refs[4 * nl], refs[4 * nl + 1], refs[4 * nl + 2]
        n = pl.program_id(0)
        last = n == nb - 1
        dk = jnp.zeros((BLOCK, kvw), F32)
        dv = jnp.zeros((BLOCK, kvw), F32)
        for li in range(nl):
            kn, kown, vn, vown = refs[4 * li:4 * li + 4]
            dk = dk + kown[...] + jnp.where(last, 0.0, kn[...])
            dv = dv + vown[...] + jnp.where(last, 0.0, vn[...])
        for cc in range(kvw // LANES):
            cols = slice(cc * LANES, (cc + 1) * LANES)
            o_ref[:, cols] = _rope_chunk(dk[:, cols], c_ref[...], s_ref[...])
        o_ref[:, kvw:] = dv

    own = pl.BlockSpec((BLOCK, kvw), lambda n: (n, 0))
    nxt = pl.BlockSpec((BLOCK, kvw), lambda n: (jnp.minimum(n + 1, nb - 1), 0))
    tab = pl.BlockSpec((BLOCK, LANES), lambda n: (n, 0))
    return _call(body, name=name, out_shape=_sds((S, 2 * kvw), F32), grid=(nb,),
                 in_specs=[nxt, own, nxt, own] * nl + [tab, tab],
                 out_specs=pl.BlockSpec((BLOCK, 2 * kvw), lambda n: (n, 0)), dims=("parallel",))(*flat, cos_t, sin_neg_t)


def _sum_blocks(name, qc, grid, out_shape, out_block, out_imap, ins, out_dtype=F32, into=None):
    nin = len(ins)

    def body(qc_ref, *refs):
        acc = refs[0][...].astype(F32)
        for r in refs[1:nin]:
            acc = acc + r[...].astype(F32)
        refs[-1][...] = acc.astype(refs[-1].dtype)

    in_specs = [pl.BlockSpec(b, m) for _, b, m in ins]
    operands = [a for a, _, _ in ins]
    aliases = None
    if into is not None:
        in_specs.append(ANY)
        operands.append(into)
        aliases = {1 + nin: 0}
    gs = pltpu.PrefetchScalarGridSpec(num_scalar_prefetch=1, grid=grid, in_specs=in_specs,
                                      out_specs=pl.BlockSpec(out_block, out_imap))
    return _call(body, name=name, out_shape=_sds(out_shape, out_dtype), grid_spec=gs,
                 dims=("parallel",) * len(grid), aliases=aliases)(qc, *operands)


def _adamw(w, g, m, v, name, part=None, into=None):
    shape = w.shape
    C = shape[-1]
    R = w.size // C
    k, cnt, nparts = part if part is not None else (0, 1, 1)
    tr = _tile(R // nparts, max(8, (1 << 18) // C))
    first = k * (R // nparts // tr)
    rows = cnt * (R // nparts)

    def body(w_ref, g_ref, m_ref, v_ref, *outs):
        go_ref, d_ref, nm_ref, nv_ref = outs[-4:]
        gv = g_ref[...]
        nm = ADAM_B1 * m_ref[...] + (1.0 - ADAM_B1) * gv
        nv = ADAM_B2 * v_ref[...] + (1.0 - ADAM_B2) * (gv * gv)
        m_hat = nm / (1.0 - ADAM_B1 ** ADAM_STEP)
        v_hat = nv / (1.0 - ADAM_B2 ** ADAM_STEP)
        go_ref[...] = gv
        d_ref[...] = -ADAM_LR * (m_hat / (jnp.sqrt(v_hat) + ADAM_EPS) + ADAM_WD * w_ref[...])
        nm_ref[...] = nm
        nv_ref[...] = nv

    blk = pl.BlockSpec((tr, C), lambda i: (first + i, 0))
    flat = _sds((R, C), F32)
    operands = [a.reshape(-1, C) for a in (w, g, m, v)]
    in_specs, aliases = [blk] * 4, None
    if g.size != w.size:
        assert g.size == rows * C, (name, g.shape, shape, part)
        in_specs[1] = pl.BlockSpec((tr, C), lambda i: (i, 0))
    if into is not None:
        operands += [a.reshape(R, C) for a in into]
        in_specs = in_specs + [ANY] * 4
        aliases = {4 + i: i for i in range(4)}
    outs = _call(body, name=name, out_shape=(flat,) * 4, grid=(rows // tr,), in_specs=in_specs,
                 out_specs=(blk,) * 4, dims=("parallel",), aliases=aliases)(*operands)
    return tuple(o.reshape(shape) for o in outs)


def _place():
    x, y, c = lax.axis_index("x"), lax.axis_index("y"), lax.axis_index("c")
    chips = [(1 - x, y), (x, 1 - y), (1 - x, 1 - y)]
    return x, y, c, chips


def _at(ref, nd, dims):
    idx = [slice(None)] * nd
    for d, v in dims.items():
        idx[d] = pl.ds(v[0], v[1]) if isinstance(v, tuple) else v
    return ref.at[tuple(idx)]


def _remote(src, dst, send_sem, recv_sem, dev):
    return pltpu.make_async_remote_copy(src_ref=src, dst_ref=dst, send_sem=send_sem, recv_sem=recv_sem,
                                        device_id=dev, device_id_type=MESH)


def _split_call(body, name, out_shape, in_specs, out_specs, aliases):
    return pl.pallas_call(body, name=name, out_shape=out_shape, in_specs=in_specs, out_specs=out_specs,
                          input_output_aliases=aliases,
                          compiler_params=pltpu.CompilerParams(has_side_effects=EFFECT))


def _hbm(a):
    return pltpu.with_memory_space_constraint(a, pltpu.HBM)


def _copies_start(srcs, lands, after, name, plan, ncopies):
    n, m = len(srcs), len(lands)

    def body(*refs):
        src, land = refs[:n], refs[n:n + m]
        send_sems, recv_sems, token = refs[n + m + 1], refs[n + m + 2], refs[-1]
        x, y, c, chips = _place()
        for k, (s, d, dev) in enumerate(plan(x, y, c, chips, src, land)):
            _remote(s, d, send_sems.at[k], recv_sems.at[k], dev).start()
        token[...] = jnp.zeros_like(token)

    thru = tuple(pltpu.HBM(a.shape, a.dtype) for a in list(srcs) + list(lands))
    outs = _split_call(
        body, name,
        out_shape=(pltpu.SemaphoreType.DMA((ncopies,)), pltpu.SemaphoreType.DMA((ncopies,))) + thru + (_sds((8, LANES), F32),),
        in_specs=(HBM,) * (n + m) + (ANY,), out_specs=(SEM, SEM) + (HBM,) * (n + m) + (VMEM,),
        aliases={i: 2 + i for i in range(n + m)})(*[_hbm(a) for a in srcs], *[_hbm(a) for a in lands], after)
    return dict(send=outs[0], recv=outs[1], srcs=outs[2:2 + n], lands=outs[2 + n:2 + n + m], token=outs[-1])


def _copies_wait(handle, after, name, plan):
    srcs, lands = handle['srcs'], handle['lands']
    n, m = len(srcs), len(lands)

    def body(*refs):
        src, land = refs[:n], refs[n:n + m]
        send_sems, recv_sems = refs[n + m], refs[n + m + 1]
        x, y, c, chips = _place()
        for k, (s, d, dev) in enumerate(plan(x, y, c, chips, src, land)):
            cp = _remote(s, d, send_sems.at[k], recv_sems.at[k], dev)
            cp.wait_send()
            cp.wait_recv()

    thru = tuple(pltpu.HBM(a.shape, a.dtype) for a in list(srcs) + list(lands))
    outs = _split_call(body, name, out_shape=thru, in_specs=(HBM,) * (n + m) + (SEM, SEM, ANY),
                       out_specs=(HBM,) * (n + m), aliases={i: i for i in range(n + m)})(
        *srcs, *lands, handle['send'], handle['recv'], after)
    return outs[:n], outs[n:]


def _gather_plan(x, y, c, chips, src, land, arriving):
    q = 2 * x + y
    peers = [(ch[0], ch[1], c) for ch in chips] + [(x, y, 1 - c)]
    slots = [2 * ch[0] + ch[1] for ch in chips] + [q]
    return [(s, d.at[slots[j] if arriving else q], peers[j]) for s, d in zip(src, land) for j in range(4)]


def _gather_half_plan(x, y, c, chips, src, land, arriving):
    q = 2 * x + y
    out = []
    for s, d in zip(src, land):
        hs = s.shape[0] // 2
        rows = pl.ds(c * hs, hs)
        for ch in chips:
            out.append((s.at[rows], d.at[2 * ch[0] + ch[1] if arriving else q, rows], (ch[0], ch[1], c)))
        out.append((s, d.at[q], (x, y, 1 - c)))
    return out


def _exchange_plan(x, y, c, chips, src, land, arriving):
    out = []
    for d in land:
        hs = d.shape[1] // 2
        for ch in chips:
            slot = 2 * ch[0] + ch[1]
            out.append((d.at[slot, pl.ds(c * hs, hs)], d.at[slot, pl.ds(((1 - c) if arriving else c) * hs, hs)], (x, y, 1 - c)))
    return out


def _scatter_plan(shard_axes, shapes):
    def plan(x, y, c, chips, src, land):
        out = []
        for s, d, sd, shp in zip(src, land, shard_axes, shapes):
            ss = shp[sd] // N_SHARDS
            for j, ch in enumerate(chips):
                out.append((_at(s, len(shp), {sd: ((2 * ch[0] + ch[1]) * ss, ss)}), d.at[j], (ch[0], ch[1], c)))
        return out
    return plan


def _half_dims(shape, hd, c):
    hs = shape[hd] // 2
    return {hd: (c * hs, hs)}


def _swap_halves(grads, specs, name):
    n = len(grads)
    outs_shape = []
    for a, (sd, hd) in zip(grads, specs):
        shp = list(a.shape)
        shp[hd] //= 2
        outs_shape.append(_sds(shp, F32))

    def body(*refs):
        ins, outs = refs[:n], refs[n:2 * n]
        send_sems, recv_sems = refs[2 * n:]
        x, y, c, _ = _place()
        cps = []
        for ai, (sd, hd) in enumerate(specs):
            shp = grads[ai].shape
            cp = _remote(_at(ins[ai], len(shp), _half_dims(shp, hd, 1 - c)), outs[ai],
                         send_sems.at[ai], recv_sems.at[ai], (x, y, 1 - c))
            cp.start()
            cps.append(cp)
        for cp in cps:
            cp.wait()

    return _call(body, name=name, out_shape=tuple(outs_shape), in_specs=[ANY] * n, out_specs=tuple([ANY] * n),
                 scratch_shapes=[pltpu.SemaphoreType.DMA((n,)), pltpu.SemaphoreType.DMA((n,))])(*grads)


def _share_halves(arrs, half_axes, layers, after, name):
    n = len(arrs)

    def body(*refs):
        ins, outs = refs[:n], refs[n + 1:2 * n + 1]
        send_sems, recv_sems = refs[2 * n + 1:]
        x, y, c, _ = _place()

        def half(ref, ai, which):
            shp = arrs[ai].shape
            hs = shp[half_axes[ai]] // 2
            dims = {half_axes[ai]: (which * hs, hs)}
            if layers[ai] is not None:
                dims[0] = layers[ai]
            return _at(ref, len(shp), dims)

        sends = []
        for ai in range(n):
            cp = _remote(half(ins[ai], ai, c), half(outs[ai], ai, c), send_sems.at[ai], recv_sems.at[ai], (x, y, 1 - c))
            cp.start()
            sends.append(cp)
        for ai in range(n):
            land = half(outs[ai], ai, 1 - c)
            _remote(land, land, send_sems.at[ai], recv_sems.at[ai], (x, y, c)).wait_recv()
        for cp in sends:
            cp.wait_send()

    return _call(body, name=name, out_shape=tuple(_sds(a.shape, a.dtype) for a in arrs), in_specs=[ANY] * (n + 1),
                 out_specs=tuple([ANY] * n), aliases={i: i for i in range(n)},
                 scratch_shapes=[pltpu.SemaphoreType.DMA((n,)), pltpu.SemaphoreType.DMA((n,))])(*arrs, after)


def _gather_small(v, name):
    R, C = v.shape

    def body(x_ref, out_ref, send_sems, recv_sems, local_sem):
        x, y, c, chips = _place()
        me, sibling = (x, y, c), (x, y, 1 - c)

        def rows(px, py, pc):
            return out_ref.at[pl.ds((4 * px + 2 * py + pc) * R, R), :]

        def copy(k, block, to, src=None):
            return _remote(rows(*block) if src is None else src, rows(*block), send_sems.at[k], recv_sems.at[k], to)

        mine = pltpu.make_async_copy(x_ref, rows(*me), local_sem)
        mine.start()
        first = [copy(0, me, sibling, src=x_ref)]
        first += [copy(1 + j, me, (ch[0], ch[1], c), src=x_ref) for j, ch in enumerate(chips)]
        for cp in first:
            cp.start()
        passed = [copy(4 + j, (ch[0], ch[1], c), sibling) for j, ch in enumerate(chips)]
        for j, ch in enumerate(chips):
            copy(1 + j, (ch[0], ch[1], c), me).wait_recv()
            passed[j].start()
        copy(0, sibling, me).wait_recv()
        for j, ch in enumerate(chips):
            copy(4 + j, (ch[0], ch[1], 1 - c), me).wait_recv()
        for cp in first + passed:
            cp.wait_send()
        mine.wait()

    vm = pl.BlockSpec(memory_space=pltpu.VMEM)
    return _call(body, name=name, out_shape=_sds((8 * R, C), v.dtype), in_specs=[vm], out_specs=vm,
                 scratch_shapes=[pltpu.SemaphoreType.DMA((7,)), pltpu.SemaphoreType.DMA((7,)),
                                 pltpu.SemaphoreType.DMA])(v)


def _sum8(g, name):
    _, R, C = g.shape

    def body(g_ref, o_ref):
        acc = g_ref[0]
        for d in range(1, 8):
            acc = acc + g_ref[d]
        o_ref[...] = acc

    return _call(body, name=name, out_shape=_sds((R, C), F32), in_specs=[pl.BlockSpec(memory_space=pltpu.VMEM)],
                 out_specs=pl.BlockSpec(memory_space=pltpu.VMEM))(g)


def _rope_tables(positions):
    inv_freq = 1.0 / (ROPE_THETA ** (jnp.arange(0, HEAD_DIM, 2, dtype=F32) / HEAD_DIM))
    ang = positions.astype(F32)[:, None] * inv_freq
    cosv, sinv = jnp.cos(ang), jnp.sin(ang)
    return jnp.tile(cosv, (1, 4)), jnp.tile(jnp.concatenate([-sinv, sinv], axis=1), (1, 2))


def _blocked(a):
    parts = jnp.split(a, 4, axis=-1)
    return jnp.concatenate([parts[0], parts[2], parts[1], parts[3]], axis=-1)


def _arrived(Wl, name, after):
    if callable(Wl[name]):
        Wl[name] = Wl[name](after)
    return Wl[name]


def _local_step(x, target, positions, P, weights_of, ffn_grads_done, grads_done):
    S, D = x.shape
    depth = P['mix_pre_g'].shape[0]
    n_a = depth // 2
    cos_t, sin_t = _rope_tables(positions)
    row = lambda a, l: a[l][None]
    cb = [_blocked(P['ffn_conv_b'][l])[None] for l in range(depth)]
    sv, W = {}, {}
    kv = k_rot = None
    for l in range(depth):
        t = f"l{l}"
        W[l], zero = weights_of(l, x)
        sv[l, 'x_in'] = x
        g_pre = row(P['mix_pre_g'], l) + zero
        if l < n_a:
            d = _pool_fwd(x, g_pre, "pool_fwd_" + t)
            y, x = _pool_mm_fwd(d, W[l]['pool_w'], W[l]['pool_scale'], x, row(P['mix_post_g'], l), "pool_mm_fwd_" + t)
            sv[l, 'd'], sv[l, 'y'] = d, y
        else:
            j = l - n_a
            h = _rms_fwd(x, g_pre, BF16, "q_norm_" + t)
            if l == n_a:
                hkv = _rms_fwd(x, P['kv_norm_g'][None], BF16, "kv_norm")
                kv = _matmul(hkv, W[l]['w_kv'], 'nn', F32, "kv_proj", 1024, 512, 1024)
                k_rot = _rope(kv, KV_DIM // 2, cos_t, sin_t, "k_rope")
                sv['hkv'] = hkv
            qraw = _matmul(h, W[l]['w_q'], 'nn', F32, "q_proj_" + t, 1024,1024, 1024)
            q = _rope(qraw, D, cos_t, sin_t, "q_rope_" + t)
            o, lse = _attn_fwd(q, k_rot, kv, P['sinks'][j], "attn_fwd_" + t)
            m = _matmul(o, W[l]['w_o'], 'nn', F32, "o_proj_" + t, 1024,1024, 1024)
            x = _res_rms_fwd(x, m, row(P['mix_post_g'], l), "mix_post_" + t)
            sv[l, 'h'], sv[l, 'q'], sv[l, 'o'], sv[l, 'lse'], sv[l, 'm'] = h, q, o, lse, m
        sv[l, 'x1'] = x
        if 'pre_ffn' in W[l]:
            W[l].pop('pre_ffn')(x)
        h2 = _rms_fwd(x, row(P['ffn_pre_g'], l), BF16, "ffn_norm_" + t)
        w_in = _arrived(W[l], 'w_in', h2)
        u = _matmul(h2, w_in, 'nn', F32, "ffn_up_" + t, 1024,w_in.shape[2], 1024, b_blocks=True)
        a, sv[l, 'act'] = _conv_glu_fwd(u, W[l]['cw'], cb[l] + W[l].pop('tie', 0.0), "ffn_glu_" + t)
        f = _matmul(a, _arrived(W[l], 'w_out', a), 'nn', F32, "ffn_down_" + t, 1024,1024, 2816)
        x = _res_rms_fwd(x, f, row(P['ffn_post_g'], l), "ffn_post_" + t)
        sv[l, 'h2'], sv[l, 'u'], sv[l, 'a'], sv[l, 'f'] = h2, u, a, f

    dx, sq = _loss_grad(x, target, "loss")
    kv_parts = []
    zero = 0.0
    for l in reversed(range(depth)):
        t = f"l{l}"
        G = {}
        wd = W[l]['w_in'].shape[2]
        df, G['ffn_post_g'] = _rms_bwd(sv[l, 'f'], row(P['ffn_post_g'], l) + zero, dx, None, "ffn_post_bwd_" + t)
        da = _matmul(df, W[l]['w_out'], 'nt', F32, "ffn_down_dx_" + t, 1024,wd, 1024)
        G['ffn_w_out'] = _matmul(sv[l, 'a'], df, 'tn', F32, "ffn_down_dw_" + t, wd, 1024, 1024)
        du, acc = _conv_glu_bwd(sv[l, 'u'], sv[l, 'act'], da, W[l]['cw'], "ffn_glu_bwd_" + t)
        G['ffn_conv_w'] = _blocked(acc[0:3])
        G['ffn_conv_b'] = _blocked(acc[3:4])
        dh2 = _matmul(du, W[l]['w_in'], 'nt', F32, "ffn_up_dx_" + t, 1024,1024, wd, b_blocks=True)
        G['ffn_w_in'] = _matmul(sv[l, 'h2'], du, 'tn', F32, "ffn_up_dw_" + t, 1024, wd, 2048, out_perm=True)
        zero = ffn_grads_done(l, G, dh2)
        dx, G['ffn_pre_g'] = _rms_bwd(sv[l, 'x1'], row(P['ffn_pre_g'], l) + zero, dh2, dx, "ffn_norm_bwd_" + t)
        if l < n_a:
            dd, G['pool_w'], G['pool_scale'], G['mix_post_g'] = _pool_mm_bwd(
                dx, sv[l, 'y'], sv[l, 'd'], W[l]['pool_w'], W[l]['pool_scale'], row(P['mix_post_g'], l), "pool_mm_bwd_" + t)
            dx, G['mix_pre_g'] = _pool_bwd(dd, sv[l, 'x_in'], row(P['mix_pre_g'], l), dx, "pool_bwd_" + t)
        else:
            j = l - n_a
            dm, G['mix_post_g'] = _rms_bwd(sv[l, 'm'], row(P['mix_post_g'], l), dx, None, "mix_post_bwd_" + t)
            do = _matmul(dm, W[l]['w_o'], 'nt', F32, "o_proj_dx_" + t, 1024,1024, 1024)
            G['w_o'] = _matmul(sv[l, 'o'], dm, 'tn', F32, "o_proj_dw_" + t, 1024, 1024, 1024)
            dq, dkn, dko, dvn, dvo, dsk = _attn_bwd(sv[l, 'q'], k_rot, kv, do, sv[l, 'lse'], P['sinks'][j], "attn_bwd_" + t)
            G['sinks'] = dsk[0:1]
            kv_parts.append((dkn, dko, dvn, dvo))
            dqraw = _rope(dq, D, cos_t, -sin_t, "q_rope_bwd_" + t)
            dh = _matmul(dqraw, W[l]['w_q'], 'nt', F32, "q_proj_dx_" + t, 1024,1024, 1024)
            G['w_q'] = _matmul(sv[l, 'h'], dqraw, 'tn', F32, "q_proj_dw_" + t, 1024, 1024, 1024)
            dx, G['mix_pre_g'] = _rms_bwd(sv[l, 'x_in'], row(P['mix_pre_g'], l), dh, dx, "q_norm_bwd_" + t)
            if l == n_a:
                dkv = _kv_grad(kv_parts, cos_t, -sin_t, "kv_grad")
                dhkv = _matmul(dkv, W[l]['w_kv'], 'nt', F32, "kv_proj_dx", 1024, 1024, 512)
                G['w_kv'] = _matmul(sv['hkv'], dkv, 'tn', F32, "kv_proj_dw", 1024, 512, 1024)
                dx, G['kv_norm_g'] = _rms_bwd(sv[l, 'x_in'], P['kv_norm_g'][None], dhkv, dx, "kv_norm_bwd")
        zero = grads_done(l, G, dx)
    return sq, dx


SMALL = ['mix_pre_g', 'mix_post_g', 'kv_norm_g', 'sinks', 'ffn_pre_g', 'ffn_post_g', 'ffn_conv_b', 'ffn_conv_w', 'pool_scale']
BIG = {'ffn_w_in': (1, 0, 1), 'ffn_w_out': (0, 1, 2), 'w_q': (0, 1, 2), 'w_o': (0, 1, 2), 'w_kv': (0, 1, 1),
       'pool_w': (1, 0, 1)}


def _swap_plan(half_axes, shapes):
    def plan(x, y, c, chips, src, land):
        return [(_at(s, len(shp), _half_dims(shp, hd, 1 - c)), d, (x, y, 1 - c))
                for s, d, hd, shp in zip(src, land, half_axes, shapes)]
    return plan


def _swap_start(pieces, after, name):
    arrs = [p[0] for p in pieces]
    plan = _swap_plan([p[2] for p in pieces], [a.shape for a in arrs])
    lands = []
    for a, p in zip(arrs, pieces):
        shp = list(a.shape)
        shp[p[2]] //= 2
        lands.append(lax.empty(tuple(shp), F32))
    return pieces, _copies_start(arrs, lands, after, name, plan, len(arrs)), plan


def _swap_wait(started, after, name):
    pieces, handle, plan = started
    arrs, theirs = _copies_wait(handle, after, name, plan)
    return [(a,) + p[1:] for a, p in zip(arrs, pieces)], list(theirs)


def _reduce_start(pieces, theirs, qc, after, tag):
    arrs = [p[0] for p in pieces]
    specs = [(p[1], p[2]) for p in pieces]
    sums = []
    for pi, (a, (sd, hd), r) in enumerate(zip(arrs, specs, theirs)):
        shp = r.shape
        nd = len(shp)
        if nd == 3:
            blk, grid = tuple(shp), (1,)
            mine = lambda i, s: (s[1], 0, 0)
            zero = lambda i, s: (0, 0, 0)
        elif hd == 0:
            tr = _tile(shp[0], max(16, (1 << 18) // shp[1]), 16)
            blk, grid = (tr, shp[1]), (shp[0] // tr,)
            nblk = shp[0] // tr
            mine = lambda i, s, nblk=nblk: (s[1] * nblk + i, 0)
            zero = lambda i, s: (i, 0)
        else:
            tr = _tile(shp[0], max(16, (1 << 18) // shp[1]), 16)
            blk, grid = (tr, shp[1]), (shp[0] // tr,)
            mine = lambda i, s: (i, s[1])
            zero = lambda i, s: (i, 0)
        sums.append(_sum_blocks(f"grad_chip_sum_{tag}_{pi}", qc, grid, shp, blk, zero, [(a, blk, mine), (r, blk, zero)],
                                out_dtype=BF16))

    lands = []
    for s_arr, (sd, hd) in zip(sums, specs):
        shp = list(s_arr.shape)
        shp[sd] //= N_SHARDS
        lands.append(lax.empty((3,) + tuple(shp), BF16))
    plan = _scatter_plan([sd for sd, _ in specs], [s.shape for s in sums])
    handle = _copies_start(sums, lands, after, "grad_scatter_start_" + tag, plan, 3 * len(sums))
    return dict(handle=handle, plan=plan, pieces=pieces, tag=tag)


def _reduce_finish(state, after, qc, outs, out_shapes):
    handle, pieces, tag = state['handle'], state['pieces'], state['tag']
    sums, recvd = _copies_wait(handle, after, "grad_scatter_wait_" + tag, state['plan'])
    for pi, ((a, sd, hd, oname, fixed, ohd), s_arr, r) in enumerate(zip(pieces, sums, recvd)):
        shp = r.shape[1:]
        nd = len(shp)
        lead = (fixed[0],) if fixed else ()
        none = (None,) if fixed else ()
        n_stack = out_shapes[oname][0]
        if nd == 3:
            blk, grid = tuple(shp), (1,)
            mine = lambda i, s: (0, s[0], 0)
            rk = [lambda i, s, k=k: (k, 0, 0, 0) for k in range(3)]
            oshape = (n_stack, 2 * shp[0]) + tuple(shp[1:])
            oblk = none + blk
            omap = lambda i, s, lead=lead: lead + (s[1], 0, 0)
        elif sd == 1:
            tr = _tile(shp[0], max(16, (1 << 18) // shp[1]), 16)
            blk, grid = (tr, shp[1]), (shp[0] // tr,)
            nblk = shp[0] // tr
            mine = lambda i, s: (i, s[0])
            rk = [lambda i, s, k=k: (k, i, 0) for k in range(3)]
            oshape = (n_stack, 2 * shp[0], shp[1])
            oblk = none + blk
            omap = lambda i, s, lead=lead, nblk=nblk: lead + (s[1] * nblk + i, 0)
        else:
            tr = _tile(shp[0], max(16, (1 << 18) // shp[1]), 16)
            blk, grid = (tr, shp[1]), (shp[0] // tr,)
            nblk = shp[0] // tr
            mine = lambda i, s, nblk=nblk: (s[0] * nblk + i, 0)
            rk = [lambda i, s, k=k: (k, i, 0) for k in range(3)]
            oshape = ((n_stack,) if fixed else ()) + (shp[0], 2 * shp[1])
            oblk = none + blk
            omap = lambda i, s, lead=lead: lead + (i, s[1])
        assert tuple(oshape) == tuple(out_shapes[oname]), (oname, oshape, out_shapes[oname])
        ins = [(s_arr, blk, mine)] + [(r, (None,) + blk, rk[k]) for k in range(3)]
        outs[oname] = _sum_blocks(f"grad_total_{tag}_{pi}", qc, grid, oshape, oblk, omap, ins, into=outs.get(oname))


def _pack_small(parts):
    rows, offs, r = [], [], 0
    for a in parts:
        flat = a.reshape(-1)
        nr = -(-flat.size // (8 * LANES)) * 8
        rows.append(jnp.pad(flat, (0, nr * LANES - flat.size)).reshape(nr, LANES))
        offs.append((r, nr, a.shape))
        r += nr
    return jnp.concatenate(rows, axis=0), offs


def _unpack_small(packed, offs):
    return [packed[r:r + nr].reshape(-1)[:math.prod(shape)].reshape(shape) for r, nr, shape in offs]


def kernel(x, positions, mix_pre_g, mix_post_g, pool_w, pool_scale, kv_norm_g, w_kv, w_q, w_o, sinks, ffn_pre_g, ffn_post_g, ffn_w_in, ffn_conv_w, ffn_conv_b, ffn_w_out, loss_target, m_mix_pre_g, m_mix_post_g, m_pool_w, m_pool_scale, m_kv_norm_g, m_w_kv, m_w_q, m_w_o, m_sinks, m_ffn_pre_g, m_ffn_post_g, m_ffn_w_in, m_ffn_conv_w, m_ffn_conv_b, m_ffn_w_out, v_mix_pre_g, v_mix_post_g, v_pool_w, v_pool_scale, v_kv_norm_g, v_w_kv, v_w_q, v_w_o, v_sinks, v_ffn_pre_g, v_ffn_post_g, v_ffn_w_in, v_ffn_conv_w, v_ffn_conv_b, v_ffn_w_out):
    w = dict(mix_pre_g=mix_pre_g, mix_post_g=mix_post_g, pool_w=pool_w, pool_scale=pool_scale, kv_norm_g=kv_norm_g,
             w_kv=w_kv, w_q=w_q, w_o=w_o, sinks=sinks, ffn_pre_g=ffn_pre_g, ffn_post_g=ffn_post_g, ffn_w_in=ffn_w_in,
             ffn_conv_w=ffn_conv_w, ffn_conv_b=ffn_conv_b, ffn_w_out=ffn_w_out)
    mom = dict(mix_pre_g=m_mix_pre_g, mix_post_g=m_mix_post_g, pool_w=m_pool_w, pool_scale=m_pool_scale,
               kv_norm_g=m_kv_norm_g, w_kv=m_w_kv, w_q=m_w_q, w_o=m_w_o, sinks=m_sinks, ffn_pre_g=m_ffn_pre_g,
               ffn_post_g=m_ffn_post_g, ffn_w_in=m_ffn_w_in, ffn_conv_w=m_ffn_conv_w, ffn_conv_b=m_ffn_conv_b,
               ffn_w_out=m_ffn_w_out)
    var = dict(mix_pre_g=v_mix_pre_g, mix_post_g=v_mix_post_g, pool_w=v_pool_w, pool_scale=v_pool_scale,
               kv_norm_g=v_kv_norm_g, w_kv=v_w_kv, w_q=v_w_q, w_o=v_w_o, sinks=v_sinks, ffn_pre_g=v_ffn_pre_g,
               ffn_post_g=v_ffn_post_g, ffn_w_in=v_ffn_w_in, ffn_conv_w=v_ffn_conv_w, ffn_conv_b=v_ffn_conv_b,
               ffn_w_out=v_ffn_w_out)
    depth = mix_pre_g.shape[0]
    q_chip = 2 * lax.axis_index("x") + lax.axis_index("y")
    qc = jnp.stack([q_chip, lax.axis_index("c")]).astype(jnp.int32)

    n_a = depth // 2
    D = x.shape[-1]
    gc = pool_w.shape[3]

    def layer_shards(l):
        first = []
        if l < n_a:
            first.append(('pool_w', pool_w[l].astype(BF16)))
        else:
            first += [('w_q', w_q[l - n_a].astype(BF16)), ('w_o', w_o[l - n_a].astype(BF16))]
            if l == n_a:
                first.append(('w_kv', w_kv.astype(BF16)))
        ffn = [('w_in', ffn_w_in[l].astype(BF16)), ('w_out', ffn_w_out[l].astype(BF16))]
        if l == 0:
            return [first + [('conv_w', ffn_conv_w), ('pool_scale', pool_scale)], ffn[:1], ffn[1:]]
        return [first + ffn]

    def start_group(items, after, name, plan):
        srcs = [a for _, a in items]
        lands = [lax.empty((N_SHARDS,) + a.shape, a.dtype) for a in srcs]
        handle = _copies_start(srcs, lands, after, name, functools.partial(plan, arriving=False), 4 * len(srcs))
        return [n for n, _ in items], handle, plan

    def wait_group(started, after, name):
        names, handle, plan = started
        _, lands = _copies_wait(handle, after, name, functools.partial(plan, arriving=True))
        return dict(zip(names, lands))

    def start_exchange(got, after, name):
        names = list(got)
        handle = _copies_start([], [got[n] for n in names], after, name,
                               functools.partial(_exchange_plan, arriving=False), 3 * len(names))
        return names, handle, _exchange_plan

    groups0 = layer_shards(0)
    small0 = start_group(groups0[0], x, "weight_gather_start_l0_small", _gather_plan)
    in0 = start_group(groups0[1], small0[1]['token'], "weight_gather_start_l0_in", _gather_half_plan)
    out0 = start_group(groups0[2], in0[1]['token'], "weight_gather_start_l0_out", _gather_half_plan)
    pending, shared, steps = {}, {}, {}

    def start_next(l, after):
        pending[l + 1] = start_group(layer_shards(l + 1)[0], after, f"weight_gather_start_l{l + 1}", _gather_plan)
        return pending[l + 1][1]['token'][0, 0]

    def pre_ffn0(after):
        landed = wait_group(in0, after, "weight_gather_wait_l0_in")
        steps['in'] = start_exchange(landed, after, "weight_exchange_start_l0_in")

    def w_in0(Wl, after):
        both = wait_group(steps['in'], after, "weight_exchange_wait_l0_in")
        Wl['tie'] = start_next(0, both['w_in'])
        return both['w_in']

    def w_out0(after):
        landed = wait_group(out0, after, "weight_gather_wait_l0_out")
        both = wait_group(start_exchange(landed, after, "weight_exchange_start_l0_out"), after,
                          "weight_exchange_wait_l0_out")
        return both['w_out'].reshape(-1, D)

    def weights_of(l, x_now):
        zero = 0.0
        if l == 0:
            got = wait_group(small0, out0[1]['token'], "weight_gather_wait_l0_small")
            shared['conv_w'] = got['conv_w']
            shared['pool_scale'] = got['pool_scale'].transpose(1, 0, 2).reshape(n_a, D)
        else:
            got = wait_group(pending.pop(l), x_now, f"weight_gather_wait_l{l}")
            if l + 1 < depth:
                zero = start_next(l, got['w_in'])
        taps = jnp.concatenate([shared['conv_w'][p, l] for p in (0, 2, 1, 3)], axis=-1)
        Wl = dict(cw=jnp.pad(taps, ((0, 5), (0, 0))))
        if l == 0:
            Wl['pre_ffn'], Wl['w_in'], Wl['w_out'] = pre_ffn0, functools.partial(w_in0, Wl), w_out0
        else:
            Wl['w_in'], Wl['w_out'] = got['w_in'], got['w_out'].reshape(-1, D)
        if l < n_a:
            Wl['pool_w'] = got['pool_w'].transpose(1, 0, 2, 3).reshape(-1, gc, gc)
            Wl['pool_scale'] = shared['pool_scale'][l][None]
        else:
            Wl['w_q'], Wl['w_o'] = got['w_q'].reshape(D, D), got['w_o'].reshape(D, D)
            if l == n_a:
                Wl['w_kv'] = got['w_kv'].reshape(D, -1)
        return Wl, zero

    big_shapes = {n: w[n].shape for n in BIG}
    big, G, scattering = {}, {}, {}

    swapping = {}

    def piece(n, g, l):
        lead = {0: (l if n.startswith('ffn') or n == 'pool_w' else l - n_a)} if len(big_shapes[n]) > 2 else {}
        return (g, BIG[n][0], BIG[n][1], n, lead, BIG[n][2])

    def ffn_grads_done(l, Gl, after):
        swapping[l] = _swap_start([piece(n, Gl[n], l) for n in ('ffn_w_in', 'ffn_w_out')], after, f"grad_swap_start_l{l}")
        return swapping[l][1]['token'][0, 0]

    small = {}

    def gather_small_grads():
        local = {
            'mix_pre_g': jnp.concatenate([G['mix_pre_g', l] for l in range(depth)], axis=0),
            'mix_post_g': jnp.concatenate([G['mix_post_g', l] for l in range(depth)], axis=0),
            'kv_norm_g': G['kv_norm_g', n_a][0],
            'sinks': jnp.concatenate([G['sinks', l][:, :sinks.shape[1]] for l in range(n_a, depth)], axis=0),
            'ffn_pre_g': jnp.concatenate([G['ffn_pre_g', l] for l in range(depth)], axis=0),
            'ffn_post_g': jnp.concatenate([G['ffn_post_g', l] for l in range(depth)], axis=0),
            'ffn_conv_b': jnp.concatenate([G['ffn_conv_b', l] for l in range(depth)], axis=0),
            'ffn_conv_w': jnp.stack([G['ffn_conv_w', l] for l in range(depth)], axis=0),
            'pool_scale': jnp.concatenate([G['pool_scale', l] for l in range(n_a)], axis=0),
        }
        packed, small['offs'] = _pack_small([local[n] for n in SMALL])
        small['gathered'] = _gather_small(packed, "small_grad_gather").reshape(8, packed.shape[0], LANES)

    def grads_done(l, Gl, dx_now):
        for n, g in Gl.items():
            if n not in BIG:
                G[n, l] = g
        after = dx_now
        if l == 0:
            gather_small_grads()
            after = small['gathered']
        pieces, theirs = _swap_wait(swapping.pop(l), after, f"grad_swap_wait_l{l}")
        rest = [piece(n, Gl[n], l) for n in BIG if n in Gl and not n.startswith('ffn')]
        theirs += list(_swap_halves([p[0] for p in rest], [(p[1], p[2]) for p in rest], f"grad_swap_halves_l{l}"))
        scattering[l] = _reduce_start(pieces + rest, theirs, qc, after, f"l{l}")
        if l + 1 in scattering:
            _reduce_finish(scattering.pop(l + 1), dx_now, qc, big, big_shapes)
        return scattering[l]['handle']['token'][0, 0]

    P = {n: w[n] for n in ('mix_pre_g', 'mix_post_g', 'kv_norm_g', 'sinks', 'ffn_pre_g', 'ffn_post_g', 'ffn_conv_b')}
    sq, dx = _local_step(x[0], loss_target[0], positions[0], P, weights_of, ffn_grads_done, grads_done)
    loss = 0.5 / D * lax.psum(jnp.sum(sq), ("x", "y", "c"))

    late = {'ffn_w_in': (1, depth - 1), 'ffn_w_out': (1, depth - 1), 'pool_w': (1, n_a - 1), 'w_q': None, 'w_o': None,
            'w_kv': None}
    names = list(big)
    in_flight = scattering[0]['handle']['token']
    whole = _share_halves([big[n] for n in names], [BIG[n][2] for n in names], [late[n] for n in names], in_flight,
                          "grad_share_halves_late")
    upd = {}
    for n, g in zip(names, whole):
        part = None if late[n] is None else late[n] + (w[n].shape[0],)
        upd[n] = _adamw(w[n], g, mom[n], var[n], "adamw_late_" + n, part=part)

    summed = _sum8(small['gathered'] + in_flight[0, 0], "small_grad_sum")
    grads = dict(zip(SMALL, _unpack_small(summed, small['offs'])))
    wd = ffn_conv_w.shape[2]
    grads['ffn_conv_w'] = lax.dynamic_slice_in_dim(grads['ffn_conv_w'], q_chip * wd, wd, axis=2)
    ps = pool_scale.shape[1]
    grads['pool_scale'] = lax.dynamic_slice_in_dim(grads['pool_scale'], q_chip * ps, ps, axis=1)

    delta, new_m, new_v = {}, {}, {}
    for n in SMALL:
        upd[n] = _adamw(w[n], grads[n], mom[n], var[n], "adamw_" + n)

    first, shapes0 = {}, {n: (1,) + w[n].shape[1:] for n in ('ffn_w_in', 'ffn_w_out', 'pool_w')}
    done = jnp.stack([upd[n][1][(-1,) * upd[n][1].ndim] for n in upd])
    _reduce_finish(scattering.pop(0), done, qc, first, shapes0)
    names0 = list(first)
    whole0 = _share_halves([first[n] for n in names0], [BIG[n][2] for n in names0], [None] * len(names0), done,
                           "grad_share_halves_l0")
    for n, g in zip(names0, whole0):
        upd[n] = _adamw(w[n], g, mom[n], var[n], "adamw_l0_" + n, part=(0, 1, w[n].shape[0]), into=upd[n])
    for n in upd:
        grads[n], delta[n], new_m[n], new_v[n] = upd[n]

    return (loss, dx[None], *[grads[n] for n in WEIGHTS], *[delta[n] for n in WEIGHTS],
            *[new_m[n] for n in WEIGHTS], *[new_v[n] for n in WEIGHTS])
```

```python
import functools
import math

import jax
import jax.numpy as jnp
from jax import lax
from jax.experimental import pallas as pl
from jax.experimental.pallas import tpu as pltpu

F32 = jnp.float32
BF16 = jnp.bfloat16
MESH = pl.DeviceIdType.MESH
ANY = pl.BlockSpec(memory_space=pl.ANY)
HBM = pl.BlockSpec(memory_space=pltpu.HBM)
VMEM = pl.BlockSpec(memory_space=pltpu.VMEM)
SEM = pl.BlockSpec(memory_space=pltpu.SEMAPHORE)
EFFECT = pltpu.SideEffectType.DATAFLOW_SIDE_EFFECTING

HEAD_DIM = 64
N_KV_HEADS = 4
KV_DIM = 2 * N_KV_HEADS * HEAD_DIM
WINDOW = 128
BLOCK = 128
POOL_WINDOWS = (2, 4, 8, 16)
POOL_HALO = 16
CONV_HALO = 8
ROPE_THETA = 10000.0
ATTN_SCALE = 1.0 / math.sqrt(HEAD_DIM)
NEG_INF = -1e30
RMS_EPS = 1e-6
ADAM_LR, ADAM_B1, ADAM_B2, ADAM_EPS, ADAM_WD, ADAM_STEP = 0.001, 0.9, 0.999, 1e-08, 0.01, 10
N_SHARDS = 4
LANES = 128
VMEM_LIMIT_BYTES = 48 << 20
ROW_TILE = 512

WEIGHTS = ['mix_pre_g', 'mix_post_g', 'pool_w', 'pool_scale', 'kv_norm_g', 'w_kv', 'w_q', 'w_o', 'sinks',
           'ffn_pre_g', 'ffn_post_g', 'ffn_w_in', 'ffn_conv_w', 'ffn_conv_b', 'ffn_w_out']


def _call(body, *, name, out_shape, grid=None, in_specs=None, out_specs=None, scratch_shapes=(), dims=None,
          grid_spec=None, aliases=None):
    params = pltpu.CompilerParams(dimension_semantics=dims, vmem_limit_bytes=VMEM_LIMIT_BYTES)
    kw = {} if aliases is None else dict(input_output_aliases=aliases)
    if grid_spec is not None:
        return pl.pallas_call(body, name=name, out_shape=out_shape, grid_spec=grid_spec, compiler_params=params, **kw)
    if grid is not None:
        kw['grid'] = grid
    return pl.pallas_call(body, name=name, out_shape=out_shape, in_specs=in_specs, out_specs=out_specs,
                          scratch_shapes=list(scratch_shapes), compiler_params=params, **kw)


def _tile(n, pref, mult=8):
    if n <= pref:
        return n
    for t in range(pref, 0, -1):
        if n % t == 0 and t % mult == 0:
            return t
    raise ValueError((n, pref, mult))


def _sds(shape, dtype):
    return jax.ShapeDtypeStruct(tuple(shape), dtype)


def _perm4(j):
    return (j % 2) * 2 + j // 2


def _matmul(a, b, mode, out_dtype, name, tm, tn, tk, b_blocks=False, out_perm=False):
    a2 = a.shape
    b2 = (b.shape[1], 4 * b.shape[2]) if b_blocks else b.shape
    if mode == 'nn':
        (M, K), (K2, N) = a2, b2
    elif mode == 'nt':
        (M, K), (N, K2) = a2, b2
    else:
        (K, M), (K2, N) = a2, b2
    assert K == K2, (name, a.shape, b.shape)
    tm, tn, tk = _tile(M, tm), _tile(N, tn, LANES), _tile(K, tk, LANES if mode != 'tn' else 16)
    assert M % tm == 0 and N % tn == 0 and K % tk == 0
    nk = K // tk
    grid = (N // tn, M // tm, nk)

    if mode == 'nn':
        a_spec = pl.BlockSpec((tm, tk), lambda j, i, k: (i, k))
        if b_blocks:
            assert tn == b.shape[2]
            b_spec = pl.BlockSpec((None, tk, tn), lambda j, i, k: (_perm4(j), k, 0))
        else:
            b_spec = pl.BlockSpec((tk, tn), lambda j, i, k: (k, j))
        dn = (((1,), (0,)), ((), ()))
    elif mode == 'nt':
        a_spec = pl.BlockSpec((tm, tk), lambda j, i, k: (i, k))
        if b_blocks:
            assert tk == b.shape[2]
            b_spec = pl.BlockSpec((None, tn, tk), lambda j, i, k: (_perm4(k), j, 0))
        else:
            b_spec = pl.BlockSpec((tn, tk), lambda j, i, k: (j, k))
        dn = (((1,), (1,)), ((), ()))
    else:
        a_spec = pl.BlockSpec((tk, tm), lambda j, i, k: (k, i))
        b_spec = pl.BlockSpec((tk, tn), lambda j, i, k: (k, j))
        dn = (((0,), (0,)), ((), ()))
    po = _perm4 if out_perm else (lambda j: j)
    o_spec = pl.BlockSpec((tm, tn), lambda j, i, k: (i, po(j)))

    def body(a_ref, b_ref, o_ref, *acc):
        prod = lax.dot_general(a_ref[...].astype(BF16), b_ref[...].astype(BF16), dn, preferred_element_type=F32)
        if nk == 1:
            o_ref[...] = prod.astype(o_ref.dtype)
        else:
            k = pl.program_id(2)

            @pl.when(k == 0)
            def _():
                acc[0][...] = prod

            @pl.when(k > 0)
            def _():
                acc[0][...] += prod

            @pl.when(k == nk - 1)
            def _():
                o_ref[...] = acc[0][...].astype(o_ref.dtype)

    scratch = [] if nk == 1 else [pltpu.VMEM((tm, tn), F32)]
    return _call(body, name=name, out_shape=_sds((M, N), out_dtype), grid=grid, in_specs=[a_spec, b_spec],
                 out_specs=o_spec, scratch_shapes=scratch, dims=("parallel", "parallel", "arbitrary"))(a, b)


def _rstd(x):
    return lax.rsqrt(jnp.mean(x * x, axis=-1, keepdims=True) + RMS_EPS)


def _rms_fwd(x, g, out_dtype, name):
    S, D = x.shape
    tr = _tile(S, ROW_TILE)

    def body(x_ref, g_ref, o_ref):
        xv = x_ref[...]
        o_ref[...] = (xv * _rstd(xv) * g_ref[...]).astype(o_ref.dtype)

    row = pl.BlockSpec((tr, D), lambda i: (i, 0))
    vec = pl.BlockSpec((1, D), lambda i: (0, 0))
    return _call(body, name=name, out_shape=_sds((S, D), out_dtype), grid=(S // tr,), in_specs=[row, vec],
                 out_specs=row, dims=("parallel",))(x, g)


def _res_rms_fwd(x, f, g, name, norm_gains=()):
    S, D = x.shape
    tr = _tile(S, ROW_TILE)
    k = len(norm_gains)

    def body(x_ref, f_ref, g_ref, *rest):
        fv = f_ref[...]
        xn = x_ref[...] + fv * _rstd(fv) * g_ref[...]
        rest[k][...] = xn
        if k:
            xh = xn * _rstd(xn)
            for g2_ref, h_ref in zip(rest[:k], rest[k + 1:]):
                h_ref[...] = (xh * g2_ref[...]).astype(h_ref.dtype)

    row = pl.BlockSpec((tr, D), lambda i: (i, 0))
    vec = pl.BlockSpec((1, D), lambda i: (0, 0))
    outs = _call(body, name=name, out_shape=(_sds((S, D), F32),) + (_sds((S, D), BF16),) * k, grid=(S // tr,),
                 in_specs=[row, row, vec] + [vec] * k, out_specs=(row,) * (1 + k), dims=("parallel",))(x, f, g, *norm_gains)
    return outs[0], list(outs[1:])


def _res_rms_loss(x, f, g, target, name):
    S, D = x.shape
    tr = _tile(S, ROW_TILE)

    def body(x_ref, f_ref, g_ref, t_ref, dy_ref, acc_ref):
        fv = f_ref[...]
        e = x_ref[...] + fv * _rstd(fv) * g_ref[...] - t_ref[...]
        dy_ref[...] = e * (1.0 / D)

        @pl.when(pl.program_id(0) == 0)
        def _():
            acc_ref[...] = jnp.zeros_like(acc_ref)

        acc_ref[...] += jnp.sum(e * e, axis=0, keepdims=True)

    row = pl.BlockSpec((tr, D), lambda i: (i, 0))
    vec = pl.BlockSpec((1, D), lambda i: (0, 0))
    return _call(body, name=name, out_shape=(_sds((S, D), F32), _sds((1, D), F32)), grid=(S // tr,),
                 in_specs=[row, row, vec, row], out_specs=(row, vec), dims=("arbitrary",))(x, f, g, target)


def _rms_bwd_math(xin, g, dy):
    r = _rstd(xin)
    xh = xin * r
    gy = dy * g
    dx = r * (gy - xh * jnp.mean(gy * xh, axis=-1, keepdims=True))
    return dx, dy * xh


def _rms_bwd(xin, g, dy, res, name, then=None):
    S, D = xin.shape
    tr = _tile(S, ROW_TILE)
    has_res = res is not None

    def body(*refs):
        x_ref, g_ref, dy_ref = refs[:3]
        ins = list(refs[3:])
        res_ref = ins.pop(0) if has_res else None
        x2_ref, g2_ref = (ins.pop(0), ins.pop(0)) if then else (None, None)
        dx_ref, dg_ref = ins[:2]
        dx, dgr = _rms_bwd_math(x_ref[...], g_ref[...], dy_ref[...])
        if has_res:
            dx = dx + res_ref[...]
        dx_ref[...] = dx
        i = pl.program_id(0)

        @pl.when(i == 0)
        def _():
            dg_ref[...] = jnp.zeros_like(dg_ref)
            if then:
                ins[3][...] = jnp.zeros_like(ins[3])

        dg_ref[...] += jnp.sum(dgr, axis=0, keepdims=True)
        if then:
            d2, dgr2 = _rms_bwd_math(x2_ref[...], g2_ref[...], dx)
            ins[2][...] = d2
            ins[3][...] += jnp.sum(dgr2, axis=0, keepdims=True)

    row = pl.BlockSpec((tr, D), lambda i: (i, 0))
    vec = pl.BlockSpec((1, D), lambda i: (0, 0))
    operands = [xin, g, dy] + ([res] if has_res else []) + (list(then) if then else [])
    in_specs = [row, vec, row] + ([row] if has_res else []) + ([row, vec] if then else [])
    n_out = 2 if then else 1
    return _call(body, name=name, out_shape=(_sds((S, D), F32), _sds((1, D), F32)) * n_out, grid=(S // tr,),
                 in_specs=in_specs, out_specs=(row, vec) * n_out, dims=("arbitrary",))(*operands)


def _pool_counts(t0, rows):
    return t0 + lax.broadcasted_iota(jnp.int32, (rows, 1), 0)


def _pool_fwd(x, g, name):
    S, D = x.shape
    gc = D // len(POOL_WINDOWS)
    tp = _tile(S, ROW_TILE)

    def body(x_ref, g_ref, d_ref, ext_ref):
        i = pl.program_id(0)

        @pl.when(i == 0)
        def _():
            ext_ref[pl.ds(0, POOL_HALO), :] = jnp.zeros((POOL_HALO, D), F32)

        xv = x_ref[...]
        ext_ref[pl.ds(POOL_HALO, tp), :] = xv * _rstd(xv) * g_ref[...]
        t = _pool_counts(i * tp, tp)
        for gi, w in enumerate(POOL_WINDOWS):
            cols = slice(gi * gc, (gi + 1) * gc)
            s = ext_ref[:, cols]
            h = s[POOL_HALO:]
            sh = 1
            while sh < w:
                s = s + pltpu.roll(s, sh, 0)
                sh *= 2
            cnt = jnp.minimum(t + 1, w).astype(F32)
            d_ref[:, cols] = (s[POOL_HALO:] / cnt - h).astype(d_ref.dtype)
        ext_ref[pl.ds(0, POOL_HALO), :] = ext_ref[pl.ds(tp, POOL_HALO), :]

    row = pl.BlockSpec((tp, D), lambda i: (i, 0))
    vec = pl.BlockSpec((1, D), lambda i: (0, 0))
    return _call(body, name=name, out_shape=_sds((S, D), BF16), grid=(S // tp,), in_specs=[row, vec],
                 out_specs=row, scratch_shapes=[pltpu.VMEM((tp + POOL_HALO, D), F32)], dims=("arbitrary",))(x, g)


def _pool_mm_fwd(d, wp, scale, x, gpost, gnext, name):
    S, D = x.shape
    ng = len(POOL_WINDOWS)
    gc = D // ng
    tp = _tile(S, ROW_TILE)

    def body(d_ref, w_ref, sc_ref, x_ref, g_ref, gn_ref, y_ref, o_ref, h_ref):
        for gi in range(ng):
            cols = slice(gi * gc, (gi + 1) * gc)
            y_ref[:, cols] = jnp.dot(d_ref[:, cols], w_ref[gi], preferred_element_type=F32)
        m = y_ref[...] * sc_ref[...]
        xn = x_ref[...] + m * _rstd(m) * g_ref[...]
        o_ref[...] = xn
        h_ref[...] = (xn * _rstd(xn) * gn_ref[...]).astype(h_ref.dtype)

    row = pl.BlockSpec((tp, D), lambda i: (i, 0))
    vec = pl.BlockSpec((1, D), lambda i: (0, 0))
    wsp = pl.BlockSpec((ng, gc, gc), lambda i: (0, 0, 0))
    return _call(body, name=name, out_shape=(_sds((S, D), F32), _sds((S, D), F32), _sds((S, D), BF16)), grid=(S // tp,),
                 in_specs=[row, wsp, vec, row, vec, vec], out_specs=(row, row, row),
                 dims=("parallel",))(d, wp, scale, x, gpost, gnext)


def _pool_mm_bwd(dx, y, d, wp, scale, gpost, name):
    S, D = dx.shape
    ng = len(POOL_WINDOWS)
    gc = D // ng
    tp = _tile(S, ROW_TILE)

    def body(dx_ref, y_ref, d_ref, w_ref, sc_ref, g_ref, dd_ref, dw_ref, dsc_ref, dg_ref):
        i = pl.program_id(0)

        @pl.when(i == 0)
        def _():
            dw_ref[...] = jnp.zeros_like(dw_ref)
            dsc_ref[...] = jnp.zeros_like(dsc_ref)
            dg_ref[...] = jnp.zeros_like(dg_ref)

        yv = y_ref[...]
        sc = sc_ref[...]
        dm, dgr = _rms_bwd_math(yv * sc, g_ref[...], dx_ref[...])
        dg_ref[...] += jnp.sum(dgr, axis=0, keepdims=True)
        dsc_ref[...] += jnp.sum(dm * yv, axis=0, keepdims=True)
        dyv = (dm * sc).astype(BF16)
        for gi in range(ng):
            cols = slice(gi * gc, (gi + 1) * gc)
            dyg = dyv[:, cols]
            dd_ref[:, cols] = lax.dot_general(dyg, w_ref[gi], (((1,), (1,)), ((), ())), preferred_element_type=F32)
            dw_ref[gi] += lax.dot_general(d_ref[:, cols], dyg, (((0,), (0,)), ((), ())), preferred_element_type=F32)

    row = pl.BlockSpec((tp, D), lambda i: (i, 0))
    vec = pl.BlockSpec((1, D), lambda i: (0, 0))
    wsp = pl.BlockSpec((ng, gc, gc), lambda i: (0, 0, 0))
    dwsp = pl.BlockSpec((ng, gc, gc), lambda i: (0, 0, 0))
    return _call(body, name=name,
                 out_shape=(_sds((S, D), F32), _sds((ng, gc, gc), F32), _sds((1, D), F32), _sds((1, D), F32)),
                 grid=(S // tp,), in_specs=[row, row, row, wsp, vec, vec], out_specs=(row, dwsp, vec, vec),
                 dims=("arbitrary",))(dx, y, d, wp, scale, gpost)


def _pool_bwd(dd, x, g, res, name):
    S, D = x.shape
    gc = D // len(POOL_WINDOWS)
    tp = _tile(S, ROW_TILE)
    nt = S // tp

    def body(dd_ref, x_ref, g_ref, res_ref, dx_ref, dg_ref, ext_ref, dh_ref):
        i = pl.program_id(0)

        @pl.when(i == 0)
        def _():
            ext_ref[pl.ds(tp, POOL_HALO), :] = jnp.zeros((POOL_HALO, D), F32)
            dg_ref[...] = jnp.zeros_like(dg_ref)

        t = _pool_counts((nt - 1 - i) * tp, tp)
        for gi, w in enumerate(POOL_WINDOWS):
            cols = slice(gi * gc, (gi + 1) * gc)
            ddv = dd_ref[:, cols]
            ext_ref[pl.ds(0, tp), cols] = ddv / jnp.minimum(t + 1, w).astype(F32)
            s = ext_ref[:, cols]
            sh = 1
            while sh < w:
                s = s + pltpu.roll(s, tp + POOL_HALO - sh, 0)
                sh *= 2
            dh_ref[:, cols] = s[:tp] - ddv
        ext_ref[pl.ds(tp, POOL_HALO), :] = ext_ref[pl.ds(0, POOL_HALO), :]
        dx, dgr = _rms_bwd_math(x_ref[...], g_ref[...], dh_ref[...])
        dx_ref[...] = dx + res_ref[...]
        dg_ref[...] += jnp.sum(dgr, axis=0, keepdims=True)

    row = pl.BlockSpec((tp, D), lambda i: (nt - 1 - i, 0))
    vec = pl.BlockSpec((1, D), lambda i: (0, 0))
    return _call(body, name=name, out_shape=(_sds((S, D), F32), _sds((1, D), F32)), grid=(nt,),
                 in_specs=[row, row, vec, row], out_specs=(row, vec),
                 scratch_shapes=[pltpu.VMEM((tp + POOL_HALO, D), F32), pltpu.VMEM((tp, D), F32)],
                 dims=("arbitrary",))(dd, x, g, res)


def _gelu(x):
    return 0.5 * x * (1.0 + jnp.tanh(0.7978845608028654 * (x + 0.044715 * x * x * x)))


def _gelu_grad(x):
    th = jnp.tanh(0.7978845608028654 * (x + 0.044715 * x * x * x))
    return 0.5 * (1.0 + th) + 0.5 * x * (1.0 - th * th) * 0.7978845608028654 * (1.0 + 3.0 * 0.044715 * x * x)


def _conv_taps(ext_ref, cols, tt):
    return tuple(ext_ref[pl.ds(CONV_HALO - k, tt), cols] for k in range(3))


def _conv_glu_fwd(u, cw, cb, name):
    S, F2 = u.shape
    wd = F2 // 4
    tt = _tile(S, ROW_TILE)

    def body(u_ref, cw_ref, cb_ref, a_ref, act_ref, ext_ref):
        it = pl.program_id(1)

        @pl.when(it == 0)
        def _():
            ext_ref[pl.ds(0, CONV_HALO), :] = jnp.zeros((CONV_HALO, 2 * wd), F32)

        ext_ref[pl.ds(CONV_HALO, tt), :] = u_ref[...]
        for cc in range(wd // LANES):
            act = []
            for half in range(2):
                cols = slice(half * wd + cc * LANES, half * wd + (cc + 1) * LANES)
                u0, u1, u2 = _conv_taps(ext_ref, cols, tt)
                act.append(cw_ref[2:3, cols] * u0 + cw_ref[1:2, cols] * u1 + cw_ref[0:1, cols] * u2 + cb_ref[:, cols])
                act_ref[:, cols] = act[half]
            a_ref[:, cc * LANES:(cc + 1) * LANES] = (_gelu(act[0]) * act[1]).astype(a_ref.dtype)
        ext_ref[pl.ds(0, CONV_HALO), :] = ext_ref[pl.ds(tt, CONV_HALO), :]

    wide = pl.BlockSpec((tt, 2 * wd), lambda h, t: (t, h))
    return _call(body, name=name, out_shape=(_sds((S, F2 // 2), BF16), _sds((S, F2), F32)), grid=(2, S // tt),
                 in_specs=[wide, pl.BlockSpec((8, 2 * wd), lambda h, t: (0, h)), pl.BlockSpec((1, 2 * wd), lambda h, t: (0, h))],
                 out_specs=(pl.BlockSpec((tt, wd), lambda h, t: (t, h)), wide),
                 scratch_shapes=[pltpu.VMEM((tt + CONV_HALO, 2 * wd), F32)], dims=("parallel", "arbitrary"))(u, cw, cb)


def _conv_glu_bwd(u, act, da, cw, name):
    S, F2 = u.shape
    wd = F2 // 4
    tt = _tile(S, ROW_TILE)
    nt = S // tt
    n = tt + CONV_HALO

    def body(u_ref, act_ref, da_ref, cw_ref, du_ref, acc_ref, carry_ref):
        it = pl.program_id(1)

        @pl.when(it == 0)
        def _():
            carry_ref[...] = jnp.zeros_like(carry_ref)
            acc_ref[...] = jnp.zeros_like(acc_ref)

        for cc in range(wd // LANES):
            gate = act_ref[:, cc * LANES:(cc + 1) * LANES]
            val = act_ref[:, wd + cc * LANES:wd + (cc + 1) * LANES]
            dav = da_ref[:, cc * LANES:(cc + 1) * LANES]
            dact = (dav * val * _gelu_grad(gate), dav * _gelu(gate))
            for half in range(2):
                cols = slice(half * wd + cc * LANES, half * wd + (cc + 1) * LANES)
                d0 = dact[half]
                e = jnp.concatenate([d0, carry_ref[:, cols]], axis=0)
                d1, d2 = pltpu.roll(e, n - 1, 0)[:tt], pltpu.roll(e, n - 2, 0)[:tt]
                du_ref[:, cols] = (cw_ref[2:3, cols] * d0 + cw_ref[1:2, cols] * d1 + cw_ref[0:1, cols] * d2).astype(du_ref.dtype)
                u0 = u_ref[:, cols]
                acc_ref[2:3, cols] += jnp.sum(d0 * u0, axis=0, keepdims=True)
                acc_ref[1:2, cols] += jnp.sum(d1 * u0, axis=0, keepdims=True)
                acc_ref[0:1, cols] += jnp.sum(d2 * u0, axis=0, keepdims=True)
                acc_ref[3:4, cols] += jnp.sum(d0, axis=0, keepdims=True)
                carry_ref[:, cols] = d0[:CONV_HALO]

    wide = pl.BlockSpec((tt, 2 * wd), lambda h, t: (nt - 1 - t, h))
    acc = pl.BlockSpec((8, 2 * wd), lambda h, t: (0, h))
    return _call(body, name=name, out_shape=(_sds((S, F2), BF16), _sds((8, F2), F32)), grid=(2, nt),
                 in_specs=[wide, wide, pl.BlockSpec((tt, wd), lambda h, t: (nt - 1 - t, h)), acc],
                 out_specs=(wide, acc), scratch_shapes=[pltpu.VMEM((CONV_HALO, 2 * wd), F32)],
                 dims=("parallel", "arbitrary"))(u, act, da, cw)


def _rope_chunk(x, cosv, sinv):
    lane = lax.broadcasted_iota(jnp.int32, x.shape, 1)
    partner = jnp.where(lane % HEAD_DIM < HEAD_DIM // 2, pltpu.roll(x, LANES - HEAD_DIM // 2, 1),
                        pltpu.roll(x, HEAD_DIM // 2, 1))
    return x * cosv + partner * sinv


def _rope(x, width, cos_t, sin_t, name):
    S = x.shape[0]
    tr = _tile(S, ROW_TILE)

    def body(x_ref, c_ref, s_ref, o_ref):
        for cc in range(width // LANES):
            cols = slice(cc * LANES, (cc + 1) * LANES)
            o_ref[:, cols] = _rope_chunk(x_ref[:, cols], c_ref[...], s_ref[...])

    row = pl.BlockSpec((tr, width), lambda i: (i, 0))
    tab = pl.BlockSpec((tr, LANES), lambda i: (i, 0))
    return _call(body, name=name, out_shape=_sds((S, width), F32), grid=(S // tr,), in_specs=[row, tab, tab],
                 out_specs=row, dims=("parallel",))(x, cos_t, sin_t)


def _attn_mask(n, reps):
    row = lax.broadcasted_iota(jnp.int32, (reps * BLOCK, 2 * BLOCK), 0) & (BLOCK - 1)
    col = lax.broadcasted_iota(jnp.int32, (reps * BLOCK, 2 * BLOCK), 1)
    rel = BLOCK + row - col
    return (rel >= 0) & (rel < WINDOW) & (n * BLOCK + col - BLOCK >= 0)


def _per_head_column(values, reps):
    grp = lax.broadcasted_iota(jnp.int32, (reps * BLOCK, 1), 0) // BLOCK
    col = jnp.zeros((reps * BLOCK, 1), F32)
    for g, v in enumerate(values):
        col = jnp.where(grp == g, v, col)
    return col


def _stack_heads(ref, heads, qpk, lane, scale):
    parts = []
    for h in heads:
        qc, qpar, _, kpar = _head_place(h, qpk)
        x = ref[:, qc * LANES:(qc + 1) * LANES]
        if scale != 1.0:
            x = x * scale
        if qpar != kpar:
            x = pltpu.roll(x, HEAD_DIM, 1)
        keep = (lane >= kpar * HEAD_DIM) & (lane < (kpar + 1) * HEAD_DIM)
        parts.append(jnp.where(keep, x, 0.0).astype(BF16))
    return jnp.concatenate(parts, axis=0)


def _unstack_heads(vals, ref, heads, qpk, lane, dtype):
    pair = None
    for g, h in enumerate(heads):
        qc, qpar, _, kpar = _head_place(h, qpk)
        v = vals[g * BLOCK:(g + 1) * BLOCK]
        if qpar != kpar:
            v = pltpu.roll(v, HEAD_DIM, 1)
        if qpar == 0:
            pair = v
        else:
            ref[:, qc * LANES:(qc + 1) * LANES] = jnp.where(lane < HEAD_DIM, pair, v).astype(dtype)


def _head_place(h, qpk):
    hk = h // qpk
    return h // 2, h % 2, hk // 2, hk % 2


def _attn_specs(S, D):
    nb = S // BLOCK
    kvw = KV_DIM // 2
    qsp = pl.BlockSpec((BLOCK, D), lambda n: (n, 0))
    prev = lambda n: jnp.maximum(n - 1, 0)
    kp = pl.BlockSpec((BLOCK, kvw), lambda n: (prev(n), 0))
    ko = pl.BlockSpec((BLOCK, kvw), lambda n: (n, 0))
    vp = pl.BlockSpec((BLOCK, kvw), lambda n: (prev(n), 1))
    vo = pl.BlockSpec((BLOCK, kvw), lambda n: (n, 1))
    stat = pl.BlockSpec((BLOCK, LANES), lambda n: (n, 0))
    smem = pl.BlockSpec(memory_space=pltpu.SMEM)
    return nb, kvw, qsp, kp, ko, vp, vo, stat, smem


def _attn_fwd(q, k, kv, sinks, name):
    S, D = q.shape
    nh = D // HEAD_DIM
    qpk = nh // N_KV_HEADS
    nb, kvw, qsp, kp, ko, vp, vo, stat, smem = _attn_specs(S, D)

    def body(q_ref, kp_ref, ko_ref, vp_ref, vo_ref, s_ref, o_ref, l_ref):
        n = pl.program_id(0)
        valid = _attn_mask(n, 2 * qpk)
        lane = lax.broadcasted_iota(jnp.int32, (BLOCK, LANES), 1)
        lacc = jnp.zeros((BLOCK, LANES), F32)
        for kc in range(N_KV_HEADS // 2):
            heads = list(range(2 * kc * qpk, 2 * (kc + 1) * qpk))
            kcols = slice(kc * LANES, (kc + 1) * LANES)
            k2 = jnp.concatenate([kp_ref[:, kcols], ko_ref[:, kcols]], axis=0).astype(BF16)
            v2 = jnp.concatenate([vp_ref[:, kcols], vo_ref[:, kcols]], axis=0).astype(BF16)
            qm = _stack_heads(q_ref, heads, qpk, lane, ATTN_SCALE)
            s = lax.dot_general(qm, k2, (((1,), (1,)), ((), ())), preferred_element_type=F32)
            s = jnp.where(valid, s, NEG_INF)
            sink = _per_head_column([s_ref[h] for h in heads], len(heads))
            m = jnp.maximum(jnp.max(s, axis=1, keepdims=True), sink)
            p = jnp.exp(s - m)
            den = jnp.sum(p, axis=1, keepdims=True) + jnp.exp(sink - m)
            of = jnp.dot(p.astype(BF16), v2, preferred_element_type=F32) / den
            lse = m + jnp.log(den)
            for g, h in enumerate(heads):
                lacc = jnp.where(lane == h, lse[g * BLOCK:(g + 1) * BLOCK], lacc)
            _unstack_heads(of, o_ref, heads, qpk, lane, o_ref.dtype)
        l_ref[...] = lacc

    return _call(body, name=name, out_shape=(_sds((S, D), BF16), _sds((S, LANES), F32)), grid=(nb,),
                 in_specs=[qsp, kp, ko, vp, vo, smem], out_specs=(qsp, stat), dims=("parallel",))(q, k, k, kv, kv, sinks)


def _attn_bwd(q, k, kv, do, lse, sinks, name):
    S, D = q.shape
    nh = D // HEAD_DIM
    qpk = nh // N_KV_HEADS
    nb, kvw, qsp, kp, ko, vp, vo, stat, smem = _attn_specs(S, D)

    def body(q_ref, kp_ref, ko_ref, vp_ref, vo_ref, do_ref, l_ref, s_ref,
             dq_ref, dkp_ref, dko_ref, dvp_ref, dvo_ref, ds_ref):
        n = pl.program_id(0)

        @pl.when(n == 0)
        def _():
            ds_ref[...] = jnp.zeros_like(ds_ref)

        valid = _attn_mask(n, 2 * qpk)
        lane = lax.broadcasted_iota(jnp.int32, (BLOCK, LANES), 1)
        lane8 = lax.broadcasted_iota(jnp.int32, (8, LANES), 1)
        lv = l_ref[...]
        dsink = jnp.zeros((8, LANES), F32)
        for kc in range(N_KV_HEADS // 2):
            heads = list(range(2 * kc * qpk, 2 * (kc + 1) * qpk))
            kcols = slice(kc * LANES, (kc + 1) * LANES)
            k2 = jnp.concatenate([kp_ref[:, kcols], ko_ref[:, kcols]], axis=0).astype(BF16)
            v2 = jnp.concatenate([vp_ref[:, kcols], vo_ref[:, kcols]], axis=0).astype(BF16)
            qm = _stack_heads(q_ref, heads, qpk, lane, ATTN_SCALE)
            gm = _stack_heads(do_ref, heads, qpk, lane, 1.0)
            s = lax.dot_general(qm, k2, (((1,), (1,)), ((), ())), preferred_element_type=F32)
            lh = jnp.concatenate([jnp.sum(jnp.where(lane == h, lv, 0.0), axis=1, keepdims=True) for h in heads], axis=0)
            p = jnp.where(valid, jnp.exp(s - lh), 0.0)
            dp = lax.dot_general(gm, v2, (((1,), (1,)), ((), ())), preferred_element_type=F32)
            delta = jnp.sum(p * dp, axis=1, keepdims=True)
            dsb = (p * (dp - delta)).astype(BF16)
            lost = jnp.exp(_per_head_column([s_ref[h] for h in heads], len(heads)) - lh) * delta
            for g, h in enumerate(heads):
                dsink = dsink - jnp.where(lane8 == h, jnp.sum(lost[g * BLOCK:(g + 1) * BLOCK]), 0.0)
            dqf = jnp.dot(dsb, k2, preferred_element_type=F32) * ATTN_SCALE
            _unstack_heads(dqf, dq_ref, heads, qpk, lane, F32)
            dk2 = lax.dot_general(dsb, qm, (((0,), (0,)), ((), ())), preferred_element_type=F32)
            dv2 = lax.dot_general(p.astype(BF16), gm, (((0,), (0,)), ((), ())), preferred_element_type=F32)
            dkp_ref[:, kcols] = dk2[:BLOCK]
            dko_ref[:, kcols] = dk2[BLOCK:]
            dvp_ref[:, kcols] = dv2[:BLOCK]
            dvo_ref[:, kcols] = dv2[BLOCK:]
        ds_ref[...] += dsink

    kvo = pl.BlockSpec((BLOCK, kvw), lambda n: (n, 0))
    acc = pl.BlockSpec((8, LANES), lambda n: (0, 0))
    part = _sds((S, kvw), F32)
    return _call(body, name=name, out_shape=(_sds((S, D), F32), part, part, part, part, _sds((8, LANES), F32)),
                 grid=(nb,), in_specs=[qsp, kp, ko, vp, vo, qsp, stat, smem],
                 out_specs=(qsp, kvo, kvo, kvo, kvo, acc), dims=("arbitrary",))(q, k, k, kv, kv, do, lse, sinks)


def _kv_grad(parts, cos_t, sin_neg_t, name):
    S, kvw = parts[0][0].shape
    nb = S // BLOCK
    flat = [a for p in parts for a in p]
    nl = len(parts)

    def body(*refs):
        c_ref, s_ref, o_ref = refs[4 * nl], refs[4 * nl + 1], refs[4 * nl + 2]
        n = pl.program_id(0)
        last = n == nb - 1
        dk = jnp.zeros((BLOCK, kvw), F32)
        dv = jnp.zeros((BLOCK, kvw), F32)
        for li in range(nl):
            kn, kown, vn, vown = refs[4 * li:4 * li + 4]
            dk = dk + kown[...] + jnp.where(last, 0.0, kn[...])
            dv = dv + vown[...] + jnp.where(last, 0.0, vn[...])
        for cc in range(kvw // LANES):
            cols = slice(cc * LANES, (cc + 1) * LANES)
            o_ref[:, cols] = _rope_chunk(dk[:, cols], c_ref[...], s_ref[...])
        o_ref[:, kvw:] = dv

    own = pl.BlockSpec((BLOCK, kvw), lambda n: (n, 0))
    nxt = pl.BlockSpec((BLOCK, kvw), lambda n: (jnp.minimum(n + 1, nb - 1), 0))
    tab = pl.BlockSpec((BLOCK, LANES), lambda n: (n, 0))
    return _call(body, name=name, out_shape=_sds((S, 2 * kvw), F32), grid=(nb,),
                 in_specs=[nxt, own, nxt, own] * nl + [tab, tab],
                 out_specs=pl.BlockSpec((BLOCK, 2 * kvw), lambda n: (n, 0)), dims=("parallel",))(*flat, cos_t, sin_neg_t)


def _sum_blocks(name, qc, grid, out_shape, out_block, out_imap, ins, out_dtype=F32, into=None):
    nin = len(ins)

    def body(qc_ref, *refs):
        acc = refs[0][...].astype(F32)
        for r in refs[1:nin]:
            acc = acc + r[...].astype(F32)
        refs[-1][...] = acc.astype(refs[-1].dtype)

    in_specs = [pl.BlockSpec(b, m) for _, b, m in ins]
    operands = [a for a, _, _ in ins]
    aliases = None
    if into is not None:
        in_specs.append(ANY)
        operands.append(into)
        aliases = {1 + nin: 0}
    gs = pltpu.PrefetchScalarGridSpec(num_scalar_prefetch=1, grid=grid, in_specs=in_specs,
                                      out_specs=pl.BlockSpec(out_block, out_imap))
    return _call(body, name=name, out_shape=_sds(out_shape, out_dtype), grid_spec=gs,
                 dims=("parallel",) * len(grid), aliases=aliases)(qc, *operands)


def _adamw(w, g, m, v, name, part=None, into=None):
    shape = w.shape
    C = shape[-1]
    R = w.size // C
    k, cnt, nparts = part if part is not None else (0, 1, 1)
    tr = _tile(R // nparts, max(8, (1 << 18) // C))
    first = k * (R // nparts // tr)
    rows = cnt * (R // nparts)

    def body(w_ref, g_ref, m_ref, v_ref, *outs):
        go_ref, d_ref, nm_ref, nv_ref = outs[-4:]
        gv = g_ref[...]
        nm = ADAM_B1 * m_ref[...] + (1.0 - ADAM_B1) * gv
        nv = ADAM_B2 * v_ref[...] + (1.0 - ADAM_B2) * (gv * gv)
        m_hat = nm / (1.0 - ADAM_B1 ** ADAM_STEP)
        v_hat = nv / (1.0 - ADAM_B2 ** ADAM_STEP)
        go_ref[...] = gv
        d_ref[...] = -ADAM_LR * (m_hat / (jnp.sqrt(v_hat) + ADAM_EPS) + ADAM_WD * w_ref[...])
        nm_ref[...] = nm
        nv_ref[...] = nv

    blk = pl.BlockSpec((tr, C), lambda i: (first + i, 0))
    flat = _sds((R, C), F32)
    operands = [a.reshape(-1, C) for a in (w, g, m, v)]
    in_specs, aliases = [blk] * 4, None
    if g.size != w.size:
        assert g.size == rows * C, (name, g.shape, shape, part)
        in_specs[1] = pl.BlockSpec((tr, C), lambda i: (i, 0))
    if into is not None:
        operands += [a.reshape(R, C) for a in into]
        in_specs = in_specs + [ANY] * 4
        aliases = {4 + i: i for i in range(4)}
    outs = _call(body, name=name, out_shape=(flat,) * 4, grid=(rows // tr,), in_specs=in_specs,
                 out_specs=(blk,) * 4, dims=("parallel",), aliases=aliases)(*operands)
    return tuple(o.reshape(shape) for o in outs)


def _place():
    x, y, c = lax.axis_index("x"), lax.axis_index("y"), lax.axis_index("c")
    chips = [(1 - x, y), (x, 1 - y), (1 - x, 1 - y)]
    return x, y, c, chips


def _at(ref, nd, dims):
    idx = [slice(None)] * nd
    for d, v in dims.items():
        idx[d] = pl.ds(v[0], v[1]) if isinstance(v, tuple) else v
    return ref.at[tuple(idx)]


def _remote(src, dst, send_sem, recv_sem, dev):
    return pltpu.make_async_remote_copy(src_ref=src, dst_ref=dst, send_sem=send_sem, recv_sem=recv_sem,
                                        device_id=dev, device_id_type=MESH)


def _split_call(body, name, out_shape, in_specs, out_specs, aliases):
    return pl.pallas_call(body, name=name, out_shape=out_shape, in_specs=in_specs, out_specs=out_specs,
                          input_output_aliases=aliases,
                          compiler_params=pltpu.CompilerParams(has_side_effects=EFFECT))


def _hbm(a):
    return pltpu.with_memory_space_constraint(a, pltpu.HBM)


def _copies_start(srcs, lands, after, name, plan, ncopies):
    n, m = len(srcs), len(lands)

    def body(*refs):
        src, land = refs[:n], refs[n:n + m]
        send_sems, recv_sems, token = refs[n + m + 1], refs[n + m + 2], refs[-1]
        x, y, c, chips = _place()
        for k, (s, d, dev) in enumerate(plan(x, y, c, chips, src, land)):
            _remote(s, d, send_sems.at[k], recv_sems.at[k], dev).start()
        token[...] = jnp.zeros_like(token)

    thru = tuple(pltpu.HBM(a.shape, a.dtype) for a in list(srcs) + list(lands))
    outs = _split_call(
        body, name,
        out_shape=(pltpu.SemaphoreType.DMA((ncopies,)), pltpu.SemaphoreType.DMA((ncopies,))) + thru + (_sds((8, LANES), F32),),
        in_specs=(HBM,) * (n + m) + (ANY,), out_specs=(SEM, SEM) + (HBM,) * (n + m) + (VMEM,),
        aliases={i: 2 + i for i in range(n + m)})(*[_hbm(a) for a in srcs], *[_hbm(a) for a in lands], after)
    return dict(send=outs[0], recv=outs[1], srcs=outs[2:2 + n], lands=outs[2 + n:2 + n + m], token=outs[-1])


def _copies_wait(handle, after, name, plan):
    srcs, lands = handle['srcs'], handle['lands']
    n, m = len(srcs), len(lands)

    def body(*refs):
        src, land = refs[:n], refs[n:n + m]
        send_sems, recv_sems = refs[n + m], refs[n + m + 1]
        x, y, c, chips = _place()
        for k, (s, d, dev) in enumerate(plan(x, y, c, chips, src, land)):
            cp = _remote(s, d, send_sems.at[k], recv_sems.at[k], dev)
            cp.wait_send()
            cp.wait_recv()

    thru = tuple(pltpu.HBM(a.shape, a.dtype) for a in list(srcs) + list(lands))
    outs = _split_call(body, name, out_shape=thru, in_specs=(HBM,) * (n + m) + (SEM, SEM, ANY),
                       out_specs=(HBM,) * (n + m), aliases={i: i for i in range(n + m)})(
        *srcs, *lands, handle['send'], handle['recv'], after)
    return outs[:n], outs[n:]


def _gather_plan(x, y, c, chips, src, land, arriving):
    q = 2 * x + y
    peers = [(ch[0], ch[1], c) for ch in chips] + [(x, y, 1 - c)]
    slots = [2 * ch[0] + ch[1] for ch in chips] + [q]
    return [(s, d.at[slots[j] if arriving else q], peers[j]) for s, d in zip(src, land) for j in range(4)]


def _gather_half_plan(x, y, c, chips, src, land, arriving):
    q = 2 * x + y
    out = []
    for s, d in zip(src, land):
        hs = s.shape[0] // 2
        rows = pl.ds(c * hs, hs)
        for ch in chips:
            out.append((s.at[rows], d.at[2 * ch[0] + ch[1] if arriving else q, rows], (ch[0], ch[1], c)))
        out.append((s, d.at[q], (x, y, 1 - c)))
    return out


def _exchange_plan(x, y, c, chips, src, land, arriving):
    out = []
    for d in land:
        hs = d.shape[1] // 2
        for ch in chips:
            slot = 2 * ch[0] + ch[1]
            out.append((d.at[slot, pl.ds(c * hs, hs)], d.at[slot, pl.ds(((1 - c) if arriving else c) * hs, hs)], (x, y, 1 - c)))
    return out


def _scatter_plan(shard_axes, shapes):
    def plan(x, y, c, chips, src, land):
        out = []
        for s, d, sd, shp in zip(src, land, shard_axes, shapes):
            ss = shp[sd] // N_SHARDS
            for j, ch in enumerate(chips):
                out.append((_at(s, len(shp), {sd: ((2 * ch[0] + ch[1]) * ss, ss)}), d.at[j], (ch[0], ch[1], c)))
        return out
    return plan


def _half_dims(shape, hd, c):
    hs = shape[hd] // 2
    return {hd: (c * hs, hs)}


def _swap_halves(grads, specs, name):
    n = len(grads)
    outs_shape = []
    for a, (sd, hd) in zip(grads, specs):
        shp = list(a.shape)
        shp[hd] //= 2
        outs_shape.append(_sds(shp, F32))

    def body(*refs):
        ins, outs = refs[:n], refs[n:2 * n]
        send_sems, recv_sems = refs[2 * n:]
        x, y, c, _ = _place()
        cps = []
        for ai, (sd, hd) in enumerate(specs):
            shp = grads[ai].shape
            cp = _remote(_at(ins[ai], len(shp), _half_dims(shp, hd, 1 - c)), outs[ai],
                         send_sems.at[ai], recv_sems.at[ai], (x, y, 1 - c))
            cp.start()
            cps.append(cp)
        for cp in cps:
            cp.wait()

    return _call(body, name=name, out_shape=tuple(outs_shape), in_specs=[ANY] * n, out_specs=tuple([ANY] * n),
                 scratch_shapes=[pltpu.SemaphoreType.DMA((n,)), pltpu.SemaphoreType.DMA((n,))])(*grads)


def _share_halves(arrs, half_axes, layers, after, name):
    n = len(arrs)

    def body(*refs):
        ins, outs = refs[:n], refs[n + 1:2 * n + 1]
        send_sems, recv_sems = refs[2 * n + 1:]
        x, y, c, _ = _place()

        def half(ref, ai, which):
            shp = arrs[ai].shape
            hs = shp[half_axes[ai]] // 2
            dims = {half_axes[ai]: (which * hs, hs)}
            if layers[ai] is not None:
                dims[0] = layers[ai]
            return _at(ref, len(shp), dims)

        sends = []
        for ai in range(n):
            cp = _remote(half(ins[ai], ai, c), half(outs[ai], ai, c), send_sems.at[ai], recv_sems.at[ai], (x, y, 1 - c))
            cp.start()
            sends.append(cp)
        for ai in range(n):
            land = half(outs[ai], ai, 1 - c)
            _remote(land, land, send_sems.at[ai], recv_sems.at[ai], (x, y, c)).wait_recv()
        for cp in sends:
            cp.wait_send()

    return _call(body, name=name, out_shape=tuple(_sds(a.shape, a.dtype) for a in arrs), in_specs=[ANY] * (n + 1),
                 out_specs=tuple([ANY] * n), aliases={i: i for i in range(n)},
                 scratch_shapes=[pltpu.SemaphoreType.DMA((n,)), pltpu.SemaphoreType.DMA((n,))])(*arrs, after)


def _gather_small(v, name):
    R, C = v.shape

    def body(x_ref, out_ref, send_sems, recv_sems, local_sem):
        x, y, c, chips = _place()
        me, sibling = (x, y, c), (x, y, 1 - c)

        def rows(px, py, pc):
            return out_ref.at[pl.ds((4 * px + 2 * py + pc) * R, R), :]

        def copy(k, block, to, src=None):
            return _remote(rows(*block) if src is None else src, rows(*block), send_sems.at[k], recv_sems.at[k], to)

        mine = pltpu.make_async_copy(x_ref, rows(*me), local_sem)
        mine.start()
        first = [copy(0, me, sibling, src=x_ref)]
        first += [copy(1 + j, me, (ch[0], ch[1], c), src=x_ref) for j, ch in enumerate(chips)]
        for cp in first:
            cp.start()
        passed = [copy(4 + j, (ch[0], ch[1], c), sibling) for j, ch in enumerate(chips)]
        for j, ch in enumerate(chips):
            copy(1 + j, (ch[0], ch[1], c), me).wait_recv()
            passed[j].start()
        copy(0, sibling, me).wait_recv()
        for j, ch in enumerate(chips):
            copy(4 + j, (ch[0], ch[1], 1 - c), me).wait_recv()
        for cp in first + passed:
            cp.wait_send()
        mine.wait()

    vm = pl.BlockSpec(memory_space=pltpu.VMEM)
    return _call(body, name=name, out_shape=_sds((8 * R, C), v.dtype), in_specs=[vm], out_specs=vm,
                 scratch_shapes=[pltpu.SemaphoreType.DMA((7,)), pltpu.SemaphoreType.DMA((7,)),
                                 pltpu.SemaphoreType.DMA])(v)


def _sum8(g, name):
    _, R, C = g.shape

    def body(g_ref, o_ref):
        acc = g_ref[0]
        for d in range(1, 8):
            acc = acc + g_ref[d]
        o_ref[...] = acc

    return _call(body, name=name, out_shape=_sds((R, C), F32), in_specs=[pl.BlockSpec(memory_space=pltpu.VMEM)],
                 out_specs=pl.BlockSpec(memory_space=pltpu.VMEM))(g)


def _rope_tables(positions):
    inv_freq = 1.0 / (ROPE_THETA ** (jnp.arange(0, HEAD_DIM, 2, dtype=F32) / HEAD_DIM))
    ang = positions.astype(F32)[:, None] * inv_freq
    cosv, sinv = jnp.cos(ang), jnp.sin(ang)
    return jnp.tile(cosv, (1, 4)), jnp.tile(jnp.concatenate([-sinv, sinv], axis=1), (1, 2))


def _blocked(a):
    parts = jnp.split(a, 4, axis=-1)
    return jnp.concatenate([parts[0], parts[2], parts[1], parts[3]], axis=-1)


def _arrived(Wl, name, after):
    if callable(Wl[name]):
        Wl[name] = Wl[name](after)
    return Wl[name]


def _local_step(x, target, positions, P, weights_of, ffn_grads_done, grads_done):
    S, D = x.shape
    depth = P['mix_pre_g'].shape[0]
    n_a = depth // 2
    cos_t, sin_t = _rope_tables(positions)
    row = lambda a, l: a[l][None]
    cb = [_blocked(P['ffn_conv_b'][l])[None] for l in range(depth)]
    sv, W = {}, {}
    kv = k_rot = None
    pre = []
    for l in range(depth):
        t = f"l{l}"
        W[l], zero = weights_of(l, x)
        sv[l, 'x_in'] = x
        if l < n_a:
            d = _pool_fwd(x, row(P['mix_pre_g'], l) + zero, "pool_fwd_" + t)
            y, x, h2 = _pool_mm_fwd(d, W[l]['pool_w'], W[l]['pool_scale'], x, row(P['mix_post_g'], l),
                                    row(P['ffn_pre_g'], l), "pool_mm_fwd_" + t)
            sv[l, 'd'], sv[l, 'y'] = d, y
        else:
            j = l - n_a
            if not pre:
                pre = [_rms_fwd(x, row(P['mix_pre_g'], l), BF16, "q_norm_" + t)]
                if l == n_a:
                    pre.append(_rms_fwd(x, P['kv_norm_g'][None], BF16, "kv_norm"))
            h = pre[0]
            if l == n_a:
                kv = _matmul(pre[1], W[l]['w_kv'], 'nn', F32, "kv_proj", 1024, 512, 1024)
                k_rot = _rope(kv, KV_DIM // 2, cos_t, sin_t, "k_rope")
                sv['hkv'] = pre[1]
            qraw = _matmul(h, W[l]['w_q'], 'nn', F32, "q_proj_" + t, 1024,1024, 1024)
            q = _rope(qraw, D, cos_t + zero, sin_t, "q_rope_" + t)
            o, lse = _attn_fwd(q, k_rot, kv, P['sinks'][j], "attn_fwd_" + t)
            m = _matmul(o, W[l]['w_o'], 'nn', F32, "o_proj_" + t, 1024,1024, 1024)
            x, (h2,) = _res_rms_fwd(x, m, row(P['mix_post_g'], l), "mix_post_" + t, [row(P['ffn_pre_g'], l)])
            sv[l, 'h'], sv[l, 'q'], sv[l, 'o'], sv[l, 'lse'], sv[l, 'm'] = h, q, o, lse, m
        sv[l, 'x1'] = x
        if 'pre_ffn' in W[l]:
            W[l].pop('pre_ffn')(x)
        w_in = _arrived(W[l], 'w_in', h2)
        u = _matmul(h2, w_in, 'nn', F32, "ffn_up_" + t, 1024,w_in.shape[2], 1024, b_blocks=True)
        a, sv[l, 'act'] = _conv_glu_fwd(u, W[l]['cw'], cb[l] + W[l].pop('tie', 0.0), "ffn_glu_" + t)
        f = _matmul(a, _arrived(W[l], 'w_out', a), 'nn', F32, "ffn_down_" + t, 1024,1024, 2816)
        sv[l, 'h2'], sv[l, 'u'], sv[l, 'a'], sv[l, 'f'] = h2, u, a, f
        if l + 1 == depth:
            dx, sq = _res_rms_loss(x, f, row(P['ffn_post_g'], l), target, "ffn_post_loss")
        else:
            gains = []
            if l + 1 >= n_a:
                gains = [row(P['mix_pre_g'], l + 1)] + ([P['kv_norm_g'][None]] if l + 1 == n_a else [])
            x, pre = _res_rms_fwd(x, f, row(P['ffn_post_g'], l), "ffn_post_" + t, gains)

    kv_parts = []
    zero = 0.0
    for l in reversed(range(depth)):
        t = f"l{l}"
        G = {}
        wd = W[l]['w_in'].shape[2]
        df, G['ffn_post_g'] = _rms_bwd(sv[l, 'f'], row(P['ffn_post_g'], l) + zero, dx, None, "ffn_post_bwd_" + t)
        da = _matmul(df, W[l]['w_out'], 'nt', F32, "ffn_down_dx_" + t, 1024,wd, 1024)
        G['ffn_w_out'] = _matmul(sv[l, 'a'], df, 'tn', F32, "ffn_down_dw_" + t, wd, 1024, 1024)
        du, acc = _conv_glu_bwd(sv[l, 'u'], sv[l, 'act'], da, W[l]['cw'], "ffn_glu_bwd_" + t)
        G['ffn_conv_w'] = _blocked(acc[0:3])
        G['ffn_conv_b'] = _blocked(acc[3:4])
        dh2 = _matmul(du, W[l]['w_in'], 'nt', F32, "ffn_up_dx_" + t, 1024,1024, wd, b_blocks=True)
        G['ffn_w_in'] = _matmul(sv[l, 'h2'], du, 'tn', F32, "ffn_up_dw_" + t, 1024, wd, 2048, out_perm=True)
        zero = ffn_grads_done(l, G, dh2)
        if l < n_a:
            dx, G['ffn_pre_g'] = _rms_bwd(sv[l, 'x1'], row(P['ffn_pre_g'], l) + zero, dh2, dx, "ffn_norm_bwd_" + t)
        else:
            dx, G['ffn_pre_g'], dm, G['mix_post_g'] = _rms_bwd(
                sv[l, 'x1'], row(P['ffn_pre_g'], l) + zero, dh2, dx, "ffn_norm_bwd_" + t,
                then=(sv[l, 'm'], row(P['mix_post_g'], l)))
        if l < n_a:
            dd, G['pool_w'], G['pool_scale'], G['mix_post_g'] = _pool_mm_bwd(
                dx, sv[l, 'y'], sv[l, 'd'], W[l]['pool_w'], W[l]['pool_scale'], row(P['mix_post_g'], l), "pool_mm_bwd_" + t)
            dx, G['mix_pre_g'] = _pool_bwd(dd, sv[l, 'x_in'], row(P['mix_pre_g'], l), dx, "pool_bwd_" + t)
        else:
            j = l - n_a
            do = _matmul(dm, W[l]['w_o'], 'nt', F32, "o_proj_dx_" + t, 1024,1024, 1024)
            G['w_o'] = _matmul(sv[l, 'o'], dm, 'tn', F32, "o_proj_dw_" + t, 1024, 1024, 1024)
            dq, dkn, dko, dvn, dvo, dsk = _attn_bwd(sv[l, 'q'], k_rot, kv, do, sv[l, 'lse'], P['sinks'][j], "attn_bwd_" + t)
            G['sinks'] = dsk[0:1]
            kv_parts.append((dkn, dko, dvn, dvo))
            dqraw = _rope(dq, D, cos_t, -sin_t, "q_rope_bwd_" + t)
            dh = _matmul(dqraw, W[l]['w_q'], 'nt', F32, "q_proj_dx_" + t, 1024,1024, 1024)
            G['w_q'] = _matmul(sv[l, 'h'], dqraw, 'tn', F32, "q_proj_dw_" + t, 1024, 1024, 1024)
            dx, G['mix_pre_g'] = _rms_bwd(sv[l, 'x_in'], row(P['mix_pre_g'], l), dh, dx, "q_norm_bwd_" + t)
            if l == n_a:
                dkv = _kv_grad(kv_parts, cos_t, -sin_t, "kv_grad")
                dhkv = _matmul(dkv, W[l]['w_kv'], 'nt', F32, "kv_proj_dx", 1024, 1024, 512)
                G['w_kv'] = _matmul(sv['hkv'], dkv, 'tn', F32, "kv_proj_dw", 1024, 512, 1024)
                dx, G['kv_norm_g'] = _rms_bwd(sv[l, 'x_in'], P['kv_norm_g'][None], dhkv, dx, "kv_norm_bwd")
        zero = grads_done(l, G, dx)
    return sq, dx


SMALL = ['mix_pre_g', 'mix_post_g', 'kv_norm_g', 'sinks', 'ffn_pre_g', 'ffn_post_g', 'ffn_conv_b', 'ffn_conv_w', 'pool_scale']
BIG = {'ffn_w_in': (1, 0, 1), 'ffn_w_out': (0, 1, 2), 'w_q': (0, 1, 2), 'w_o': (0, 1, 2), 'w_kv': (0, 1, 1),
       'pool_w': (1, 0, 1)}


def _swap_plan(half_axes, shapes):
    def plan(x, y, c, chips, src, land):
        return [(_at(s, len(shp), _half_dims(shp, hd, 1 - c)), d, (x, y, 1 - c))
                for s, d, hd, shp in zip(src, land, half_axes, shapes)]
    return plan


def _swap_start(pieces, after, name):
    arrs = [p[0] for p in pieces]
    plan = _swap_plan([p[2] for p in pieces], [a.shape for a in arrs])
    lands = []
    for a, p in zip(arrs, pieces):
        shp = list(a.shape)
        shp[p[2]] //= 2
        lands.append(lax.empty(tuple(shp), F32))
    return pieces, _copies_start(arrs, lands, after, name, plan, len(arrs)), plan


def _swap_wait(started, after, name):
    pieces, handle, plan = started
    arrs, theirs = _copies_wait(handle, after, name, plan)
    return [(a,) + p[1:] for a, p in zip(arrs, pieces)], list(theirs)


def _reduce_start(pieces, theirs, qc, after, tag):
    arrs = [p[0] for p in pieces]
    specs = [(p[1], p[2]) for p in pieces]
    sums = []
    for pi, (a, (sd, hd), r) in enumerate(zip(arrs, specs, theirs)):
        shp = r.shape
        nd = len(shp)
        if nd == 3:
            blk, grid = tuple(shp), (1,)
            mine = lambda i, s: (s[1], 0, 0)
            zero = lambda i, s: (0, 0, 0)
        elif hd == 0:
            tr = _tile(shp[0], max(16, (1 << 18) // shp[1]), 16)
            blk, grid = (tr, shp[1]), (shp[0] // tr,)
            nblk = shp[0] // tr
            mine = lambda i, s, nblk=nblk: (s[1] * nblk + i, 0)
            zero = lambda i, s: (i, 0)
        else:
            tr = _tile(shp[0], max(16, (1 << 18) // shp[1]), 16)
            blk, grid = (tr, shp[1]), (shp[0] // tr,)
            mine = lambda i, s: (i, s[1])
            zero = lambda i, s: (i, 0)
        sums.append(_sum_blocks(f"grad_chip_sum_{tag}_{pi}", qc, grid, shp, blk, zero, [(a, blk, mine), (r, blk, zero)],
                                out_dtype=BF16))

    lands = []
    for s_arr, (sd, hd) in zip(sums, specs):
        shp = list(s_arr.shape)
        shp[sd] //= N_SHARDS
        lands.append(lax.empty((3,) + tuple(shp), BF16))
    plan = _scatter_plan([sd for sd, _ in specs], [s.shape for s in sums])
    handle = _copies_start(sums, lands, after, "grad_scatter_start_" + tag, plan, 3 * len(sums))
    return dict(handle=handle, plan=plan, pieces=pieces, tag=tag)


def _reduce_finish(state, after, qc, outs, out_shapes):
    handle, pieces, tag = state['handle'], state['pieces'], state['tag']
    sums, recvd = _copies_wait(handle, after, "grad_scatter_wait_" + tag, state['plan'])
    for pi, ((a, sd, hd, oname, fixed, ohd), s_arr, r) in enumerate(zip(pieces, sums, recvd)):
        shp = r.shape[1:]
        nd = len(shp)
        lead = (fixed[0],) if fixed else ()
        none = (None,) if fixed else ()
        n_stack = out_shapes[oname][0]
        if nd == 3:
            blk, grid = tuple(shp), (1,)
            mine = lambda i, s: (0, s[0], 0)
            rk = [lambda i, s, k=k: (k, 0, 0, 0) for k in range(3)]
            oshape = (n_stack, 2 * shp[0]) + tuple(shp[1:])
            oblk = none + blk
            omap = lambda i, s, lead=lead: lead + (s[1], 0, 0)
        elif sd == 1:
            tr = _tile(shp[0], max(16, (1 << 18) // shp[1]), 16)
            blk, grid = (tr, shp[1]), (shp[0] // tr,)
            nblk = shp[0] // tr
            mine = lambda i, s: (i, s[0])
            rk = [lambda i, s, k=k: (k, i, 0) for k in range(3)]
            oshape = (n_stack, 2 * shp[0], shp[1])
            oblk = none + blk
            omap = lambda i, s, lead=lead, nblk=nblk: lead + (s[1] * nblk + i, 0)
        else:
            tr = _tile(shp[0], max(16, (1 << 18) // shp[1]), 16)
            blk, grid = (tr, shp[1]), (shp[0] // tr,)
            nblk = shp[0] // tr
            mine = lambda i, s, nblk=nblk: (s[0] * nblk + i, 0)
            rk = [lambda i, s, k=k: (k, i, 0) for k in range(3)]
            oshape = ((n_stack,) if fixed else ()) + (shp[0], 2 * shp[1])
            oblk = none + blk
            omap = lambda i, s, lead=lead: lead + (i, s[1])
        assert tuple(oshape) == tuple(out_shapes[oname]), (oname, oshape, out_shapes[oname])
        ins = [(s_arr, blk, mine)] + [(r, (None,) + blk, rk[k]) for k in range(3)]
        outs[oname] = _sum_blocks(f"grad_total_{tag}_{pi}", qc, grid, oshape, oblk, omap, ins, into=outs.get(oname))


def _pack_small(parts):
    rows, offs, r = [], [], 0
    for a in parts:
        flat = a.reshape(-1)
        nr = -(-flat.size // (8 * LANES)) * 8
        rows.append(jnp.pad(flat, (0, nr * LANES - flat.size)).reshape(nr, LANES))
        offs.append((r, nr, a.shape))
        r += nr
    return jnp.concatenate(rows, axis=0), offs


def _unpack_small(packed, offs):
    return [packed[r:r + nr].reshape(-1)[:math.prod(shape)].reshape(shape) for r, nr, shape in offs]


def kernel(x, positions, mix_pre_g, mix_post_g, pool_w, pool_scale, kv_norm_g, w_kv, w_q, w_o, sinks, ffn_pre_g, ffn_post_g, ffn_w_in, ffn_conv_w, ffn_conv_b, ffn_w_out, loss_target, m_mix_pre_g, m_mix_post_g, m_pool_w, m_pool_scale, m_kv_norm_g, m_w_kv, m_w_q, m_w_o, m_sinks, m_ffn_pre_g, m_ffn_post_g, m_ffn_w_in, m_ffn_conv_w, m_ffn_conv_b, m_ffn_w_out, v_mix_pre_g, v_mix_post_g, v_pool_w, v_pool_scale, v_kv_norm_g, v_w_kv, v_w_q, v_w_o, v_sinks, v_ffn_pre_g, v_ffn_post_g, v_ffn_w_in, v_ffn_conv_w, v_ffn_conv_b, v_ffn_w_out):
    w = dict(mix_pre_g=mix_pre_g, mix_post_g=mix_post_g, pool_w=pool_w, pool_scale=pool_scale, kv_norm_g=kv_norm_g,
             w_kv=w_kv, w_q=w_q, w_o=w_o, sinks=sinks, ffn_pre_g=ffn_pre_g, ffn_post_g=ffn_post_g, ffn_w_in=ffn_w_in,
             ffn_conv_w=ffn_conv_w, ffn_conv_b=ffn_conv_b, ffn_w_out=ffn_w_out)
    mom = dict(mix_pre_g=m_mix_pre_g, mix_post_g=m_mix_post_g, pool_w=m_pool_w, pool_scale=m_pool_scale,
               kv_norm_g=m_kv_norm_g, w_kv=m_w_kv, w_q=m_w_q, w_o=m_w_o, sinks=m_sinks, ffn_pre_g=m_ffn_pre_g,
               ffn_post_g=m_ffn_post_g, ffn_w_in=m_ffn_w_in, ffn_conv_w=m_ffn_conv_w, ffn_conv_b=m_ffn_conv_b,
               ffn_w_out=m_ffn_w_out)
    var = dict(mix_pre_g=v_mix_pre_g, mix_post_g=v_mix_post_g, pool_w=v_pool_w, pool_scale=v_pool_scale,
               kv_norm_g=v_kv_norm_g, w_kv=v_w_kv, w_q=v_w_q, w_o=v_w_o, sinks=v_sinks, ffn_pre_g=v_ffn_pre_g,
               ffn_post_g=v_ffn_post_g, ffn_w_in=v_ffn_w_in, ffn_conv_w=v_ffn_conv_w, ffn_conv_b=v_ffn_conv_b,
               ffn_w_out=v_ffn_w_out)
    depth = mix_pre_g.shape[0]
    q_chip = 2 * lax.axis_index("x") + lax.axis_index("y")
    qc = jnp.stack([q_chip, lax.axis_index("c")]).astype(jnp.int32)

    n_a = depth // 2
    D = x.shape[-1]
    gc = pool_w.shape[3]

    def layer_shards(l):
        first = []
        if l < n_a:
            first.append(('pool_w', pool_w[l].astype(BF16)))
        else:
            first += [('w_q', w_q[l - n_a].astype(BF16)), ('w_o', w_o[l - n_a].astype(BF16))]
            if l == n_a:
                first.append(('w_kv', w_kv.astype(BF16)))
        ffn = [('w_in', ffn_w_in[l].astype(BF16)), ('w_out', ffn_w_out[l].astype(BF16))]
        if l == 0:
            return [first + [('conv_w', ffn_conv_w), ('pool_scale', pool_scale)], ffn[:1], ffn[1:]]
        return [first + ffn]

    def start_group(items, after, name, plan):
        srcs = [a for _, a in items]
        lands = [lax.empty((N_SHARDS,) + a.shape, a.dtype) for a in srcs]
        handle = _copies_start(srcs, lands, after, name, functools.partial(plan, arriving=False), 4 * len(srcs))
        return [n for n, _ in items], handle, plan

    def wait_group(started, after, name):
        names, handle, plan = started
        _, lands = _copies_wait(handle, after, name, functools.partial(plan, arriving=True))
        return dict(zip(names, lands))

    def start_exchange(got, after, name):
        names = list(got)
        handle = _copies_start([], [got[n] for n in names], after, name,
                               functools.partial(_exchange_plan, arriving=False), 3 * len(names))
        return names, handle, _exchange_plan

    groups0 = layer_shards(0)
    small0 = start_group(groups0[0], x, "weight_gather_start_l0_small", _gather_plan)
    in0 = start_group(groups0[1], small0[1]['token'], "weight_gather_start_l0_in", _gather_half_plan)
    out0 = start_group(groups0[2], in0[1]['token'], "weight_gather_start_l0_out", _gather_half_plan)
    pending, shared, steps = {}, {}, {}

    def start_next(l, after):
        pending[l + 1] = start_group(layer_shards(l + 1)[0], after, f"weight_gather_start_l{l + 1}", _gather_plan)
        return pending[l + 1][1]['token'][0, 0]

    def pre_ffn0(after):
        landed = wait_group(in0, after, "weight_gather_wait_l0_in")
        steps['in'] = start_exchange(landed, after, "weight_exchange_start_l0_in")

    def w_in0(Wl, after):
        both = wait_group(steps['in'], after, "weight_exchange_wait_l0_in")
        Wl['tie'] = start_next(0, both['w_in'])
        return both['w_in']

    def w_out0(after):
        landed = wait_group(out0, after, "weight_gather_wait_l0_out")
        both = wait_group(start_exchange(landed, after, "weight_exchange_start_l0_out"), after,
                          "weight_exchange_wait_l0_out")
        return both['w_out'].reshape(-1, D)

    def weights_of(l, x_now):
        zero = 0.0
        if l == 0:
            got = wait_group(small0, out0[1]['token'], "weight_gather_wait_l0_small")
            shared['conv_w'] = got['conv_w']
            shared['pool_scale'] = got['pool_scale'].transpose(1, 0, 2).reshape(n_a, D)
        else:
            got = wait_group(pending.pop(l), x_now, f"weight_gather_wait_l{l}")
            if l + 1 < depth:
                zero = start_next(l, got['w_in'])
        taps = jnp.concatenate([shared['conv_w'][p, l] for p in (0, 2, 1, 3)], axis=-1)
        Wl = dict(cw=jnp.pad(taps, ((0, 5), (0, 0))))
        if l == 0:
            Wl['pre_ffn'], Wl['w_in'], Wl['w_out'] = pre_ffn0, functools.partial(w_in0, Wl), w_out0
        else:
            Wl['w_in'], Wl['w_out'] = got['w_in'], got['w_out'].reshape(-1, D)
        if l < n_a:
            Wl['pool_w'] = got['pool_w'].transpose(1, 0, 2, 3).reshape(-1, gc, gc)
            Wl['pool_scale'] = shared['pool_scale'][l][None]
        else:
            Wl['w_q'], Wl['w_o'] = got['w_q'].reshape(D, D), got['w_o'].reshape(D, D)
            if l == n_a:
                Wl['w_kv'] = got['w_kv'].reshape(D, -1)
        return Wl, zero

    big_shapes = {n: w[n].shape for n in BIG}
    big, G, scattering = {}, {}, {}

    swapping = {}

    def piece(n, g, l):
        lead = {0: (l if n.startswith('ffn') or n == 'pool_w' else l - n_a)} if len(big_shapes[n]) > 2 else {}
        return (g, BIG[n][0], BIG[n][1], n, lead, BIG[n][2])

    def ffn_grads_done(l, Gl, after):
        swapping[l] = _swap_start([piece(n, Gl[n], l) for n in ('ffn_w_in', 'ffn_w_out')], after, f"grad_swap_start_l{l}")
        return swapping[l][1]['token'][0, 0]

    small = {}

    def gather_small_grads():
        local = {
            'mix_pre_g': jnp.concatenate([G['mix_pre_g', l] for l in range(depth)], axis=0),
            'mix_post_g': jnp.concatenate([G['mix_post_g', l] for l in range(depth)], axis=0),
            'kv_norm_g': G['kv_norm_g', n_a][0],
            'sinks': jnp.concatenate([G['sinks', l][:, :sinks.shape[1]] for l in range(n_a, depth)], axis=0),
            'ffn_pre_g': jnp.concatenate([G['ffn_pre_g', l] for l in range(depth)], axis=0),
            'ffn_post_g': jnp.concatenate([G['ffn_post_g', l] for l in range(depth)], axis=0),
            'ffn_conv_b': jnp.concatenate([G['ffn_conv_b', l] for l in range(depth)], axis=0),
            'ffn_conv_w': jnp.stack([G['ffn_conv_w', l] for l in range(depth)], axis=0),
            'pool_scale': jnp.concatenate([G['pool_scale', l] for l in range(n_a)], axis=0),
        }
        packed, small['offs'] = _pack_small([local[n] for n in SMALL])
        small['gathered'] = _gather_small(packed, "small_grad_gather").reshape(8, packed.shape[0], LANES)

    def grads_done(l, Gl, dx_now):
        for n, g in Gl.items():
            if n not in BIG:
                G[n, l] = g
        after = dx_now
        if l == 0:
            gather_small_grads()
            after = small['gathered']
        pieces, theirs = _swap_wait(swapping.pop(l), after, f"grad_swap_wait_l{l}")
        rest = [piece(n, Gl[n], l) for n in BIG if n in Gl and not n.startswith('ffn')]
        theirs += list(_swap_halves([p[0] for p in rest], [(p[1], p[2]) for p in rest], f"grad_swap_halves_l{l}"))
        scattering[l] = _reduce_start(pieces + rest, theirs, qc, after, f"l{l}")
        if l + 1 in scattering:
            _reduce_finish(scattering.pop(l + 1), dx_now, qc, big, big_shapes)
        return scattering[l]['handle']['token'][0, 0]

    P = {n: w[n] for n in ('mix_pre_g', 'mix_post_g', 'kv_norm_g', 'sinks', 'ffn_pre_g', 'ffn_post_g', 'ffn_conv_b')}
    sq, dx = _local_step(x[0], loss_target[0], positions[0], P, weights_of, ffn_grads_done, grads_done)
    loss = 0.5 / D * lax.psum(jnp.sum(sq), ("x", "y", "c"))

    late = {'ffn_w_in': (1, depth - 1), 'ffn_w_out': (1, depth - 1), 'pool_w': (1, n_a - 1), 'w_q': None, 'w_o': None,
            'w_kv': None}
    names = list(big)
    in_flight = scattering[0]['handle']['token']
    whole = _share_halves([big[n] for n in names], [BIG[n][2] for n in names], [late[n] for n in names], in_flight,
                          "grad_share_halves_late")
    upd = {}
    for n, g in zip(names, whole):
        part = None if late[n] is None else late[n] + (w[n].shape[0],)
        upd[n] = _adamw(w[n], g, mom[n], var[n], "adamw_late_" + n, part=part)

    summed = _sum8(small['gathered'] + in_flight[0, 0], "small_grad_sum")
    grads = dict(zip(SMALL, _unpack_small(summed, small['offs'])))
    wd = ffn_conv_w.shape[2]
    grads['ffn_conv_w'] = lax.dynamic_slice_in_dim(grads['ffn_conv_w'], q_chip * wd, wd, axis=2)
    ps = pool_scale.shape[1]
    grads['pool_scale'] = lax.dynamic_slice_in_dim(grads['pool_scale'], q_chip * ps, ps, axis=1)

    delta, new_m, new_v = {}, {}, {}
    for n in SMALL:
        upd[n] = _adamw(w[n], grads[n], mom[n], var[n], "adamw_" + n)

    first, shapes0 = {}, {n: (1,) + w[n].shape[1:] for n in ('ffn_w_in', 'ffn_w_out', 'pool_w')}
    done = jnp.stack([upd[n][1][(-1,) * upd[n][1].ndim] for n in upd])
    _reduce_finish(scattering.pop(0), done, qc, first, shapes0)
    names0 = list(first)
    whole0 = _share_halves([first[n] for n in names0], [BIG[n][2] for n in names0], [None] * len(names0), done,
                           "grad_share_halves_l0")
    for n, g in zip(names0, whole0):
        upd[n] = _adamw(w[n], g, mom[n], var[n], "adamw_l0_" + n, part=(0, 1, w[n].shape[0]), into=upd[n])
    for n in upd:
        grads[n], delta[n], new_m[n], new_v[n] = upd[n]

    return (loss, dx[None], *[grads[n] for n in WEIGHTS], *[delta[n] for n in WEIGHTS],
            *[new_m[n] for n in WEIGHTS], *[new_v[n] for n in WEIGHTS])
```

```python
import functools
import math

import jax
import jax.numpy as jnp
from jax import lax
from jax.experimental import pallas as pl
from jax.experimental.pallas import tpu as pltpu

F32 = jnp.float32
BF16 = jnp.bfloat16
MESH = pl.DeviceIdType.MESH
ANY = pl.BlockSpec(memory_space=pl.ANY)
HBM = pl.BlockSpec(memory_space=pltpu.HBM)
VMEM = pl.BlockSpec(memory_space=pltpu.VMEM)
SEM = pl.BlockSpec(memory_space=pltpu.SEMAPHORE)
EFFECT = pltpu.SideEffectType.DATAFLOW_SIDE_EFFECTING

HEAD_DIM = 64
N_KV_HEADS = 4
KV_DIM = 2 * N_KV_HEADS * HEAD_DIM
WINDOW = 128
BLOCK = 128
POOL_WINDOWS = (2, 4, 8, 16)
POOL_HALO = 16
CONV_HALO = 8
ROPE_THETA = 10000.0
ATTN_SCALE = 1.0 / math.sqrt(HEAD_DIM)
NEG_INF = -1e30
RMS_EPS = 1e-6
ADAM_LR, ADAM_B1, ADAM_B2, ADAM_EPS, ADAM_WD, ADAM_STEP = 0.001, 0.9, 0.999, 1e-08, 0.01, 10
N_SHARDS = 4
LANES = 128
VMEM_LIMIT_BYTES = 48 << 20
ROW_TILE = 512
BLOCK_ELEMS = 1 << 19

WEIGHTS = ['mix_pre_g', 'mix_post_g', 'pool_w', 'pool_scale', 'kv_norm_g', 'w_kv', 'w_q', 'w_o', 'sinks',
           'ffn_pre_g', 'ffn_post_g', 'ffn_w_in', 'ffn_conv_w', 'ffn_conv_b', 'ffn_w_out']


def _call(body, *, name, out_shape, grid=None, in_specs=None, out_specs=None, scratch_shapes=(), dims=None,
          grid_spec=None, aliases=None):
    params = pltpu.CompilerParams(dimension_semantics=dims, vmem_limit_bytes=VMEM_LIMIT_BYTES)
    kw = {} if aliases is None else dict(input_output_aliases=aliases)
    if grid_spec is not None:
        return pl.pallas_call(body, name=name, out_shape=out_shape, grid_spec=grid_spec, compiler_params=params, **kw)
    if grid is not None:
        kw['grid'] = grid
    return pl.pallas_call(body, name=name, out_shape=out_shape, in_specs=in_specs, out_specs=out_specs,
                          scratch_shapes=list(scratch_shapes), compiler_params=params, **kw)


def _tile(n, pref, mult=8):
    if n <= pref:
        return n
    for t in range(pref, 0, -1):
        if n % t == 0 and t % mult == 0:
            return t
    raise ValueError((n, pref, mult))


def _sds(shape, dtype):
    return jax.ShapeDtypeStruct(tuple(shape), dtype)


def _perm4(j):
    return (j % 2) * 2 + j // 2


def _rope_wide(x, cosv, sinv):
    return jnp.concatenate([_rope_chunk(x[:, c * LANES:(c + 1) * LANES], cosv, sinv) for c in range(x.shape[1] // LANES)],
                           axis=1)


def _matmul(a, b, mode, out_dtype, name, tm, tn, tk, b_blocks=False, out_perm=False, rope=None):
    a2 = a.shape
    b2 = (b.shape[1], 4 * b.shape[2]) if b_blocks else b.shape
    if mode == 'nn':
        (M, K), (K2, N) = a2, b2
    elif mode == 'nt':
        (M, K), (N, K2) = a2, b2
    else:
        (K, M), (K2, N) = a2, b2
    assert K == K2, (name, a.shape, b.shape)
    tm, tn, tk = _tile(M, tm), _tile(N, tn, LANES), _tile(K, tk, LANES if mode != 'tn' else 16)
    assert M % tm == 0 and N % tn == 0 and K % tk == 0
    nk = K // tk
    grid = (N // tn, M // tm, nk)

    if mode == 'nn':
        a_spec = pl.BlockSpec((tm, tk), lambda j, i, k: (i, k))
        if b_blocks:
            assert tn == b.shape[2]
            b_spec = pl.BlockSpec((None, tk, tn), lambda j, i, k: (_perm4(j), k, 0))
        else:
            b_spec = pl.BlockSpec((tk, tn), lambda j, i, k: (k, j))
        dn = (((1,), (0,)), ((), ()))
    elif mode == 'nt':
        a_spec = pl.BlockSpec((tm, tk), lambda j, i, k: (i, k))
        if b_blocks:
            assert tk == b.shape[2]
            b_spec = pl.BlockSpec((None, tn, tk), lambda j, i, k: (_perm4(k), j, 0))
        else:
            b_spec = pl.BlockSpec((tn, tk), lambda j, i, k: (j, k))
        dn = (((1,), (1,)), ((), ()))
    else:
        a_spec = pl.BlockSpec((tk, tm), lambda j, i, k: (k, i))
        b_spec = pl.BlockSpec((tk, tn), lambda j, i, k: (k, j))
        dn = (((0,), (0,)), ((), ()))
    po = _perm4 if out_perm else (lambda j: j)
    o_spec = pl.BlockSpec((tm, tn), lambda j, i, k: (i, po(j)))
    where, tables, table_specs = None, [], []
    if rope is not None:
        where, tables = rope[0], list(rope[1:])
        assert (where == 'out' and mode == 'nn' and tn == N and nk == 1) or (where == 'a' and mode == 'nt' and nk == 1) \
            or (where == 'b' and mode == 'tn' and tn == N), (name, where, mode)
        rows, imap = (tk, lambda j, i, k: (k, 0)) if where == 'b' else (tm, lambda j, i, k: (i, 0))
        table_specs = [pl.BlockSpec((rows, LANES), imap)] * 2

    def body(a_ref, b_ref, *rest):
        o_ref, acc = rest[len(tables)], rest[len(tables) + 1:]
        av, bv = a_ref[...], b_ref[...]
        if where == 'a':
            av = _rope_wide(av, rest[0][...], rest[1][...])
        if where == 'b':
            bv = _rope_wide(bv, rest[0][...], rest[1][...])
        prod = lax.dot_general(av.astype(BF16), bv.astype(BF16), dn, preferred_element_type=F32)
        if where == 'out':
            prod = _rope_wide(prod, rest[0][...], rest[1][...])
        if nk == 1:
            o_ref[...] = prod.astype(o_ref.dtype)
        else:
            k = pl.program_id(2)

            @pl.when(k == 0)
            def _():
                acc[0][...] = prod

            @pl.when(k > 0)
            def _():
                acc[0][...] += prod

            @pl.when(k == nk - 1)
            def _():
                o_ref[...] = acc[0][...].astype(o_ref.dtype)

    scratch = [] if nk == 1 else [pltpu.VMEM((tm, tn), F32)]
    return _call(body, name=name, out_shape=_sds((M, N), out_dtype), grid=grid, in_specs=[a_spec, b_spec] + table_specs,
                 out_specs=o_spec, scratch_shapes=scratch, dims=("parallel", "parallel", "arbitrary"))(a, b, *tables)


def _rstd(x):
    return lax.rsqrt(jnp.mean(x * x, axis=-1, keepdims=True) + RMS_EPS)


def _rms_fwd(x, g, out_dtype, name):
    S, D = x.shape
    tr = _tile(S, ROW_TILE)

    def body(x_ref, g_ref, o_ref):
        xv = x_ref[...]
        o_ref[...] = (xv * _rstd(xv) * g_ref[...]).astype(o_ref.dtype)

    row = pl.BlockSpec((tr, D), lambda i: (i, 0))
    vec = pl.BlockSpec((1, D), lambda i: (0, 0))
    return _call(body, name=name, out_shape=_sds((S, D), out_dtype), grid=(S // tr,), in_specs=[row, vec],
                 out_specs=row, dims=("parallel",))(x, g)


def _res_rms_fwd(x, f, g, name, norm_gains=()):
    S, D = x.shape
    tr = _tile(S, ROW_TILE)
    k = len(norm_gains)

    def body(x_ref, f_ref, g_ref, *rest):
        fv = f_ref[...]
        xn = x_ref[...] + fv * _rstd(fv) * g_ref[...]
        rest[k][...] = xn
        if k:
            xh = xn * _rstd(xn)
            for g2_ref, h_ref in zip(rest[:k], rest[k + 1:]):
                h_ref[...] = (xh * g2_ref[...]).astype(h_ref.dtype)

    row = pl.BlockSpec((tr, D), lambda i: (i, 0))
    vec = pl.BlockSpec((1, D), lambda i: (0, 0))
    outs = _call(body, name=name, out_shape=(_sds((S, D), F32),) + (_sds((S, D), BF16),) * k, grid=(S // tr,),
                 in_specs=[row, row, vec] + [vec] * k, out_specs=(row,) * (1 + k), dims=("parallel",))(x, f, g, *norm_gains)
    return outs[0], list(outs[1:])


def _res_rms_loss(x, f, g, target, name):
    S, D = x.shape
    tr = _tile(S, ROW_TILE)

    def body(x_ref, f_ref, g_ref, t_ref, dy_ref, acc_ref):
        fv = f_ref[...]
        e = x_ref[...] + fv * _rstd(fv) * g_ref[...] - t_ref[...]
        dy_ref[...] = e * (1.0 / D)

        @pl.when(pl.program_id(0) == 0)
        def _():
            acc_ref[...] = jnp.zeros_like(acc_ref)

        acc_ref[...] += jnp.sum(e * e, axis=0, keepdims=True)

    row = pl.BlockSpec((tr, D), lambda i: (i, 0))
    vec = pl.BlockSpec((1, D), lambda i: (0, 0))
    return _call(body, name=name, out_shape=(_sds((S, D), F32), _sds((1, D), F32)), grid=(S // tr,),
                 in_specs=[row, row, vec, row], out_specs=(row, vec), dims=("arbitrary",))(x, f, g, target)


def _rms_bwd_math(xin, g, dy):
    r = _rstd(xin)
    xh = xin * r
    gy = dy * g
    dx = r * (gy - xh * jnp.mean(gy * xh, axis=-1, keepdims=True))
    return dx, dy * xh


def _rms_bwd(xin, g, dy, res, name, then=None):
    S, D = xin.shape
    tr = _tile(S, ROW_TILE)
    has_res = res is not None

    def body(*refs):
        x_ref, g_ref, dy_ref = refs[:3]
        ins = list(refs[3:])
        res_ref = ins.pop(0) if has_res else None
        x2_ref, g2_ref = (ins.pop(0), ins.pop(0)) if then else (None, None)
        dx_ref, dg_ref = ins[:2]
        dx, dgr = _rms_bwd_math(x_ref[...], g_ref[...], dy_ref[...])
        if has_res:
            dx = dx + res_ref[...]
        dx_ref[...] = dx
        i = pl.program_id(0)

        @pl.when(i == 0)
        def _():
            dg_ref[...] = jnp.zeros_like(dg_ref)
            if then:
                ins[3][...] = jnp.zeros_like(ins[3])

        dg_ref[...] += jnp.sum(dgr, axis=0, keepdims=True)
        if then:
            d2, dgr2 = _rms_bwd_math(x2_ref[...], g2_ref[...], dx)
            ins[2][...] = d2
            ins[3][...] += jnp.sum(dgr2, axis=0, keepdims=True)

    row = pl.BlockSpec((tr, D), lambda i: (i, 0))
    vec = pl.BlockSpec((1, D), lambda i: (0, 0))
    operands = [xin, g, dy] + ([res] if has_res else []) + (list(then) if then else [])
    in_specs = [row, vec, row] + ([row] if has_res else []) + ([row, vec] if then else [])
    n_out = 2 if then else 1
    return _call(body, name=name, out_shape=(_sds((S, D), F32), _sds((1, D), F32)) * n_out, grid=(S // tr,),
                 in_specs=in_specs, out_specs=(row, vec) * n_out, dims=("arbitrary",))(*operands)


def _pool_counts(t0, rows):
    return t0 + lax.broadcasted_iota(jnp.int32, (rows, 1), 0)


def _pool_fwd(x, g, name):
    S, D = x.shape
    gc = D // len(POOL_WINDOWS)
    tp = _tile(S, ROW_TILE)

    def body(x_ref, g_ref, d_ref, ext_ref):
        i = pl.program_id(0)

        @pl.when(i == 0)
        def _():
            ext_ref[pl.ds(0, POOL_HALO), :] = jnp.zeros((POOL_HALO, D), F32)

        xv = x_ref[...]
        ext_ref[pl.ds(POOL_HALO, tp), :] = xv * _rstd(xv) * g_ref[...]
        t = _pool_counts(i * tp, tp)
        for gi, w in enumerate(POOL_WINDOWS):
            cols = slice(gi * gc, (gi + 1) * gc)
            s = ext_ref[:, cols]
            h = s[POOL_HALO:]
            sh = 1
            while sh < w:
                s = s + pltpu.roll(s, sh, 0)
                sh *= 2
            cnt = jnp.minimum(t + 1, w).astype(F32)
            d_ref[:, cols] = (s[POOL_HALO:] / cnt - h).astype(d_ref.dtype)
        ext_ref[pl.ds(0, POOL_HALO), :] = ext_ref[pl.ds(tp, POOL_HALO), :]

    row = pl.BlockSpec((tp, D), lambda i: (i, 0))
    vec = pl.BlockSpec((1, D), lambda i: (0, 0))
    return _call(body, name=name, out_shape=_sds((S, D), BF16), grid=(S // tp,), in_specs=[row, vec],
                 out_specs=row, scratch_shapes=[pltpu.VMEM((tp + POOL_HALO, D), F32)], dims=("arbitrary",))(x, g)


def _pool_mm_fwd(d, wp, scale, x, gpost, gnext, name):
    S, D = x.shape
    ng = len(POOL_WINDOWS)
    gc = D // ng
    tp = _tile(S, ROW_TILE)

    def body(d_ref, w_ref, sc_ref, x_ref, g_ref, gn_ref, y_ref, o_ref, h_ref):
        for gi in range(ng):
            cols = slice(gi * gc, (gi + 1) * gc)
            y_ref[:, cols] = jnp.dot(d_ref[:, cols], w_ref[gi], preferred_element_type=F32)
        m = y_ref[...] * sc_ref[...]
        xn = x_ref[...] + m * _rstd(m) * g_ref[...]
        o_ref[...] = xn
        h_ref[...] = (xn * _rstd(xn) * gn_ref[...]).astype(h_ref.dtype)

    row = pl.BlockSpec((tp, D), lambda i: (i, 0))
    vec = pl.BlockSpec((1, D), lambda i: (0, 0))
    wsp = pl.BlockSpec((ng, gc, gc), lambda i: (0, 0, 0))
    return _call(body, name=name, out_shape=(_sds((S, D), F32), _sds((S, D), F32), _sds((S, D), BF16)), grid=(S // tp,),
                 in_specs=[row, wsp, vec, row, vec, vec], out_specs=(row, row, row),
                 dims=("parallel",))(d, wp, scale, x, gpost, gnext)


def _pool_mm_bwd(dx, y, d, wp, scale, gpost, name):
    S, D = dx.shape
    ng = len(POOL_WINDOWS)
    gc = D // ng
    tp = _tile(S, ROW_TILE)

    def body(dx_ref, y_ref, d_ref, w_ref, sc_ref, g_ref, dd_ref, dw_ref, dsc_ref, dg_ref):
        i = pl.program_id(0)

        @pl.when(i == 0)
        def _():
            dw_ref[...] = jnp.zeros_like(dw_ref)
            dsc_ref[...] = jnp.zeros_like(dsc_ref)
            dg_ref[...] = jnp.zeros_like(dg_ref)

        yv = y_ref[...]
        sc = sc_ref[...]
        dm, dgr = _rms_bwd_math(yv * sc, g_ref[...], dx_ref[...])
        dg_ref[...] += jnp.sum(dgr, axis=0, keepdims=True)
        dsc_ref[...] += jnp.sum(dm * yv, axis=0, keepdims=True)
        dyv = (dm * sc).astype(BF16)
        for gi in range(ng):
            cols = slice(gi * gc, (gi + 1) * gc)
            dyg = dyv[:, cols]
            dd_ref[:, cols] = lax.dot_general(dyg, w_ref[gi], (((1,), (1,)), ((), ())), preferred_element_type=F32)
            dw_ref[gi] += lax.dot_general(d_ref[:, cols], dyg, (((0,), (0,)), ((), ())), preferred_element_type=F32)

    row = pl.BlockSpec((tp, D), lambda i: (i, 0))
    vec = pl.BlockSpec((1, D), lambda i: (0, 0))
    wsp = pl.BlockSpec((ng, gc, gc), lambda i: (0, 0, 0))
    dwsp = pl.BlockSpec((ng, gc, gc), lambda i: (0, 0, 0))
    return _call(body, name=name,
                 out_shape=(_sds((S, D), F32), _sds((ng, gc, gc), F32), _sds((1, D), F32), _sds((1, D), F32)),
                 grid=(S // tp,), in_specs=[row, row, row, wsp, vec, vec], out_specs=(row, dwsp, vec, vec),
                 dims=("arbitrary",))(dx, y, d, wp, scale, gpost)


def _pool_bwd(dd, x, g, res, name):
    S, D = x.shape
    gc = D // len(POOL_WINDOWS)
    tp = _tile(S, ROW_TILE)
    nt = S // tp

    def body(dd_ref, x_ref, g_ref, res_ref, dx_ref, dg_ref, ext_ref, dh_ref):
        i = pl.program_id(0)

        @pl.when(i == 0)
        def _():
            ext_ref[pl.ds(tp, POOL_HALO), :] = jnp.zeros((POOL_HALO, D), F32)
            dg_ref[...] = jnp.zeros_like(dg_ref)

        t = _pool_counts((nt - 1 - i) * tp, tp)
        for gi, w in enumerate(POOL_WINDOWS):
            cols = slice(gi * gc, (gi + 1) * gc)
            ddv = dd_ref[:, cols]
            ext_ref[pl.ds(0, tp), cols] = ddv / jnp.minimum(t + 1, w).astype(F32)
            s = ext_ref[:, cols]
            sh = 1
            while sh < w:
                s = s + pltpu.roll(s, tp + POOL_HALO - sh, 0)
                sh *= 2
            dh_ref[:, cols] = s[:tp] - ddv
        ext_ref[pl.ds(tp, POOL_HALO), :] = ext_ref[pl.ds(0, POOL_HALO), :]
        dx, dgr = _rms_bwd_math(x_ref[...], g_ref[...], dh_ref[...])
        dx_ref[...] = dx + res_ref[...]
        dg_ref[...] += jnp.sum(dgr, axis=0, keepdims=True)

    row = pl.BlockSpec((tp, D), lambda i: (nt - 1 - i, 0))
    vec = pl.BlockSpec((1, D), lambda i: (0, 0))
    return _call(body, name=name, out_shape=(_sds((S, D), F32), _sds((1, D), F32)), grid=(nt,),
                 in_specs=[row, row, vec, row], out_specs=(row, vec),
                 scratch_shapes=[pltpu.VMEM((tp + POOL_HALO, D), F32), pltpu.VMEM((tp, D), F32)],
                 dims=("arbitrary",))(dd, x, g, res)


def _gelu(x):
    return 0.5 * x * (1.0 + jnp.tanh(0.7978845608028654 * (x + 0.044715 * x * x * x)))


def _gelu_grad(x):
    th = jnp.tanh(0.7978845608028654 * (x + 0.044715 * x * x * x))
    return 0.5 * (1.0 + th) + 0.5 * x * (1.0 - th * th) * 0.7978845608028654 * (1.0 + 3.0 * 0.044715 * x * x)


def _conv_taps(ext_ref, cols, tt):
    return tuple(ext_ref[pl.ds(CONV_HALO - k, tt), cols] for k in range(3))


def _conv_glu_fwd(u, cw, cb, name):
    S, F2 = u.shape
    wd = F2 // 4
    tt = _tile(S, ROW_TILE)

    def body(u_ref, cw_ref, cb_ref, a_ref, act_ref, ext_ref):
        it = pl.program_id(1)

        @pl.when(it == 0)
        def _():
            ext_ref[pl.ds(0, CONV_HALO), :] = jnp.zeros((CONV_HALO, 2 * wd), F32)

        ext_ref[pl.ds(CONV_HALO, tt), :] = u_ref[...]
        for cc in range(wd // LANES):
            act = []
            for half in range(2):
                cols = slice(half * wd + cc * LANES, half * wd + (cc + 1) * LANES)
                u0, u1, u2 = _conv_taps(ext_ref, cols, tt)
                act.append(cw_ref[2:3, cols] * u0 + cw_ref[1:2, cols] * u1 + cw_ref[0:1, cols] * u2 + cb_ref[:, cols])
                act_ref[:, cols] = act[half]
            a_ref[:, cc * LANES:(cc + 1) * LANES] = (_gelu(act[0]) * act[1]).astype(a_ref.dtype)
        ext_ref[pl.ds(0, CONV_HALO), :] = ext_ref[pl.ds(tt, CONV_HALO), :]

    wide = pl.BlockSpec((tt, 2 * wd), lambda h, t: (t, h))
    return _call(body, name=name, out_shape=(_sds((S, F2 // 2), BF16), _sds((S, F2), F32)), grid=(2, S // tt),
                 in_specs=[wide, pl.BlockSpec((8, 2 * wd), lambda h, t: (0, h)), pl.BlockSpec((1, 2 * wd), lambda h, t: (0, h))],
                 out_specs=(pl.BlockSpec((tt, wd), lambda h, t: (t, h)), wide),
                 scratch_shapes=[pltpu.VMEM((tt + CONV_HALO, 2 * wd), F32)], dims=("parallel", "arbitrary"))(u, cw, cb)


def _conv_glu_bwd(u, act, da, cw, name):
    S, F2 = u.shape
    wd = F2 // 4
    tt = _tile(S, ROW_TILE)
    nt = S // tt
    n = tt + CONV_HALO

    def body(u_ref, act_ref, da_ref, cw_ref, du_ref, acc_ref, carry_ref):
        it = pl.program_id(1)

        @pl.when(it == 0)
        def _():
            carry_ref[...] = jnp.zeros_like(carry_ref)
            acc_ref[...] = jnp.zeros_like(acc_ref)

        for cc in range(wd // LANES):
            gate = act_ref[:, cc * LANES:(cc + 1) * LANES]
            val = act_ref[:, wd + cc * LANES:wd + (cc + 1) * LANES]
            dav = da_ref[:, cc * LANES:(cc + 1) * LANES]
            dact = (dav * val * _gelu_grad(gate), dav * _gelu(gate))
            for half in range(2):
                cols = slice(half * wd + cc * LANES, half * wd + (cc + 1) * LANES)
                d0 = dact[half]
                e = jnp.concatenate([d0, carry_ref[:, cols]], axis=0)
                d1, d2 = pltpu.roll(e, n - 1, 0)[:tt], pltpu.roll(e, n - 2, 0)[:tt]
                du_ref[:, cols] = (cw_ref[2:3, cols] * d0 + cw_ref[1:2, cols] * d1 + cw_ref[0:1, cols] * d2).astype(du_ref.dtype)
                u0 = u_ref[:, cols]
                acc_ref[2:3, cols] += jnp.sum(d0 * u0, axis=0, keepdims=True)
                acc_ref[1:2, cols] += jnp.sum(d1 * u0, axis=0, keepdims=True)
                acc_ref[0:1, cols] += jnp.sum(d2 * u0, axis=0, keepdims=True)
                acc_ref[3:4, cols] += jnp.sum(d0, axis=0, keepdims=True)
                carry_ref[:, cols] = d0[:CONV_HALO]

    wide = pl.BlockSpec((tt, 2 * wd), lambda h, t: (nt - 1 - t, h))
    acc = pl.BlockSpec((8, 2 * wd), lambda h, t: (0, h))
    return _call(body, name=name, out_shape=(_sds((S, F2), BF16), _sds((8, F2), F32)), grid=(2, nt),
                 in_specs=[wide, wide, pl.BlockSpec((tt, wd), lambda h, t: (nt - 1 - t, h)), acc],
                 out_specs=(wide, acc), scratch_shapes=[pltpu.VMEM((CONV_HALO, 2 * wd), F32)],
                 dims=("parallel", "arbitrary"))(u, act, da, cw)


def _rope_chunk(x, cosv, sinv):
    lane = lax.broadcasted_iota(jnp.int32, x.shape, 1)
    partner = jnp.where(lane % HEAD_DIM < HEAD_DIM // 2, pltpu.roll(x, LANES - HEAD_DIM // 2, 1),
                        pltpu.roll(x, HEAD_DIM // 2, 1))
    return x * cosv + partner * sinv


def _rope(x, width, cos_t, sin_t, name):
    S = x.shape[0]
    tr = _tile(S, ROW_TILE)

    def body(x_ref, c_ref, s_ref, o_ref):
        for cc in range(width // LANES):
            cols = slice(cc * LANES, (cc + 1) * LANES)
            o_ref[:, cols] = _rope_chunk(x_ref[:, cols], c_ref[...], s_ref[...])

    row = pl.BlockSpec((tr, width), lambda i: (i, 0))
    tab = pl.BlockSpec((tr, LANES), lambda i: (i, 0))
    return _call(body, name=name, out_shape=_sds((S, width), F32), grid=(S // tr,), in_specs=[row, tab, tab],
                 out_specs=row, dims=("parallel",))(x, cos_t, sin_t)


def _attn_mask(n, reps):
    row = lax.broadcasted_iota(jnp.int32, (reps * BLOCK, 2 * BLOCK), 0) & (BLOCK - 1)
    col = lax.broadcasted_iota(jnp.int32, (reps * BLOCK, 2 * BLOCK), 1)
    rel = BLOCK + row - col
    return (rel >= 0) & (rel < WINDOW) & (n * BLOCK + col - BLOCK >= 0)


def _per_head_column(values, reps):
    grp = lax.broadcasted_iota(jnp.int32, (reps * BLOCK, 1), 0) // BLOCK
    col = jnp.zeros((reps * BLOCK, 1), F32)
    for g, v in enumerate(values):
        col = jnp.where(grp == g, v, col)
    return col


def _stack_heads(ref, heads, qpk, lane, scale):
    parts = []
    for h in heads:
        qc, qpar, _, kpar = _head_place(h, qpk)
        x = ref[:, qc * LANES:(qc + 1) * LANES]
        if scale != 1.0:
            x = x * scale
        if qpar != kpar:
            x = pltpu.roll(x, HEAD_DIM, 1)
        keep = (lane >= kpar * HEAD_DIM) & (lane < (kpar + 1) * HEAD_DIM)
        parts.append(jnp.where(keep, x, 0.0).astype(BF16))
    return jnp.concatenate(parts, axis=0)


def _unstack_heads(vals, ref, heads, qpk, lane, dtype):
    pair = None
    for g, h in enumerate(heads):
        qc, qpar, _, kpar = _head_place(h, qpk)
        v = vals[g * BLOCK:(g + 1) * BLOCK]
        if qpar != kpar:
            v = pltpu.roll(v, HEAD_DIM, 1)
        if qpar == 0:
            pair = v
        else:
            ref[:, qc * LANES:(qc + 1) * LANES] = jnp.where(lane < HEAD_DIM, pair, v).astype(dtype)


def _head_place(h, qpk):
    hk = h // qpk
    return h // 2, h % 2, hk // 2, hk % 2


def _attn_specs(S, D):
    nb = S // BLOCK
    kvw = KV_DIM // 2
    qsp = pl.BlockSpec((BLOCK, D), lambda n: (n, 0))
    prev = lambda n: jnp.maximum(n - 1, 0)
    kp = pl.BlockSpec((BLOCK, kvw), lambda n: (prev(n), 0))
    ko = pl.BlockSpec((BLOCK, kvw), lambda n: (n, 0))
    vp = pl.BlockSpec((BLOCK, kvw), lambda n: (prev(n), 1))
    vo = pl.BlockSpec((BLOCK, kvw), lambda n: (n, 1))
    stat = pl.BlockSpec((BLOCK, LANES), lambda n: (n, 0))
    smem = pl.BlockSpec(memory_space=pltpu.SMEM)
    return nb, kvw, qsp, kp, ko, vp, vo, stat, smem


def _attn_fwd(q, k, kv, sinks, name):
    S, D = q.shape
    nh = D // HEAD_DIM
    qpk = nh // N_KV_HEADS
    nb, kvw, qsp, kp, ko, vp, vo, stat, smem = _attn_specs(S, D)

    def body(q_ref, kp_ref, ko_ref, vp_ref, vo_ref, s_ref, o_ref, l_ref):
        n = pl.program_id(0)
        valid = _attn_mask(n, 2 * qpk)
        lane = lax.broadcasted_iota(jnp.int32, (BLOCK, LANES), 1)
        lacc = jnp.zeros((BLOCK, LANES), F32)
        for kc in range(N_KV_HEADS // 2):
            heads = list(range(2 * kc * qpk, 2 * (kc + 1) * qpk))
            kcols = slice(kc * LANES, (kc + 1) * LANES)
            k2 = jnp.concatenate([kp_ref[:, kcols], ko_ref[:, kcols]], axis=0).astype(BF16)
            v2 = jnp.concatenate([vp_ref[:, kcols], vo_ref[:, kcols]], axis=0).astype(BF16)
            qm = _stack_heads(q_ref, heads, qpk, lane, ATTN_SCALE)
            s = lax.dot_general(qm, k2, (((1,), (1,)), ((), ())), preferred_element_type=F32)
            s = jnp.where(valid, s, NEG_INF)
            sink = _per_head_column([s_ref[h] for h in heads], len(heads))
            m = jnp.maximum(jnp.max(s, axis=1, keepdims=True), sink)
            p = jnp.exp(s - m)
            den = jnp.sum(p, axis=1, keepdims=True) + jnp.exp(sink - m)
            of = jnp.dot(p.astype(BF16), v2, preferred_element_type=F32) / den
            lse = m + jnp.log(den)
            for g, h in enumerate(heads):
                lacc = jnp.where(lane == h, lse[g * BLOCK:(g + 1) * BLOCK], lacc)
            _unstack_heads(of, o_ref, heads, qpk, lane, o_ref.dtype)
        l_ref[...] = lacc

    return _call(body, name=name, out_shape=(_sds((S, D), BF16), _sds((S, LANES), F32)), grid=(nb,),
                 in_specs=[qsp, kp, ko, vp, vo, smem], out_specs=(qsp, stat), dims=("parallel",))(q, k, k, kv, kv, sinks)


def _attn_bwd(q, k, kv, do, lse, sinks, name):
    S, D = q.shape
    nh = D // HEAD_DIM
    qpk = nh // N_KV_HEADS
    nb, kvw, qsp, kp, ko, vp, vo, stat, smem = _attn_specs(S, D)

    def body(q_ref, kp_ref, ko_ref, vp_ref, vo_ref, do_ref, l_ref, s_ref,
             dq_ref, dkp_ref, dko_ref, dvp_ref, dvo_ref, ds_ref):
        n = pl.program_id(0)

        @pl.when(n == 0)
        def _():
            ds_ref[...] = jnp.zeros_like(ds_ref)

        valid = _attn_mask(n, 2 * qpk)
        lane = lax.broadcasted_iota(jnp.int32, (BLOCK, LANES), 1)
        lane8 = lax.broadcasted_iota(jnp.int32, (8, LANES), 1)
        lv = l_ref[...]
        dsink = jnp.zeros((8, LANES), F32)
        for kc in range(N_KV_HEADS // 2):
            heads = list(range(2 * kc * qpk, 2 * (kc + 1) * qpk))
            kcols = slice(kc * LANES, (kc + 1) * LANES)
            k2 = jnp.concatenate([kp_ref[:, kcols], ko_ref[:, kcols]], axis=0).astype(BF16)
            v2 = jnp.concatenate([vp_ref[:, kcols], vo_ref[:, kcols]], axis=0).astype(BF16)
            qm = _stack_heads(q_ref, heads, qpk, lane, ATTN_SCALE)
            gm = _stack_heads(do_ref, heads, qpk, lane, 1.0)
            s = lax.dot_general(qm, k2, (((1,), (1,)), ((), ())), preferred_element_type=F32)
            lh = jnp.concatenate([jnp.sum(jnp.where(lane == h, lv, 0.0), axis=1, keepdims=True) for h in heads], axis=0)
            p = jnp.where(valid, jnp.exp(s - lh), 0.0)
            dp = lax.dot_general(gm, v2, (((1,), (1,)), ((), ())), preferred_element_type=F32)
            delta = jnp.sum(p * dp, axis=1, keepdims=True)
            dsb = (p * (dp - delta)).astype(BF16)
            lost = jnp.exp(_per_head_column([s_ref[h] for h in heads], len(heads)) - lh) * delta
            for g, h in enumerate(heads):
                dsink = dsink - jnp.where(lane8 == h, jnp.sum(lost[g * BLOCK:(g + 1) * BLOCK]), 0.0)
            dqf = jnp.dot(dsb, k2, preferred_element_type=F32) * ATTN_SCALE
            _unstack_heads(dqf, dq_ref, heads, qpk, lane, F32)
            dk2 = lax.dot_general(dsb, qm, (((0,), (0,)), ((), ())), preferred_element_type=F32)
            dv2 = lax.dot_general(p.astype(BF16), gm, (((0,), (0,)), ((), ())), preferred_element_type=F32)
            dkp_ref[:, kcols] = dk2[:BLOCK]
            dko_ref[:, kcols] = dk2[BLOCK:]
            dvp_ref[:, kcols] = dv2[:BLOCK]
            dvo_ref[:, kcols] = dv2[BLOCK:]
        ds_ref[...] += dsink

    kvo = pl.BlockSpec((BLOCK, kvw), lambda n: (n, 0))
    acc = pl.BlockSpec((8, LANES), lambda n: (0, 0))
    part = _sds((S, kvw), F32)
    return _call(body, name=name, out_shape=(_sds((S, D), F32), part, part, part, part, _sds((8, LANES), F32)),
                 grid=(nb,), in_specs=[qsp, kp, ko, vp, vo, qsp, stat, smem],
                 out_specs=(qsp, kvo, kvo, kvo, kvo, acc), dims=("arbitrary",))(q, k, k, kv, kv, do, lse, sinks)


def _kv_grad(parts, cos_t, sin_neg_t, name):
    S, kvw = parts[0][0].shape
    nb = S // BLOCK
    flat = [a for p in parts for a in p]
    nl = len(parts)

    def body(*refs):
        c_ref, s_ref, o_ref = refs[4 * nl], refs[4 * nl + 1], refs[4 * nl + 2]
        n = pl.program_id(0)
        last = n == nb - 1
        dk = jnp.zeros((BLOCK, kvw), F32)
        dv = jnp.zeros((BLOCK, kvw), F32)
        for li in range(nl):
            kn, kown, vn, vown = refs[4 * li:4 * li + 4]
            dk = dk + kown[...] + jnp.where(last, 0.0, kn[...])
            dv = dv + vown[...] + jnp.where(last, 0.0, vn[...])
        for cc in range(kvw // LANES):
            cols = slice(cc * LANES, (cc + 1) * LANES)
            o_ref[:, cols] = _rope_chunk(dk[:, cols], c_ref[...], s_ref[...])
        o_ref[:, kvw:] = dv

    own = pl.BlockSpec((BLOCK, kvw), lambda n: (n, 0))
    nxt = pl.BlockSpec((BLOCK, kvw), lambda n: (jnp.minimum(n + 1, nb - 1), 0))
    tab = pl.BlockSpec((BLOCK, LANES), lambda n: (n, 0))
    return _call(body, name=name, out_shape=_sds((S, 2 * kvw), F32), grid=(nb,),
                 in_specs=[nxt, own, nxt, own] * nl + [tab, tab],
                 out_specs=pl.BlockSpec((BLOCK, 2 * kvw), lambda n: (n, 0)), dims=("parallel",))(*flat, cos_t, sin_neg_t)


def _sum_blocks(name, qc, grid, out_shape, out_block, out_imap, ins, out_dtype=F32, into=None):
    nin = len(ins)

    def body(qc_ref, *refs):
        acc = refs[0][...].astype(F32)
        for r in refs[1:nin]:
            acc = acc + r[...].astype(F32)
        refs[-1][...] = acc.astype(refs[-1].dtype)

    in_specs = [pl.BlockSpec(b, m) for _, b, m in ins]
    operands = [a for a, _, _ in ins]
    aliases = None
    if into is not None:
        in_specs.append(ANY)
        operands.append(into)
        aliases = {1 + nin: 0}
    gs = pltpu.PrefetchScalarGridSpec(num_scalar_prefetch=1, grid=grid, in_specs=in_specs,
                                      out_specs=pl.BlockSpec(out_block, out_imap))
    return _call(body, name=name, out_shape=_sds(out_shape, out_dtype), grid_spec=gs,
                 dims=("parallel",) * len(grid), aliases=aliases)(qc, *operands)


def _adamw(w, g, m, v, name, part=None, into=None):
    shape = w.shape
    C = shape[-1]
    R = w.size // C
    k, cnt, nparts = part if part is not None else (0, 1, 1)
    tr = _tile(R // nparts, max(8, BLOCK_ELEMS //C))
    first = k * (R // nparts // tr)
    rows = cnt * (R // nparts)

    def body(w_ref, g_ref, m_ref, v_ref, *outs):
        go_ref, d_ref, nm_ref, nv_ref = outs[-4:]
        gv = g_ref[...]
        nm = ADAM_B1 * m_ref[...] + (1.0 - ADAM_B1) * gv
        nv = ADAM_B2 * v_ref[...] + (1.0 - ADAM_B2) * (gv * gv)
        m_hat = nm / (1.0 - ADAM_B1 ** ADAM_STEP)
        v_hat = nv / (1.0 - ADAM_B2 ** ADAM_STEP)
        go_ref[...] = gv
        d_ref[...] = -ADAM_LR * (m_hat / (jnp.sqrt(v_hat) + ADAM_EPS) + ADAM_WD * w_ref[...])
        nm_ref[...] = nm
        nv_ref[...] = nv

    blk = pl.BlockSpec((tr, C), lambda i: (first + i, 0))
    flat = _sds((R, C), F32)
    operands = [a.reshape(-1, C) for a in (w, g, m, v)]
    in_specs, aliases = [blk] * 4, None
    if g.size != w.size:
        assert g.size == rows * C, (name, g.shape, shape, part)
        in_specs[1] = pl.BlockSpec((tr, C), lambda i: (i, 0))
    if into is not None:
        operands += [a.reshape(R, C) for a in into]
        in_specs = in_specs + [ANY] * 4
        aliases = {4 + i: i for i in range(4)}
    outs = _call(body, name=name, out_shape=(flat,) * 4, grid=(rows // tr,), in_specs=in_specs,
                 out_specs=(blk,) * 4, dims=("parallel",), aliases=aliases)(*operands)
    return tuple(o.reshape(shape) for o in outs)


def _place():
    x, y, c = lax.axis_index("x"), lax.axis_index("y"), lax.axis_index("c")
    chips = [(1 - x, y), (x, 1 - y), (1 - x, 1 - y)]
    return x, y, c, chips


def _at(ref, nd, dims):
    idx = [slice(None)] * nd
    for d, v in dims.items():
        idx[d] = pl.ds(v[0], v[1]) if isinstance(v, tuple) else v
    return ref.at[tuple(idx)]


def _remote(src, dst, send_sem, recv_sem, dev):
    return pltpu.make_async_remote_copy(src_ref=src, dst_ref=dst, send_sem=send_sem, recv_sem=recv_sem,
                                        device_id=dev, device_id_type=MESH)


def _split_call(body, name, out_shape, in_specs, out_specs, aliases):
    return pl.pallas_call(body, name=name, out_shape=out_shape, in_specs=in_specs, out_specs=out_specs,
                          input_output_aliases=aliases,
                          compiler_params=pltpu.CompilerParams(has_side_effects=EFFECT))


def _hbm(a):
    return pltpu.with_memory_space_constraint(a, pltpu.HBM)


def _copies_start(srcs, lands, after, name, plan, ncopies):
    n, m = len(srcs), len(lands)

    def body(*refs):
        src, land = refs[:n], refs[n:n + m]
        send_sems, recv_sems, token = refs[n + m + 1], refs[n + m + 2], refs[-1]
        x, y, c, chips = _place()
        for k, (s, d, dev) in enumerate(plan(x, y, c, chips, src, land)):
            _remote(s, d, send_sems.at[k], recv_sems.at[k], dev).start()
        token[...] = jnp.zeros_like(token)

    thru = tuple(pltpu.HBM(a.shape, a.dtype) for a in list(srcs) + list(lands))
    outs = _split_call(
        body, name,
        out_shape=(pltpu.SemaphoreType.DMA((ncopies,)), pltpu.SemaphoreType.DMA((ncopies,))) + thru + (_sds((8, LANES), F32),),
        in_specs=(HBM,) * (n + m) + (ANY,), out_specs=(SEM, SEM) + (HBM,) * (n + m) + (VMEM,),
        aliases={i: 2 + i for i in range(n + m)})(*[_hbm(a) for a in srcs], *[_hbm(a) for a in lands], after)
    return dict(send=outs[0], recv=outs[1], srcs=outs[2:2 + n], lands=outs[2 + n:2 + n + m], token=outs[-1])


def _copies_wait(handle, after, name, plan):
    srcs, lands = handle['srcs'], handle['lands']
    n, m = len(srcs), len(lands)

    def body(*refs):
        src, land = refs[:n], refs[n:n + m]
        send_sems, recv_sems = refs[n + m], refs[n + m + 1]
        x, y, c, chips = _place()
        for k, (s, d, dev) in enumerate(plan(x, y, c, chips, src, land)):
            cp = _remote(s, d, send_sems.at[k], recv_sems.at[k], dev)
            cp.wait_send()
            cp.wait_recv()

    thru = tuple(pltpu.HBM(a.shape, a.dtype) for a in list(srcs) + list(lands))
    outs = _split_call(body, name, out_shape=thru, in_specs=(HBM,) * (n + m) + (SEM, SEM, ANY),
                       out_specs=(HBM,) * (n + m), aliases={i: i for i in range(n + m)})(
        *srcs, *lands, handle['send'], handle['recv'], after)
    return outs[:n], outs[n:]


def _gather_plan(x, y, c, chips, src, land, arriving):
    q = 2 * x + y
    peers = [(ch[0], ch[1], c) for ch in chips] + [(x, y, 1 - c)]
    slots = [2 * ch[0] + ch[1] for ch in chips] + [q]
    return [(s, d.at[slots[j] if arriving else q], peers[j]) for s, d in zip(src, land) for j in range(4)]


def _gather_half_plan(x, y, c, chips, src, land, arriving):
    q = 2 * x + y
    out = []
    for s, d in zip(src, land):
        hs = s.shape[0] // 2
        rows = pl.ds(c * hs, hs)
        for ch in chips:
            out.append((s.at[rows], d.at[2 * ch[0] + ch[1] if arriving else q, rows], (ch[0], ch[1], c)))
        out.append((s, d.at[q], (x, y, 1 - c)))
    return out


def _exchange_plan(x, y, c, chips, src, land, arriving):
    out = []
    for d in land:
        hs = d.shape[1] // 2
        for ch in chips:
            slot = 2 * ch[0] + ch[1]
            out.append((d.at[slot, pl.ds(c * hs, hs)], d.at[slot, pl.ds(((1 - c) if arriving else c) * hs, hs)], (x, y, 1 - c)))
    return out


def _scatter_plan(shard_axes, shapes):
    def plan(x, y, c, chips, src, land):
        out = []
        for s, d, sd, shp in zip(src, land, shard_axes, shapes):
            ss = shp[sd] // N_SHARDS
            for j, ch in enumerate(chips):
                out.append((_at(s, len(shp), {sd: ((2 * ch[0] + ch[1]) * ss, ss)}), d.at[j], (ch[0], ch[1], c)))
        return out
    return plan


def _half_dims(shape, hd, c):
    hs = shape[hd] // 2
    return {hd: (c * hs, hs)}


def _swap_halves(grads, specs, name):
    n = len(grads)
    outs_shape = []
    for a, (sd, hd) in zip(grads, specs):
        shp = list(a.shape)
        shp[hd] //= 2
        outs_shape.append(_sds(shp, F32))

    def body(*refs):
        ins, outs = refs[:n], refs[n:2 * n]
        send_sems, recv_sems = refs[2 * n:]
        x, y, c, _ = _place()
        cps = []
        for ai, (sd, hd) in enumerate(specs):
            shp = grads[ai].shape
            cp = _remote(_at(ins[ai], len(shp), _half_dims(shp, hd, 1 - c)), outs[ai],
                         send_sems.at[ai], recv_sems.at[ai], (x, y, 1 - c))
            cp.start()
            cps.append(cp)
        for cp in cps:
            cp.wait()

    return _call(body, name=name, out_shape=tuple(outs_shape), in_specs=[ANY] * n, out_specs=tuple([ANY] * n),
                 scratch_shapes=[pltpu.SemaphoreType.DMA((n,)), pltpu.SemaphoreType.DMA((n,))])(*grads)


def _share_halves(arrs, half_axes, layers, after, name):
    n = len(arrs)

    def body(*refs):
        ins, outs = refs[:n], refs[n + 1:2 * n + 1]
        send_sems, recv_sems = refs[2 * n + 1:]
        x, y, c, _ = _place()

        def half(ref, ai, which):
            shp = arrs[ai].shape
            hs = shp[half_axes[ai]] // 2
            dims = {half_axes[ai]: (which * hs, hs)}
            if layers[ai] is not None:
                dims[0] = layers[ai]
            return _at(ref, len(shp), dims)

        sends = []
        for ai in range(n):
            cp = _remote(half(ins[ai], ai, c), half(outs[ai], ai, c), send_sems.at[ai], recv_sems.at[ai], (x, y, 1 - c))
            cp.start()
            sends.append(cp)
        for ai in range(n):
            land = half(outs[ai], ai, 1 - c)
            _remote(land, land, send_sems.at[ai], recv_sems.at[ai], (x, y, c)).wait_recv()
        for cp in sends:
            cp.wait_send()

    return _call(body, name=name, out_shape=tuple(_sds(a.shape, a.dtype) for a in arrs), in_specs=[ANY] * (n + 1),
                 out_specs=tuple([ANY] * n), aliases={i: i for i in range(n)},
                 scratch_shapes=[pltpu.SemaphoreType.DMA((n,)), pltpu.SemaphoreType.DMA((n,))])(*arrs, after)


def _gather_small(v, name):
    R, C = v.shape

    def body(x_ref, out_ref, send_sems, recv_sems, local_sem):
        x, y, c, chips = _place()
        me, sibling = (x, y, c), (x, y, 1 - c)

        def rows(px, py, pc):
            return out_ref.at[pl.ds((4 * px + 2 * py + pc) * R, R), :]

        def copy(k, block, to, src=None):
            return _remote(rows(*block) if src is None else src, rows(*block), send_sems.at[k], recv_sems.at[k], to)

        mine = pltpu.make_async_copy(x_ref, rows(*me), local_sem)
        mine.start()
        first = [copy(0, me, sibling, src=x_ref)]
        first += [copy(1 + j, me, (ch[0], ch[1], c), src=x_ref) for j, ch in enumerate(chips)]
        for cp in first:
            cp.start()
        passed = [copy(4 + j, (ch[0], ch[1], c), sibling) for j, ch in enumerate(chips)]
        for j, ch in enumerate(chips):
            copy(1 + j, (ch[0], ch[1], c), me).wait_recv()
            passed[j].start()
        copy(0, sibling, me).wait_recv()
        for j, ch in enumerate(chips):
            copy(4 + j, (ch[0], ch[1], 1 - c), me).wait_recv()
        for cp in first + passed:
            cp.wait_send()
        mine.wait()

    vm = pl.BlockSpec(memory_space=pltpu.VMEM)
    return _call(body, name=name, out_shape=_sds((8 * R, C), v.dtype), in_specs=[vm], out_specs=vm,
                 scratch_shapes=[pltpu.SemaphoreType.DMA((7,)), pltpu.SemaphoreType.DMA((7,)),
                                 pltpu.SemaphoreType.DMA])(v)


def _sum8(g, name):
    _, R, C = g.shape

    def body(g_ref, o_ref):
        acc = g_ref[0]
        for d in range(1, 8):
            acc = acc + g_ref[d]
        o_ref[...] = acc

    return _call(body, name=name, out_shape=_sds((R, C), F32), in_specs=[pl.BlockSpec(memory_space=pltpu.VMEM)],
                 out_specs=pl.BlockSpec(memory_space=pltpu.VMEM))(g)


def _rope_tables(positions):
    inv_freq = 1.0 / (ROPE_THETA ** (jnp.arange(0, HEAD_DIM, 2, dtype=F32) / HEAD_DIM))
    ang = positions.astype(F32)[:, None] * inv_freq
    cosv, sinv = jnp.cos(ang), jnp.sin(ang)
    return jnp.tile(cosv, (1, 4)), jnp.tile(jnp.concatenate([-sinv, sinv], axis=1), (1, 2))


def _blocked(a):
    parts = jnp.split(a, 4, axis=-1)
    return jnp.concatenate([parts[0], parts[2], parts[1], parts[3]], axis=-1)


def _arrived(Wl, name, after):
    if callable(Wl[name]):
        Wl[name] = Wl[name](after)
    return Wl[name]


def _local_step(x, target, positions, P, weights_of, ffn_grads_done, grads_done):
    S, D = x.shape
    depth = P['mix_pre_g'].shape[0]
    n_a = depth // 2
    cos_t, sin_t = _rope_tables(positions)
    row = lambda a, l: a[l][None]
    cb = [_blocked(P['ffn_conv_b'][l])[None] for l in range(depth)]
    sv, W = {}, {}
    kv = k_rot = None
    pre = []
    for l in range(depth):
        t = f"l{l}"
        W[l], zero = weights_of(l, x)
        sv[l, 'x_in'] = x
        if l < n_a:
            d = _pool_fwd(x, row(P['mix_pre_g'], l) + zero, "pool_fwd_" + t)
            y, x, h2 = _pool_mm_fwd(d, W[l]['pool_w'], W[l]['pool_scale'], x, row(P['mix_post_g'], l),
                                    row(P['ffn_pre_g'], l), "pool_mm_fwd_" + t)
            sv[l, 'd'], sv[l, 'y'] = d, y
        else:
            j = l - n_a
            if not pre:
                pre = [_rms_fwd(x, row(P['mix_pre_g'], l), BF16, "q_norm_" + t)]
                if l == n_a:
                    pre.append(_rms_fwd(x, P['kv_norm_g'][None], BF16, "kv_norm"))
            h = pre[0]
            if l == n_a:
                kv = _matmul(pre[1], W[l]['w_kv'], 'nn', F32, "kv_proj", 1024, 512, 1024)
                k_rot = _rope(kv, KV_DIM // 2, cos_t, sin_t, "k_rope")
                sv['hkv'] = pre[1]
            q = _matmul(h, W[l]['w_q'], 'nn', F32, "q_proj_" + t, 1024,1024, 1024, rope=('out', cos_t + zero, sin_t))
            o, lse = _attn_fwd(q, k_rot, kv, P['sinks'][j], "attn_fwd_" + t)
            m = _matmul(o, W[l]['w_o'], 'nn', F32, "o_proj_" + t, 1024,1024, 1024)
            x, (h2,) = _res_rms_fwd(x, m, row(P['mix_post_g'], l), "mix_post_" + t, [row(P['ffn_pre_g'], l)])
            sv[l, 'h'], sv[l, 'q'], sv[l, 'o'], sv[l, 'lse'], sv[l, 'm'] = h, q, o, lse, m
        sv[l, 'x1'] = x
        if 'pre_ffn' in W[l]:
            W[l].pop('pre_ffn')(x)
        w_in = _arrived(W[l], 'w_in', h2)
        u = _matmul(h2, w_in, 'nn', F32, "ffn_up_" + t, 1024,w_in.shape[2], 1024, b_blocks=True)
        a, sv[l, 'act'] = _conv_glu_fwd(u, W[l]['cw'], cb[l] + W[l].pop('tie', 0.0), "ffn_glu_" + t)
        f = _matmul(a, _arrived(W[l], 'w_out', a), 'nn', F32, "ffn_down_" + t, 1024,1024, 2816)
        sv[l, 'h2'], sv[l, 'u'], sv[l, 'a'], sv[l, 'f'] = h2, u, a, f
        if l + 1 == depth:
            dx, sq = _res_rms_loss(x, f, row(P['ffn_post_g'], l), target, "ffn_post_loss")
        else:
            gains = []
            if l + 1 >= n_a:
                gains = [row(P['mix_pre_g'], l + 1)] + ([P['kv_norm_g'][None]] if l + 1 == n_a else [])
            x, pre = _res_rms_fwd(x, f, row(P['ffn_post_g'], l), "ffn_post_" + t, gains)

    kv_parts = []
    zero = 0.0
    for l in reversed(range(depth)):
        t = f"l{l}"
        G = {}
        wd = W[l]['w_in'].shape[2]
        df, G['ffn_post_g'] = _rms_bwd(sv[l, 'f'], row(P['ffn_post_g'], l) + zero, dx, None, "ffn_post_bwd_" + t)
        da = _matmul(df, W[l]['w_out'], 'nt', F32, "ffn_down_dx_" + t, 1024,wd, 1024)
        G['ffn_w_out'] = _matmul(sv[l, 'a'], df, 'tn', F32, "ffn_down_dw_" + t, wd, 1024, 1024)
        du, acc = _conv_glu_bwd(sv[l, 'u'], sv[l, 'act'], da, W[l]['cw'], "ffn_glu_bwd_" + t)
        G['ffn_conv_w'] = _blocked(acc[0:3])
        G['ffn_conv_b'] = _blocked(acc[3:4])
        dh2 = _matmul(du, W[l]['w_in'], 'nt', F32, "ffn_up_dx_" + t, 1024,1024, wd, b_blocks=True)
        G['ffn_w_in'] = _matmul(sv[l, 'h2'], du, 'tn', F32, "ffn_up_dw_" + t, 1024, wd, 2048, out_perm=True)
        zero = ffn_grads_done(l, G, dh2)
        if l < n_a:
            dx, G['ffn_pre_g'] = _rms_bwd(sv[l, 'x1'], row(P['ffn_pre_g'], l) + zero, dh2, dx, "ffn_norm_bwd_" + t)
        else:
            dx, G['ffn_pre_g'], dm, G['mix_post_g'] = _rms_bwd(
                sv[l, 'x1'], row(P['ffn_pre_g'], l) + zero, dh2, dx, "ffn_norm_bwd_" + t,
                then=(sv[l, 'm'], row(P['mix_post_g'], l)))
        if l < n_a:
            dd, G['pool_w'], G['pool_scale'], G['mix_post_g'] = _pool_mm_bwd(
                dx, sv[l, 'y'], sv[l, 'd'], W[l]['pool_w'], W[l]['pool_scale'], row(P['mix_post_g'], l), "pool_mm_bwd_" + t)
            dx, G['mix_pre_g'] = _pool_bwd(dd, sv[l, 'x_in'], row(P['mix_pre_g'], l), dx, "pool_bwd_" + t)
        else:
            j = l - n_a
            do = _matmul(dm, W[l]['w_o'], 'nt', F32, "o_proj_dx_" + t, 1024,1024, 1024)
            G['w_o'] = _matmul(sv[l, 'o'], dm, 'tn', F32, "o_proj_dw_" + t, 1024, 1024, 1024)
            dq, dkn, dko, dvn, dvo, dsk = _attn_bwd(sv[l, 'q'], k_rot, kv, do, sv[l, 'lse'], P['sinks'][j], "attn_bwd_" + t)
            G['sinks'] = dsk[0:1]
            kv_parts.append((dkn, dko, dvn, dvo))
            dh = _matmul(dq, W[l]['w_q'], 'nt', F32, "q_proj_dx_" + t, 1024,1024, 1024, rope=('a', cos_t, -sin_t))
            G['w_q'] = _matmul(sv[l, 'h'], dq, 'tn', F32, "q_proj_dw_" + t, 1024, 1024, 1024, rope=('b', cos_t, -sin_t))
            dx, G['mix_pre_g'] = _rms_bwd(sv[l, 'x_in'], row(P['mix_pre_g'], l), dh, dx, "q_norm_bwd_" + t)
            if l == n_a:
                dkv = _kv_grad(kv_parts, cos_t, -sin_t, "kv_grad")
                dhkv = _matmul(dkv, W[l]['w_kv'], 'nt', F32, "kv_proj_dx", 1024, 1024, 512)
                G['w_kv'] = _matmul(sv['hkv'], dkv, 'tn', F32, "kv_proj_dw", 1024, 512, 1024)
                dx, G['kv_norm_g'] = _rms_bwd(sv[l, 'x_in'], P['kv_norm_g'][None], dhkv, dx, "kv_norm_bwd")
        zero = grads_done(l, G, dx)
    return sq, dx


SMALL = ['mix_pre_g', 'mix_post_g', 'kv_norm_g', 'sinks', 'ffn_pre_g', 'ffn_post_g', 'ffn_conv_b', 'ffn_conv_w', 'pool_scale']
BIG = {'ffn_w_in': (1, 0, 1), 'ffn_w_out': (0, 1, 2), 'w_q': (0, 1, 2), 'w_o': (0, 1, 2), 'w_kv': (0, 1, 1),
       'pool_w': (1, 0, 1)}


def _swap_plan(half_axes, shapes):
    def plan(x, y, c, chips, src, land):
        return [(_at(s, len(shp), _half_dims(shp, hd, 1 - c)), d, (x, y, 1 - c))
                for s, d, hd, shp in zip(src, land, half_axes, shapes)]
    return plan


def _swap_start(pieces, after, name):
    arrs = [p[0] for p in pieces]
    plan = _swap_plan([p[2] for p in pieces], [a.shape for a in arrs])
    lands = []
    for a, p in zip(arrs, pieces):
        shp = list(a.shape)
        shp[p[2]] //= 2
        lands.append(lax.empty(tuple(shp), F32))
    return pieces, _copies_start(arrs, lands, after, name, plan, len(arrs)), plan


def _swap_wait(started, after, name):
    pieces, handle, plan = started
    arrs, theirs = _copies_wait(handle, after, name, plan)
    return [(a,) + p[1:] for a, p in zip(arrs, pieces)], list(theirs)


def _reduce_start(pieces, theirs, qc, after, tag):
    arrs = [p[0] for p in pieces]
    specs = [(p[1], p[2]) for p in pieces]
    sums = []
    for pi, (a, (sd, hd), r) in enumerate(zip(arrs, specs, theirs)):
        shp = r.shape
        nd = len(shp)
        if nd == 3:
            blk, grid = tuple(shp), (1,)
            mine = lambda i, s: (s[1], 0, 0)
            zero = lambda i, s: (0, 0, 0)
        elif hd == 0:
            tr = _tile(shp[0], max(16, BLOCK_ELEMS //shp[1]), 16)
            blk, grid = (tr, shp[1]), (shp[0] // tr,)
            nblk = shp[0] // tr
            mine = lambda i, s, nblk=nblk: (s[1] * nblk + i, 0)
            zero = lambda i, s: (i, 0)
        else:
            tr = _tile(shp[0], max(16, BLOCK_ELEMS //shp[1]), 16)
            blk, grid = (tr, shp[1]), (shp[0] // tr,)
            mine = lambda i, s: (i, s[1])
            zero = lambda i, s: (i, 0)
        sums.append(_sum_blocks(f"grad_chip_sum_{tag}_{pi}", qc, grid, shp, blk, zero, [(a, blk, mine), (r, blk, zero)],
                                out_dtype=BF16))

    lands = []
    for s_arr, (sd, hd) in zip(sums, specs):
        shp = list(s_arr.shape)
        shp[sd] //= N_SHARDS
        lands.append(lax.empty((3,) + tuple(shp), BF16))
    plan = _scatter_plan([sd for sd, _ in specs], [s.shape for s in sums])
    handle = _copies_start(sums, lands, after, "grad_scatter_start_" + tag, plan, 3 * len(sums))
    return dict(handle=handle, plan=plan, pieces=pieces, tag=tag)


def _reduce_finish(state, after, qc, outs, out_shapes):
    handle, pieces, tag = state['handle'], state['pieces'], state['tag']
    sums, recvd = _copies_wait(handle, after, "grad_scatter_wait_" + tag, state['plan'])
    for pi, ((a, sd, hd, oname, fixed, ohd), s_arr, r) in enumerate(zip(pieces, sums, recvd)):
        shp = r.shape[1:]
        nd = len(shp)
        lead = (fixed[0],) if fixed else ()
        none = (None,) if fixed else ()
        n_stack = out_shapes[oname][0]
        if nd == 3:
            blk, grid = tuple(shp), (1,)
            mine = lambda i, s: (0, s[0], 0)
            rk = [lambda i, s, k=k: (k, 0, 0, 0) for k in range(3)]
            oshape = (n_stack, 2 * shp[0]) + tuple(shp[1:])
            oblk = none + blk
            omap = lambda i, s, lead=lead: lead + (s[1], 0, 0)
        elif sd == 1:
            tr = _tile(shp[0], max(16, BLOCK_ELEMS //shp[1]), 16)
            blk, grid = (tr, shp[1]), (shp[0] // tr,)
            nblk = shp[0] // tr
            mine = lambda i, s: (i, s[0])
            rk = [lambda i, s, k=k: (k, i, 0) for k in range(3)]
            oshape = (n_stack, 2 * shp[0], shp[1])
            oblk = none + blk
            omap = lambda i, s, lead=lead, nblk=nblk: lead + (s[1] * nblk + i, 0)
        else:
            tr = _tile(shp[0], max(16, BLOCK_ELEMS //shp[1]), 16)
            blk, grid = (tr, shp[1]), (shp[0] // tr,)
            nblk = shp[0] // tr
            mine = lambda i, s, nblk=nblk: (s[0] * nblk + i, 0)
            rk = [lambda i, s, k=k: (k, i, 0) for k in range(3)]
            oshape = ((n_stack,) if fixed else ()) + (shp[0], 2 * shp[1])
            oblk = none + blk
            omap = lambda i, s, lead=lead: lead + (i, s[1])
        assert tuple(oshape) == tuple(out_shapes[oname]), (oname, oshape, out_shapes[oname])
        ins = [(s_arr, blk, mine)] + [(r, (None,) + blk, rk[k]) for k in range(3)]
        outs[oname] = _sum_blocks(f"grad_total_{tag}_{pi}", qc, grid, oshape, oblk, omap, ins, into=outs.get(oname))


def _pack_small(parts):
    rows, offs, r = [], [], 0
    for a in parts:
        flat = a.reshape(-1)
        nr = -(-flat.size // (8 * LANES)) * 8
        rows.append(jnp.pad(flat, (0, nr * LANES - flat.size)).reshape(nr, LANES))
        offs.append((r, nr, a.shape))
        r += nr
    return jnp.concatenate(rows, axis=0), offs


def _unpack_small(packed, offs):
    return [packed[r:r + nr].reshape(-1)[:math.prod(shape)].reshape(shape) for r, nr, shape in offs]


def kernel(x, positions, mix_pre_g, mix_post_g, pool_w, pool_scale, kv_norm_g, w_kv, w_q, w_o, sinks, ffn_pre_g, ffn_post_g, ffn_w_in, ffn_conv_w, ffn_conv_b, ffn_w_out, loss_target, m_mix_pre_g, m_mix_post_g, m_pool_w, m_pool_scale, m_kv_norm_g, m_w_kv, m_w_q, m_w_o, m_sinks, m_ffn_pre_g, m_ffn_post_g, m_ffn_w_in, m_ffn_conv_w, m_ffn_conv_b, m_ffn_w_out, v_mix_pre_g, v_mix_post_g, v_pool_w, v_pool_scale, v_kv_norm_g, v_w_kv, v_w_q, v_w_o, v_sinks, v_ffn_pre_g, v_ffn_post_g, v_ffn_w_in, v_ffn_conv_w, v_ffn_conv_b, v_ffn_w_out):
    w = dict(mix_pre_g=mix_pre_g, mix_post_g=mix_post_g, pool_w=pool_w, pool_scale=pool_scale, kv_norm_g=kv_norm_g,
             w_kv=w_kv, w_q=w_q, w_o=w_o, sinks=sinks, ffn_pre_g=ffn_pre_g, ffn_post_g=ffn_post_g, ffn_w_in=ffn_w_in,
             ffn_conv_w=ffn_conv_w, ffn_conv_b=ffn_conv_b, ffn_w_out=ffn_w_out)
    mom = dict(mix_pre_g=m_mix_pre_g, mix_post_g=m_mix_post_g, pool_w=m_pool_w, pool_scale=m_pool_scale,
               kv_norm_g=m_kv_norm_g, w_kv=m_w_kv, w_q=m_w_q, w_o=m_w_o, sinks=m_sinks, ffn_pre_g=m_ffn_pre_g,
               ffn_post_g=m_ffn_post_g, ffn_w_in=m_ffn_w_in, ffn_conv_w=m_ffn_conv_w, ffn_conv_b=m_ffn_conv_b,
               ffn_w_out=m_ffn_w_out)
    var = dict(mix_pre_g=v_mix_pre_g, mix_post_g=v_mix_post_g, pool_w=v_pool_w, pool_scale=v_pool_scale,
               kv_norm_g=v_kv_norm_g, w_kv=v_w_kv, w_q=v_w_q, w_o=v_w_o, sinks=v_sinks, ffn_pre_g=v_ffn_pre_g,
               ffn_post_g=v_ffn_post_g, ffn_w_in=v_ffn_w_in, ffn_conv_w=v_ffn_conv_w, ffn_conv_b=v_ffn_conv_b,
               ffn_w_out=v_ffn_w_out)
    depth = mix_pre_g.shape[0]
    q_chip = 2 * lax.axis_index("x") + lax.axis_index("y")
    qc = jnp.stack([q_chip, lax.axis_index("c")]).astype(jnp.int32)

    n_a = depth // 2
    D = x.shape[-1]
    gc = pool_w.shape[3]

    def layer_shards(l):
        first = []
        if l < n_a:
            first.append(('pool_w', pool_w[l].astype(BF16)))
        else:
            first += [('w_q', w_q[l - n_a].astype(BF16)), ('w_o', w_o[l - n_a].astype(BF16))]
            if l == n_a:
                first.append(('w_kv', w_kv.astype(BF16)))
        ffn = [('w_in', ffn_w_in[l].astype(BF16)), ('w_out', ffn_w_out[l].astype(BF16))]
        if l == 0:
            return [first + [('conv_w', ffn_conv_w), ('pool_scale', pool_scale)], ffn[:1], ffn[1:]]
        return [first + ffn]

    def start_group(items, after, name, plan):
        srcs = [a for _, a in items]
        lands = [lax.empty((N_SHARDS,) + a.shape, a.dtype) for a in srcs]
        handle = _copies_start(srcs, lands, after, name, functools.partial(plan, arriving=False), 4 * len(srcs))
        return [n for n, _ in items], handle, plan

    def wait_group(started, after, name):
        names, handle, plan = started
        _, lands = _copies_wait(handle, after, name, functools.partial(plan, arriving=True))
        return dict(zip(names, lands))

    def start_exchange(got, after, name):
        names = list(got)
        handle = _copies_start([], [got[n] for n in names], after, name,
                               functools.partial(_exchange_plan, arriving=False), 3 * len(names))
        return names, handle, _exchange_plan

    groups0 = layer_shards(0)
    small0 = start_group(groups0[0], x, "weight_gather_start_l0_small", _gather_plan)
    in0 = start_group(groups0[1], small0[1]['token'], "weight_gather_start_l0_in", _gather_half_plan)
    out0 = start_group(groups0[2], in0[1]['token'], "weight_gather_start_l0_out", _gather_half_plan)
    pending, shared, steps = {}, {}, {}

    def start_next(l, after):
        pending[l + 1] = start_group(layer_shards(l + 1)[0], after, f"weight_gather_start_l{l + 1}", _gather_plan)
        return pending[l + 1][1]['token'][0, 0]

    def pre_ffn0(after):
        landed = wait_group(in0, after, "weight_gather_wait_l0_in")
        steps['in'] = start_exchange(landed, after, "weight_exchange_start_l0_in")

    def w_in0(Wl, after):
        both = wait_group(steps['in'], after, "weight_exchange_wait_l0_in")
        Wl['tie'] = start_next(0, both['w_in'])
        return both['w_in']

    def w_out0(after):
        landed = wait_group(out0, after, "weight_gather_wait_l0_out")
        both = wait_group(start_exchange(landed, after, "weight_exchange_start_l0_out"), after,
                          "weight_exchange_wait_l0_out")
        return both['w_out'].reshape(-1, D)

    def weights_of(l, x_now):
        zero = 0.0
        if l == 0:
            got = wait_group(small0, out0[1]['token'], "weight_gather_wait_l0_small")
            shared['conv_w'] = got['conv_w']
            shared['pool_scale'] = got['pool_scale'].transpose(1, 0, 2).reshape(n_a, D)
        else:
            got = wait_group(pending.pop(l), x_now, f"weight_gather_wait_l{l}")
            if l + 1 < depth:
                zero = start_next(l, got['w_in'])
        taps = jnp.concatenate([shared['conv_w'][p, l] for p in (0, 2, 1, 3)], axis=-1)
        Wl = dict(cw=jnp.pad(taps, ((0, 5), (0, 0))))
        if l == 0:
            Wl['pre_ffn'], Wl['w_in'], Wl['w_out'] = pre_ffn0, functools.partial(w_in0, Wl), w_out0
        else:
            Wl['w_in'], Wl['w_out'] = got['w_in'], got['w_out'].reshape(-1, D)
        if l < n_a:
            Wl['pool_w'] = got['pool_w'].transpose(1, 0, 2, 3).reshape(-1, gc, gc)
            Wl['pool_scale'] = shared['pool_scale'][l][None]
        else:
            Wl['w_q'], Wl['w_o'] = got['w_q'].reshape(D, D), got['w_o'].reshape(D, D)
            if l == n_a:
                Wl['w_kv'] = got['w_kv'].reshape(D, -1)
        return Wl, zero

    big_shapes = {n: w[n].shape for n in BIG}
    big, G, scattering = {}, {}, {}

    swapping = {}

    def piece(n, g, l):
        lead = {0: (l if n.startswith('ffn') or n == 'pool_w' else l - n_a)} if len(big_shapes[n]) > 2 else {}
        return (g, BIG[n][0], BIG[n][1], n, lead, BIG[n][2])

    def ffn_grads_done(l, Gl, after):
        swapping[l] = _swap_start([piece(n, Gl[n], l) for n in ('ffn_w_in', 'ffn_w_out')], after, f"grad_swap_start_l{l}")
        return swapping[l][1]['token'][0, 0]

    small = {}

    def gather_small_grads():
        local = {
            'mix_pre_g': jnp.concatenate([G['mix_pre_g', l] for l in range(depth)], axis=0),
            'mix_post_g': jnp.concatenate([G['mix_post_g', l] for l in range(depth)], axis=0),
            'kv_norm_g': G['kv_norm_g', n_a][0],
            'sinks': jnp.concatenate([G['sinks', l][:, :sinks.shape[1]] for l in range(n_a, depth)], axis=0),
            'ffn_pre_g': jnp.concatenate([G['ffn_pre_g', l] for l in range(depth)], axis=0),
            'ffn_post_g': jnp.concatenate([G['ffn_post_g', l] for l in range(depth)], axis=0),
            'ffn_conv_b': jnp.concatenate([G['ffn_conv_b', l] for l in range(depth)], axis=0),
            'ffn_conv_w': jnp.stack([G['ffn_conv_w', l] for l in range(depth)], axis=0),
            'pool_scale': jnp.concatenate([G['pool_scale', l] for l in range(n_a)], axis=0),
        }
        packed, small['offs'] = _pack_small([local[n] for n in SMALL])
        small['gathered'] = _gather_small(packed, "small_grad_gather").reshape(8, packed.shape[0], LANES)

    def grads_done(l, Gl, dx_now):
        for n, g in Gl.items():
            if n not in BIG:
                G[n, l] = g
        after = dx_now
        if l == 0:
            gather_small_grads()
            after = small['gathered']
        pieces, theirs = _swap_wait(swapping.pop(l), after, f"grad_swap_wait_l{l}")
        rest = [piece(n, Gl[n], l) for n in BIG if n in Gl and not n.startswith('ffn')]
        theirs += list(_swap_halves([p[0] for p in rest], [(p[1], p[2]) for p in rest], f"grad_swap_halves_l{l}"))
        scattering[l] = _reduce_start(pieces + rest, theirs, qc, after, f"l{l}")
        if l + 1 in scattering:
            _reduce_finish(scattering.pop(l + 1), dx_now, qc, big, big_shapes)
        return scattering[l]['handle']['token'][0, 0]

    P = {n: w[n] for n in ('mix_pre_g', 'mix_post_g', 'kv_norm_g', 'sinks', 'ffn_pre_g', 'ffn_post_g', 'ffn_conv_b')}
    sq, dx = _local_step(x[0], loss_target[0], positions[0], P, weights_of, ffn_grads_done, grads_done)
    loss = 0.5 / D * lax.psum(jnp.sum(sq), ("x", "y", "c"))

    late = {'ffn_w_in': (1, depth - 1), 'ffn_w_out': (1, depth - 1), 'pool_w': (1, n_a - 1), 'w_q': None, 'w_o': None,
            'w_kv': None}
    names = list(big)
    in_flight = scattering[0]['handle']['token']
    whole = _share_halves([big[n] for n in names], [BIG[n][2] for n in names], [late[n] for n in names], in_flight,
                          "grad_share_halves_late")
    upd = {}
    for n, g in zip(names, whole):
        part = None if late[n] is None else late[n] + (w[n].shape[0],)
        upd[n] = _adamw(w[n], g, mom[n], var[n], "adamw_late_" + n, part=part)

    summed = _sum8(small['gathered'] + in_flight[0, 0], "small_grad_sum")
    grads = dict(zip(SMALL, _unpack_small(summed, small['offs'])))
    wd = ffn_conv_w.shape[2]
    grads['ffn_conv_w'] = lax.dynamic_slice_in_dim(grads['ffn_conv_w'], q_chip * wd, wd, axis=2)
    ps = pool_scale.shape[1]
    grads['pool_scale'] = lax.dynamic_slice_in_dim(grads['pool_scale'], q_chip * ps, ps, axis=1)

    delta, new_m, new_v = {}, {}, {}
    for n in SMALL:
        upd[n] = _adamw(w[n], grads[n], mom[n], var[n], "adamw_" + n)

    first, shapes0 = {}, {n: (1,) + w[n].shape[1:] for n in ('ffn_w_in', 'ffn_w_out', 'pool_w')}
    done = jnp.stack([upd[n][1][(-1,) * upd[n][1].ndim] for n in upd])
    _reduce_finish(scattering.pop(0), done, qc, first, shapes0)
    names0 = list(first)
    whole0 = _share_halves([first[n] for n in names0], [BIG[n][2] for n in names0], [None] * len(names0), done,
                           "grad_share_halves_l0")
    for n, g in zip(names0, whole0):
        upd[n] = _adamw(w[n], g, mom[n], var[n], "adamw_l0_" + n, part=(0, 1, w[n].shape[0]), into=upd[n])
    for n in upd:
        grads[n], delta[n], new_m[n], new_v[n] = upd[n]

    return (loss, dx[None], *[grads[n] for n in WEIGHTS], *[delta[n] for n in WEIGHTS],
            *[new_m[n] for n in WEIGHTS], *[new_v[n] for n in WEIGHTS])
```

```python
import functools
import math

import jax
import jax.numpy as jnp
from jax import lax
from jax.experimental import pallas as pl
from jax.experimental.pallas import tpu as pltpu

F32 = jnp.float32
BF16 = jnp.bfloat16
MESH = pl.DeviceIdType.MESH
ANY = pl.BlockSpec(memory_space=pl.ANY)
HBM = pl.BlockSpec(memory_space=pltpu.HBM)
VMEM = pl.BlockSpec(memory_space=pltpu.VMEM)
SEM = pl.BlockSpec(memory_space=pltpu.SEMAPHORE)
EFFECT = pltpu.SideEffectType.DATAFLOW_SIDE_EFFECTING

HEAD_DIM = 64
N_KV_HEADS = 4
KV_DIM = 2 * N_KV_HEADS * HEAD_DIM
WINDOW = 128
BLOCK = 128
POOL_WINDOWS = (2, 4, 8, 16)
POOL_HALO = 16
CONV_HALO = 8
ROPE_THETA = 10000.0
ATTN_SCALE = 1.0 / math.sqrt(HEAD_DIM)
NEG_INF = -1e30
RMS_EPS = 1e-6
ADAM_LR, ADAM_B1, ADAM_B2, ADAM_EPS, ADAM_WD, ADAM_STEP = 0.001, 0.9, 0.999, 1e-08, 0.01, 10
N_SHARDS = 4
LANES = 128
VMEM_LIMIT_BYTES = 48 << 20
ROW_TILE = 512
BLOCK_ELEMS = 1 << 19

WEIGHTS = ['mix_pre_g', 'mix_post_g', 'pool_w', 'pool_scale', 'kv_norm_g', 'w_kv', 'w_q', 'w_o', 'sinks',
           'ffn_pre_g', 'ffn_post_g', 'ffn_w_in', 'ffn_conv_w', 'ffn_conv_b', 'ffn_w_out']


def _call(body, *, name, out_shape, grid=None, in_specs=None, out_specs=None, scratch_shapes=(), dims=None,
          grid_spec=None, aliases=None):
    params = pltpu.CompilerParams(dimension_semantics=dims, vmem_limit_bytes=VMEM_LIMIT_BYTES)
    kw = {} if aliases is None else dict(input_output_aliases=aliases)
    if grid_spec is not None:
        return pl.pallas_call(body, name=name, out_shape=out_shape, grid_spec=grid_spec, compiler_params=params, **kw)
    if grid is not None:
        kw['grid'] = grid
    return pl.pallas_call(body, name=name, out_shape=out_shape, in_specs=in_specs, out_specs=out_specs,
                          scratch_shapes=list(scratch_shapes), compiler_params=params, **kw)


def _tile(n, pref, mult=8):
    if n <= pref:
        return n
    for t in range(pref, 0, -1):
        if n % t == 0 and t % mult == 0:
            return t
    raise ValueError((n, pref, mult))


def _sds(shape, dtype):
    return jax.ShapeDtypeStruct(tuple(shape), dtype)


def _perm4(j):
    return (j % 2) * 2 + j // 2


def _rope_wide(x, cosv, sinv):
    return jnp.concatenate([_rope_chunk(x[:, c * LANES:(c + 1) * LANES], cosv, sinv) for c in range(x.shape[1] // LANES)],
                           axis=1)


def _matmul(a, b, mode, out_dtype, name, tm, tn, tk, b_blocks=False, out_perm=False, rope=None):
    a2 = a.shape
    b2 = (b.shape[1], 4 * b.shape[2]) if b_blocks else b.shape
    if mode == 'nn':
        (M, K), (K2, N) = a2, b2
    elif mode == 'nt':
        (M, K), (N, K2) = a2, b2
    else:
        (K, M), (K2, N) = a2, b2
    assert K == K2, (name, a.shape, b.shape)
    tm, tn, tk = _tile(M, tm), _tile(N, tn, LANES), _tile(K, tk, LANES if mode != 'tn' else 16)
    assert M % tm == 0 and N % tn == 0 and K % tk == 0
    nk = K // tk
    grid = (N // tn, M // tm, nk)

    if mode == 'nn':
        a_spec = pl.BlockSpec((tm, tk), lambda j, i, k: (i, k))
        if b_blocks:
            assert tn == b.shape[2]
            b_spec = pl.BlockSpec((None, tk, tn), lambda j, i, k: (_perm4(j), k, 0))
        else:
            b_spec = pl.BlockSpec((tk, tn), lambda j, i, k: (k, j))
        dn = (((1,), (0,)), ((), ()))
    elif mode == 'nt':
        a_spec = pl.BlockSpec((tm, tk), lambda j, i, k: (i, k))
        if b_blocks:
            assert tk == b.shape[2]
            b_spec = pl.BlockSpec((None, tn, tk), lambda j, i, k: (_perm4(k), j, 0))
        else:
            b_spec = pl.BlockSpec((tn, tk), lambda j, i, k: (j, k))
        dn = (((1,), (1,)), ((), ()))
    else:
        a_spec = pl.BlockSpec((tk, tm), lambda j, i, k: (k, i))
        b_spec = pl.BlockSpec((tk, tn), lambda j, i, k: (k, j))
        dn = (((0,), (0,)), ((), ()))
    po = _perm4 if out_perm else (lambda j: j)
    o_spec = pl.BlockSpec((tm, tn), lambda j, i, k: (i, po(j)))
    where, tables, table_specs = None, [], []
    if rope is not None:
        where, tables = rope[0], list(rope[1:])
        assert (where == 'out' and mode == 'nn' and tn == N and nk == 1) or (where == 'a' and mode == 'nt' and nk == 1) \
            or (where == 'b' and mode == 'tn' and tn == N), (name, where, mode)
        rows, imap = (tk, lambda j, i, k: (k, 0)) if where == 'b' else (tm, lambda j, i, k: (i, 0))
        table_specs = [pl.BlockSpec((rows, LANES), imap)] * 2

    def body(a_ref, b_ref, *rest):
        o_ref, acc = rest[len(tables)], rest[len(tables) + 1:]
        av, bv = a_ref[...], b_ref[...]
        if where == 'a':
            av = _rope_wide(av, rest[0][...], rest[1][...])
        if where == 'b':
            bv = _rope_wide(bv, rest[0][...], rest[1][...])
        prod = lax.dot_general(av.astype(BF16), bv.astype(BF16), dn, preferred_element_type=F32)
        if where == 'out':
            prod = _rope_wide(prod, rest[0][...], rest[1][...])
        if nk == 1:
            o_ref[...] = prod.astype(o_ref.dtype)
        else:
            k = pl.program_id(2)

            @pl.when(k == 0)
            def _():
                acc[0][...] = prod

            @pl.when(k > 0)
            def _():
                acc[0][...] += prod

            @pl.when(k == nk - 1)
            def _():
                o_ref[...] = acc[0][...].astype(o_ref.dtype)

    scratch = [] if nk == 1 else [pltpu.VMEM((tm, tn), F32)]
    return _call(body, name=name, out_shape=_sds((M, N), out_dtype), grid=grid, in_specs=[a_spec, b_spec] + table_specs,
                 out_specs=o_spec, scratch_shapes=scratch, dims=("parallel", "parallel", "arbitrary"))(a, b, *tables)


def _rstd(x):
    return lax.rsqrt(jnp.mean(x * x, axis=-1, keepdims=True) + RMS_EPS)


def _rms_fwd(x, g, out_dtype, name):
    S, D = x.shape
    tr = _tile(S, ROW_TILE)

    def body(x_ref, g_ref, o_ref):
        xv = x_ref[...]
        o_ref[...] = (xv * _rstd(xv) * g_ref[...]).astype(o_ref.dtype)

    row = pl.BlockSpec((tr, D), lambda i: (i, 0))
    vec = pl.BlockSpec((1, D), lambda i: (0, 0))
    return _call(body, name=name, out_shape=_sds((S, D), out_dtype), grid=(S // tr,), in_specs=[row, vec],
                 out_specs=row, dims=("parallel",))(x, g)


def _res_rms_fwd(x, f, g, name, norm_gains=()):
    S, D = x.shape
    tr = _tile(S, ROW_TILE)
    k = len(norm_gains)

    def body(x_ref, f_ref, g_ref, *rest):
        fv = f_ref[...]
        xn = x_ref[...] + fv * _rstd(fv) * g_ref[...]
        rest[k][...] = xn
        if k:
            xh = xn * _rstd(xn)
            for g2_ref, h_ref in zip(rest[:k], rest[k + 1:]):
                h_ref[...] = (xh * g2_ref[...]).astype(h_ref.dtype)

    row = pl.BlockSpec((tr, D), lambda i: (i, 0))
    vec = pl.BlockSpec((1, D), lambda i: (0, 0))
    outs = _call(body, name=name, out_shape=(_sds((S, D), F32),) + (_sds((S, D), BF16),) * k, grid=(S // tr,),
                 in_specs=[row, row, vec] + [vec] * k, out_specs=(row,) * (1 + k), dims=("parallel",))(x, f, g, *norm_gains)
    return outs[0], list(outs[1:])


def _res_rms_loss(x, f, g, target, name):
    S, D = x.shape
    tr = _tile(S, ROW_TILE)

    def body(x_ref, f_ref, g_ref, t_ref, dy_ref, acc_ref):
        fv = f_ref[...]
        e = x_ref[...] + fv * _rstd(fv) * g_ref[...] - t_ref[...]
        dy_ref[...] = e * (1.0 / D)

        @pl.when(pl.program_id(0) == 0)
        def _():
            acc_ref[...] = jnp.zeros_like(acc_ref)

        acc_ref[...] += jnp.sum(e * e, axis=0, keepdims=True)

    row = pl.BlockSpec((tr, D), lambda i: (i, 0))
    vec = pl.BlockSpec((1, D), lambda i: (0, 0))
    return _call(body, name=name, out_shape=(_sds((S, D), F32), _sds((1, D), F32)), grid=(S // tr,),
                 in_specs=[row, row, vec, row], out_specs=(row, vec), dims=("arbitrary",))(x, f, g, target)


def _rms_bwd_math(xin, g, dy):
    r = _rstd(xin)
    xh = xin * r
    gy = dy * g
    dx = r * (gy - xh * jnp.mean(gy * xh, axis=-1, keepdims=True))
    return dx, dy * xh


def _rms_bwd(xin, g, dy, res, name, then=None, dx_dtype=F32, then_dtype=F32):
    S, D = xin.shape
    tr = _tile(S, ROW_TILE)
    has_res = res is not None

    def body(*refs):
        x_ref, g_ref, dy_ref = refs[:3]
        ins = list(refs[3:])
        res_ref = ins.pop(0) if has_res else None
        x2_ref, g2_ref = (ins.pop(0), ins.pop(0)) if then else (None, None)
        dx_ref, dg_ref = ins[:2]
        dx, dgr = _rms_bwd_math(x_ref[...], g_ref[...], dy_ref[...])
        if has_res:
            dx = dx + res_ref[...]
        dx_ref[...] = dx.astype(dx_ref.dtype)
        i = pl.program_id(0)

        @pl.when(i == 0)
        def _():
            dg_ref[...] = jnp.zeros_like(dg_ref)
            if then:
                ins[3][...] = jnp.zeros_like(ins[3])

        dg_ref[...] += jnp.sum(dgr, axis=0, keepdims=True)
        if then:
            d2, dgr2 = _rms_bwd_math(x2_ref[...], g2_ref[...], dx)
            ins[2][...] = d2.astype(ins[2].dtype)
            ins[3][...] += jnp.sum(dgr2, axis=0, keepdims=True)

    row = pl.BlockSpec((tr, D), lambda i: (i, 0))
    vec = pl.BlockSpec((1, D), lambda i: (0, 0))
    operands = [xin, g, dy] + ([res] if has_res else []) + (list(then) if then else [])
    in_specs = [row, vec, row] + ([row] if has_res else []) + ([row, vec] if then else [])
    n_out = 2 if then else 1
    shapes = (_sds((S, D), dx_dtype), _sds((1, D), F32), _sds((S, D), then_dtype), _sds((1, D), F32))[:2 * n_out]
    return _call(body, name=name, out_shape=shapes, grid=(S // tr,),
                 in_specs=in_specs, out_specs=(row, vec) * n_out, dims=("arbitrary",))(*operands)


def _pool_counts(t0, rows):
    return t0 + lax.broadcasted_iota(jnp.int32, (rows, 1), 0)


def _pool_fwd(x, g, name):
    S, D = x.shape
    gc = D // len(POOL_WINDOWS)
    tp = _tile(S, ROW_TILE)

    def body(x_ref, g_ref, d_ref, ext_ref):
        i = pl.program_id(0)

        @pl.when(i == 0)
        def _():
            ext_ref[pl.ds(0, POOL_HALO), :] = jnp.zeros((POOL_HALO, D), F32)

        xv = x_ref[...]
        ext_ref[pl.ds(POOL_HALO, tp), :] = xv * _rstd(xv) * g_ref[...]
        t = _pool_counts(i * tp, tp)
        for gi, w in enumerate(POOL_WINDOWS):
            cols = slice(gi * gc, (gi + 1) * gc)
            s = ext_ref[:, cols]
            h = s[POOL_HALO:]
            sh = 1
            while sh < w:
                s = s + pltpu.roll(s, sh, 0)
                sh *= 2
            cnt = jnp.minimum(t + 1, w).astype(F32)
            d_ref[:, cols] = (s[POOL_HALO:] / cnt - h).astype(d_ref.dtype)
        ext_ref[pl.ds(0, POOL_HALO), :] = ext_ref[pl.ds(tp, POOL_HALO), :]

    row = pl.BlockSpec((tp, D), lambda i: (i, 0))
    vec = pl.BlockSpec((1, D), lambda i: (0, 0))
    return _call(body, name=name, out_shape=_sds((S, D), BF16), grid=(S // tp,), in_specs=[row, vec],
                 out_specs=row, scratch_shapes=[pltpu.VMEM((tp + POOL_HALO, D), F32)], dims=("arbitrary",))(x, g)


def _pool_mm_fwd(d, wp, scale, x, gpost, gnext, name):
    S, D = x.shape
    ng = len(POOL_WINDOWS)
    gc = D // ng
    tp = _tile(S, ROW_TILE)

    def body(d_ref, w_ref, sc_ref, x_ref, g_ref, gn_ref, y_ref, o_ref, h_ref):
        for gi in range(ng):
            cols = slice(gi * gc, (gi + 1) * gc)
            y_ref[:, cols] = jnp.dot(d_ref[:, cols], w_ref[gi], preferred_element_type=F32)
        m = y_ref[...] * sc_ref[...]
        xn = x_ref[...] + m * _rstd(m) * g_ref[...]
        o_ref[...] = xn
        h_ref[...] = (xn * _rstd(xn) * gn_ref[...]).astype(h_ref.dtype)

    row = pl.BlockSpec((tp, D), lambda i: (i, 0))
    vec = pl.BlockSpec((1, D), lambda i: (0, 0))
    wsp = pl.BlockSpec((ng, gc, gc), lambda i: (0, 0, 0))
    return _call(body, name=name, out_shape=(_sds((S, D), F32), _sds((S, D), F32), _sds((S, D), BF16)), grid=(S // tp,),
                 in_specs=[row, wsp, vec, row, vec, vec], out_specs=(row, row, row),
                 dims=("parallel",))(d, wp, scale, x, gpost, gnext)


def _pool_mm_bwd(dx, y, d, wp, scale, gpost, name):
    S, D = dx.shape
    ng = len(POOL_WINDOWS)
    gc = D // ng
    tp = _tile(S, ROW_TILE)

    def body(dx_ref, y_ref, d_ref, w_ref, sc_ref, g_ref, dd_ref, dw_ref, dsc_ref, dg_ref):
        i = pl.program_id(0)

        @pl.when(i == 0)
        def _():
            dw_ref[...] = jnp.zeros_like(dw_ref)
            dsc_ref[...] = jnp.zeros_like(dsc_ref)
            dg_ref[...] = jnp.zeros_like(dg_ref)

        yv = y_ref[...]
        sc = sc_ref[...]
        dm, dgr = _rms_bwd_math(yv * sc, g_ref[...], dx_ref[...])
        dg_ref[...] += jnp.sum(dgr, axis=0, keepdims=True)
        dsc_ref[...] += jnp.sum(dm * yv, axis=0, keepdims=True)
        dyv = (dm * sc).astype(BF16)
        for gi in range(ng):
            cols = slice(gi * gc, (gi + 1) * gc)
            dyg = dyv[:, cols]
            dd_ref[:, cols] = lax.dot_general(dyg, w_ref[gi], (((1,), (1,)), ((), ())), preferred_element_type=F32)
            dw_ref[gi] += lax.dot_general(d_ref[:, cols], dyg, (((0,), (0,)), ((), ())), preferred_element_type=F32)

    row = pl.BlockSpec((tp, D), lambda i: (i, 0))
    vec = pl.BlockSpec((1, D), lambda i: (0, 0))
    wsp = pl.BlockSpec((ng, gc, gc), lambda i: (0, 0, 0))
    dwsp = pl.BlockSpec((ng, gc, gc), lambda i: (0, 0, 0))
    return _call(body, name=name,
                 out_shape=(_sds((S, D), F32), _sds((ng, gc, gc), F32), _sds((1, D), F32), _sds((1, D), F32)),
                 grid=(S // tp,), in_specs=[row, row, row, wsp, vec, vec], out_specs=(row, dwsp, vec, vec),
                 dims=("arbitrary",))(dx, y, d, wp, scale, gpost)


def _pool_bwd(dd, x, g, res, name):
    S, D = x.shape
    gc = D // len(POOL_WINDOWS)
    tp = _tile(S, ROW_TILE)
    nt = S // tp

    def body(dd_ref, x_ref, g_ref, res_ref, dx_ref, dg_ref, ext_ref, dh_ref):
        i = pl.program_id(0)

        @pl.when(i == 0)
        def _():
            ext_ref[pl.ds(tp, POOL_HALO), :] = jnp.zeros((POOL_HALO, D), F32)
            dg_ref[...] = jnp.zeros_like(dg_ref)

        t = _pool_counts((nt - 1 - i) * tp, tp)
        for gi, w in enumerate(POOL_WINDOWS):
            cols = slice(gi * gc, (gi + 1) * gc)
            ddv = dd_ref[:, cols]
            ext_ref[pl.ds(0, tp), cols] = ddv / jnp.minimum(t + 1, w).astype(F32)
            s = ext_ref[:, cols]
            sh = 1
            while sh < w:
                s = s + pltpu.roll(s, tp + POOL_HALO - sh, 0)
                sh *= 2
            dh_ref[:, cols] = s[:tp] - ddv
        ext_ref[pl.ds(tp, POOL_HALO), :] = ext_ref[pl.ds(0, POOL_HALO), :]
        dx, dgr = _rms_bwd_math(x_ref[...], g_ref[...], dh_ref[...])
        dx_ref[...] = dx + res_ref[...]
        dg_ref[...] += jnp.sum(dgr, axis=0, keepdims=True)

    row = pl.BlockSpec((tp, D), lambda i: (nt - 1 - i, 0))
    vec = pl.BlockSpec((1, D), lambda i: (0, 0))
    return _call(body, name=name, out_shape=(_sds((S, D), F32), _sds((1, D), F32)), grid=(nt,),
                 in_specs=[row, row, vec, row], out_specs=(row, vec),
                 scratch_shapes=[pltpu.VMEM((tp + POOL_HALO, D), F32), pltpu.VMEM((tp, D), F32)],
                 dims=("arbitrary",))(dd, x, g, res)


def _gelu(x):
    return 0.5 * x * (1.0 + jnp.tanh(0.7978845608028654 * (x + 0.044715 * x * x * x)))


def _gelu_grad(x):
    th = jnp.tanh(0.7978845608028654 * (x + 0.044715 * x * x * x))
    return 0.5 * (1.0 + th) + 0.5 * x * (1.0 - th * th) * 0.7978845608028654 * (1.0 + 3.0 * 0.044715 * x * x)


def _conv_taps(ext_ref, cols, tt):
    return tuple(ext_ref[pl.ds(CONV_HALO - k, tt), cols] for k in range(3))


def _conv_glu_fwd(u, cw, cb, name):
    S, F2 = u.shape
    wd = F2 // 4
    tt = _tile(S, ROW_TILE // 2)

    def body(u_ref, cw_ref, cb_ref, a_ref, act_ref, ext_ref):
        it = pl.program_id(1)

        @pl.when(it == 0)
        def _():
            ext_ref[pl.ds(0, CONV_HALO), :] = jnp.zeros((CONV_HALO, 2 * wd), F32)

        ext_ref[pl.ds(CONV_HALO, tt), :] = u_ref[...]
        for cc in range(wd // LANES):
            act = []
            for half in range(2):
                cols = slice(half * wd + cc * LANES, half * wd + (cc + 1) * LANES)
                u0, u1, u2 = _conv_taps(ext_ref, cols, tt)
                act.append(cw_ref[2:3, cols] * u0 + cw_ref[1:2, cols] * u1 + cw_ref[0:1, cols] * u2 + cb_ref[:, cols])
                act_ref[:, cols] = act[half]
            a_ref[:, cc * LANES:(cc + 1) * LANES] = (_gelu(act[0]) * act[1]).astype(a_ref.dtype)
        ext_ref[pl.ds(0, CONV_HALO), :] = ext_ref[pl.ds(tt, CONV_HALO), :]

    wide = pl.BlockSpec((tt, 2 * wd), lambda h, t: (t, h))
    return _call(body, name=name, out_shape=(_sds((S, F2 // 2), BF16), _sds((S, F2), F32)), grid=(2, S // tt),
                 in_specs=[wide, pl.BlockSpec((8, 2 * wd), lambda h, t: (0, h)), pl.BlockSpec((1, 2 * wd), lambda h, t: (0, h))],
                 out_specs=(pl.BlockSpec((tt, wd), lambda h, t: (t, h)), wide),
                 scratch_shapes=[pltpu.VMEM((tt + CONV_HALO, 2 * wd), F32)], dims=("parallel", "arbitrary"))(u, cw, cb)


def _conv_glu_bwd(u, act, da, cw, name):
    S, F2 = u.shape
    wd = F2 // 4
    tt = _tile(S, ROW_TILE)
    nt = S // tt
    n = tt + CONV_HALO

    def body(u_ref, act_ref, da_ref, cw_ref, du_ref, acc_ref, carry_ref):
        it = pl.program_id(1)

        @pl.when(it == 0)
        def _():
            carry_ref[...] = jnp.zeros_like(carry_ref)
            acc_ref[...] = jnp.zeros_like(acc_ref)

        for cc in range(wd // LANES):
            gate = act_ref[:, cc * LANES:(cc + 1) * LANES]
            val = act_ref[:, wd + cc * LANES:wd + (cc + 1) * LANES]
            dav = da_ref[:, cc * LANES:(cc + 1) * LANES]
            dact = (dav * val * _gelu_grad(gate), dav * _gelu(gate))
            for half in range(2):
                cols = slice(half * wd + cc * LANES, half * wd + (cc + 1) * LANES)
                d0 = dact[half]
                e = jnp.concatenate([d0, carry_ref[:, cols]], axis=0)
                d1, d2 = pltpu.roll(e, n - 1, 0)[:tt], pltpu.roll(e, n - 2, 0)[:tt]
                du_ref[:, cols] = (cw_ref[2:3, cols] * d0 + cw_ref[1:2, cols] * d1 + cw_ref[0:1, cols] * d2).astype(du_ref.dtype)
                u0 = u_ref[:, cols]
                acc_ref[2:3, cols] += jnp.sum(d0 * u0, axis=0, keepdims=True)
                acc_ref[1:2, cols] += jnp.sum(d1 * u0, axis=0, keepdims=True)
                acc_ref[0:1, cols] += jnp.sum(d2 * u0, axis=0, keepdims=True)
                acc_ref[3:4, cols] += jnp.sum(d0, axis=0, keepdims=True)
                carry_ref[:, cols] = d0[:CONV_HALO]

    wide = pl.BlockSpec((tt, 2 * wd), lambda h, t: (nt - 1 - t, h))
    acc = pl.BlockSpec((8, 2 * wd), lambda h, t: (0, h))
    return _call(body, name=name, out_shape=(_sds((S, F2), BF16), _sds((8, F2), F32)), grid=(2, nt),
                 in_specs=[wide, wide, pl.BlockSpec((tt, wd), lambda h, t: (nt - 1 - t, h)), acc],
                 out_specs=(wide, acc), scratch_shapes=[pltpu.VMEM((CONV_HALO, 2 * wd), F32)],
                 dims=("parallel", "arbitrary"))(u, act, da, cw)


def _rope_chunk(x, cosv, sinv):
    lane = lax.broadcasted_iota(jnp.int32, x.shape, 1)
    partner = jnp.where(lane % HEAD_DIM < HEAD_DIM // 2, pltpu.roll(x, LANES - HEAD_DIM // 2, 1),
                        pltpu.roll(x, HEAD_DIM // 2, 1))
    return x * cosv + partner * sinv


def _rope(x, width, cos_t, sin_t, name):
    S = x.shape[0]
    tr = _tile(S, ROW_TILE)

    def body(x_ref, c_ref, s_ref, o_ref):
        for cc in range(width // LANES):
            cols = slice(cc * LANES, (cc + 1) * LANES)
            o_ref[:, cols] = _rope_chunk(x_ref[:, cols], c_ref[...], s_ref[...])

    row = pl.BlockSpec((tr, width), lambda i: (i, 0))
    tab = pl.BlockSpec((tr, LANES), lambda i: (i, 0))
    return _call(body, name=name, out_shape=_sds((S, width), F32), grid=(S // tr,), in_specs=[row, tab, tab],
                 out_specs=row, dims=("parallel",))(x, cos_t, sin_t)


def _attn_mask(n, reps):
    row = lax.broadcasted_iota(jnp.int32, (reps * BLOCK, 2 * BLOCK), 0) & (BLOCK - 1)
    col = lax.broadcasted_iota(jnp.int32, (reps * BLOCK, 2 * BLOCK), 1)
    rel = BLOCK + row - col
    return (rel >= 0) & (rel < WINDOW) & (n * BLOCK + col - BLOCK >= 0)


def _per_head_column(values, reps):
    grp = lax.broadcasted_iota(jnp.int32, (reps * BLOCK, 1), 0) // BLOCK
    col = jnp.zeros((reps * BLOCK, 1), F32)
    for g, v in enumerate(values):
        col = jnp.where(grp == g, v, col)
    return col


def _stack_heads(ref, heads, qpk, lane, scale):
    parts = []
    for h in heads:
        qc, qpar, _, kpar = _head_place(h, qpk)
        x = ref[:, qc * LANES:(qc + 1) * LANES]
        if scale != 1.0:
            x = x * scale
        if qpar != kpar:
            x = pltpu.roll(x, HEAD_DIM, 1)
        keep = (lane >= kpar * HEAD_DIM) & (lane < (kpar + 1) * HEAD_DIM)
        parts.append(jnp.where(keep, x, 0.0).astype(BF16))
    return jnp.concatenate(parts, axis=0)


def _unstack_heads(vals, ref, heads, qpk, lane, dtype):
    pair = None
    for g, h in enumerate(heads):
        qc, qpar, _, kpar = _head_place(h, qpk)
        v = vals[g * BLOCK:(g + 1) * BLOCK]
        if qpar != kpar:
            v = pltpu.roll(v, HEAD_DIM, 1)
        if qpar == 0:
            pair = v
        else:
            ref[:, qc * LANES:(qc + 1) * LANES] = jnp.where(lane < HEAD_DIM, pair, v).astype(dtype)


def _head_place(h, qpk):
    hk = h // qpk
    return h // 2, h % 2, hk // 2, hk % 2


def _attn_specs(S, D):
    nb = S // BLOCK
    kvw = KV_DIM // 2
    qsp = pl.BlockSpec((BLOCK, D), lambda n: (n, 0))
    prev = lambda n: jnp.maximum(n - 1, 0)
    kp = pl.BlockSpec((BLOCK, kvw), lambda n: (prev(n), 0))
    ko = pl.BlockSpec((BLOCK, kvw), lambda n: (n, 0))
    vp = pl.BlockSpec((BLOCK, kvw), lambda n: (prev(n), 1))
    vo = pl.BlockSpec((BLOCK, kvw), lambda n: (n, 1))
    stat = pl.BlockSpec((BLOCK, LANES), lambda n: (n, 0))
    smem = pl.BlockSpec(memory_space=pltpu.SMEM)
    return nb, kvw, qsp, kp, ko, vp, vo, stat, smem


def _attn_fwd(q, k, kv, sinks, name):
    S, D = q.shape
    nh = D // HEAD_DIM
    qpk = nh // N_KV_HEADS
    nb, kvw, qsp, kp, ko, vp, vo, stat, smem = _attn_specs(S, D)

    def body(q_ref, kp_ref, ko_ref, vp_ref, vo_ref, s_ref, o_ref, l_ref):
        n = pl.program_id(0)
        valid = _attn_mask(n, 2 * qpk)
        lane = lax.broadcasted_iota(jnp.int32, (BLOCK, LANES), 1)
        lacc = jnp.zeros((BLOCK, LANES), F32)
        for kc in range(N_KV_HEADS // 2):
            heads = list(range(2 * kc * qpk, 2 * (kc + 1) * qpk))
            kcols = slice(kc * LANES, (kc + 1) * LANES)
            k2 = jnp.concatenate([kp_ref[:, kcols], ko_ref[:, kcols]], axis=0).astype(BF16)
            v2 = jnp.concatenate([vp_ref[:, kcols], vo_ref[:, kcols]], axis=0).astype(BF16)
            qm = _stack_heads(q_ref, heads, qpk, lane, ATTN_SCALE)
            s = lax.dot_general(qm, k2, (((1,), (1,)), ((), ())), preferred_element_type=F32)
            s = jnp.where(valid, s, NEG_INF)
            sink = _per_head_column([s_ref[h] for h in heads], len(heads))
            m = jnp.maximum(jnp.max(s, axis=1, keepdims=True), sink)
            p = jnp.exp(s - m)
            den = jnp.sum(p, axis=1, keepdims=True) + jnp.exp(sink - m)
            of = jnp.dot(p.astype(BF16), v2, preferred_element_type=F32) / den
            lse = m + jnp.log(den)
            for g, h in enumerate(heads):
                lacc = jnp.where(lane == h, lse[g * BLOCK:(g + 1) * BLOCK], lacc)
            _unstack_heads(of, o_ref, heads, qpk, lane, o_ref.dtype)
        l_ref[...] = lacc

    return _call(body, name=name, out_shape=(_sds((S, D), BF16), _sds((S, LANES), F32)), grid=(nb,),
                 in_specs=[qsp, kp, ko, vp, vo, smem], out_specs=(qsp, stat), dims=("parallel",))(q, k, k, kv, kv, sinks)


def _attn_bwd(q, k, kv, do, lse, sinks, name):
    S, D = q.shape
    nh = D // HEAD_DIM
    qpk = nh // N_KV_HEADS
    nb, kvw, qsp, kp, ko, vp, vo, stat, smem = _attn_specs(S, D)

    def body(q_ref, kp_ref, ko_ref, vp_ref, vo_ref, do_ref, l_ref, s_ref,
             dq_ref, dkp_ref, dko_ref, dvp_ref, dvo_ref, ds_ref):
        n = pl.program_id(0)

        @pl.when(n == 0)
        def _():
            ds_ref[...] = jnp.zeros_like(ds_ref)

        valid = _attn_mask(n, 2 * qpk)
        lane = lax.broadcasted_iota(jnp.int32, (BLOCK, LANES), 1)
        lane8 = lax.broadcasted_iota(jnp.int32, (8, LANES), 1)
        lv = l_ref[...]
        dsink = jnp.zeros((8, LANES), F32)
        for kc in range(N_KV_HEADS // 2):
            heads = list(range(2 * kc * qpk, 2 * (kc + 1) * qpk))
            kcols = slice(kc * LANES, (kc + 1) * LANES)
            k2 = jnp.concatenate([kp_ref[:, kcols], ko_ref[:, kcols]], axis=0).astype(BF16)
            v2 = jnp.concatenate([vp_ref[:, kcols], vo_ref[:, kcols]], axis=0).astype(BF16)
            qm = _stack_heads(q_ref, heads, qpk, lane, ATTN_SCALE)
            gm = _stack_heads(do_ref, heads, qpk, lane, 1.0)
            s = lax.dot_general(qm, k2, (((1,), (1,)), ((), ())), preferred_element_type=F32)
            lh = jnp.concatenate([jnp.sum(jnp.where(lane == h, lv, 0.0), axis=1, keepdims=True) for h in heads], axis=0)
            p = jnp.where(valid, jnp.exp(s - lh), 0.0)
            dp = lax.dot_general(gm, v2, (((1,), (1,)), ((), ())), preferred_element_type=F32)
            delta = jnp.sum(p * dp, axis=1, keepdims=True)
            dsb = (p * (dp - delta)).astype(BF16)
            lost = jnp.exp(_per_head_column([s_ref[h] for h in heads], len(heads)) - lh) * delta
            for g, h in enumerate(heads):
                dsink = dsink - jnp.where(lane8 == h, jnp.sum(lost[g * BLOCK:(g + 1) * BLOCK]), 0.0)
            dqf = jnp.dot(dsb, k2, preferred_element_type=F32) * ATTN_SCALE
            _unstack_heads(dqf, dq_ref, heads, qpk, lane, F32)
            dk2 = lax.dot_general(dsb, qm, (((0,), (0,)), ((), ())), preferred_element_type=F32)
            dv2 = lax.dot_general(p.astype(BF16), gm, (((0,), (0,)), ((), ())), preferred_element_type=F32)
            dkp_ref[:, kcols] = dk2[:BLOCK]
            dko_ref[:, kcols] = dk2[BLOCK:]
            dvp_ref[:, kcols] = dv2[:BLOCK]
            dvo_ref[:, kcols] = dv2[BLOCK:]
        ds_ref[...] += dsink

    kvo = pl.BlockSpec((BLOCK, kvw), lambda n: (n, 0))
    acc = pl.BlockSpec((8, LANES), lambda n: (0, 0))
    part = _sds((S, kvw), F32)
    return _call(body, name=name, out_shape=(_sds((S, D), F32), part, part, part, part, _sds((8, LANES), F32)),
                 grid=(nb,), in_specs=[qsp, kp, ko, vp, vo, qsp, stat, smem],
                 out_specs=(qsp, kvo, kvo, kvo, kvo, acc), dims=("arbitrary",))(q, k, k, kv, kv, do, lse, sinks)


def _kv_grad(parts, cos_t, sin_neg_t, name):
    S, kvw = parts[0][0].shape
    nb = S // BLOCK
    flat = [a for p in parts for a in p]
    nl = len(parts)

    def body(*refs):
        c_ref, s_ref, o_ref = refs[4 * nl], refs[4 * nl + 1], refs[4 * nl + 2]
        n = pl.program_id(0)
        last = n == nb - 1
        dk = jnp.zeros((BLOCK, kvw), F32)
        dv = jnp.zeros((BLOCK, kvw), F32)
        for li in range(nl):
            kn, kown, vn, vown = refs[4 * li:4 * li + 4]
            dk = dk + kown[...] + jnp.where(last, 0.0, kn[...])
            dv = dv + vown[...] + jnp.where(last, 0.0, vn[...])
        for cc in range(kvw // LANES):
            cols = slice(cc * LANES, (cc + 1) * LANES)
            o_ref[:, cols] = _rope_chunk(dk[:, cols], c_ref[...], s_ref[...])
        o_ref[:, kvw:] = dv

    own = pl.BlockSpec((BLOCK, kvw), lambda n: (n, 0))
    nxt = pl.BlockSpec((BLOCK, kvw), lambda n: (jnp.minimum(n + 1, nb - 1), 0))
    tab = pl.BlockSpec((BLOCK, LANES), lambda n: (n, 0))
    return _call(body, name=name, out_shape=_sds((S, 2 * kvw), F32), grid=(nb,),
                 in_specs=[nxt, own, nxt, own] * nl + [tab, tab],
                 out_specs=pl.BlockSpec((BLOCK, 2 * kvw), lambda n: (n, 0)), dims=("parallel",))(*flat, cos_t, sin_neg_t)


def _sum_blocks(name, qc, grid, out_shape, out_block, out_imap, ins, out_dtype=F32, into=None):
    nin = len(ins)

    def body(qc_ref, *refs):
        acc = refs[0][...].astype(F32)
        for r in refs[1:nin]:
            acc = acc + r[...].astype(F32)
        refs[-1][...] = acc.astype(refs[-1].dtype)

    in_specs = [pl.BlockSpec(b, m) for _, b, m in ins]
    operands = [a for a, _, _ in ins]
    aliases = None
    if into is not None:
        in_specs.append(ANY)
        operands.append(into)
        aliases = {1 + nin: 0}
    gs = pltpu.PrefetchScalarGridSpec(num_scalar_prefetch=1, grid=grid, in_specs=in_specs,
                                      out_specs=pl.BlockSpec(out_block, out_imap))
    return _call(body, name=name, out_shape=_sds(out_shape, out_dtype), grid_spec=gs,
                 dims=("parallel",) * len(grid), aliases=aliases)(qc, *operands)


def _adamw(w, g, m, v, name, part=None, into=None):
    shape = w.shape
    C = shape[-1]
    R = w.size // C
    k, cnt, nparts = part if part is not None else (0, 1, 1)
    tr = _tile(R // nparts, max(8, BLOCK_ELEMS //C))
    first = k * (R // nparts // tr)
    rows = cnt * (R // nparts)

    def body(w_ref, g_ref, m_ref, v_ref, *outs):
        go_ref, d_ref, nm_ref, nv_ref = outs[-4:]
        gv = g_ref[...]
        nm = ADAM_B1 * m_ref[...] + (1.0 - ADAM_B1) * gv
        nv = ADAM_B2 * v_ref[...] + (1.0 - ADAM_B2) * (gv * gv)
        m_hat = nm / (1.0 - ADAM_B1 ** ADAM_STEP)
        v_hat = nv / (1.0 - ADAM_B2 ** ADAM_STEP)
        go_ref[...] = gv
        d_ref[...] = -ADAM_LR * (m_hat / (jnp.sqrt(v_hat) + ADAM_EPS) + ADAM_WD * w_ref[...])
        nm_ref[...] = nm
        nv_ref[...] = nv

    blk = pl.BlockSpec((tr, C), lambda i: (first + i, 0))
    flat = _sds((R, C), F32)
    operands = [a.reshape(-1, C) for a in (w, g, m, v)]
    in_specs, aliases = [blk] * 4, None
    if g.size != w.size:
        assert g.size == rows * C, (name, g.shape, shape, part)
        in_specs[1] = pl.BlockSpec((tr, C), lambda i: (i, 0))
    if into is not None:
        operands += [a.reshape(R, C) for a in into]
        in_specs = in_specs + [ANY] * 4
        aliases = {4 + i: i for i in range(4)}
    outs = _call(body, name=name, out_shape=(flat,) * 4, grid=(rows // tr,), in_specs=in_specs,
                 out_specs=(blk,) * 4, dims=("parallel",), aliases=aliases)(*operands)
    return tuple(o.reshape(shape) for o in outs)


def _place():
    x, y, c = lax.axis_index("x"), lax.axis_index("y"), lax.axis_index("c")
    chips = [(1 - x, y), (x, 1 - y), (1 - x, 1 - y)]
    return x, y, c, chips


def _at(ref, nd, dims):
    idx = [slice(None)] * nd
    for d, v in dims.items():
        idx[d] = pl.ds(v[0], v[1]) if isinstance(v, tuple) else v
    return ref.at[tuple(idx)]


def _remote(src, dst, send_sem, recv_sem, dev):
    return pltpu.make_async_remote_copy(src_ref=src, dst_ref=dst, send_sem=send_sem, recv_sem=recv_sem,
                                        device_id=dev, device_id_type=MESH)


def _split_call(body, name, out_shape, in_specs, out_specs, aliases):
    return pl.pallas_call(body, name=name, out_shape=out_shape, in_specs=in_specs, out_specs=out_specs,
                          input_output_aliases=aliases,
                          compiler_params=pltpu.CompilerParams(has_side_effects=EFFECT))


def _hbm(a):
    return pltpu.with_memory_space_constraint(a, pltpu.HBM)


def _copies_start(srcs, lands, after, name, plan, ncopies):
    n, m = len(srcs), len(lands)

    def body(*refs):
        src, land = refs[:n], refs[n:n + m]
        send_sems, recv_sems, token = refs[n + m + 1], refs[n + m + 2], refs[-1]
        x, y, c, chips = _place()
        for k, (s, d, dev) in enumerate(plan(x, y, c, chips, src, land)):
            _remote(s, d, send_sems.at[k], recv_sems.at[k], dev).start()
        token[...] = jnp.zeros_like(token)

    thru = tuple(pltpu.HBM(a.shape, a.dtype) for a in list(srcs) + list(lands))
    outs = _split_call(
        body, name,
        out_shape=(pltpu.SemaphoreType.DMA((ncopies,)), pltpu.SemaphoreType.DMA((ncopies,))) + thru + (_sds((8, LANES), F32),),
        in_specs=(HBM,) * (n + m) + (ANY,), out_specs=(SEM, SEM) + (HBM,) * (n + m) + (VMEM,),
        aliases={i: 2 + i for i in range(n + m)})(*[_hbm(a) for a in srcs], *[_hbm(a) for a in lands], after)
    return dict(send=outs[0], recv=outs[1], srcs=outs[2:2 + n], lands=outs[2 + n:2 + n + m], token=outs[-1])


def _copies_wait(handle, after, name, plan):
    srcs, lands = handle['srcs'], handle['lands']
    n, m = len(srcs), len(lands)

    def body(*refs):
        src, land = refs[:n], refs[n:n + m]
        send_sems, recv_sems = refs[n + m], refs[n + m + 1]
        x, y, c, chips = _place()
        for k, (s, d, dev) in enumerate(plan(x, y, c, chips, src, land)):
            cp = _remote(s, d, send_sems.at[k], recv_sems.at[k], dev)
            cp.wait_send()
            cp.wait_recv()

    thru = tuple(pltpu.HBM(a.shape, a.dtype) for a in list(srcs) + list(lands))
    outs = _split_call(body, name, out_shape=thru, in_specs=(HBM,) * (n + m) + (SEM, SEM, ANY),
                       out_specs=(HBM,) * (n + m), aliases={i: i for i in range(n + m)})(
        *srcs, *lands, handle['send'], handle['recv'], after)
    return outs[:n], outs[n:]


def _gather_plan(x, y, c, chips, src, land, arriving):
    q = 2 * x + y
    peers = [(ch[0], ch[1], c) for ch in chips] + [(x, y, 1 - c)]
    slots = [2 * ch[0] + ch[1] for ch in chips] + [q]
    return [(s, d.at[slots[j] if arriving else q], peers[j]) for s, d in zip(src, land) for j in range(4)]


def _gather_half_plan(x, y, c, chips, src, land, arriving):
    q = 2 * x + y
    out = []
    for s, d in zip(src, land):
        hs = s.shape[0] // 2
        rows = pl.ds(c * hs, hs)
        for ch in chips:
            out.append((s.at[rows], d.at[2 * ch[0] + ch[1] if arriving else q, rows], (ch[0], ch[1], c)))
        out.append((s, d.at[q], (x, y, 1 - c)))
    return out


def _exchange_plan(x, y, c, chips, src, land, arriving):
    out = []
    for d in land:
        hs = d.shape[1] // 2
        for ch in chips:
            slot = 2 * ch[0] + ch[1]
            out.append((d.at[slot, pl.ds(c * hs, hs)], d.at[slot, pl.ds(((1 - c) if arriving else c) * hs, hs)], (x, y, 1 - c)))
    return out


def _scatter_plan(shard_axes, shapes):
    def plan(x, y, c, chips, src, land):
        out = []
        for s, d, sd, shp in zip(src, land, shard_axes, shapes):
            ss = shp[sd] // N_SHARDS
            for j, ch in enumerate(chips):
                out.append((_at(s, len(shp), {sd: ((2 * ch[0] + ch[1]) * ss, ss)}), d.at[j], (ch[0], ch[1], c)))
        return out
    return plan


def _half_dims(shape, hd, c):
    hs = shape[hd] // 2
    return {hd: (c * hs, hs)}


def _swap_halves(grads, specs, name):
    n = len(grads)
    outs_shape = []
    for a, (sd, hd) in zip(grads, specs):
        shp = list(a.shape)
        shp[hd] //= 2
        outs_shape.append(_sds(shp, F32))

    def body(*refs):
        ins, outs = refs[:n], refs[n:2 * n]
        send_sems, recv_sems = refs[2 * n:]
        x, y, c, _ = _place()
        cps = []
        for ai, (sd, hd) in enumerate(specs):
            shp = grads[ai].shape
            cp = _remote(_at(ins[ai], len(shp), _half_dims(shp, hd, 1 - c)), outs[ai],
                         send_sems.at[ai], recv_sems.at[ai], (x, y, 1 - c))
            cp.start()
            cps.append(cp)
        for cp in cps:
            cp.wait()

    return _call(body, name=name, out_shape=tuple(outs_shape), in_specs=[ANY] * n, out_specs=tuple([ANY] * n),
                 scratch_shapes=[pltpu.SemaphoreType.DMA((n,)), pltpu.SemaphoreType.DMA((n,))])(*grads)


def _share_halves(arrs, half_axes, layers, after, name):
    n = len(arrs)

    def body(*refs):
        ins, outs = refs[:n], refs[n + 1:2 * n + 1]
        send_sems, recv_sems = refs[2 * n + 1:]
        x, y, c, _ = _place()

        def half(ref, ai, which):
            shp = arrs[ai].shape
            hs = shp[half_axes[ai]] // 2
            dims = {half_axes[ai]: (which * hs, hs)}
            if layers[ai] is not None:
                dims[0] = layers[ai]
            return _at(ref, len(shp), dims)

        sends = []
        for ai in range(n):
            cp = _remote(half(ins[ai], ai, c), half(outs[ai], ai, c), send_sems.at[ai], recv_sems.at[ai], (x, y, 1 - c))
            cp.start()
            sends.append(cp)
        for ai in range(n):
            land = half(outs[ai], ai, 1 - c)
            _remote(land, land, send_sems.at[ai], recv_sems.at[ai], (x, y, c)).wait_recv()
        for cp in sends:
            cp.wait_send()

    return _call(body, name=name, out_shape=tuple(_sds(a.shape, a.dtype) for a in arrs), in_specs=[ANY] * (n + 1),
                 out_specs=tuple([ANY] * n), aliases={i: i for i in range(n)},
                 scratch_shapes=[pltpu.SemaphoreType.DMA((n,)), pltpu.SemaphoreType.DMA((n,))])(*arrs, after)


def _gather_small(v, name):
    R, C = v.shape

    def body(x_ref, out_ref, send_sems, recv_sems, local_sem):
        x, y, c, chips = _place()
        me, sibling = (x, y, c), (x, y, 1 - c)

        def rows(px, py, pc):
            return out_ref.at[pl.ds((4 * px + 2 * py + pc) * R, R), :]

        def copy(k, block, to, src=None):
            return _remote(rows(*block) if src is None else src, rows(*block), send_sems.at[k], recv_sems.at[k], to)

        mine = pltpu.make_async_copy(x_ref, rows(*me), local_sem)
        mine.start()
        first = [copy(0, me, sibling, src=x_ref)]
        first += [copy(1 + j, me, (ch[0], ch[1], c), src=x_ref) for j, ch in enumerate(chips)]
        for cp in first:
            cp.start()
        passed = [copy(4 + j, (ch[0], ch[1], c), sibling) for j, ch in enumerate(chips)]
        for j, ch in enumerate(chips):
            copy(1 + j, (ch[0], ch[1], c), me).wait_recv()
            passed[j].start()
        copy(0, sibling, me).wait_recv()
        for j, ch in enumerate(chips):
            copy(4 + j, (ch[0], ch[1], 1 - c), me).wait_recv()
        for cp in first + passed:
            cp.wait_send()
        mine.wait()

    vm = pl.BlockSpec(memory_space=pltpu.VMEM)
    return _call(body, name=name, out_shape=_sds((8 * R, C), v.dtype), in_specs=[vm], out_specs=vm,
                 scratch_shapes=[pltpu.SemaphoreType.DMA((7,)), pltpu.SemaphoreType.DMA((7,)),
                                 pltpu.SemaphoreType.DMA])(v)


def _sum8(g, name):
    _, R, C = g.shape

    def body(g_ref, o_ref):
        acc = g_ref[0]
        for d in range(1, 8):
            acc = acc + g_ref[d]
        o_ref[...] = acc

    return _call(body, name=name, out_shape=_sds((R, C), F32), in_specs=[pl.BlockSpec(memory_space=pltpu.VMEM)],
                 out_specs=pl.BlockSpec(memory_space=pltpu.VMEM))(g)


def _rope_tables(positions):
    inv_freq = 1.0 / (ROPE_THETA ** (jnp.arange(0, HEAD_DIM, 2, dtype=F32) / HEAD_DIM))
    ang = positions.astype(F32)[:, None] * inv_freq
    cosv, sinv = jnp.cos(ang), jnp.sin(ang)
    return jnp.tile(cosv, (1, 4)), jnp.tile(jnp.concatenate([-sinv, sinv], axis=1), (1, 2))


def _blocked(a):
    parts = jnp.split(a, 4, axis=-1)
    return jnp.concatenate([parts[0], parts[2], parts[1], parts[3]], axis=-1)


def _arrived(Wl, name, after):
    if callable(Wl[name]):
        Wl[name] = Wl[name](after)
    return Wl[name]


def _local_step(x, target, positions, P, weights_of, ffn_grads_done, grads_done):
    S, D = x.shape
    depth = P['mix_pre_g'].shape[0]
    n_a = depth // 2
    cos_t, sin_t = _rope_tables(positions)
    row = lambda a, l: a[l][None]
    cb = [_blocked(P['ffn_conv_b'][l])[None] for l in range(depth)]
    sv, W = {}, {}
    kv = k_rot = None
    pre = []
    for l in range(depth):
        t = f"l{l}"
        W[l], zero = weights_of(l, x)
        sv[l, 'x_in'] = x
        if l < n_a:
            d = _pool_fwd(x, row(P['mix_pre_g'], l) + zero, "pool_fwd_" + t)
            y, x, h2 = _pool_mm_fwd(d, W[l]['pool_w'], W[l]['pool_scale'], x, row(P['mix_post_g'], l),
                                    row(P['ffn_pre_g'], l), "pool_mm_fwd_" + t)
            sv[l, 'd'], sv[l, 'y'] = d, y
        else:
            j = l - n_a
            if not pre:
                pre = [_rms_fwd(x, row(P['mix_pre_g'], l), BF16, "q_norm_" + t)]
                if l == n_a:
                    pre.append(_rms_fwd(x, P['kv_norm_g'][None], BF16, "kv_norm"))
            h = pre[0]
            if l == n_a:
                kv = _matmul(pre[1], W[l]['w_kv'], 'nn', F32, "kv_proj", 1024, 512, 1024)
                k_rot = _rope(kv, KV_DIM // 2, cos_t, sin_t, "k_rope")
                sv['hkv'] = pre[1]
            q = _matmul(h, W[l]['w_q'], 'nn', F32, "q_proj_" + t, 1024,1024, 1024, rope=('out', cos_t + zero, sin_t))
            o, lse = _attn_fwd(q, k_rot, kv, P['sinks'][j], "attn_fwd_" + t)
            m = _matmul(o, W[l]['w_o'], 'nn', F32, "o_proj_" + t, 1024,1024, 1024)
            x, (h2,) = _res_rms_fwd(x, m, row(P['mix_post_g'], l), "mix_post_" + t, [row(P['ffn_pre_g'], l)])
            sv[l, 'h'], sv[l, 'q'], sv[l, 'o'], sv[l, 'lse'], sv[l, 'm'] = h, q, o, lse, m
        sv[l, 'x1'] = x
        if 'pre_ffn' in W[l]:
            W[l].pop('pre_ffn')(x)
        w_in = _arrived(W[l], 'w_in', h2)
        u = _matmul(h2, w_in, 'nn', F32, "ffn_up_" + t, 1024,w_in.shape[2], 1024, b_blocks=True)
        a, sv[l, 'act'] = _conv_glu_fwd(u, W[l]['cw'], cb[l] + W[l].pop('tie', 0.0), "ffn_glu_" + t)
        f = _matmul(a, _arrived(W[l], 'w_out', a), 'nn', F32, "ffn_down_" + t, 1024,1024, 2816)
        sv[l, 'h2'], sv[l, 'u'], sv[l, 'a'], sv[l, 'f'] = h2, u, a, f
        if l + 1 == depth:
            dx, sq = _res_rms_loss(x, f, row(P['ffn_post_g'], l), target, "ffn_post_loss")
        else:
            gains = []
            if l + 1 >= n_a:
                gains = [row(P['mix_pre_g'], l + 1)] + ([P['kv_norm_g'][None]] if l + 1 == n_a else [])
            x, pre = _res_rms_fwd(x, f, row(P['ffn_post_g'], l), "ffn_post_" + t, gains)

    kv_parts = []
    zero = 0.0
    for l in reversed(range(depth)):
        t = f"l{l}"
        G = {}
        wd = W[l]['w_in'].shape[2]
        df, G['ffn_post_g'] = _rms_bwd(sv[l, 'f'], row(P['ffn_post_g'], l) + zero, dx, None, "ffn_post_bwd_" + t,
                                       dx_dtype=BF16)
        da = _matmul(df, W[l]['w_out'], 'nt', F32, "ffn_down_dx_" + t, 1024,wd, 1024)
        G['ffn_w_out'] = _matmul(sv[l, 'a'], df, 'tn', F32, "ffn_down_dw_" + t, wd, 1024, 1024)
        du, acc = _conv_glu_bwd(sv[l, 'u'], sv[l, 'act'], da, W[l]['cw'], "ffn_glu_bwd_" + t)
        G['ffn_conv_w'] = _blocked(acc[0:3])
        G['ffn_conv_b'] = _blocked(acc[3:4])
        dh2 = _matmul(du, W[l]['w_in'], 'nt', F32, "ffn_up_dx_" + t, 1024,1024, wd, b_blocks=True)
        G['ffn_w_in'] = _matmul(sv[l, 'h2'], du, 'tn', F32, "ffn_up_dw_" + t, 1024, wd, 2048, out_perm=True)
        zero = ffn_grads_done(l, G, dh2)
        if l < n_a:
            dx, G['ffn_pre_g'] = _rms_bwd(sv[l, 'x1'], row(P['ffn_pre_g'], l) + zero, dh2, dx, "ffn_norm_bwd_" + t)
        else:
            dx, G['ffn_pre_g'], dm, G['mix_post_g'] = _rms_bwd(
                sv[l, 'x1'], row(P['ffn_pre_g'], l) + zero, dh2, dx, "ffn_norm_bwd_" + t,
                then=(sv[l, 'm'], row(P['mix_post_g'], l)), then_dtype=BF16)
        if l < n_a:
            dd, G['pool_w'], G['pool_scale'], G['mix_post_g'] = _pool_mm_bwd(
                dx, sv[l, 'y'], sv[l, 'd'], W[l]['pool_w'], W[l]['pool_scale'], row(P['mix_post_g'], l), "pool_mm_bwd_" + t)
            dx, G['mix_pre_g'] = _pool_bwd(dd, sv[l, 'x_in'], row(P['mix_pre_g'], l), dx, "pool_bwd_" + t)
        else:
            j = l - n_a
            do = _matmul(dm, W[l]['w_o'], 'nt', F32, "o_proj_dx_" + t, 1024,1024, 1024)
            G['w_o'] = _matmul(sv[l, 'o'], dm, 'tn', F32, "o_proj_dw_" + t, 1024, 1024, 1024)
            dq, dkn, dko, dvn, dvo, dsk = _attn_bwd(sv[l, 'q'], k_rot, kv, do, sv[l, 'lse'], P['sinks'][j], "attn_bwd_" + t)
            G['sinks'] = dsk[0:1]
            kv_parts.append((dkn, dko, dvn, dvo))
            dh = _matmul(dq, W[l]['w_q'], 'nt', F32, "q_proj_dx_" + t, 1024,1024, 1024, rope=('a', cos_t, -sin_t))
            G['w_q'] = _matmul(sv[l, 'h'], dq, 'tn', F32, "q_proj_dw_" + t, 1024, 1024, 1024, rope=('b', cos_t, -sin_t))
            dx, G['mix_pre_g'] = _rms_bwd(sv[l, 'x_in'], row(P['mix_pre_g'], l), dh, dx, "q_norm_bwd_" + t)
            if l == n_a:
                dkv = _kv_grad(kv_parts, cos_t, -sin_t, "kv_grad")
                dhkv = _matmul(dkv, W[l]['w_kv'], 'nt', F32, "kv_proj_dx", 1024, 1024, 512)
                G['w_kv'] = _matmul(sv['hkv'], dkv, 'tn', F32, "kv_proj_dw", 1024, 512, 1024)
                dx, G['kv_norm_g'] = _rms_bwd(sv[l, 'x_in'], P['kv_norm_g'][None], dhkv, dx, "kv_norm_bwd")
        zero = grads_done(l, G, dx)
    return sq, dx


SMALL = ['mix_pre_g', 'mix_post_g', 'kv_norm_g', 'sinks', 'ffn_pre_g', 'ffn_post_g', 'ffn_conv_b', 'ffn_conv_w', 'pool_scale']
BIG = {'ffn_w_in': (1, 0, 1), 'ffn_w_out': (0, 1, 2), 'w_q': (0, 1, 2), 'w_o': (0, 1, 2), 'w_kv': (0, 1, 1),
       'pool_w': (1, 0, 1)}


def _swap_plan(half_axes, shapes):
    def plan(x, y, c, chips, src, land):
        return [(_at(s, len(shp), _half_dims(shp, hd, 1 - c)), d, (x, y, 1 - c))
                for s, d, hd, shp in zip(src, land, half_axes, shapes)]
    return plan


def _swap_start(pieces, after, name):
    arrs = [p[0] for p in pieces]
    plan = _swap_plan([p[2] for p in pieces], [a.shape for a in arrs])
    lands = []
    for a, p in zip(arrs, pieces):
        shp = list(a.shape)
        shp[p[2]] //= 2
        lands.append(lax.empty(tuple(shp), F32))
    return pieces, _copies_start(arrs, lands, after, name, plan, len(arrs)), plan


def _swap_wait(started, after, name):
    pieces, handle, plan = started
    arrs, theirs = _copies_wait(handle, after, name, plan)
    return [(a,) + p[1:] for a, p in zip(arrs, pieces)], list(theirs)


def _reduce_start(pieces, theirs, qc, after, tag):
    arrs = [p[0] for p in pieces]
    specs = [(p[1], p[2]) for p in pieces]
    sums = []
    for pi, (a, (sd, hd), r) in enumerate(zip(arrs, specs, theirs)):
        shp = r.shape
        nd = len(shp)
        if nd == 3:
            blk, grid = tuple(shp), (1,)
            mine = lambda i, s: (s[1], 0, 0)
            zero = lambda i, s: (0, 0, 0)
        elif hd == 0:
            tr = _tile(shp[0], max(16, BLOCK_ELEMS //shp[1]), 16)
            blk, grid = (tr, shp[1]), (shp[0] // tr,)
            nblk = shp[0] // tr
            mine = lambda i, s, nblk=nblk: (s[1] * nblk + i, 0)
            zero = lambda i, s: (i, 0)
        else:
            tr = _tile(shp[0], max(16, BLOCK_ELEMS //shp[1]), 16)
            blk, grid = (tr, shp[1]), (shp[0] // tr,)
            mine = lambda i, s: (i, s[1])
            zero = lambda i, s: (i, 0)
        sums.append(_sum_blocks(f"grad_chip_sum_{tag}_{pi}", qc, grid, shp, blk, zero, [(a, blk, mine), (r, blk, zero)],
                                out_dtype=BF16))

    lands = []
    for s_arr, (sd, hd) in zip(sums, specs):
        shp = list(s_arr.shape)
        shp[sd] //= N_SHARDS
        lands.append(lax.empty((3,) + tuple(shp), BF16))
    plan = _scatter_plan([sd for sd, _ in specs], [s.shape for s in sums])
    handle = _copies_start(sums, lands, after, "grad_scatter_start_" + tag, plan, 3 * len(sums))
    return dict(handle=handle, plan=plan, pieces=pieces, tag=tag)


def _reduce_finish(state, after, qc, outs, out_shapes):
    handle, pieces, tag = state['handle'], state['pieces'], state['tag']
    sums, recvd = _copies_wait(handle, after, "grad_scatter_wait_" + tag, state['plan'])
    for pi, ((a, sd, hd, oname, fixed, ohd), s_arr, r) in enumerate(zip(pieces, sums, recvd)):
        shp = r.shape[1:]
        nd = len(shp)
        lead = (fixed[0],) if fixed else ()
        none = (None,) if fixed else ()
        n_stack = out_shapes[oname][0]
        if nd == 3:
            blk, grid = tuple(shp), (1,)
            mine = lambda i, s: (0, s[0], 0)
            rk = [lambda i, s, k=k: (k, 0, 0, 0) for k in range(3)]
            oshape = (n_stack, 2 * shp[0]) + tuple(shp[1:])
            oblk = none + blk
            omap = lambda i, s, lead=lead: lead + (s[1], 0, 0)
        elif sd == 1:
            tr = _tile(shp[0], max(16, BLOCK_ELEMS //shp[1]), 16)
            blk, grid = (tr, shp[1]), (shp[0] // tr,)
            nblk = shp[0] // tr
            mine = lambda i, s: (i, s[0])
            rk = [lambda i, s, k=k: (k, i, 0) for k in range(3)]
            oshape = (n_stack, 2 * shp[0], shp[1])
            oblk = none + blk
            omap = lambda i, s, lead=lead, nblk=nblk: lead + (s[1] * nblk + i, 0)
        else:
            tr = _tile(shp[0], max(16, BLOCK_ELEMS //shp[1]), 16)
            blk, grid = (tr, shp[1]), (shp[0] // tr,)
            nblk = shp[0] // tr
            mine = lambda i, s, nblk=nblk: (s[0] * nblk + i, 0)
            rk = [lambda i, s, k=k: (k, i, 0) for k in range(3)]
            oshape = ((n_stack,) if fixed else ()) + (shp[0], 2 * shp[1])
            oblk = none + blk
            omap = lambda i, s, lead=lead: lead + (i, s[1])
        assert tuple(oshape) == tuple(out_shapes[oname]), (oname, oshape, out_shapes[oname])
        ins = [(s_arr, blk, mine)] + [(r, (None,) + blk, rk[k]) for k in range(3)]
        outs[oname] = _sum_blocks(f"grad_total_{tag}_{pi}", qc, grid, oshape, oblk, omap, ins, into=outs.get(oname))


def _pack_small(parts):
    rows, offs, r = [], [], 0
    for a in parts:
        flat = a.reshape(-1)
        nr = -(-flat.size // (8 * LANES)) * 8
        rows.append(jnp.pad(flat, (0, nr * LANES - flat.size)).reshape(nr, LANES))
        offs.append((r, nr, a.shape))
        r += nr
    return jnp.concatenate(rows, axis=0), offs


def _unpack_small(packed, offs):
    return [packed[r:r + nr].reshape(-1)[:math.prod(shape)].reshape(shape) for r, nr, shape in offs]


def kernel(x, positions, mix_pre_g, mix_post_g, pool_w, pool_scale, kv_norm_g, w_kv, w_q, w_o, sinks, ffn_pre_g, ffn_post_g, ffn_w_in, ffn_conv_w, ffn_conv_b, ffn_w_out, loss_target, m_mix_pre_g, m_mix_post_g, m_pool_w, m_pool_scale, m_kv_norm_g, m_w_kv, m_w_q, m_w_o, m_sinks, m_ffn_pre_g, m_ffn_post_g, m_ffn_w_in, m_ffn_conv_w, m_ffn_conv_b, m_ffn_w_out, v_mix_pre_g, v_mix_post_g, v_pool_w, v_pool_scale, v_kv_norm_g, v_w_kv, v_w_q, v_w_o, v_sinks, v_ffn_pre_g, v_ffn_post_g, v_ffn_w_in, v_ffn_conv_w, v_ffn_conv_b, v_ffn_w_out):
    w = dict(mix_pre_g=mix_pre_g, mix_post_g=mix_post_g, pool_w=pool_w, pool_scale=pool_scale, kv_norm_g=kv_norm_g,
             w_kv=w_kv, w_q=w_q, w_o=w_o, sinks=sinks, ffn_pre_g=ffn_pre_g, ffn_post_g=ffn_post_g, ffn_w_in=ffn_w_in,
             ffn_conv_w=ffn_conv_w, ffn_conv_b=ffn_conv_b, ffn_w_out=ffn_w_out)
    mom = dict(mix_pre_g=m_mix_pre_g, mix_post_g=m_mix_post_g, pool_w=m_pool_w, pool_scale=m_pool_scale,
               kv_norm_g=m_kv_norm_g, w_kv=m_w_kv, w_q=m_w_q, w_o=m_w_o, sinks=m_sinks, ffn_pre_g=m_ffn_pre_g,
               ffn_post_g=m_ffn_post_g, ffn_w_in=m_ffn_w_in, ffn_conv_w=m_ffn_conv_w, ffn_conv_b=m_ffn_conv_b,
               ffn_w_out=m_ffn_w_out)
    var = dict(mix_pre_g=v_mix_pre_g, mix_post_g=v_mix_post_g, pool_w=v_pool_w, pool_scale=v_pool_scale,
               kv_norm_g=v_kv_norm_g, w_kv=v_w_kv, w_q=v_w_q, w_o=v_w_o, sinks=v_sinks, ffn_pre_g=v_ffn_pre_g,
               ffn_post_g=v_ffn_post_g, ffn_w_in=v_ffn_w_in, ffn_conv_w=v_ffn_conv_w, ffn_conv_b=v_ffn_conv_b,
               ffn_w_out=v_ffn_w_out)
    depth = mix_pre_g.shape[0]
    q_chip = 2 * lax.axis_index("x") + lax.axis_index("y")
    qc = jnp.stack([q_chip, lax.axis_index("c")]).astype(jnp.int32)

    n_a = depth // 2
    D = x.shape[-1]
    gc = pool_w.shape[3]

    def layer_shards(l):
        first = []
        if l < n_a:
            first.append(('pool_w', pool_w[l].astype(BF16)))
        else:
            first += [('w_q', w_q[l - n_a].astype(BF16)), ('w_o', w_o[l - n_a].astype(BF16))]
            if l == n_a:
                first.append(('w_kv', w_kv.astype(BF16)))
        ffn = [('w_in', ffn_w_in[l].astype(BF16)), ('w_out', ffn_w_out[l].astype(BF16))]
        if l == 0:
            return [first + [('conv_w', ffn_conv_w), ('pool_scale', pool_scale)], ffn[:1], ffn[1:]]
        return [first + ffn]

    def start_group(items, after, name, plan):
        srcs = [a for _, a in items]
        lands = [lax.empty((N_SHARDS,) + a.shape, a.dtype) for a in srcs]
        handle = _copies_start(srcs, lands, after, name, functools.partial(plan, arriving=False), 4 * len(srcs))
        return [n for n, _ in items], handle, plan

    def wait_group(started, after, name):
        names, handle, plan = started
        _, lands = _copies_wait(handle, after, name, functools.partial(plan, arriving=True))
        return dict(zip(names, lands))

    def start_exchange(got, after, name):
        names = list(got)
        handle = _copies_start([], [got[n] for n in names], after, name,
                               functools.partial(_exchange_plan, arriving=False), 3 * len(names))
        return names, handle, _exchange_plan

    groups0 = layer_shards(0)
    small0 = start_group(groups0[0], x, "weight_gather_start_l0_small", _gather_plan)
    in0 = start_group(groups0[1], small0[1]['token'], "weight_gather_start_l0_in", _gather_half_plan)
    out0 = start_group(groups0[2], in0[1]['token'], "weight_gather_start_l0_out", _gather_half_plan)
    pending, shared, steps = {}, {}, {}

    def start_next(l, after):
        pending[l + 1] = start_group(layer_shards(l + 1)[0], after, f"weight_gather_start_l{l + 1}", _gather_plan)
        return pending[l + 1][1]['token'][0, 0]

    def pre_ffn0(after):
        landed = wait_group(in0, after, "weight_gather_wait_l0_in")
        steps['in'] = start_exchange(landed, after, "weight_exchange_start_l0_in")

    def w_in0(Wl, after):
        both = wait_group(steps['in'], after, "weight_exchange_wait_l0_in")
        Wl['tie'] = start_next(0, both['w_in'])
        return both['w_in']

    def w_out0(after):
        landed = wait_group(out0, after, "weight_gather_wait_l0_out")
        both = wait_group(start_exchange(landed, after, "weight_exchange_start_l0_out"), after,
                          "weight_exchange_wait_l0_out")
        return both['w_out'].reshape(-1, D)

    def weights_of(l, x_now):
        zero = 0.0
        if l == 0:
            got = wait_group(small0, out0[1]['token'], "weight_gather_wait_l0_small")
            shared['conv_w'] = got['conv_w']
            shared['pool_scale'] = got['pool_scale'].transpose(1, 0, 2).reshape(n_a, D)
        else:
            got = wait_group(pending.pop(l), x_now, f"weight_gather_wait_l{l}")
            if l + 1 < depth:
                zero = start_next(l, got['w_in'])
        taps = jnp.concatenate([shared['conv_w'][p, l] for p in (0, 2, 1, 3)], axis=-1)
        Wl = dict(cw=jnp.pad(taps, ((0, 5), (0, 0))))
        if l == 0:
            Wl['pre_ffn'], Wl['w_in'], Wl['w_out'] = pre_ffn0, functools.partial(w_in0, Wl), w_out0
        else:
            Wl['w_in'], Wl['w_out'] = got['w_in'], got['w_out'].reshape(-1, D)
        if l < n_a:
            Wl['pool_w'] = got['pool_w'].transpose(1, 0, 2, 3).reshape(-1, gc, gc)
            Wl['pool_scale'] = shared['pool_scale'][l][None]
        else:
            Wl['w_q'], Wl['w_o'] = got['w_q'].reshape(D, D), got['w_o'].reshape(D, D)
            if l == n_a:
                Wl['w_kv'] = got['w_kv'].reshape(D, -1)
        return Wl, zero

    big_shapes = {n: w[n].shape for n in BIG}
    big, G, scattering = {}, {}, {}

    swapping = {}

    def piece(n, g, l):
        lead = {0: (l if n.startswith('ffn') or n == 'pool_w' else l - n_a)} if len(big_shapes[n]) > 2 else {}
        return (g, BIG[n][0], BIG[n][1], n, lead, BIG[n][2])

    def ffn_grads_done(l, Gl, after):
        swapping[l] = _swap_start([piece(n, Gl[n], l) for n in ('ffn_w_in', 'ffn_w_out')], after, f"grad_swap_start_l{l}")
        return swapping[l][1]['token'][0, 0]

    small = {}

    def gather_small_grads():
        local = {
            'mix_pre_g': jnp.concatenate([G['mix_pre_g', l] for l in range(depth)], axis=0),
            'mix_post_g': jnp.concatenate([G['mix_post_g', l] for l in range(depth)], axis=0),
            'kv_norm_g': G['kv_norm_g', n_a][0],
            'sinks': jnp.concatenate([G['sinks', l][:, :sinks.shape[1]] for l in range(n_a, depth)], axis=0),
            'ffn_pre_g': jnp.concatenate([G['ffn_pre_g', l] for l in range(depth)], axis=0),
            'ffn_post_g': jnp.concatenate([G['ffn_post_g', l] for l in range(depth)], axis=0),
            'ffn_conv_b': jnp.concatenate([G['ffn_conv_b', l] for l in range(depth)], axis=0),
            'ffn_conv_w': jnp.stack([G['ffn_conv_w', l] for l in range(depth)], axis=0),
            'pool_scale': jnp.concatenate([G['pool_scale', l] for l in range(n_a)], axis=0),
        }
        packed, small['offs'] = _pack_small([local[n] for n in SMALL])
        small['gathered'] = _gather_small(packed, "small_grad_gather").reshape(8, packed.shape[0], LANES)

    def grads_done(l, Gl, dx_now):
        for n, g in Gl.items():
            if n not in BIG:
                G[n, l] = g
        after = dx_now
        if l == 0:
            gather_small_grads()
            after = small['gathered']
        pieces, theirs = _swap_wait(swapping.pop(l), after, f"grad_swap_wait_l{l}")
        rest = [piece(n, Gl[n], l) for n in BIG if n in Gl and not n.startswith('ffn')]
        theirs += list(_swap_halves([p[0] for p in rest], [(p[1], p[2]) for p in rest], f"grad_swap_halves_l{l}"))
        scattering[l] = _reduce_start(pieces + rest, theirs, qc, after, f"l{l}")
        if l + 1 in scattering:
            _reduce_finish(scattering.pop(l + 1), dx_now, qc, big, big_shapes)
        return scattering[l]['handle']['token'][0, 0]

    P = {n: w[n] for n in ('mix_pre_g', 'mix_post_g', 'kv_norm_g', 'sinks', 'ffn_pre_g', 'ffn_post_g', 'ffn_conv_b')}
    sq, dx = _local_step(x[0], loss_target[0], positions[0], P, weights_of, ffn_grads_done, grads_done)
    loss = 0.5 / D * lax.psum(jnp.sum(sq), ("x", "y", "c"))

    late = {'ffn_w_in': (1, depth - 1), 'ffn_w_out': (1, depth - 1), 'pool_w': (1, n_a - 1), 'w_q': None, 'w_o': None,
            'w_kv': None}
    names = list(big)
    in_flight = scattering[0]['handle']['token']
    whole = _share_halves([big[n] for n in names], [BIG[n][2] for n in names], [late[n] for n in names], in_flight,
                          "grad_share_halves_late")
    upd = {}
    for n, g in zip(names, whole):
        part = None if late[n] is None else late[n] + (w[n].shape[0],)
        upd[n] = _adamw(w[n], g, mom[n], var[n], "adamw_late_" + n, part=part)

    summed = _sum8(small['gathered'] + in_flight[0, 0], "small_grad_sum")
    grads = dict(zip(SMALL, _unpack_small(summed, small['offs'])))
    wd = ffn_conv_w.shape[2]
    grads['ffn_conv_w'] = lax.dynamic_slice_in_dim(grads['ffn_conv_w'], q_chip * wd, wd, axis=2)
    ps = pool_scale.shape[1]
    grads['pool_scale'] = lax.dynamic_slice_in_dim(grads['pool_scale'], q_chip * ps, ps, axis=1)

    delta, new_m, new_v = {}, {}, {}
    for n in SMALL:
        upd[n] = _adamw(w[n], grads[n], mom[n], var[n], "adamw_" + n)

    first, shapes0 = {}, {n: (1,) + w[n].shape[1:] for n in ('ffn_w_in', 'ffn_w_out', 'pool_w')}
    done = jnp.stack([upd[n][1][(-1,) * upd[n][1].ndim] for n in upd])
    _reduce_finish(scattering.pop(0), done, qc, first, shapes0)
    names0 = list(first)
    whole0 = _share_halves([first[n] for n in names0], [BIG[n][2] for n in names0], [None] * len(names0), done,
                           "grad_share_halves_l0")
    for n, g in zip(names0, whole0):
        upd[n] = _adamw(w[n], g, mom[n], var[n], "adamw_l0_" + n, part=(0, 1, w[n].shape[0]), into=upd[n])
    for n in upd:
        grads[n], delta[n], new_m[n], new_v[n] = upd[n]

    return (loss, dx[None], *[grads[n] for n in WEIGHTS], *[delta[n] for n in WEIGHTS],
            *[new_m[n] for n in WEIGHTS], *[new_v[n] for n in WEIGHTS])
```
